```python
import jax, jax.numpy as jnp
from jax import lax
import numpy as np

D_MODEL = 1024
BATCH = 16
SEQ = 4096
DEPTH = 1

HEAD_DIM = 64
N_HEADS_SB = 8
N_HEADS_FOX = 8
D_SB = N_HEADS_SB * HEAD_DIM
D_FOX = N_HEADS_FOX * HEAD_DIM
D_FF = 2816
CONV_WIDTH = 3
Q_BLOCK = 128
LN_EPS = 1e-5
DEEPNORM_ALPHA = (2.0 * DEPTH) ** 0.25
DEEPNORM_BETA = (8.0 * DEPTH) ** -0.25
FORGET_BIAS_MEAN = 3.0

SPLIT_SIZES = (D_SB, D_SB, D_SB, D_FOX, D_FOX, D_FOX, N_HEADS_FOX, D_MODEL, D_MODEL)
SPLIT_POINTS = tuple(int(v) for v in np.cumsum(SPLIT_SIZES)[:-1])
N_IN = int(sum(SPLIT_SIZES))

kernel_name = 'hybrid_stickbreak_fox_convglu_deepnorm'


def layer_norm(x, g, b):
    xf = x.astype(jnp.float32)
    mean = jnp.mean(xf, axis=-1, keepdims=True)
    var = jnp.mean(jnp.square(xf - mean), axis=-1, keepdims=True)
    y = (xf - mean) * lax.rsqrt(var + LN_EPS)
    return (y * g.astype(jnp.float32) + b.astype(jnp.float32)).astype(x.dtype)


def split_heads(t, n_heads):
    b, s, _ = t.shape
    return t.reshape(b, s, n_heads, HEAD_DIM).transpose(0, 2, 1, 3)


def merge_heads(t):
    b, h, s, d = t.shape
    return t.transpose(0, 2, 1, 3).reshape(b, s, h * d)


def stick_breaking_attention(q, k, v):
    seq = q.shape[2]
    scale = HEAD_DIM ** -0.5
    outs = []
    for i in range(seq // Q_BLOCK):
        end = (i + 1) * Q_BLOCK
        qb = q[:, :, i * Q_BLOCK:end]
        kb, vb = k[:, :, :end], v[:, :, :end]
        z = jnp.einsum('bhqd,bhkd->bhqk', qb, kb).astype(jnp.float32) * scale
        q_pos = jnp.arange(i * Q_BLOCK, end)
        k_pos = jnp.arange(end)
        strict = k_pos[None, :] < q_pos[:, None]
        log_beta = jax.nn.log_sigmoid(z)
        log_one_minus = jnp.where(strict, jax.nn.log_sigmoid(-z), 0.0)
        suffix = lax.cumsum(log_one_minus, axis=3, reverse=True) - log_one_minus
        weights = jnp.where(strict, jnp.exp(log_beta + suffix), 0.0)
        outs.append(jnp.einsum('bhqk,bhkd->bhqd', weights.astype(vb.dtype), vb))
    return jnp.concatenate(outs, axis=2)


def forgetting_attention(q, k, v, cum_log_f):
    seq = q.shape[2]
    scale = HEAD_DIM ** -0.5
    outs = []
    for i in range(seq // Q_BLOCK):
        end = (i + 1) * Q_BLOCK
        qb = q[:, :, i * Q_BLOCK:end]
        kb, vb = k[:, :, :end], v[:, :, :end]
        c_q = cum_log_f[:, :, i * Q_BLOCK:end]
        c_k = cum_log_f[:, :, :end]
        z = jnp.einsum('bhqd,bhkd->bhqk', qb, kb).astype(jnp.float32) * scale
        z = z + c_q[..., :, None] - c_k[..., None, :]
        q_pos = jnp.arange(i * Q_BLOCK, end)
        k_pos = jnp.arange(end)
        causal = k_pos[None, :] <= q_pos[:, None]
        p = jax.nn.softmax(jnp.where(causal, z, -jnp.inf), axis=-1)
        outs.append(jnp.einsum('bhqk,bhkd->bhqd', p.astype(vb.dtype), vb))
    return jnp.concatenate(outs, axis=2)


def causal_depthwise_conv(u, w_conv, b_conv):
    seq = u.shape[1]
    u_pad = jnp.pad(u, ((0, 0), (CONV_WIDTH - 1, 0), (0, 0)))
    y = b_conv
    for tap in range(CONV_WIDTH):
        y = y + w_conv[tap] * u_pad[:, tap:tap + seq]
    return y


def _fwd_setup_inputs(seed: int = 0) -> dict:
    key = jax.random.key(seed)
    ks = jax.random.split(key, 16)
    f32 = jnp.float32
    x = jax.random.normal(ks[0], (BATCH, SEQ, D_MODEL), f32)
    col_scale = np.ones((N_IN,), np.float32)
    bounds = (0,) + SPLIT_POINTS + (N_IN,)
    for idx in (2, 5):
        col_scale[bounds[idx]:bounds[idx + 1]] = DEEPNORM_BETA
    bias_offset = np.zeros((N_IN,), np.float32)
    bias_offset[bounds[6]:bounds[7]] = FORGET_BIAS_MEAN
    w_in = jax.random.normal(ks[1], (DEPTH, D_MODEL, N_IN), f32) * D_MODEL ** -0.5 * jnp.asarray(col_scale)
    b_in = 0.02 * jax.random.normal(ks[2], (DEPTH, N_IN), f32) + jnp.asarray(bias_offset)
    w_proj_sb = jax.random.normal(ks[3], (DEPTH, D_SB, D_MODEL), f32) * D_SB ** -0.5 * DEEPNORM_BETA
    w_proj_fox = jax.random.normal(ks[4], (DEPTH, D_FOX, D_MODEL), f32) * D_FOX ** -0.5 * DEEPNORM_BETA
    w_out = jax.random.normal(ks[5], (DEPTH, D_MODEL, D_MODEL), f32) * D_MODEL ** -0.5 * DEEPNORM_BETA
    ln1_g = 1.0 + 0.02 * jax.random.normal(ks[6], (DEPTH, D_MODEL), f32)
    ln1_b = 0.02 * jax.random.normal(ks[7], (DEPTH, D_MODEL), f32)
    w_up = jax.random.normal(ks[8], (DEPTH, D_MODEL, 2 * D_FF), f32) * D_MODEL ** -0.5
    w_conv = jax.random.normal(ks[9], (DEPTH, CONV_WIDTH, D_FF), f32) * CONV_WIDTH ** -0.5
    b_conv = 0.02 * jax.random.normal(ks[10], (DEPTH, D_FF), f32)
    w_down = jax.random.normal(ks[11], (DEPTH, D_FF, D_MODEL), f32) * D_FF ** -0.5 * DEEPNORM_BETA
    ln2_g = 1.0 + 0.02 * jax.random.normal(ks[12], (DEPTH, D_MODEL), f32)
    ln2_b = 0.02 * jax.random.normal(ks[13], (DEPTH, D_MODEL), f32)
    return {'x': x, 'w_in': w_in, 'b_in': b_in, 'w_proj_sb': w_proj_sb,
            'w_proj_fox': w_proj_fox, 'w_out': w_out, 'ln1_g': ln1_g, 'ln1_b': ln1_b,
            'w_up': w_up, 'w_conv': w_conv, 'b_conv': b_conv, 'w_down': w_down,
            'ln2_g': ln2_g, 'ln2_b': ln2_b}


def _fwd_reference(x, w_in, b_in, w_proj_sb, w_proj_fox, w_out, ln1_g, ln1_b,
              w_up, w_conv, b_conv, w_down, ln2_g, ln2_b):
    for l in range(DEPTH):
        h = jnp.einsum('bsd,dn->bsn', x, w_in[l]) + b_in[l]
        q_sb, k_sb, v_sb, q_fx, k_fx, v_fx, f_logit, g_sb, g_fx = jnp.split(h, SPLIT_POINTS, axis=-1)
        o_sb = stick_breaking_attention(split_heads(q_sb, N_HEADS_SB),
                                        split_heads(k_sb, N_HEADS_SB),
                                        split_heads(v_sb, N_HEADS_SB))
        cum_log_f = lax.cumsum(jax.nn.log_sigmoid(f_logit.astype(jnp.float32)), axis=1)
        o_fx = forgetting_attention(split_heads(q_fx, N_HEADS_FOX),
                                    split_heads(k_fx, N_HEADS_FOX),
                                    split_heads(v_fx, N_HEADS_FOX),
                                    cum_log_f.transpose(0, 2, 1))
        y_sb = jnp.einsum('bse,ed->bsd', merge_heads(o_sb), w_proj_sb[l])
        y_fx = jnp.einsum('bse,ed->bsd', merge_heads(o_fx), w_proj_fox[l])
        merged = jax.nn.sigmoid(g_sb) * y_sb + jax.nn.sigmoid(g_fx) * y_fx
        mix = jnp.einsum('bsd,de->bse', merged, w_out[l])
        x = layer_norm(DEEPNORM_ALPHA * x + mix, ln1_g[l], ln1_b[l])
        u = jnp.einsum('bsd,df->bsf', x, w_up[l])
        u_gate, u_val = jnp.split(u, 2, axis=-1)
        a = jax.nn.gelu(causal_depthwise_conv(u_gate, w_conv[l], b_conv[l]), approximate=False) * u_val
        ffn = jnp.einsum('bsf,fd->bsd', a, w_down[l])
        x = layer_norm(DEEPNORM_ALPHA * x + ffn, ln2_g[l], ln2_b[l])
    return x


import jax as _jax
import jax.numpy as _jnp

TWIN_FORMAT = 'train_step'
FWD_PARAMS = ['x', 'w_in', 'b_in', 'w_proj_sb', 'w_proj_fox', 'w_out', 'ln1_g', 'ln1_b', 'w_up', 'w_conv', 'b_conv', 'w_down', 'ln2_g', 'ln2_b']
TWIN_WEIGHTS = ['w_in', 'b_in', 'w_proj_sb', 'w_proj_fox', 'w_out', 'ln1_g', 'ln1_b', 'w_up', 'w_conv', 'b_conv', 'w_down', 'ln2_g', 'ln2_b']
TWIN_DIFF_INPUT = 'x'
TWIN_INPUTS = ['x', 'w_in', 'b_in', 'w_proj_sb', 'w_proj_fox', 'w_out', 'ln1_g', 'ln1_b', 'w_up', 'w_conv', 'b_conv', 'w_down', 'ln2_g', 'ln2_b', 'loss_target', 'm_w_in', 'm_b_in', 'm_w_proj_sb', 'm_w_proj_fox', 'm_w_out', 'm_ln1_g', 'm_ln1_b', 'm_w_up', 'm_w_conv', 'm_b_conv', 'm_w_down', 'm_ln2_g', 'm_ln2_b', 'v_w_in', 'v_b_in', 'v_w_proj_sb', 'v_w_proj_fox', 'v_w_out', 'v_ln1_g', 'v_ln1_b', 'v_w_up', 'v_w_conv', 'v_b_conv', 'v_w_down', 'v_ln2_g', 'v_ln2_b']
TWIN_OUTPUTS = ['loss', 'grad_x', 'grad_w_in', 'grad_b_in', 'grad_w_proj_sb', 'grad_w_proj_fox', 'grad_w_out', 'grad_ln1_g', 'grad_ln1_b', 'grad_w_up', 'grad_w_conv', 'grad_b_conv', 'grad_w_down', 'grad_ln2_g', 'grad_ln2_b', 'delta_w_in', 'delta_b_in', 'delta_w_proj_sb', 'delta_w_proj_fox', 'delta_w_out', 'delta_ln1_g', 'delta_ln1_b', 'delta_w_up', 'delta_w_conv', 'delta_b_conv', 'delta_w_down', 'delta_ln2_g', 'delta_ln2_b', 'new_m_w_in', 'new_m_b_in', 'new_m_w_proj_sb', 'new_m_w_proj_fox', 'new_m_w_out', 'new_m_ln1_g', 'new_m_ln1_b', 'new_m_w_up', 'new_m_w_conv', 'new_m_b_conv', 'new_m_w_down', 'new_m_ln2_g', 'new_m_ln2_b', 'new_v_w_in', 'new_v_b_in', 'new_v_w_proj_sb', 'new_v_w_proj_fox', 'new_v_w_out', 'new_v_ln1_g', 'new_v_ln1_b', 'new_v_w_up', 'new_v_w_conv', 'new_v_b_conv', 'new_v_w_down', 'new_v_ln2_g', 'new_v_ln2_b']
TWIN_LEAF_KINDS = {'loss': 'loss', 'grad_x': 'grad_x', 'grad_w_in': 'grad_w', 'grad_b_in': 'grad_w', 'grad_w_proj_sb': 'grad_w', 'grad_w_proj_fox': 'grad_w', 'grad_w_out': 'grad_w', 'grad_ln1_g': 'grad_w', 'grad_ln1_b': 'grad_w', 'grad_w_up': 'grad_w', 'grad_w_conv': 'grad_w', 'grad_b_conv': 'grad_w', 'grad_w_down': 'grad_w', 'grad_ln2_g': 'grad_w', 'grad_ln2_b': 'grad_w', 'delta_w_in': 'delta_w', 'delta_b_in': 'delta_w', 'delta_w_proj_sb': 'delta_w', 'delta_w_proj_fox': 'delta_w', 'delta_w_out': 'delta_w', 'delta_ln1_g': 'delta_w', 'delta_ln1_b': 'delta_w', 'delta_w_up': 'delta_w', 'delta_w_conv': 'delta_w', 'delta_b_conv': 'delta_w', 'delta_w_down': 'delta_w', 'delta_ln2_g': 'delta_w', 'delta_ln2_b': 'delta_w', 'new_m_w_in': 'new_m', 'new_m_b_in': 'new_m', 'new_m_w_proj_sb': 'new_m', 'new_m_w_proj_fox': 'new_m', 'new_m_w_out': 'new_m', 'new_m_ln1_g': 'new_m', 'new_m_ln1_b': 'new_m', 'new_m_w_up': 'new_m', 'new_m_w_conv': 'new_m', 'new_m_b_conv': 'new_m', 'new_m_w_down': 'new_m', 'new_m_ln2_g': 'new_m', 'new_m_ln2_b': 'new_m', 'new_v_w_in': 'new_v', 'new_v_b_in': 'new_v', 'new_v_w_proj_sb': 'new_v', 'new_v_w_proj_fox': 'new_v', 'new_v_w_out': 'new_v', 'new_v_ln1_g': 'new_v', 'new_v_ln1_b': 'new_v', 'new_v_w_up': 'new_v', 'new_v_w_conv': 'new_v', 'new_v_b_conv': 'new_v', 'new_v_w_down': 'new_v', 'new_v_ln2_g': 'new_v', 'new_v_ln2_b': 'new_v'}


def _forward(args):
    return _fwd_reference(*[args[k] for k in FWD_PARAMS])


def _output_shape():
    out = _jax.eval_shape(lambda: _forward(_fwd_setup_inputs(0)))
    return out.shape, out.dtype

N_MICROBATCH = 1
ADAM_LR = 0.001
ADAM_B1 = 0.9
ADAM_B2 = 0.999
ADAM_EPS = 1e-08
ADAM_WD = 0.01
ADAM_STEP = 10
PER_EXAMPLE_BATCH_AXIS = {'x': 0, 'loss_target': 0}
SHARED_INPUTS = []
_WEIGHT_DTYPES = {'w_in': _jnp.float32, 'b_in': _jnp.float32, 'w_proj_sb': _jnp.float32, 'w_proj_fox': _jnp.float32, 'w_out': _jnp.float32, 'ln1_g': _jnp.float32, 'ln1_b': _jnp.float32, 'w_up': _jnp.float32, 'w_conv': _jnp.float32, 'b_conv': _jnp.float32, 'w_down': _jnp.float32, 'ln2_g': _jnp.float32, 'ln2_b': _jnp.float32}
MOMENT_SCALE = {'w_in': 1.522376e-02, 'b_in': 7.508679e-02, 'w_proj_sb': 2.753039e-02, 'w_proj_fox': 1.342779e-02, 'w_out': 3.034131e-02, 'ln1_g': 1.609804e+00, 'ln1_b': 7.416240e-01, 'w_up': 4.830315e-02, 'w_conv': 4.917636e-02, 'b_conv': 4.600346e-02, 'w_down': 1.326507e-01, 'ln2_g': 6.400419e+01, 'ln2_b': 1.075044e+00}


def _to_microbatches(a, axis):
    t = _jnp.moveaxis(a, axis, 0)
    t = t.reshape((N_MICROBATCH, t.shape[0] // N_MICROBATCH) + t.shape[1:])
    return _jnp.moveaxis(t, 1, axis + 1)


def setup_inputs(seed: int = 0) -> dict:
    inp = _fwd_setup_inputs(seed)
    key = _jax.random.fold_in(_jax.random.key(seed), 7919)
    shape, _ = _output_shape()
    out = dict(inp)
    out["loss_target"] = _jax.random.normal(_jax.random.fold_in(key, 0), shape, _jnp.float32)
    for i, name in enumerate(TWIN_WEIGHTS):
        w = inp[name].astype(_jnp.float32)
        if MOMENT_SCALE is None:
            s = _jnp.sqrt(_jnp.mean(_jnp.square(w)) + 1e-30)
        else:
            s = MOMENT_SCALE[name]
        km, kv = _jax.random.split(_jax.random.fold_in(key, i + 1))
        out[name] = w
        out["m_" + name] = s * _jax.random.normal(km, w.shape, _jnp.float32)
        out["v_" + name] = (s * s) * _jax.random.uniform(kv, w.shape, _jnp.float32, 0.5, 1.5)
    if N_MICROBATCH > 1:
        for name, axis in PER_EXAMPLE_BATCH_AXIS.items():
            out[name] = _to_microbatches(out[name], axis)
    return {'x': out['x'], 'w_in': out['w_in'], 'b_in': out['b_in'], 'w_proj_sb': out['w_proj_sb'], 'w_proj_fox': out['w_proj_fox'], 'w_out': out['w_out'], 'ln1_g': out['ln1_g'], 'ln1_b': out['ln1_b'], 'w_up': out['w_up'], 'w_conv': out['w_conv'], 'b_conv': out['b_conv'], 'w_down': out['w_down'], 'ln2_g': out['ln2_g'], 'ln2_b': out['ln2_b'], 'loss_target': out['loss_target'], 'm_w_in': out['m_w_in'], 'm_b_in': out['m_b_in'], 'm_w_proj_sb': out['m_w_proj_sb'], 'm_w_proj_fox': out['m_w_proj_fox'], 'm_w_out': out['m_w_out'], 'm_ln1_g': out['m_ln1_g'], 'm_ln1_b': out['m_ln1_b'], 'm_w_up': out['m_w_up'], 'm_w_conv': out['m_w_conv'], 'm_b_conv': out['m_b_conv'], 'm_w_down': out['m_w_down'], 'm_ln2_g': out['m_ln2_g'], 'm_ln2_b': out['m_ln2_b'], 'v_w_in': out['v_w_in'], 'v_b_in': out['v_b_in'], 'v_w_proj_sb': out['v_w_proj_sb'], 'v_w_proj_fox': out['v_w_proj_fox'], 'v_w_out': out['v_w_out'], 'v_ln1_g': out['v_ln1_g'], 'v_ln1_b': out['v_ln1_b'], 'v_w_up': out['v_w_up'], 'v_w_conv': out['v_w_conv'], 'v_b_conv': out['v_b_conv'], 'v_w_down': out['v_w_down'], 'v_ln2_g': out['v_ln2_g'], 'v_ln2_b': out['v_ln2_b']}


def _loss(weights, diff, rest, loss_target):
    with _jax.named_scope("forward"):
        args = {**rest, TWIN_DIFF_INPUT: diff, **{k: w.astype(_WEIGHT_DTYPES[k]) for k, w in weights.items()}}
        y = _forward(args)
    with _jax.named_scope("loss_head"):
        err = _jnp.square(y.astype(_jnp.float32) - loss_target)
        return 0.5 * _jnp.sum(_jnp.mean(err, axis=-1)) if err.ndim else 0.5 * err


def _adamw(w, g, m, v):
    m = ADAM_B1 * m + (1.0 - ADAM_B1) * g
    v = ADAM_B2 * v + (1.0 - ADAM_B2) * _jnp.square(g)
    m_hat = m / (1.0 - ADAM_B1 ** ADAM_STEP)
    v_hat = v / (1.0 - ADAM_B2 ** ADAM_STEP)
    delta = -ADAM_LR * (m_hat / (_jnp.sqrt(v_hat) + ADAM_EPS) + ADAM_WD * w)
    return delta, m, v


def reference(x, w_in, b_in, w_proj_sb, w_proj_fox, w_out, ln1_g, ln1_b, w_up, w_conv, b_conv, w_down, ln2_g, ln2_b, loss_target, m_w_in, m_b_in, m_w_proj_sb, m_w_proj_fox, m_w_out, m_ln1_g, m_ln1_b, m_w_up, m_w_conv, m_b_conv, m_w_down, m_ln2_g, m_ln2_b, v_w_in, v_b_in, v_w_proj_sb, v_w_proj_fox, v_w_out, v_ln1_g, v_ln1_b, v_w_up, v_w_conv, v_b_conv, v_w_down, v_ln2_g, v_ln2_b):
    given = dict(x=x, w_in=w_in, b_in=b_in, w_proj_sb=w_proj_sb, w_proj_fox=w_proj_fox, w_out=w_out, ln1_g=ln1_g, ln1_b=ln1_b, w_up=w_up, w_conv=w_conv, b_conv=b_conv, w_down=w_down, ln2_g=ln2_g, ln2_b=ln2_b, loss_target=loss_target, m_w_in=m_w_in, m_b_in=m_b_in, m_w_proj_sb=m_w_proj_sb, m_w_proj_fox=m_w_proj_fox, m_w_out=m_w_out, m_ln1_g=m_ln1_g, m_ln1_b=m_ln1_b, m_w_up=m_w_up, m_w_conv=m_w_conv, m_b_conv=m_b_conv, m_w_down=m_w_down, m_ln2_g=m_ln2_g, m_ln2_b=m_ln2_b, v_w_in=v_w_in, v_b_in=v_b_in, v_w_proj_sb=v_w_proj_sb, v_w_proj_fox=v_w_proj_fox, v_w_out=v_w_out, v_ln1_g=v_ln1_g, v_ln1_b=v_ln1_b, v_w_up=v_w_up, v_w_conv=v_w_conv, v_b_conv=v_b_conv, v_w_down=v_w_down, v_ln2_g=v_ln2_g, v_ln2_b=v_ln2_b)
    weights = {n: given[n] for n in TWIN_WEIGHTS}
    shared = {n: given[n] for n in SHARED_INPUTS}
    per_example = {n: given[n] for n in ['x']}
    grad_fn = _jax.value_and_grad(_loss, argnums=(0, 1))

    def one_microbatch(ex, loss_target):
        ex = dict(ex)
        diff = ex.pop(TWIN_DIFF_INPUT)
        return grad_fn(weights, diff, {**shared, **ex}, loss_target)

    if N_MICROBATCH == 1:
        loss, (grad_w, grad_x) = one_microbatch(per_example, given["loss_target"])
    else:
        def body(carry, xs):
            loss_sum, grad_sum = carry
            l_k, (gw_k, gx_k) = one_microbatch(xs[0], xs[1])
            with _jax.named_scope("update"):
                return (loss_sum + l_k, _jax.tree.map(_jnp.add, grad_sum, gw_k)), gx_k

        init = (_jnp.zeros((), _jnp.float32), _jax.tree.map(_jnp.zeros_like, weights))
        (loss, grad_w), grad_x = _jax.lax.scan(body, init, (per_example, given["loss_target"]))
    with _jax.named_scope("update"):
        delta_w, new_m, new_v = {}, {}, {}
        for n in TWIN_WEIGHTS:
            delta_w[n], new_m[n], new_v[n] = _adamw(weights[n], grad_w[n], given["m_" + n], given["v_" + n])
    return (loss, grad_x, *[grad_w[n] for n in TWIN_WEIGHTS], *[delta_w[n] for n in TWIN_WEIGHTS],
            *[new_m[n] for n in TWIN_WEIGHTS], *[new_v[n] for n in TWIN_WEIGHTS])
```

```python
import functools
import math

import jax
import jax.numpy as jnp
from jax import lax
from jax.experimental import pallas as pl
from jax.experimental.pallas import tpu as pltpu

F32 = jnp.float32
BF16 = jnp.bfloat16

HEAD_DIM = 64
LN_EPS = 1e-5
DEPTH = 1
ALPHA = (2.0 * DEPTH) ** 0.25
ADAM_LR, ADAM_B1, ADAM_B2, ADAM_EPS, ADAM_WD, ADAM_STEP = 0.001, 0.9, 0.999, 1e-08, 0.01, 10
N_DEV = 8
LANES = 128
SUBLANES = 8
VMEM_LIMIT = 56 * 1024 * 1024

NN = ((1,), (0,))
NT = ((1,), (1,))
TN = ((0,), (0,))


def _dot(a, b, dims):
    return lax.dot_general(a, b, (dims, ((), ())), preferred_element_type=F32)


def _split_dot(x, t):
    hi = x.astype(BF16)
    lo = (x - hi.astype(F32)).astype(BF16)
    return _dot(hi, t, NN) + _dot(lo, t, NN)


def _params(*sem):
    return pltpu.CompilerParams(dimension_semantics=sem, vmem_limit_bytes=VMEM_LIMIT)


def _iotas(blk):
    row = lax.broadcasted_iota(jnp.int32, (blk, blk), 0)
    col = lax.broadcasted_iota(jnp.int32, (blk, blk), 1)
    return row, col


def _sb_terms(z):
    e = jnp.exp(-jnp.abs(z))
    sp = jnp.log1p(e)
    return jnp.minimum(z, 0.0) - sp, -jnp.maximum(z, 0.0) - sp, e


def _sb_fwd(q, k, v, blk):
    bh, s, dh = q.shape
    scale = dh ** -0.5

    def body(q_ref, k_ref, v_ref, o_ref, tot_ref):
        qi = pl.program_id(1)
        qb = (q_ref[...] * scale).astype(BF16)
        row, col = _iotas(blk)
        strict = col < row
        after = (row > col).astype(BF16)

        def block(j, o_acc, run, diag):
            off = pl.multiple_of(j * blk, blk)
            kj = k_ref[pl.ds(off, blk), :]
            vj = v_ref[pl.ds(off, blk), :]
            lb, lom, _ = _sb_terms(_dot(qb, kj, NT))
            if diag:
                lom = jnp.where(strict, lom, 0.0)
            sfx = _split_dot(lom, after)
            a = jnp.exp(lb + sfx + run)
            if diag:
                a = jnp.where(strict, a, 0.0)
            o_acc = o_acc + _dot(a.astype(BF16), vj, NN)
            return o_acc, run + sfx[:, 0:1] + lom[:, 0:1]

        carry = block(qi, jnp.zeros((blk, dh), F32), jnp.zeros((blk, 1), F32), True)
        o_acc, run = lax.fori_loop(0, qi, lambda it, c: block(qi - 1 - it, c[0], c[1], False), carry)
        o_ref[...] = o_acc
        tot_ref[...] = run

    return pl.pallas_call(
        body, name="sb_fwd", grid=(bh, s // blk),
        in_specs=[pl.BlockSpec((None, blk, dh), lambda b, i: (b, i, 0)),
                  pl.BlockSpec((None, s, dh), lambda b, i: (b, 0, 0)),
                  pl.BlockSpec((None, s, dh), lambda b, i: (b, 0, 0))],
        out_specs=[pl.BlockSpec((None, blk, dh), lambda b, i: (b, i, 0)),
                   pl.BlockSpec((None, blk, 1), lambda b, i: (b, i, 0))],
        out_shape=[jax.ShapeDtypeStruct((bh, s, dh), F32), jax.ShapeDtypeStruct((bh, s, 1), F32)],
        compiler_params=_params("parallel", "arbitrary"),
    )(q, k, v)


def _sb_bwd(q, k, v, do, tot, blk):
    bh, s, dh = q.shape
    scale = dh ** -0.5

    def body(q_ref, k_ref, v_ref, do_ref, tot_ref, dq_ref, dk_ref, dv_ref):
        qi = pl.program_id(1)

        @pl.when(qi == 0)
        def _():
            dk_ref[...] = jnp.zeros_like(dk_ref)
            dv_ref[...] = jnp.zeros_like(dv_ref)

        qb = (q_ref[...] * scale).astype(BF16)
        dob = do_ref[...].astype(BF16)
        tot_t = tot_ref[...]
        row, col = _iotas(blk)
        strict = col < row
        upto = (row <= col).astype(BF16)
        before = (row < col).astype(BF16)

        def block(j, dq_acc, cl, cg, diag):
            off = pl.multiple_of(j * blk, blk)
            kj = k_ref[pl.ds(off, blk), :]
            vj = v_ref[pl.ds(off, blk), :]
            z = _dot(qb, kj, NT)
            lb, lom, e = _sb_terms(z)
            if diag:
                lom = jnp.where(strict, lom, 0.0)
            pre = _split_dot(lom, upto)
            a = jnp.exp(lb + (tot_t - cl - pre))
            if diag:
                a = jnp.where(strict, a, 0.0)
            g = a * _dot(dob, vj, NT)
            pw = _split_dot(g, before)
            p = cg + pw
            r = 1.0 / (1.0 + e)
            er = e * r
            pos = z >= 0.0
            beta = jnp.where(pos, r, er)
            dz = g * jnp.where(pos, er, r) - p * beta
            if diag:
                dz = jnp.where(strict, dz, 0.0)
            dzb = dz.astype(BF16)
            dq_acc = dq_acc + _dot(dzb, kj, NN)
            dk_ref[pl.ds(off, blk), :] += _dot(dzb, qb, TN)
            dv_ref[pl.ds(off, blk), :] += _dot(a.astype(BF16), dob, TN)
            return dq_acc, cl + pre[:, blk - 1:blk], cg + pw[:, blk - 1:blk] + g[:, blk - 1:blk]

        zero = jnp.zeros((blk, 1), F32)
        carry = lax.fori_loop(0, qi, lambda j, c: block(j, c[0], c[1], c[2], False),
                              (jnp.zeros((blk, dh), F32), zero, zero))
        dq_acc, _, _ = block(qi, carry[0], carry[1], carry[2], True)
        dq_ref[...] = dq_acc * scale

    qspec = pl.BlockSpec((None, blk, dh), lambda b, i: (b, i, 0))
    full = pl.BlockSpec((None, s, dh), lambda b, i: (b, 0, 0))
    return pl.pallas_call(
        body, name="sb_bwd", grid=(bh, s // blk),
        in_specs=[qspec, full, full, qspec, pl.BlockSpec((None, blk, 1), lambda b, i: (b, i, 0))],
        out_specs=[qspec, full, full],
        out_shape=[jax.ShapeDtypeStruct((bh, s, dh), F32)] * 3,
        compiler_params=_params("parallel", "arbitrary"),
    )(q, k, v, do, tot)


NEG = -1e30


def _fox_fwd(q, k, v, c, ct, blk):
    bh, s, dh = q.shape
    scale = dh ** -0.5

    def body(q_ref, k_ref, v_ref, c_ref, ct_ref, o_ref, lse_ref):
        qi = pl.program_id(1)
        qb = (q_ref[...] * scale).astype(BF16)
        cq = c_ref[...]
        row, col = _iotas(blk)
        causal = col <= row

        def block(j, m, l, acc, diag):
            off = pl.multiple_of(j * blk, blk)
            kj = k_ref[pl.ds(off, blk), :]
            vj = v_ref[pl.ds(off, blk), :]
            z = _dot(qb, kj, NT) + (cq - ct_ref[:, pl.ds(off, blk)])
            if diag:
                z = jnp.where(causal, z, NEG)
            m_new = jnp.maximum(m, jnp.max(z, axis=1, keepdims=True))
            w = jnp.exp(m - m_new)
            p = jnp.exp(z - m_new)
            return m_new, w * l + jnp.sum(p, axis=1, keepdims=True), w * acc + _dot(p.astype(BF16), vj, NN)

        carry = block(qi, jnp.full((blk, 1), NEG, F32), jnp.zeros((blk, 1), F32), jnp.zeros((blk, dh), F32), True)
        m, l, acc = lax.fori_loop(0, qi, lambda j, c_: block(j, c_[0], c_[1], c_[2], False), carry)
        o_ref[...] = acc / l
        lse_ref[...] = m + jnp.log(l)

    qspec = pl.BlockSpec((None, blk, dh), lambda b, i: (b, i, 0))
    full = pl.BlockSpec((None, s, dh), lambda b, i: (b, 0, 0))
    rows = pl.BlockSpec((None, blk, 1), lambda b, i: (b, i, 0))
    return pl.pallas_call(
        body, name="fox_fwd", grid=(bh, s // blk),
        in_specs=[qspec, full, full, rows, pl.BlockSpec((None, 1, s), lambda b, i: (b, 0, 0))],
        out_specs=[qspec, rows],
        out_shape=[jax.ShapeDtypeStruct((bh, s, dh), F32), jax.ShapeDtypeStruct((bh, s, 1), F32)],
        compiler_params=_params("parallel", "arbitrary"),
    )(q, k, v, c, ct)


def _fox_bwd(q, k, v, c, ct, do, o, lse, blk):
    bh, s, dh = q.shape
    scale = dh ** -0.5

    def body(q_ref, k_ref, v_ref, c_ref, ct_ref, do_ref, o_ref, lse_ref, dq_ref, dk_ref, dv_ref, dct_ref):
        qi = pl.program_id(1)

        @pl.when(qi == 0)
        def _():
            dk_ref[...] = jnp.zeros_like(dk_ref)
            dv_ref[...] = jnp.zeros_like(dv_ref)
            dct_ref[...] = jnp.zeros_like(dct_ref)

        qb = (q_ref[...] * scale).astype(BF16)
        dob = do_ref[...].astype(BF16)
        delta = jnp.sum(dob.astype(F32) * o_ref[...], axis=1, keepdims=True)
        cq = c_ref[...]
        lse_t = lse_ref[...]
        row, col = _iotas(blk)
        causal = col <= row

        def block(j, dq_acc, diag):
            off = pl.multiple_of(j * blk, blk)
            kj = k_ref[pl.ds(off, blk), :]
            vj = v_ref[pl.ds(off, blk), :]
            z = _dot(qb, kj, NT) + (cq - ct_ref[:, pl.ds(off, blk)])
            p = jnp.exp(z - lse_t)
            if diag:
                p = jnp.where(causal, p, 0.0)
            ds = p * (_dot(dob, vj, NT) - delta)
            dsb = ds.astype(BF16)
            dk_ref[pl.ds(off, blk), :] += _dot(dsb, qb, TN)
            dv_ref[pl.ds(off, blk), :] += _dot(p.astype(BF16), dob, TN)
            dct_ref[:, pl.ds(off, blk)] -= jnp.sum(ds, axis=0, keepdims=True)
            return dq_acc + _dot(dsb, kj, NN)

        dq_acc = lax.fori_loop(0, qi, lambda j, a: block(j, a, False), jnp.zeros((blk, dh), F32))
        dq_ref[...] = block(qi, dq_acc, True) * scale

    qspec = pl.BlockSpec((None, blk, dh), lambda b, i: (b, i, 0))
    full = pl.BlockSpec((None, s, dh), lambda b, i: (b, 0, 0))
    rows = pl.BlockSpec((None, blk, 1), lambda b, i: (b, i, 0))
    krow = pl.BlockSpec((None, 1, s), lambda b, i: (b, 0, 0))
    return pl.pallas_call(
        body, name="fox_bwd", grid=(bh, s // blk),
        in_specs=[qspec, full, full, rows, krow, qspec, qspec, rows],
        out_specs=[qspec, full, full, krow],
        out_shape=[jax.ShapeDtypeStruct((bh, s, dh), F32)] * 3 + [jax.ShapeDtypeStruct((bh, 1, s), F32)],
        compiler_params=_params("parallel", "arbitrary"),
    )(q, k, v, c, ct, do, o, lse)


def _scan_rows(f2, group, mode, d2=None):
    n = f2.shape[0]

    def body(*refs):
        f_ref, o_ref = refs[0], refs[-1]
        f = f_ref[...]
        row, col = _iotas(LANES)
        grow = lax.broadcasted_iota(jnp.int32, (n, n), 0)
        gcol = lax.broadcasted_iota(jnp.int32, (n, n), 1)
        same = (grow // group) == (gcol // group)
        e = jnp.exp(-jnp.abs(f))
        if mode == "fwd":
            x = jnp.minimum(f, 0.0) - jnp.log1p(e)
            within = (row <= col).astype(F32)
            earlier = (same & (gcol < grow)).astype(F32)
        else:
            x = refs[1][...]
            within = (row >= col).astype(F32)
            earlier = (same & (gcol > grow)).astype(F32)
        y = jnp.dot(x, within, preferred_element_type=F32, precision=lax.Precision.HIGHEST)
        tot = jnp.sum(x, axis=1, keepdims=True)
        y = y + jnp.dot(earlier, tot, preferred_element_type=F32, precision=lax.Precision.HIGHEST)
        if mode == "bwd":
            r = 1.0 / (1.0 + e)
            y = y * jnp.where(f >= 0.0, e * r, r)
        o_ref[...] = y

    args = (f2,) if mode == "fwd" else (f2, d2)
    return pl.pallas_call(body, name="logf_" + mode, out_shape=jax.ShapeDtypeStruct(f2.shape, F32),
                          compiler_params=_params())(*args)


def _matmul(a, b, dims, *, tm, tn, tk, out_dtype, name, bias=None, res=None, res_scale=1.0):
    if dims == NN:
        (m, kk), n = a.shape, b.shape[1]
        a_spec = pl.BlockSpec((tm, tk), lambda i, j, k: (i, k))
        b_spec = pl.BlockSpec((tk, tn), lambda i, j, k: (k, j))
    elif dims == NT:
        (m, kk), n = a.shape, b.shape[0]
        a_spec = pl.BlockSpec((tm, tk), lambda i, j, k: (i, k))
        b_spec = pl.BlockSpec((tn, tk), lambda i, j, k: (j, k))
    else:
        (kk, m), n = a.shape, b.shape[1]
        a_spec = pl.BlockSpec((tk, tm), lambda i, j, k: (k, i))
        b_spec = pl.BlockSpec((tk, tn), lambda i, j, k: (k, j))
    assert m % tm == 0 and n % tn == 0 and kk % tk == 0, (name, m, n, kk, tm, tn, tk)
    nk = kk // tk
    extras, extra_specs = [], []
    if bias is not None:
        extras.append(bias)
        extra_specs.append(pl.BlockSpec((1, tn), lambda i, j, k: (0, j)))
    if res is not None:
        extras.append(res)
        extra_specs.append(pl.BlockSpec((tm, tn), lambda i, j, k: (i, j)))

    def body(a_ref, b_ref, *rest):
        o_ref, acc_ref = rest[-2], rest[-1]
        k = pl.program_id(2)
        part = _dot(a_ref[...].astype(BF16), b_ref[...].astype(BF16), dims)

        @pl.when(k == 0)
        def _():
            acc_ref[...] = part

        @pl.when(k > 0)
        def _():
            acc_ref[...] += part

        @pl.when(k == nk - 1)
        def _():
            out = acc_ref[...]
            idx = 0
            if bias is not None:
                out = out + rest[idx][...]
                idx += 1
            if res is not None:
                out = out + res_scale * rest[idx][...]
            o_ref[...] = out.astype(o_ref.dtype)

    return pl.pallas_call(
        body, name=name, grid=(m // tm, n // tn, nk),
        in_specs=[a_spec, b_spec] + extra_specs,
        out_specs=pl.BlockSpec((tm, tn), lambda i, j, k: (i, j)),
        out_shape=jax.ShapeDtypeStruct((m, n), out_dtype),
        scratch_shapes=[pltpu.VMEM((tm, tn), F32)],
        compiler_params=_params("parallel", "parallel", "arbitrary"),
    )(a, b, *extras)


def _sigmoid(x):
    e = jnp.exp(-jnp.abs(x))
    r = 1.0 / (1.0 + e)
    return jnp.where(x >= 0.0, r, e * r)


def _proj_gate_fwd(o_sb, o_fx, wp_sb, wp_fx, g, tm):
    t, e = o_sb.shape
    d = wp_sb.shape[1]

    def body(osb_ref, ofx_ref, wsb_ref, wfx_ref, gsb_ref, gfx_ref, mg_ref, ysb_ref, yfx_ref):
        ysb = _dot(osb_ref[...].astype(BF16), wsb_ref[...], NN)
        yfx = _dot(ofx_ref[...].astype(BF16), wfx_ref[...], NN)
        ysb_ref[...] = ysb
        yfx_ref[...] = yfx
        mg_ref[...] = (_sigmoid(gsb_ref[...]) * ysb + _sigmoid(gfx_ref[...]) * yfx).astype(BF16)

    rows_e = pl.BlockSpec((tm, e), lambda i: (i, 0))
    rows_d = pl.BlockSpec((tm, d), lambda i: (i, 0))
    w_spec = pl.BlockSpec((e, d), lambda i: (0, 0))
    return pl.pallas_call(
        body, name="proj_gate_fwd", grid=(t // tm,),
        in_specs=[rows_e, rows_e, w_spec, w_spec, rows_d, pl.BlockSpec((tm, d), lambda i: (i, 1))],
        out_specs=[rows_d, rows_d, rows_d],
        out_shape=[jax.ShapeDtypeStruct((t, d), BF16), jax.ShapeDtypeStruct((t, d), F32), jax.ShapeDtypeStruct((t, d), F32)],
        compiler_params=_params("parallel"),
    )(o_sb, o_fx, wp_sb, wp_fx, g, g)


def _gate_bwd(dmg, y_sb, y_fx, g, tm):
    t, d = dmg.shape

    def body(dm_ref, ysb_ref, yfx_ref, gsb_ref, gfx_ref, dysb_ref, dyfx_ref, dg_ref):
        dm = dm_ref[...]
        ssb = _sigmoid(gsb_ref[...])
        sfx = _sigmoid(gfx_ref[...])
        dysb_ref[...] = (dm * ssb).astype(BF16)
        dyfx_ref[...] = (dm * sfx).astype(BF16)
        dg_ref[:, 0:d] = (dm * ysb_ref[...] * ssb * (1.0 - ssb)).astype(BF16)
        dg_ref[:, d:2 * d] = (dm * yfx_ref[...] * sfx * (1.0 - sfx)).astype(BF16)

    rows = pl.BlockSpec((tm, d), lambda i: (i, 0))
    rows1 = pl.BlockSpec((tm, d), lambda i: (i, 1))
    return pl.pallas_call(
        body, name="gate_bwd", grid=(t // tm,),
        in_specs=[rows, rows, rows, rows, rows1],
        out_specs=[rows, rows, pl.BlockSpec((tm, 2 * d), lambda i: (i, 0))],
        out_shape=[jax.ShapeDtypeStruct((t, d), BF16)] * 2 + [jax.ShapeDtypeStruct((t, 2 * d), BF16)],
        compiler_params=_params("parallel"),
    )(dmg, y_sb, y_fx, g, g)


def _mm_res_ln(a, w, xres, gamma, beta, tm, name):
    t, kk = a.shape
    d = w.shape[1]

    def body(a_ref, w_ref, x_ref, g_ref, b_ref, xn_ref, xh_ref, rs_ref):
        r = ALPHA * x_ref[...] + _dot(a_ref[...].astype(BF16), w_ref[...], NN)
        mean = jnp.mean(r, axis=1, keepdims=True)
        cen = r - mean
        rstd = lax.rsqrt(jnp.mean(cen * cen, axis=1, keepdims=True) + LN_EPS)
        xh = cen * rstd
        xh_ref[...] = xh
        xn_ref[...] = xh * g_ref[...] + b_ref[...]
        rs_ref[...] = rstd

    rows_d = pl.BlockSpec((tm, d), lambda i: (i, 0))
    vec = pl.BlockSpec((1, d), lambda i: (0, 0))
    return pl.pallas_call(
        body, name=name, grid=(t // tm,),
        in_specs=[pl.BlockSpec((tm, kk), lambda i: (i, 0)), pl.BlockSpec((kk, d), lambda i: (0, 0)), rows_d, vec, vec],
        out_specs=[rows_d, rows_d, pl.BlockSpec((tm, 1), lambda i: (i, 0))],
        out_shape=[jax.ShapeDtypeStruct((t, d), F32), jax.ShapeDtypeStruct((t, d), F32), jax.ShapeDtypeStruct((t, 1), F32)],
        compiler_params=_params("parallel"),
    )(a, w, xres, gamma, beta)


def _ln_bwd_math(dy, xh, rstd, gamma):
    dxh = dy * gamma
    m1 = jnp.mean(dxh, axis=1, keepdims=True)
    m2 = jnp.mean(dxh * xh, axis=1, keepdims=True)
    return rstd * (dxh - m1 - xh * m2)


def _rowsum8(x):
    tm, n = x.shape
    return jnp.sum(x.reshape(tm // SUBLANES, SUBLANES, n), axis=0)


def _fold8(ref):
    ref[0:1, :] = jnp.sum(ref[...], axis=0, keepdims=True)


def _loss_ln_bwd(x2, xh, rstd, gamma, target, tm):
    t, d = x2.shape

    def body(x_ref, xh_ref, rs_ref, g_ref, tg_ref, dr_ref, dg_ref, db_ref, ls_ref):
        @pl.when(pl.program_id(0) == 0)
        def _():
            dg_ref[...] = jnp.zeros_like(dg_ref)
            db_ref[...] = jnp.zeros_like(db_ref)
            ls_ref[...] = jnp.zeros_like(ls_ref)

        err = x_ref[...] - tg_ref[...]
        xh = xh_ref[...]
        dy = err * (1.0 / d)
        dr_ref[...] = _ln_bwd_math(dy, xh, rs_ref[...], g_ref[...])
        dg_ref[...] += _rowsum8(dy * xh)
        db_ref[...] += _rowsum8(dy)
        sq = _rowsum8(err * err)
        part = sq[:, 0:LANES]
        for j in range(1, d // LANES):
            part = part + sq[:, j * LANES:(j + 1) * LANES]
        ls_ref[...] += part * (0.5 / d)

        @pl.when(pl.program_id(0) == t // tm - 1)
        def _():
            _fold8(dg_ref)
            _fold8(db_ref)
            ls_ref[0:1, 0:1] = jnp.sum(jnp.sum(ls_ref[...], axis=0, keepdims=True), axis=1, keepdims=True)

    rows = pl.BlockSpec((tm, d), lambda i: (i, 0))
    acc = pl.BlockSpec((SUBLANES, d), lambda i: (0, 0))
    return pl.pallas_call(
        body, name="loss_ln_bwd", grid=(t // tm,),
        in_specs=[rows, rows, pl.BlockSpec((tm, 1), lambda i: (i, 0)), pl.BlockSpec((1, d), lambda i: (0, 0)), rows],
        out_specs=[rows, acc, acc, pl.BlockSpec((SUBLANES, LANES), lambda i: (0, 0))],
        out_shape=[jax.ShapeDtypeStruct((t, d), F32), jax.ShapeDtypeStruct((SUBLANES, d), F32),
                   jax.ShapeDtypeStruct((SUBLANES, d), F32), jax.ShapeDtypeStruct((SUBLANES, LANES), F32)],
        compiler_params=_params("arbitrary"),
    )(x2, xh, rstd, gamma, target)


def _ln_bwd(dr_next, dlin, xh, rstd, gamma, tm):
    t, d = xh.shape

    def body(dn_ref, dl_ref, xh_ref, rs_ref, g_ref, dr_ref, dg_ref, db_ref):
        @pl.when(pl.program_id(0) == 0)
        def _():
            dg_ref[...] = jnp.zeros_like(dg_ref)
            db_ref[...] = jnp.zeros_like(db_ref)

        dy = ALPHA * dn_ref[...] + dl_ref[...]
        xh = xh_ref[...]
        dr_ref[...] = _ln_bwd_math(dy, xh, rs_ref[...], g_ref[...])
        dg_ref[...] += _rowsum8(dy * xh)
        db_ref[...] += _rowsum8(dy)

        @pl.when(pl.program_id(0) == t // tm - 1)
        def _():
            _fold8(dg_ref)
            _fold8(db_ref)

    rows = pl.BlockSpec((tm, d), lambda i: (i, 0))
    acc = pl.BlockSpec((SUBLANES, d), lambda i: (0, 0))
    return pl.pallas_call(
        body, name="ln_bwd", grid=(t // tm,),
        in_specs=[rows, rows, rows, pl.BlockSpec((tm, 1), lambda i: (i, 0)), pl.BlockSpec((1, d), lambda i: (0, 0))],
        out_specs=[rows, acc, acc],
        out_shape=[jax.ShapeDtypeStruct((t, d), F32), jax.ShapeDtypeStruct((SUBLANES, d), F32),
                   jax.ShapeDtypeStruct((SUBLANES, d), F32)],
        compiler_params=_params("arbitrary"),
    )(dr_next, dlin, xh, rstd, gamma)


def _shift_rows(x, halo, shift, row):
    out = pltpu.roll(x, shift, 0)
    for r in range(shift):
        out = jnp.where(row == r, halo[SUBLANES - shift + r:SUBLANES - shift + r + 1, :], out)
    return out


def _unshift_rows(x, halo, shift, row, tm):
    out = pltpu.roll(x, tm - shift, 0)
    for r in range(shift):
        out = jnp.where(row == tm - shift + r, halo[r:r + 1, :], out)
    return out


def _conv_pre(ug_ref, halo_ref, wc_ref, bc_ref, first, tm):
    ug = ug_ref[...]
    halo = jnp.where(first, 0.0, halo_ref[...])
    row = lax.broadcasted_iota(jnp.int32, ug.shape, 0)
    wc = wc_ref[...]
    um1 = _shift_rows(ug, halo, 1, row)
    um2 = _shift_rows(ug, halo, 2, row)
    c = bc_ref[...] + wc[2:3, :] * ug + wc[1:2, :] * um1 + wc[0:1, :] * um2
    return c, ug, um1, um2


INV_SQRT2 = 1.0 / math.sqrt(2.0)
INV_SQRT2PI = 1.0 / math.sqrt(2.0 * math.pi)


def _conv_glu_fwd(u, wc, bc, seq, tm):
    t, f2 = u.shape
    f = f2 // 2
    per_seq = seq // tm
    hb = tm // SUBLANES

    def body(ug_ref, halo_ref, uv_ref, wc_ref, bc_ref, a_ref):
        first = (pl.program_id(0) % per_seq) == 0
        c, _, _, _ = _conv_pre(ug_ref, halo_ref, wc_ref, bc_ref, first, tm)
        gelu = 0.5 * c * (1.0 + lax.erf(c * INV_SQRT2))
        a_ref[...] = (gelu * uv_ref[...]).astype(BF16)

    return pl.pallas_call(
        body, name="conv_glu_fwd", grid=(t // tm,),
        in_specs=[pl.BlockSpec((tm, f), lambda i: (i, 0)),
                  pl.BlockSpec((SUBLANES, f), lambda i: (jnp.maximum(i * hb - 1, 0), 0)),
                  pl.BlockSpec((tm, f), lambda i: (i, 1)),
                  pl.BlockSpec((3, f), lambda i: (0, 0)), pl.BlockSpec((1, f), lambda i: (0, 0))],
        out_specs=pl.BlockSpec((tm, f), lambda i: (i, 0)),
        out_shape=jax.ShapeDtypeStruct((t, f), BF16),
        compiler_params=_params("parallel"),
    )(u, u, u, wc, bc)


def _conv_glu_bwd1(u, da, wc, bc, seq, tm):
    t, f2 = u.shape
    f = f2 // 2
    per_seq = seq // tm
    hb = tm // SUBLANES

    def body(ug_ref, halo_ref, uv_ref, da_ref, wc_ref, bc_ref, dc_ref, duv_ref):
        first = (pl.program_id(0) % per_seq) == 0
        c, _, _, _ = _conv_pre(ug_ref, halo_ref, wc_ref, bc_ref, first, tm)
        cdf = 0.5 * (1.0 + lax.erf(c * INV_SQRT2))
        pdf = jnp.exp(-0.5 * c * c) * INV_SQRT2PI
        da = da_ref[...]
        duv_ref[...] = (da * (c * cdf)).astype(BF16)
        dc_ref[...] = da * uv_ref[...] * (cdf + c * pdf)

    rows = pl.BlockSpec((tm, f), lambda i: (i, 0))
    return pl.pallas_call(
        body, name="conv_glu_bwd1", grid=(t // tm,),
        in_specs=[rows, pl.BlockSpec((SUBLANES, f), lambda i: (jnp.maximum(i * hb - 1, 0), 0)),
                  pl.BlockSpec((tm, f), lambda i: (i, 1)), rows,
                  pl.BlockSpec((3, f), lambda i: (0, 0)), pl.BlockSpec((1, f), lambda i: (0, 0))],
        out_specs=[rows, rows],
        out_shape=[jax.ShapeDtypeStruct((t, f), F32), jax.ShapeDtypeStruct((t, f), BF16)],
        compiler_params=_params("parallel"),
    )(u, u, u, da, wc, bc)


def _conv_glu_bwd2(u, dc, wc, seq, tm):
    t, f2 = u.shape
    f = f2 // 2
    per_seq = seq // tm
    hb = tm // SUBLANES
    nblk = t // SUBLANES

    def body(ug_ref, halo_ref, dc_ref, nxt_ref, wc_ref, dug_ref, w0_ref, w1_ref, w2_ref, b_ref):
        i = pl.program_id(0)

        @pl.when(i == 0)
        def _():
            for r in (w0_ref, w1_ref, w2_ref, b_ref):
                r[...] = jnp.zeros_like(r)

        first = (i % per_seq) == 0
        last = (i % per_seq) == per_seq - 1
        ug = ug_ref[...]
        halo = jnp.where(first, 0.0, halo_ref[...])
        nxt = jnp.where(last, 0.0, nxt_ref[...])
        row = lax.broadcasted_iota(jnp.int32, ug.shape, 0)
        dc = dc_ref[...]
        wc = wc_ref[...]
        dp1 = _unshift_rows(dc, nxt, 1, row, tm)
        dp2 = _unshift_rows(dc, nxt, 2, row, tm)
        dug_ref[...] = (wc[2:3, :] * dc + wc[1:2, :] * dp1 + wc[0:1, :] * dp2).astype(BF16)
        w2_ref[...] += _rowsum8(dc * ug)
        w1_ref[...] += _rowsum8(dc * _shift_rows(ug, halo, 1, row))
        w0_ref[...] += _rowsum8(dc * _shift_rows(ug, halo, 2, row))
        b_ref[...] += _rowsum8(dc)

        @pl.when(i == t // tm - 1)
        def _():
            for r in (w0_ref, w1_ref, w2_ref, b_ref):
                _fold8(r)

    rows = pl.BlockSpec((tm, f), lambda i: (i, 0))
    acc = pl.BlockSpec((SUBLANES, f), lambda i: (0, 0))
    return pl.pallas_call(
        body, name="conv_glu_bwd2", grid=(t // tm,),
        in_specs=[rows, pl.BlockSpec((SUBLANES, f), lambda i: (jnp.maximum(i * hb - 1, 0), 0)),
                  rows, pl.BlockSpec((SUBLANES, f), lambda i: (jnp.minimum((i + 1) * hb, nblk - 1), 0)),
                  pl.BlockSpec((3, f), lambda i: (0, 0))],
        out_specs=[rows, acc, acc, acc, acc],
        out_shape=[jax.ShapeDtypeStruct((t, f), BF16)] + [jax.ShapeDtypeStruct((SUBLANES, f), F32)] * 4,
        compiler_params=_params("arbitrary"),
    )(u, u, dc, dc, wc)


def _colsum(x, tm, name):
    t, n = x.shape

    def body(x_ref, o_ref):
        @pl.when(pl.program_id(0) == 0)
        def _():
            o_ref[...] = jnp.zeros_like(o_ref)

        o_ref[...] += _rowsum8(x_ref[...].astype(F32))

        @pl.when(pl.program_id(0) == t // tm - 1)
        def _():
            _fold8(o_ref)

    return pl.pallas_call(
        body, name=name, grid=(t // tm,),
        in_specs=[pl.BlockSpec((tm, n), lambda i: (i, 0))],
        out_specs=pl.BlockSpec((SUBLANES, n), lambda i: (0, 0)),
        out_shape=jax.ShapeDtypeStruct((SUBLANES, n), F32),
        compiler_params=_params("arbitrary"),
    )(x)


def _adamw(w, gparts, m, v, name):
    p, r, c = gparts.shape
    tr = r
    for cand in (512, 256, 128, 64, 32, 16, 8):
        if r % cand == 0 and r > cand:
            tr = cand
            break
    c1 = 1.0 - ADAM_B1 ** ADAM_STEP
    c2 = 1.0 - ADAM_B2 ** ADAM_STEP

    def body(w_ref, g_ref, m_ref, v_ref, go_ref, d_ref, mo_ref, vo_ref):
        g = g_ref[0]
        for i in range(1, p):
            g = g + g_ref[i]
        mn = ADAM_B1 * m_ref[...] + (1.0 - ADAM_B1) * g
        vn = ADAM_B2 * v_ref[...] + (1.0 - ADAM_B2) * (g * g)
        go_ref[...] = g
        mo_ref[...] = mn
        vo_ref[...] = vn
        d_ref[...] = -ADAM_LR * ((mn / c1) / (jnp.sqrt(vn / c2) + ADAM_EPS) + ADAM_WD * w_ref[...])

    blk = pl.BlockSpec((tr, c), lambda i: (i, 0))
    return pl.pallas_call(
        body, name=name, grid=(r // tr,),
        in_specs=[blk, pl.BlockSpec((p, tr, c), lambda i: (0, i, 0)), blk, blk],
        out_specs=[blk] * 4,
        out_shape=[jax.ShapeDtypeStruct((r, c), F32)] * 4,
        compiler_params=_params("parallel"),
    )(w, gparts, m, v)


MESH = pl.DeviceIdType.MESH
ANY = pl.BlockSpec(memory_space=pl.ANY)


def _all_gather(x):
    def body(x_ref, out_ref, send_sems, recv_sems, local_sem):
        x, y, c = lax.axis_index("x"), lax.axis_index("y"), lax.axis_index("c")
        me, sibling = (x, y, c), (x, y, 1 - c)
        chips = [(1 - x, y), (x, 1 - y), (1 - x, 1 - y)]

        def slot(px, py, pc):
            return out_ref.at[4 * px + 2 * py + pc]

        def copy(k, block, to, src=None):
            return pltpu.make_async_remote_copy(
                src_ref=slot(*block) if src is None else src, dst_ref=slot(*block),
                send_sem=send_sems.at[k], recv_sem=recv_sems.at[k], device_id=to, device_id_type=MESH)

        mine = pltpu.make_async_copy(x_ref, slot(*me), local_sem)
        mine.start()
        first = [copy(0, me, sibling, src=x_ref)]
        first += [copy(1 + j, me, (*chip, c), src=x_ref) for j, chip in enumerate(chips)]
        for cp in first:
            cp.start()
        passed = [copy(4 + j, (*chip, c), sibling) for j, chip in enumerate(chips)]
        for j, chip in enumerate(chips):
            copy(1 + j, (*chip, c), me).wait_recv()
            passed[j].start()
        copy(0, sibling, me).wait_recv()
        for j, chip in enumerate(chips):
            copy(4 + j, (*chip, 1 - c), me).wait_recv()
        for cp in first + passed:
            cp.wait_send()
        mine.wait()

    return pl.pallas_call(
        body, name="all_gather_" + str(x.shape[0]),
        out_shape=jax.ShapeDtypeStruct((N_DEV,) + x.shape, x.dtype),
        in_specs=[ANY], out_specs=ANY,
        scratch_shapes=[pltpu.SemaphoreType.DMA((7,)), pltpu.SemaphoreType.DMA((7,)), pltpu.SemaphoreType.DMA],
    )(x)


def _exchange(g):
    def body(g_ref, land_ref, send_sems, recv_sems, local_sem):
        x, y, c = lax.axis_index("x"), lax.axis_index("y"), lax.axis_index("c")
        mine = 4 * x + 2 * y + c
        own = pltpu.make_async_copy(g_ref.at[mine], land_ref.at[mine], local_sem)
        own.start()
        copies = []
        for k in range(1, N_DEV):
            px = 1 - x if k & 4 else x
            py = 1 - y if k & 2 else y
            pc = 1 - c if k & 1 else c
            copies.append(pltpu.make_async_remote_copy(
                src_ref=g_ref.at[4 * px + 2 * py + pc], dst_ref=land_ref.at[mine],
                send_sem=send_sems.at[k - 1], recv_sem=recv_sems.at[k - 1],
                device_id=(px, py, pc), device_id_type=MESH))
        for cp in copies:
            cp.start()
        for cp in copies:
            cp.wait()
        own.wait()

    return pl.pallas_call(
        body, name="exchange", out_shape=jax.ShapeDtypeStruct(g.shape, g.dtype),
        in_specs=[ANY], out_specs=ANY,
        scratch_shapes=[pltpu.SemaphoreType.DMA((7,)), pltpu.SemaphoreType.DMA((7,)), pltpu.SemaphoreType.DMA],
    )(g)


def _sum_blocks(land, tr):
    p, r, l = land.shape

    def body(l_ref, o_ref):
        acc = l_ref[0]
        for i in range(1, p):
            acc = acc + l_ref[i]
        o_ref[...] = acc

    return pl.pallas_call(
        body, name="sum_blocks", grid=(r // tr,),
        in_specs=[pl.BlockSpec((p, tr, l), lambda i: (0, i, 0))],
        out_specs=pl.BlockSpec((tr, l), lambda i: (i, 0)),
        out_shape=jax.ShapeDtypeStruct((r, l), land.dtype),
        compiler_params=_params("parallel"),
    )(land)


def _tile(n, pref, unit=LANES):
    if n <= pref:
        return n
    best = None
    for cand in range(unit, pref + 1, unit):
        if n % cand == 0:
            best = cand
    assert best is not None, (n, pref, unit)
    return best


def _split_heads(t2, b, s, n):
    h = t2.shape[1] // (n * HEAD_DIM)
    return t2.reshape(b, s, n, h, HEAD_DIM).transpose(2, 0, 3, 1, 4).reshape(n, b * h, s, HEAD_DIM)


def _merge_heads(o, b):
    n, bh, s, dh = o.shape
    h = bh // b
    return o.reshape(n, b, h, s, dh).transpose(1, 3, 0, 2, 4).reshape(b * s, n * h * dh)


def _layer_step(x, target, w, attn_blk):
    b, s, d = x.shape
    t = b * s
    e = w["wp_sb"].shape[0]
    h = e // HEAD_DIM
    f = w["w_down"].shape[0]
    x2 = x.reshape(t, d)
    tg = target.reshape(t, d)
    tm = _tile(t, 512, SUBLANES)
    tmc = _tile(s, 256, SUBLANES)
    tkt = _tile(t, 512, SUBLANES)
    td = _tile(d, 1024)
    tf = _tile(f, 1408)
    t2f = _tile(2 * f, 1408)
    tqkv = _tile(6 * e, 1024)
    tg2 = _tile(2 * d, 1024)

    qkv = _matmul(x2, w["w_qkv"], NN, tm=tm, tn=tqkv, tk=d, out_dtype=BF16, name="in_qkv", bias=w["b_qkv"])
    gate = _matmul(x2, w["w_g"], NN, tm=tm, tn=tg2, tk=d, out_dtype=F32, name="in_gate", bias=w["b_g"])
    fl = _matmul(x2, w["w_f"], NN, tm=tm, tn=LANES, tk=d, out_dtype=F32, name="in_forget", bias=w["b_f"])
    heads = _split_heads(qkv, b, s, 6)
    q_sb, k_sb, v_sb, q_fx, k_fx, v_fx = (heads[i] for i in range(6))
    nr = s // LANES
    f2 = fl.reshape(b, s, LANES)[:, :, :h].transpose(0, 2, 1).reshape(b * h * nr, LANES)
    c2 = _scan_rows(f2, nr, "fwd")
    c = c2.reshape(b * h, s, 1)
    ct = c2.reshape(b * h, 1, s)
    o_sb, tot = _sb_fwd(q_sb, k_sb, v_sb, attn_blk)
    o_fx, lse = _fox_fwd(q_fx, k_fx, v_fx, c, ct, attn_blk)
    o_sb_m = _merge_heads(o_sb[None], b)
    o_fx_m = _merge_heads(o_fx[None], b)
    merged, y_sb, y_fx = _proj_gate_fwd(o_sb_m, o_fx_m, w["wp_sb"], w["wp_fx"], gate, tm)
    x1, xh1, rs1 = _mm_res_ln(merged, w["w_out"], x2, w["ln1_g"], w["ln1_b"], tm, "out_ln1")
    u = _matmul(x1, w["w_up"], NN, tm=tm, tn=t2f, tk=d, out_dtype=F32, name="ffn_up")
    act = _conv_glu_fwd(u, w["w_conv"], w["b_conv"], s, tmc)
    xo, xh2, rs2 = _mm_res_ln(act, w["w_down"], x1, w["ln2_g"], w["ln2_b"], tm, "down_ln2")

    gr = {}
    dr2, dg2, db2, ls = _loss_ln_bwd(xo, xh2, rs2, w["ln2_g"], tg, tm)
    gr["ln2_g"], gr["ln2_b"] = dg2[0:1], db2[0:1]
    da = _matmul(dr2, w["w_down"], NT, tm=tm, tn=tf, tk=d, out_dtype=F32, name="d_act")
    gr["w_down"] = _matmul(act, dr2, TN, tm=tf, tn=td, tk=tkt, out_dtype=F32, name="dw_down")
    dc, du_v = _conv_glu_bwd1(u, da, w["w_conv"], w["b_conv"], s, tmc)
    du_g, gw0, gw1, gw2, gbc = _conv_glu_bwd2(u, dc, w["w_conv"], s, tmc)
    gr["w_conv"] = jnp.concatenate([gw0[0:1], gw1[0:1], gw2[0:1]], axis=0)
    gr["b_conv"] = gbc[0:1]
    du = jnp.concatenate([du_g, du_v], axis=1)
    dlin1 = _matmul(du, w["w_up"], NT, tm=tm, tn=td, tk=t2f, out_dtype=F32, name="d_x1")
    gr["w_up"] = _matmul(x1, du, TN, tm=td, tn=t2f, tk=tkt, out_dtype=F32, name="dw_up")
    dr1, dg1, db1 = _ln_bwd(dr2, dlin1, xh1, rs1, w["ln1_g"], tm)
    gr["ln1_g"], gr["ln1_b"] = dg1[0:1], db1[0:1]
    dmg = _matmul(dr1, w["w_out"], NT, tm=tm, tn=td, tk=td, out_dtype=F32, name="d_merged")
    gr["w_out"] = _matmul(merged, dr1, TN, tm=td, tn=td, tk=tkt, out_dtype=F32, name="dw_out")
    dy_sb, dy_fx, dgate = _gate_bwd(dmg, y_sb, y_fx, gate, tm)
    do_sb = _matmul(dy_sb, w["wp_sb"], NT, tm=tm, tn=e, tk=td, out_dtype=BF16, name="d_o_sb")
    do_fx = _matmul(dy_fx, w["wp_fx"], NT, tm=tm, tn=e, tk=td, out_dtype=BF16, name="d_o_fx")
    gr["wp_sb"] = _matmul(o_sb_m, dy_sb, TN, tm=e, tn=td, tk=tkt, out_dtype=F32, name="dwp_sb")
    gr["wp_fx"] = _matmul(o_fx_m, dy_fx, TN, tm=e, tn=td, tk=tkt, out_dtype=F32, name="dwp_fx")
    dq_sb, dk_sb, dv_sb = _sb_bwd(q_sb, k_sb, v_sb, _split_heads(do_sb, b, s, 1)[0], tot, attn_blk)
    dq_fx, dk_fx, dv_fx, dct = _fox_bwd(q_fx, k_fx, v_fx, c, ct, _split_heads(do_fx, b, s, 1)[0], o_fx, lse, attn_blk)
    dqkv = _merge_heads(jnp.stack([dq_sb, dk_sb, dv_sb, dq_fx, dk_fx, dv_fx]), b).astype(BF16)
    df2 = _scan_rows(f2, nr, "bwd", dct.reshape(b * h * nr, LANES))
    df = jnp.pad(df2.reshape(b, h, s).transpose(0, 2, 1).reshape(t, h), ((0, 0), (0, LANES - h))).astype(BF16)
    dx = _matmul(dqkv, w["w_qkv"], NT, tm=tm, tn=td, tk=tqkv, out_dtype=F32, name="dx_qkv", res=dr1, res_scale=ALPHA)
    dx = _matmul(dgate, w["w_g"], NT, tm=tm, tn=td, tk=tg2, out_dtype=F32, name="dx_gate", res=dx)
    dx = _matmul(df, w["w_f"], NT, tm=tm, tn=td, tk=LANES, out_dtype=F32, name="dx_forget", res=dx)
    gr["w_qkv"] = _matmul(x2, dqkv, TN, tm=td, tn=tqkv, tk=tkt, out_dtype=F32, name="dw_qkv")
    gr["w_g"] = _matmul(x2, dgate, TN, tm=td, tn=tg2, tk=tkt, out_dtype=F32, name="dw_gate")
    gr["w_f"] = _matmul(x2, df, TN, tm=td, tn=LANES, tk=tkt, out_dtype=F32, name="dw_forget")
    gr["b_qkv"] = _colsum(dqkv, tm, "db_qkv")[0:1]
    gr["b_g"] = _colsum(dgate, tm, "db_gate")[0:1]
    gr["b_f"] = _colsum(df, tm, "db_forget")[0:1]
    return ls[0:1, 0:1], dx.reshape(b, s, d), gr


SHARDED = ("w_in", "w_proj_sb", "w_proj_fox", "w_out", "w_up", "w_conv", "w_down")
ROW_SHARDED = ("w_out", "w_down")
REPLICATED = ("b_in", "ln1_g", "ln1_b", "b_conv", "ln2_g", "ln2_b")
WEIGHTS = ("w_in", "b_in", "w_proj_sb", "w_proj_fox", "w_out", "ln1_g", "ln1_b", "w_up", "w_conv", "b_conv",
           "w_down", "ln2_g", "ln2_b")
PACK_UNIT = 64 * LANES


def _round_up(n, unit):
    return -(-n // unit) * unit


def _pack(pieces):
    cols = [jnp.pad(a, ((0, 0), (0, _round_up(a.shape[1], PACK_UNIT) - a.shape[1]))) for a in pieces]
    flat = jnp.concatenate(cols, axis=1)
    return flat.reshape(flat.shape[0], flat.shape[1] // LANES, LANES)


def _unpack(buf, sizes):
    flat = buf.reshape(buf.shape[0], -1)
    out, off = [], 0
    for n in sizes:
        out.append(flat[:, off:off + n])
        off += _round_up(n, PACK_UNIT)
    return out


def _to_blocks(full, name):
    r, c = full.shape
    if name in ROW_SHARDED:
        return full.reshape(N_DEV, (r // N_DEV) * c)
    return full.reshape(r, N_DEV, c // N_DEV).transpose(1, 0, 2).reshape(N_DEV, r * (c // N_DEV))


def _from_blocks(blocks, name, shard_shape):
    r, c = shard_shape
    if name in ROW_SHARDED:
        return blocks.reshape(N_DEV * r, c)
    return blocks.reshape(N_DEV, r, c).transpose(1, 0, 2).reshape(r, N_DEV * c)


def kernel(x, w_in, b_in, w_proj_sb, w_proj_fox, w_out, ln1_g, ln1_b, w_up, w_conv, b_conv, w_down, ln2_g, ln2_b, loss_target, m_w_in, m_b_in, m_w_proj_sb, m_w_proj_fox, m_w_out, m_ln1_g, m_ln1_b, m_w_up, m_w_conv, m_b_conv, m_w_down, m_ln2_g, m_ln2_b, v_w_in, v_b_in, v_w_proj_sb, v_w_proj_fox, v_w_out, v_ln1_g, v_ln1_b, v_w_up, v_w_conv, v_b_conv, v_w_down, v_ln2_g, v_ln2_b):
    wts = dict(w_in=w_in, b_in=b_in, w_proj_sb=w_proj_sb, w_proj_fox=w_proj_fox, w_out=w_out, ln1_g=ln1_g, ln1_b=ln1_b,
               w_up=w_up, w_conv=w_conv, b_conv=b_conv, w_down=w_down, ln2_g=ln2_g, ln2_b=ln2_b)
    mom = dict(w_in=m_w_in, b_in=m_b_in, w_proj_sb=m_w_proj_sb, w_proj_fox=m_w_proj_fox, w_out=m_w_out, ln1_g=m_ln1_g,
               ln1_b=m_ln1_b, w_up=m_w_up, w_conv=m_w_conv, b_conv=m_b_conv, w_down=m_w_down, ln2_g=m_ln2_g, ln2_b=m_ln2_b)
    var = dict(w_in=v_w_in, b_in=v_b_in, w_proj_sb=v_w_proj_sb, w_proj_fox=v_w_proj_fox, w_out=v_w_out, ln1_g=v_ln1_g,
               ln1_b=v_ln1_b, w_up=v_w_up, w_conv=v_w_conv, b_conv=v_b_conv, w_down=v_w_down, ln2_g=v_ln2_g, ln2_b=v_ln2_b)
    shard = {n: wts[n].reshape(wts[n].shape[-2:]) for n in WEIGHTS}
    shard_shapes = {n: shard[n].shape for n in SHARDED}
    sizes = [shard[n].size for n in SHARDED]

    gathered = _all_gather(_pack([shard[n].reshape(1, -1) for n in SHARDED])[0])
    full = {n: _from_blocks(p, n, shard_shapes[n]) for n, p in zip(SHARDED, _unpack(gathered, sizes))}
    e = full["w_proj_sb"].shape[0]
    h = e // HEAD_DIM
    d = full["w_out"].shape[0]
    nq = 6 * e

    def cut_in(a, pad):
        fcols = a[:, nq:nq + h]
        if pad:
            fcols = jnp.pad(fcols, ((0, 0), (0, LANES - h)))
        return a[:, :nq], a[:, nq + h:], fcols

    w_qkv, w_g, w_f = cut_in(full["w_in"], True)
    b_qkv, b_g, b_f = cut_in(shard["b_in"], True)
    w = dict(w_qkv=w_qkv.astype(BF16), w_g=w_g.astype(BF16), w_f=w_f.astype(BF16), b_qkv=b_qkv, b_g=b_g, b_f=b_f,
             wp_sb=full["w_proj_sb"].astype(BF16), wp_fx=full["w_proj_fox"].astype(BF16),
             w_out=full["w_out"].astype(BF16), w_up=full["w_up"].astype(BF16), w_down=full["w_down"].astype(BF16),
             w_conv=full["w_conv"], b_conv=shard["b_conv"], ln1_g=shard["ln1_g"], ln1_b=shard["ln1_b"],
             ln2_g=shard["ln2_g"], ln2_b=shard["ln2_b"])

    loss_local, grad_x, gr = _layer_step(x, loss_target, w, min(256, x.shape[1]))
    loss = lax.psum(loss_local[0, 0], ("x", "y", "c"))

    local = dict(
        w_in=jnp.concatenate([gr["w_qkv"], gr["w_f"][:, :h], gr["w_g"]], axis=1),
        b_in=jnp.concatenate([gr["b_qkv"], gr["b_f"][:, :h], gr["b_g"]], axis=1),
        w_proj_sb=gr["wp_sb"], w_proj_fox=gr["wp_fx"], w_out=gr["w_out"], w_up=gr["w_up"], w_conv=gr["w_conv"],
        w_down=gr["w_down"], ln1_g=gr["ln1_g"], ln1_b=gr["ln1_b"], b_conv=gr["b_conv"], ln2_g=gr["ln2_g"],
        ln2_b=gr["ln2_b"])

    landed = _exchange(_pack([_to_blocks(local[n], n) for n in SHARDED]))
    summed = _sum_blocks(landed, _tile(landed.shape[1], 1024, 64))
    gsum = {n: p.reshape((1,) + shard_shapes[n]) for n, p in zip(SHARDED, _unpack(summed[None], sizes))}

    rsizes = [shard[n].size for n in REPLICATED]
    parts = _all_gather(_pack([local[n] for n in REPLICATED])[0])
    for n, p in zip(REPLICATED, _unpack(parts, rsizes)):
        gsum[n] = p.reshape(N_DEV, 1, -1)

    grads, deltas, new_m, new_v = [], [], [], []
    for n in WEIGHTS:
        shp = wts[n].shape
        g, dl, mn, vn = _adamw(shard[n], gsum[n], mom[n].reshape(shard[n].shape), var[n].reshape(shard[n].shape),
                               "adamw_" + n)
        grads.append(g.reshape(shp))
        deltas.append(dl.reshape(shp))
        new_m.append(mn.reshape(shp))
        new_v.append(vn.reshape(shp))
    return (loss, grad_x, *grads, *deltas, *new_m, *new_v)
```

```python
import functools
import math

import jax
import jax.numpy as jnp
from jax import lax
from jax.experimental import pallas as pl
from jax.experimental.pallas import tpu as pltpu

F32 = jnp.float32
BF16 = jnp.bfloat16

HEAD_DIM = 64
LN_EPS = 1e-5
DEPTH = 1
ALPHA = (2.0 * DEPTH) ** 0.25
ADAM_LR, ADAM_B1, ADAM_B2, ADAM_EPS, ADAM_WD, ADAM_STEP = 0.001, 0.9, 0.999, 1e-08, 0.01, 10
N_DEV = 8
LANES = 128
SUBLANES = 8
VMEM_LIMIT = 56 * 1024 * 1024

NN = ((1,), (0,))
NT = ((1,), (1,))
TN = ((0,), (0,))


def _dot(a, b, dims):
    return lax.dot_general(a, b, (dims, ((), ())), preferred_element_type=F32)


def _split_dot(x, t):
    hi = x.astype(BF16)
    lo = (x - hi.astype(F32)).astype(BF16)
    return _dot(hi, t, NN) + _dot(lo, t, NN)


def _params(*sem):
    return pltpu.CompilerParams(dimension_semantics=sem, vmem_limit_bytes=VMEM_LIMIT)


def _iotas(blk):
    row = lax.broadcasted_iota(jnp.int32, (blk, blk), 0)
    col = lax.broadcasted_iota(jnp.int32, (blk, blk), 1)
    return row, col


def _sb_terms(z):
    e = jnp.exp(-jnp.abs(z))
    lb = jnp.minimum(z, 0.0) - jnp.log(1.0 + e)
    return lb, lb - z, e


def _head_specs(hp, blk, s, dh):
    qspec = pl.BlockSpec((hp, blk, dh), lambda b, i: (b, i, 0))
    full = pl.BlockSpec((hp, s, dh), lambda b, i: (b, 0, 0))
    rows = pl.BlockSpec((hp, blk, 1), lambda b, i: (b, i, 0))
    krow = pl.BlockSpec((hp, 1, s), lambda b, i: (b, 0, 0))
    return qspec, full, rows, krow


def _sb_fwd(q, k, v, blk, hp):
    bh, s, dh = q.shape
    scale = dh ** -0.5

    def body(q_ref, k_ref, v_ref, o_ref, tot_ref):
        qi = pl.program_id(1)
        qb = [(q_ref[h] * scale).astype(BF16) for h in range(hp)]
        row, col = _iotas(blk)
        strict = col < row
        after = (row > col).astype(BF16)

        def block(j, carry, diag):
            off = pl.multiple_of(j * blk, blk)
            out = []
            for h in range(hp):
                o_acc, run = carry[h]
                kj = k_ref[h, pl.ds(off, blk), :]
                vj = v_ref[h, pl.ds(off, blk), :]
                lb, lom, _ = _sb_terms(_dot(qb[h], kj, NT))
                if diag:
                    lom = jnp.where(strict, lom, 0.0)
                sfx = _split_dot(lom, after)
                a = jnp.exp(lb + sfx + run)
                if diag:
                    a = jnp.where(strict, a, 0.0)
                out.append((o_acc + _dot(a.astype(BF16), vj, NN), run + sfx[:, 0:1] + lom[:, 0:1]))
            return tuple(out)

        zero = (jnp.zeros((blk, dh), F32), jnp.zeros((blk, 1), F32))
        carry = block(qi, (zero,) * hp, True)
        carry = lax.fori_loop(0, qi, lambda it, c: block(qi - 1 - it, c, False), carry)
        for h in range(hp):
            o_ref[h] = carry[h][0]
            tot_ref[h] = carry[h][1]

    qspec, full, rows, _ = _head_specs(hp, blk, s, dh)
    return pl.pallas_call(
        body, name="sb_fwd", grid=(bh // hp, s // blk),
        in_specs=[qspec, full, full], out_specs=[qspec, rows],
        out_shape=[jax.ShapeDtypeStruct((bh, s, dh), F32), jax.ShapeDtypeStruct((bh, s, 1), F32)],
        compiler_params=_params("parallel", "arbitrary"),
    )(q, k, v)


def _sb_bwd(q, k, v, do, tot, blk, hp):
    bh, s, dh = q.shape
    scale = dh ** -0.5

    def body(q_ref, k_ref, v_ref, do_ref, tot_ref, dq_ref, dk_ref, dv_ref):
        qi = pl.program_id(1)

        @pl.when(qi == 0)
        def _():
            dk_ref[...] = jnp.zeros_like(dk_ref)
            dv_ref[...] = jnp.zeros_like(dv_ref)

        qb = [(q_ref[h] * scale).astype(BF16) for h in range(hp)]
        dob = [do_ref[h].astype(BF16) for h in range(hp)]
        tot_t = [tot_ref[h] for h in range(hp)]
        row, col = _iotas(blk)
        strict = col < row
        upto = (row <= col).astype(BF16)
        before = (row < col).astype(BF16)

        def block(j, carry, diag):
            off = pl.multiple_of(j * blk, blk)
            out = []
            for h in range(hp):
                dq_acc, cl, cg = carry[h]
                kj = k_ref[h, pl.ds(off, blk), :]
                vj = v_ref[h, pl.ds(off, blk), :]
                z = _dot(qb[h], kj, NT)
                lb, lom, e = _sb_terms(z)
                if diag:
                    lom = jnp.where(strict, lom, 0.0)
                pre = _split_dot(lom, upto)
                a = jnp.exp(lb + (tot_t[h] - cl - pre))
                if diag:
                    a = jnp.where(strict, a, 0.0)
                g = a * _dot(dob[h], vj, NT)
                pw = _split_dot(g, before)
                p = cg + pw
                r = 1.0 / (1.0 + e)
                er = e * r
                pos = z >= 0.0
                dz = g * jnp.where(pos, er, r) - p * jnp.where(pos, r, er)
                if diag:
                    dz = jnp.where(strict, dz, 0.0)
                dzb = dz.astype(BF16)
                dk_ref[h, pl.ds(off, blk), :] += _dot(dzb, qb[h], TN)
                dv_ref[h, pl.ds(off, blk), :] += _dot(a.astype(BF16), dob[h], TN)
                out.append((dq_acc + _dot(dzb, kj, NN), cl + pre[:, blk - 1:blk],
                            cg + pw[:, blk - 1:blk] + g[:, blk - 1:blk]))
            return tuple(out)

        zero = (jnp.zeros((blk, dh), F32), jnp.zeros((blk, 1), F32), jnp.zeros((blk, 1), F32))
        carry = lax.fori_loop(0, qi, lambda j, c: block(j, c, False), (zero,) * hp)
        carry = block(qi, carry, True)
        for h in range(hp):
            dq_ref[h] = carry[h][0] * scale

    qspec, full, rows, _ = _head_specs(hp, blk, s, dh)
    return pl.pallas_call(
        body, name="sb_bwd", grid=(bh // hp, s // blk),
        in_specs=[qspec, full, full, qspec, rows], out_specs=[qspec, full, full],
        out_shape=[jax.ShapeDtypeStruct((bh, s, dh), F32)] * 3,
        compiler_params=_params("parallel", "arbitrary"),
    )(q, k, v, do, tot)


NEG = -1e30


def _fox_fwd(q, k, v, c, ct, blk, hp):
    bh, s, dh = q.shape
    scale = dh ** -0.5

    def body(q_ref, k_ref, v_ref, c_ref, ct_ref, o_ref, lse_ref):
        qi = pl.program_id(1)
        qb = [(q_ref[h] * scale).astype(BF16) for h in range(hp)]
        cq = [c_ref[h] for h in range(hp)]
        row, col = _iotas(blk)
        causal = col <= row

        def block(j, carry, diag):
            off = pl.multiple_of(j * blk, blk)
            out = []
            for h in range(hp):
                m, l, acc = carry[h]
                kj = k_ref[h, pl.ds(off, blk), :]
                vj = v_ref[h, pl.ds(off, blk), :]
                z = _dot(qb[h], kj, NT) + (cq[h] - ct_ref[h, :, pl.ds(off, blk)])
                if diag:
                    z = jnp.where(causal, z, NEG)
                m_new = jnp.maximum(m, jnp.max(z, axis=1, keepdims=True))
                w = jnp.exp(m - m_new)
                p = jnp.exp(z - m_new)
                out.append((m_new, w * l + jnp.sum(p, axis=1, keepdims=True),
                            w * acc + _dot(p.astype(BF16), vj, NN)))
            return tuple(out)

        zero = (jnp.full((blk, 1), NEG, F32), jnp.zeros((blk, 1), F32), jnp.zeros((blk, dh), F32))
        carry = block(qi, (zero,) * hp, True)
        carry = lax.fori_loop(0, qi, lambda j, c_: block(j, c_, False), carry)
        for h in range(hp):
            m, l, acc = carry[h]
            o_ref[h] = acc / l
            lse_ref[h] = m + jnp.log(l)

    qspec, full, rows, krow = _head_specs(hp, blk, s, dh)
    return pl.pallas_call(
        body, name="fox_fwd", grid=(bh // hp, s // blk),
        in_specs=[qspec, full, full, rows, krow], out_specs=[qspec, rows],
        out_shape=[jax.ShapeDtypeStruct((bh, s, dh), F32), jax.ShapeDtypeStruct((bh, s, 1), F32)],
        compiler_params=_params("parallel", "arbitrary"),
    )(q, k, v, c, ct)


def _fox_bwd(q, k, v, c, ct, do, o, lse, blk, hp):
    bh, s, dh = q.shape
    scale = dh ** -0.5

    def body(q_ref, k_ref, v_ref, c_ref, ct_ref, do_ref, o_ref, lse_ref, dq_ref, dk_ref, dv_ref, dct_ref):
        qi = pl.program_id(1)

        @pl.when(qi == 0)
        def _():
            dk_ref[...] = jnp.zeros_like(dk_ref)
            dv_ref[...] = jnp.zeros_like(dv_ref)
            dct_ref[...] = jnp.zeros_like(dct_ref)

        qb = [(q_ref[h] * scale).astype(BF16) for h in range(hp)]
        dob = [do_ref[h].astype(BF16) for h in range(hp)]
        delta = [jnp.sum(dob[h].astype(F32) * o_ref[h], axis=1, keepdims=True) for h in range(hp)]
        cq = [c_ref[h] for h in range(hp)]
        lse_t = [lse_ref[h] for h in range(hp)]
        row, col = _iotas(blk)
        causal = col <= row

        def block(j, carry, diag):
            off = pl.multiple_of(j * blk, blk)
            out = []
            for h in range(hp):
                kj = k_ref[h, pl.ds(off, blk), :]
                vj = v_ref[h, pl.ds(off, blk), :]
                z = _dot(qb[h], kj, NT) + (cq[h] - ct_ref[h, :, pl.ds(off, blk)])
                p = jnp.exp(z - lse_t[h])
                if diag:
                    p = jnp.where(causal, p, 0.0)
                ds = p * (_dot(dob[h], vj, NT) - delta[h])
                dsb = ds.astype(BF16)
                dk_ref[h, pl.ds(off, blk), :] += _dot(dsb, qb[h], TN)
                dv_ref[h, pl.ds(off, blk), :] += _dot(p.astype(BF16), dob[h], TN)
                dct_ref[h, :, pl.ds(off, blk)] -= jnp.sum(ds, axis=0, keepdims=True)
                out.append(carry[h] + _dot(dsb, kj, NN))
            return tuple(out)

        carry = lax.fori_loop(0, qi, lambda j, a: block(j, a, False), (jnp.zeros((blk, dh), F32),) * hp)
        carry = block(qi, carry, True)
        for h in range(hp):
            dq_ref[h] = carry[h] * scale

    qspec, full, rows, krow = _head_specs(hp, blk, s, dh)
    return pl.pallas_call(
        body, name="fox_bwd", grid=(bh // hp, s // blk),
        in_specs=[qspec, full, full, rows, krow, qspec, qspec, rows], out_specs=[qspec, full, full, krow],
        out_shape=[jax.ShapeDtypeStruct((bh, s, dh), F32)] * 3 + [jax.ShapeDtypeStruct((bh, 1, s), F32)],
        compiler_params=_params("parallel", "arbitrary"),
    )(q, k, v, c, ct, do, o, lse)


def _scan_rows(f2, group, mode, d2=None):
    n = f2.shape[0]

    def body(*refs):
        f_ref, o_ref = refs[0], refs[-1]
        f = f_ref[...]
        row, col = _iotas(LANES)
        grow = lax.broadcasted_iota(jnp.int32, (n, n), 0)
        gcol = lax.broadcasted_iota(jnp.int32, (n, n), 1)
        same = (grow // group) == (gcol // group)
        e = jnp.exp(-jnp.abs(f))
        if mode == "fwd":
            x = jnp.minimum(f, 0.0) - jnp.log1p(e)
            within = (row <= col).astype(F32)
            earlier = (same & (gcol < grow)).astype(F32)
        else:
            x = refs[1][...]
            within = (row >= col).astype(F32)
            earlier = (same & (gcol > grow)).astype(F32)
        y = jnp.dot(x, within, preferred_element_type=F32, precision=lax.Precision.HIGHEST)
        tot = jnp.sum(x, axis=1, keepdims=True)
        y = y + jnp.dot(earlier, tot, preferred_element_type=F32, precision=lax.Precision.HIGHEST)
        if mode == "bwd":
            r = 1.0 / (1.0 + e)
            y = y * jnp.where(f >= 0.0, e * r, r)
        o_ref[...] = y

    args = (f2,) if mode == "fwd" else (f2, d2)
    return pl.pallas_call(body, name="logf_" + mode, out_shape=jax.ShapeDtypeStruct(f2.shape, F32),
                          compiler_params=_params())(*args)


def _matmul(a, b, dims, *, tm, tn, tk, out_dtype, name, bias=None, res=None, res_scale=1.0):
    if dims == NN:
        (m, kk), n = a.shape, b.shape[1]
        a_spec = pl.BlockSpec((tm, tk), lambda i, j, k: (i, k))
        b_spec = pl.BlockSpec((tk, tn), lambda i, j, k: (k, j))
    elif dims == NT:
        (m, kk), n = a.shape, b.shape[0]
        a_spec = pl.BlockSpec((tm, tk), lambda i, j, k: (i, k))
        b_spec = pl.BlockSpec((tn, tk), lambda i, j, k: (j, k))
    else:
        (kk, m), n = a.shape, b.shape[1]
        a_spec = pl.BlockSpec((tk, tm), lambda i, j, k: (k, i))
        b_spec = pl.BlockSpec((tk, tn), lambda i, j, k: (k, j))
    assert m % tm == 0 and n % tn == 0 and kk % tk == 0, (name, m, n, kk, tm, tn, tk)
    nk = kk // tk
    extras, extra_specs = [], []
    if bias is not None:
        extras.append(bias)
        extra_specs.append(pl.BlockSpec((1, tn), lambda i, j, k: (0, j)))
    if res is not None:
        extras.append(res)
        extra_specs.append(pl.BlockSpec((tm, tn), lambda i, j, k: (i, j)))

    def body(a_ref, b_ref, *rest):
        o_ref, acc_ref = rest[-2], rest[-1]
        k = pl.program_id(2)
        part = _dot(a_ref[...].astype(BF16), b_ref[...].astype(BF16), dims)

        @pl.when(k == 0)
        def _():
            acc_ref[...] = part

        @pl.when(k > 0)
        def _():
            acc_ref[...] += part

        @pl.when(k == nk - 1)
        def _():
            out = acc_ref[...]
            idx = 0
            if bias is not None:
                out = out + rest[idx][...]
                idx += 1
            if res is not None:
                out = out + res_scale * rest[idx][...]
            o_ref[...] = out.astype(o_ref.dtype)

    return pl.pallas_call(
        body, name=name, grid=(m // tm, n // tn, nk),
        in_specs=[a_spec, b_spec] + extra_specs,
        out_specs=pl.BlockSpec((tm, tn), lambda i, j, k: (i, j)),
        out_shape=jax.ShapeDtypeStruct((m, n), out_dtype),
        scratch_shapes=[pltpu.VMEM((tm, tn), F32)],
        compiler_params=_params("parallel", "parallel", "arbitrary"),
    )(a, b, *extras)


def _sigmoid(x):
    e = jnp.exp(-jnp.abs(x))
    r = 1.0 / (1.0 + e)
    return jnp.where(x >= 0.0, r, e * r)


def _proj_gate_fwd(o_sb, o_fx, wp_sb, wp_fx, g, tm):
    t, e = o_sb.shape
    d = wp_sb.shape[1]

    def body(osb_ref, ofx_ref, wsb_ref, wfx_ref, gsb_ref, gfx_ref, mg_ref, ysb_ref, yfx_ref):
        ysb = _dot(osb_ref[...].astype(BF16), wsb_ref[...], NN)
        yfx = _dot(ofx_ref[...].astype(BF16), wfx_ref[...], NN)
        ysb_ref[...] = ysb
        yfx_ref[...] = yfx
        mg_ref[...] = (_sigmoid(gsb_ref[...]) * ysb + _sigmoid(gfx_ref[...]) * yfx).astype(BF16)

    rows_e = pl.BlockSpec((tm, e), lambda i: (i, 0))
    rows_d = pl.BlockSpec((tm, d), lambda i: (i, 0))
    w_spec = pl.BlockSpec((e, d), lambda i: (0, 0))
    return pl.pallas_call(
        body, name="proj_gate_fwd", grid=(t // tm,),
        in_specs=[rows_e, rows_e, w_spec, w_spec, rows_d, pl.BlockSpec((tm, d), lambda i: (i, 1))],
        out_specs=[rows_d, rows_d, rows_d],
        out_shape=[jax.ShapeDtypeStruct((t, d), BF16), jax.ShapeDtypeStruct((t, d), F32), jax.ShapeDtypeStruct((t, d), F32)],
        compiler_params=_params("parallel"),
    )(o_sb, o_fx, wp_sb, wp_fx, g, g)


def _gate_bwd(dmg, y_sb, y_fx, g, tm):
    t, d = dmg.shape

    def body(dm_ref, ysb_ref, yfx_ref, gsb_ref, gfx_ref, dysb_ref, dyfx_ref, dg_ref):
        dm = dm_ref[...]
        ssb = _sigmoid(gsb_ref[...])
        sfx = _sigmoid(gfx_ref[...])
        dysb_ref[...] = (dm * ssb).astype(BF16)
        dyfx_ref[...] = (dm * sfx).astype(BF16)
        dg_ref[:, 0:d] = (dm * ysb_ref[...] * ssb * (1.0 - ssb)).astype(BF16)
        dg_ref[:, d:2 * d] = (dm * yfx_ref[...] * sfx * (1.0 - sfx)).astype(BF16)

    rows = pl.BlockSpec((tm, d), lambda i: (i, 0))
    rows1 = pl.BlockSpec((tm, d), lambda i: (i, 1))
    return pl.pallas_call(
        body, name="gate_bwd", grid=(t // tm,),
        in_specs=[rows, rows, rows, rows, rows1],
        out_specs=[rows, rows, pl.BlockSpec((tm, 2 * d), lambda i: (i, 0))],
        out_shape=[jax.ShapeDtypeStruct((t, d), BF16)] * 2 + [jax.ShapeDtypeStruct((t, 2 * d), BF16)],
        compiler_params=_params("parallel"),
    )(dmg, y_sb, y_fx, g, g)


def _mm_res_ln(a, w, xres, gamma, beta, tm, name):
    t, kk = a.shape
    d = w.shape[1]

    def body(a_ref, w_ref, x_ref, g_ref, b_ref, xn_ref, xh_ref, rs_ref):
        r = ALPHA * x_ref[...] + _dot(a_ref[...].astype(BF16), w_ref[...], NN)
        mean = jnp.mean(r, axis=1, keepdims=True)
        cen = r - mean
        rstd = lax.rsqrt(jnp.mean(cen * cen, axis=1, keepdims=True) + LN_EPS)
        xh = cen * rstd
        xh_ref[...] = xh
        xn_ref[...] = xh * g_ref[...] + b_ref[...]
        rs_ref[...] = rstd

    rows_d = pl.BlockSpec((tm, d), lambda i: (i, 0))
    vec = pl.BlockSpec((1, d), lambda i: (0, 0))
    return pl.pallas_call(
        body, name=name, grid=(t // tm,),
        in_specs=[pl.BlockSpec((tm, kk), lambda i: (i, 0)), pl.BlockSpec((kk, d), lambda i: (0, 0)), rows_d, vec, vec],
        out_specs=[rows_d, rows_d, pl.BlockSpec((tm, 1), lambda i: (i, 0))],
        out_shape=[jax.ShapeDtypeStruct((t, d), F32), jax.ShapeDtypeStruct((t, d), F32), jax.ShapeDtypeStruct((t, 1), F32)],
        compiler_params=_params("parallel"),
    )(a, w, xres, gamma, beta)


def _ln_bwd_math(dy, xh, rstd, gamma):
    dxh = dy * gamma
    m1 = jnp.mean(dxh, axis=1, keepdims=True)
    m2 = jnp.mean(dxh * xh, axis=1, keepdims=True)
    return rstd * (dxh - m1 - xh * m2)


def _rowsum8(x):
    tm, n = x.shape
    return jnp.sum(x.reshape(tm // SUBLANES, SUBLANES, n), axis=0)


def _fold8(ref):
    ref[0:1, :] = jnp.sum(ref[...], axis=0, keepdims=True)


def _loss_ln_bwd(x2, xh, rstd, gamma, target, tm):
    t, d = x2.shape

    def body(x_ref, xh_ref, rs_ref, g_ref, tg_ref, dr_ref, dg_ref, db_ref, ls_ref):
        @pl.when(pl.program_id(0) == 0)
        def _():
            dg_ref[...] = jnp.zeros_like(dg_ref)
            db_ref[...] = jnp.zeros_like(db_ref)
            ls_ref[...] = jnp.zeros_like(ls_ref)

        err = x_ref[...] - tg_ref[...]
        xh = xh_ref[...]
        dy = err * (1.0 / d)
        dr_ref[...] = _ln_bwd_math(dy, xh, rs_ref[...], g_ref[...])
        dg_ref[...] += _rowsum8(dy * xh)
        db_ref[...] += _rowsum8(dy)
        sq = _rowsum8(err * err)
        part = sq[:, 0:LANES]
        for j in range(1, d // LANES):
            part = part + sq[:, j * LANES:(j + 1) * LANES]
        ls_ref[...] += part * (0.5 / d)

        @pl.when(pl.program_id(0) == t // tm - 1)
        def _():
            _fold8(dg_ref)
            _fold8(db_ref)
            ls_ref[0:1, 0:1] = jnp.sum(jnp.sum(ls_ref[...], axis=0, keepdims=True), axis=1, keepdims=True)

    rows = pl.BlockSpec((tm, d), lambda i: (i, 0))
    acc = pl.BlockSpec((SUBLANES, d), lambda i: (0, 0))
    return pl.pallas_call(
        body, name="loss_ln_bwd", grid=(t // tm,),
        in_specs=[rows, rows, pl.BlockSpec((tm, 1), lambda i: (i, 0)), pl.BlockSpec((1, d), lambda i: (0, 0)), rows],
        out_specs=[rows, acc, acc, pl.BlockSpec((SUBLANES, LANES), lambda i: (0, 0))],
        out_shape=[jax.ShapeDtypeStruct((t, d), F32), jax.ShapeDtypeStruct((SUBLANES, d), F32),
                   jax.ShapeDtypeStruct((SUBLANES, d), F32), jax.ShapeDtypeStruct((SUBLANES, LANES), F32)],
        compiler_params=_params("arbitrary"),
    )(x2, xh, rstd, gamma, target)


def _ln_bwd(dr_next, dlin, xh, rstd, gamma, tm):
    t, d = xh.shape

    def body(dn_ref, dl_ref, xh_ref, rs_ref, g_ref, dr_ref, dg_ref, db_ref):
        @pl.when(pl.program_id(0) == 0)
        def _():
            dg_ref[...] = jnp.zeros_like(dg_ref)
            db_ref[...] = jnp.zeros_like(db_ref)

        dy = ALPHA * dn_ref[...] + dl_ref[...]
        xh = xh_ref[...]
        dr_ref[...] = _ln_bwd_math(dy, xh, rs_ref[...], g_ref[...])
        dg_ref[...] += _rowsum8(dy * xh)
        db_ref[...] += _rowsum8(dy)

        @pl.when(pl.program_id(0) == t // tm - 1)
        def _():
            _fold8(dg_ref)
            _fold8(db_ref)

    rows = pl.BlockSpec((tm, d), lambda i: (i, 0))
    acc = pl.BlockSpec((SUBLANES, d), lambda i: (0, 0))
    return pl.pallas_call(
        body, name="ln_bwd", grid=(t // tm,),
        in_specs=[rows, rows, rows, pl.BlockSpec((tm, 1), lambda i: (i, 0)), pl.BlockSpec((1, d), lambda i: (0, 0))],
        out_specs=[rows, acc, acc],
        out_shape=[jax.ShapeDtypeStruct((t, d), F32), jax.ShapeDtypeStruct((SUBLANES, d), F32),
                   jax.ShapeDtypeStruct((SUBLANES, d), F32)],
        compiler_params=_params("arbitrary"),
    )(dr_next, dlin, xh, rstd, gamma)


def _shift_rows(x, halo, shift, row):
    out = pltpu.roll(x, shift, 0)
    for r in range(shift):
        out = jnp.where(row == r, halo[SUBLANES - shift + r:SUBLANES - shift + r + 1, :], out)
    return out


def _unshift_rows(x, halo, shift, row, tm):
    out = pltpu.roll(x, tm - shift, 0)
    for r in range(shift):
        out = jnp.where(row == tm - shift + r, halo[r:r + 1, :], out)
    return out


def _conv_pre(ug_ref, halo_ref, wc_ref, bc_ref, first, tm):
    ug = ug_ref[...]
    halo = jnp.where(first, 0.0, halo_ref[...])
    row = lax.broadcasted_iota(jnp.int32, ug.shape, 0)
    wc = wc_ref[...]
    um1 = _shift_rows(ug, halo, 1, row)
    um2 = _shift_rows(ug, halo, 2, row)
    c = bc_ref[...] + wc[2:3, :] * ug + wc[1:2, :] * um1 + wc[0:1, :] * um2
    return c, ug, um1, um2


INV_SQRT2 = 1.0 / math.sqrt(2.0)
INV_SQRT2PI = 1.0 / math.sqrt(2.0 * math.pi)


def _conv_glu_fwd(u, wc, bc, seq, tm):
    t, f2 = u.shape
    f = f2 // 2
    per_seq = seq // tm
    hb = tm // SUBLANES

    def body(ug_ref, halo_ref, uv_ref, wc_ref, bc_ref, a_ref):
        first = (pl.program_id(0) % per_seq) == 0
        c, _, _, _ = _conv_pre(ug_ref, halo_ref, wc_ref, bc_ref, first, tm)
        gelu = 0.5 * c * (1.0 + lax.erf(c * INV_SQRT2))
        a_ref[...] = (gelu * uv_ref[...]).astype(BF16)

    return pl.pallas_call(
        body, name="conv_glu_fwd", grid=(t // tm,),
        in_specs=[pl.BlockSpec((tm, f), lambda i: (i, 0)),
                  pl.BlockSpec((SUBLANES, f), lambda i: (jnp.maximum(i * hb - 1, 0), 0)),
                  pl.BlockSpec((tm, f), lambda i: (i, 1)),
                  pl.BlockSpec((3, f), lambda i: (0, 0)), pl.BlockSpec((1, f), lambda i: (0, 0))],
        out_specs=pl.BlockSpec((tm, f), lambda i: (i, 0)),
        out_shape=jax.ShapeDtypeStruct((t, f), BF16),
        compiler_params=_params("parallel"),
    )(u, u, u, wc, bc)


def _conv_glu_bwd1(u, da, wc, bc, seq, tm):
    t, f2 = u.shape
    f = f2 // 2
    per_seq = seq // tm
    hb = tm // SUBLANES

    def body(ug_ref, halo_ref, uv_ref, da_ref, wc_ref, bc_ref, dc_ref, duv_ref):
        first = (pl.program_id(0) % per_seq) == 0
        c, _, _, _ = _conv_pre(ug_ref, halo_ref, wc_ref, bc_ref, first, tm)
        cdf = 0.5 * (1.0 + lax.erf(c * INV_SQRT2))
        pdf = jnp.exp(-0.5 * c * c) * INV_SQRT2PI
        da = da_ref[...]
        duv_ref[...] = (da * (c * cdf)).astype(BF16)
        dc_ref[...] = da * uv_ref[...] * (cdf + c * pdf)

    rows = pl.BlockSpec((tm, f), lambda i: (i, 0))
    return pl.pallas_call(
        body, name="conv_glu_bwd1", grid=(t // tm,),
        in_specs=[rows, pl.BlockSpec((SUBLANES, f), lambda i: (jnp.maximum(i * hb - 1, 0), 0)),
                  pl.BlockSpec((tm, f), lambda i: (i, 1)), rows,
                  pl.BlockSpec((3, f), lambda i: (0, 0)), pl.BlockSpec((1, f), lambda i: (0, 0))],
        out_specs=[rows, rows],
        out_shape=[jax.ShapeDtypeStruct((t, f), F32), jax.ShapeDtypeStruct((t, f), BF16)],
        compiler_params=_params("parallel"),
    )(u, u, u, da, wc, bc)


def _conv_glu_bwd2(u, dc, wc, seq, tm):
    t, f2 = u.shape
    f = f2 // 2
    per_seq = seq // tm
    hb = tm // SUBLANES
    nblk = t // SUBLANES

    def body(ug_ref, halo_ref, dc_ref, nxt_ref, wc_ref, dug_ref, w0_ref, w1_ref, w2_ref, b_ref):
        i = pl.program_id(0)

        @pl.when(i == 0)
        def _():
            for r in (w0_ref, w1_ref, w2_ref, b_ref):
                r[...] = jnp.zeros_like(r)

        first = (i % per_seq) == 0
        last = (i % per_seq) == per_seq - 1
        ug = ug_ref[...]
        halo = jnp.where(first, 0.0, halo_ref[...])
        nxt = jnp.where(last, 0.0, nxt_ref[...])
        row = lax.broadcasted_iota(jnp.int32, ug.shape, 0)
        dc = dc_ref[...]
        wc = wc_ref[...]
        dp1 = _unshift_rows(dc, nxt, 1, row, tm)
        dp2 = _unshift_rows(dc, nxt, 2, row, tm)
        dug_ref[...] = (wc[2:3, :] * dc + wc[1:2, :] * dp1 + wc[0:1, :] * dp2).astype(BF16)
        w2_ref[...] += _rowsum8(dc * ug)
        w1_ref[...] += _rowsum8(dc * _shift_rows(ug, halo, 1, row))
        w0_ref[...] += _rowsum8(dc * _shift_rows(ug, halo, 2, row))
        b_ref[...] += _rowsum8(dc)

        @pl.when(i == t // tm - 1)
        def _():
            for r in (w0_ref, w1_ref, w2_ref, b_ref):
                _fold8(r)

    rows = pl.BlockSpec((tm, f), lambda i: (i, 0))
    acc = pl.BlockSpec((SUBLANES, f), lambda i: (0, 0))
    return pl.pallas_call(
        body, name="conv_glu_bwd2", grid=(t // tm,),
        in_specs=[rows, pl.BlockSpec((SUBLANES, f), lambda i: (jnp.maximum(i * hb - 1, 0), 0)),
                  rows, pl.BlockSpec((SUBLANES, f), lambda i: (jnp.minimum((i + 1) * hb, nblk - 1), 0)),
                  pl.BlockSpec((3, f), lambda i: (0, 0))],
        out_specs=[rows, acc, acc, acc, acc],
        out_shape=[jax.ShapeDtypeStruct((t, f), BF16)] + [jax.ShapeDtypeStruct((SUBLANES, f), F32)] * 4,
        compiler_params=_params("arbitrary"),
    )(u, u, dc, dc, wc)


def _colsum(x, tm, name):
    t, n = x.shape

    def body(x_ref, o_ref):
        @pl.when(pl.program_id(0) == 0)
        def _():
            o_ref[...] = jnp.zeros_like(o_ref)

        o_ref[...] += _rowsum8(x_ref[...].astype(F32))

        @pl.when(pl.program_id(0) == t // tm - 1)
        def _():
            _fold8(o_ref)

    return pl.pallas_call(
        body, name=name, grid=(t // tm,),
        in_specs=[pl.BlockSpec((tm, n), lambda i: (i, 0))],
        out_specs=pl.BlockSpec((SUBLANES, n), lambda i: (0, 0)),
        out_shape=jax.ShapeDtypeStruct((SUBLANES, n), F32),
        compiler_params=_params("arbitrary"),
    )(x)


def _adamw(w, gparts, m, v, name):
    p, r, c = gparts.shape
    tr = r
    for cand in (512, 256, 128, 64, 32, 16, 8):
        if cand * p <= 1024 and r % cand == 0 and r > cand:
            tr = cand
            break
    c1 = 1.0 - ADAM_B1 ** ADAM_STEP
    c2 = 1.0 - ADAM_B2 ** ADAM_STEP

    def body(w_ref, g_ref, m_ref, v_ref, go_ref, d_ref, mo_ref, vo_ref):
        g = g_ref[0].astype(F32)
        for i in range(1, p):
            g = g + g_ref[i].astype(F32)
        mn = ADAM_B1 * m_ref[...] + (1.0 - ADAM_B1) * g
        vn = ADAM_B2 * v_ref[...] + (1.0 - ADAM_B2) * (g * g)
        go_ref[...] = g
        mo_ref[...] = mn
        vo_ref[...] = vn
        d_ref[...] = -ADAM_LR * ((mn / c1) / (jnp.sqrt(vn / c2) + ADAM_EPS) + ADAM_WD * w_ref[...])

    blk = pl.BlockSpec((tr, c), lambda i: (i, 0))
    return pl.pallas_call(
        body, name=name, grid=(r // tr,),
        in_specs=[blk, pl.BlockSpec((p, tr, c), lambda i: (0, i, 0)), blk, blk],
        out_specs=[blk] * 4,
        out_shape=[jax.ShapeDtypeStruct((r, c), F32)] * 4,
        compiler_params=_params("parallel"),
    )(w, gparts, m, v)


MESH = pl.DeviceIdType.MESH
ANY = pl.BlockSpec(memory_space=pl.ANY)


def _all_gather(xs, name):
    n = len(xs)

    def body(*refs):
        x_refs, out_refs = refs[:n], refs[n:2 * n]
        send_sems, recv_sems, local_sems = refs[2 * n:]
        x, y, c = lax.axis_index("x"), lax.axis_index("y"), lax.axis_index("c")
        me, sibling = (x, y, c), (x, y, 1 - c)
        chips = [(1 - x, y), (x, 1 - y), (1 - x, 1 - y)]

        def slot(a, px, py, pc):
            return out_refs[a].at[4 * px + 2 * py + pc]

        def copy(a, k, block, to, src=None):
            return pltpu.make_async_remote_copy(
                src_ref=slot(a, *block) if src is None else src, dst_ref=slot(a, *block),
                send_sem=send_sems.at[k * n + a], recv_sem=recv_sems.at[k * n + a], device_id=to, device_id_type=MESH)

        arrays = range(n)
        mine = [pltpu.make_async_copy(x_refs[a], slot(a, *me), local_sems.at[a]) for a in arrays]
        first = [copy(a, 0, me, sibling, src=x_refs[a]) for a in arrays]
        first += [copy(a, 1 + j, me, (*chip, c), src=x_refs[a]) for j, chip in enumerate(chips) for a in arrays]
        for cp in mine + first:
            cp.start()
        passed = []
        for j, chip in enumerate(chips):
            for a in arrays:
                copy(a, 1 + j, (*chip, c), me).wait_recv()
                passed.append(copy(a, 4 + j, (*chip, c), sibling))
                passed[-1].start()
        for a in arrays:
            copy(a, 0, sibling, me).wait_recv()
        for j, chip in enumerate(chips):
            for a in arrays:
                copy(a, 4 + j, (*chip, 1 - c), me).wait_recv()
        for cp in first + passed:
            cp.wait_send()
        for cp in mine:
            cp.wait()

    return pl.pallas_call(
        body, name=name,
        out_shape=[jax.ShapeDtypeStruct((N_DEV,) + x.shape, x.dtype) for x in xs],
        in_specs=[ANY] * n, out_specs=[ANY] * n,
        scratch_shapes=[pltpu.SemaphoreType.DMA((7 * n,)), pltpu.SemaphoreType.DMA((7 * n,)),
                        pltpu.SemaphoreType.DMA((n,))],
    )(*xs)


def _exchange(gs, name):
    n = len(gs)

    def body(*refs):
        g_refs, land_refs = refs[:n], refs[n:2 * n]
        send_sems, recv_sems, local_sems = refs[2 * n:]
        x, y, c = lax.axis_index("x"), lax.axis_index("y"), lax.axis_index("c")
        mine = 4 * x + 2 * y + c
        own = [pltpu.make_async_copy(g_refs[a].at[mine], land_refs[a].at[mine], local_sems.at[a]) for a in range(n)]
        copies = []
        for k in range(1, N_DEV):
            px = 1 - x if k & 4 else x
            py = 1 - y if k & 2 else y
            pc = 1 - c if k & 1 else c
            for a in range(n):
                copies.append(pltpu.make_async_remote_copy(
                    src_ref=g_refs[a].at[4 * px + 2 * py + pc], dst_ref=land_refs[a].at[mine],
                    send_sem=send_sems.at[(k - 1) * n + a], recv_sem=recv_sems.at[(k - 1) * n + a],
                    device_id=(px, py, pc), device_id_type=MESH))
        for cp in own + copies:
            cp.start()
        for cp in copies + own:
            cp.wait()

    return pl.pallas_call(
        body, name=name, out_shape=[jax.ShapeDtypeStruct(g.shape, g.dtype) for g in gs],
        in_specs=[ANY] * n, out_specs=[ANY] * n,
        scratch_shapes=[pltpu.SemaphoreType.DMA((7 * n,)), pltpu.SemaphoreType.DMA((7 * n,)),
                        pltpu.SemaphoreType.DMA((n,))],
    )(*gs)


def _tile(n, pref, unit=LANES):
    if n <= pref:
        return n
    best = None
    for cand in range(unit, pref + 1, unit):
        if n % cand == 0:
            best = cand
    assert best is not None, (n, pref, unit)
    return best


def _split_heads(t2, b, s, n):
    h = t2.shape[1] // (n * HEAD_DIM)
    return t2.reshape(b, s, n, h, HEAD_DIM).transpose(2, 0, 3, 1, 4).reshape(n, b * h, s, HEAD_DIM)


def _merge_heads(o, b):
    n, bh, s, dh = o.shape
    h = bh // b
    return o.reshape(n, b, h, s, dh).transpose(1, 3, 0, 2, 4).reshape(b * s, n * h * dh)


def _layer_step(x, target, w, attn_blk):
    b, s, d = x.shape
    t = b * s
    e = w["wp_sb"].shape[0]
    h = e // HEAD_DIM
    f = w["w_down"].shape[0]
    x2 = x.reshape(t, d)
    tg = target.reshape(t, d)
    hp = 2 if (b * h) % 2 == 0 else 1
    tm = _tile(t, 512, SUBLANES)
    tmc = _tile(s, 256, SUBLANES)
    tkt = _tile(t, 512, SUBLANES)
    td = _tile(d, 1024)
    tf = _tile(f, 1408)
    t2f = _tile(2 * f, 1408)
    tqkv = _tile(6 * e, 1024)
    tg2 = _tile(2 * d, 1024)

    qkv = _matmul(x2, w["w_qkv"], NN, tm=tm, tn=tqkv, tk=d, out_dtype=BF16, name="in_qkv", bias=w["b_qkv"])
    gate = _matmul(x2, w["w_g"], NN, tm=tm, tn=tg2, tk=d, out_dtype=F32, name="in_gate", bias=w["b_g"])
    fl = _matmul(x2, w["w_f"], NN, tm=tm, tn=LANES, tk=d, out_dtype=F32, name="in_forget", bias=w["b_f"])
    heads = _split_heads(qkv, b, s, 6)
    q_sb, k_sb, v_sb, q_fx, k_fx, v_fx = (heads[i] for i in range(6))
    nr = s // LANES
    f2 = fl.reshape(b, s, LANES)[:, :, :h].transpose(0, 2, 1).reshape(b * h * nr, LANES)
    c2 = _scan_rows(f2, nr, "fwd")
    c = c2.reshape(b * h, s, 1)
    ct = c2.reshape(b * h, 1, s)
    o_sb, tot = _sb_fwd(q_sb, k_sb, v_sb, attn_blk, hp)
    o_fx, lse = _fox_fwd(q_fx, k_fx, v_fx, c, ct, attn_blk, hp)
    o_sb_m = _merge_heads(o_sb[None], b)
    o_fx_m = _merge_heads(o_fx[None], b)
    merged, y_sb, y_fx = _proj_gate_fwd(o_sb_m, o_fx_m, w["wp_sb"], w["wp_fx"], gate, tm)
    x1, xh1, rs1 = _mm_res_ln(merged, w["w_out"], x2, w["ln1_g"], w["ln1_b"], tm, "out_ln1")
    u = _matmul(x1, w["w_up"], NN, tm=tm, tn=t2f, tk=d, out_dtype=F32, name="ffn_up")
    act = _conv_glu_fwd(u, w["w_conv"], w["b_conv"], s, tmc)
    xo, xh2, rs2 = _mm_res_ln(act, w["w_down"], x1, w["ln2_g"], w["ln2_b"], tm, "down_ln2")

    gr = {}
    dr2, dg2, db2, ls = _loss_ln_bwd(xo, xh2, rs2, w["ln2_g"], tg, tm)
    gr["ln2_g"], gr["ln2_b"] = dg2[0:1], db2[0:1]
    da = _matmul(dr2, w["w_down"], NT, tm=tm, tn=tf, tk=d, out_dtype=F32, name="d_act")
    gr["w_down"] = _matmul(act, dr2, TN, tm=tf, tn=td, tk=tkt, out_dtype=F32, name="dw_down")
    dc, du_v = _conv_glu_bwd1(u, da, w["w_conv"], w["b_conv"], s, tmc)
    du_g, gw0, gw1, gw2, gbc = _conv_glu_bwd2(u, dc, w["w_conv"], s, tmc)
    gr["w_conv"] = jnp.concatenate([gw0[0:1], gw1[0:1], gw2[0:1]], axis=0)
    gr["b_conv"] = gbc[0:1]
    du = jnp.concatenate([du_g, du_v], axis=1)
    dlin1 = _matmul(du, w["w_up"], NT, tm=tm, tn=td, tk=t2f, out_dtype=F32, name="d_x1")
    gr["w_up"] = _matmul(x1, du, TN, tm=td, tn=t2f, tk=tkt, out_dtype=F32, name="dw_up")
    dr1, dg1, db1 = _ln_bwd(dr2, dlin1, xh1, rs1, w["ln1_g"], tm)
    gr["ln1_g"], gr["ln1_b"] = dg1[0:1], db1[0:1]
    dmg = _matmul(dr1, w["w_out"], NT, tm=tm, tn=td, tk=td, out_dtype=F32, name="d_merged")
    gr["w_out"] = _matmul(merged, dr1, TN, tm=td, tn=td, tk=tkt, out_dtype=F32, name="dw_out")
    dy_sb, dy_fx, dgate = _gate_bwd(dmg, y_sb, y_fx, gate, tm)
    do_sb = _matmul(dy_sb, w["wp_sb"], NT, tm=tm, tn=e, tk=td, out_dtype=BF16, name="d_o_sb")
    do_fx = _matmul(dy_fx, w["wp_fx"], NT, tm=tm, tn=e, tk=td, out_dtype=BF16, name="d_o_fx")
    gr["wp_sb"] = _matmul(o_sb_m, dy_sb, TN, tm=e, tn=td, tk=tkt, out_dtype=F32, name="dwp_sb")
    gr["wp_fx"] = _matmul(o_fx_m, dy_fx, TN, tm=e, tn=td, tk=tkt, out_dtype=F32, name="dwp_fx")
    dq_sb, dk_sb, dv_sb = _sb_bwd(q_sb, k_sb, v_sb, _split_heads(do_sb, b, s, 1)[0], tot, attn_blk, hp)
    dq_fx, dk_fx, dv_fx, dct = _fox_bwd(q_fx, k_fx, v_fx, c, ct, _split_heads(do_fx, b, s, 1)[0], o_fx, lse, attn_blk, hp)
    dqkv = _merge_heads(jnp.stack([dq_sb, dk_sb, dv_sb, dq_fx, dk_fx, dv_fx]), b).astype(BF16)
    df2 = _scan_rows(f2, nr, "bwd", dct.reshape(b * h * nr, LANES))
    df = jnp.pad(df2.reshape(b, h, s).transpose(0, 2, 1).reshape(t, h), ((0, 0), (0, LANES - h))).astype(BF16)
    dx = _matmul(dqkv, w["w_qkv"], NT, tm=tm, tn=td, tk=tqkv, out_dtype=F32, name="dx_qkv", res=dr1, res_scale=ALPHA)
    dx = _matmul(dgate, w["w_g"], NT, tm=tm, tn=td, tk=tg2, out_dtype=F32, name="dx_gate", res=dx)
    dx = _matmul(df, w["w_f"], NT, tm=tm, tn=td, tk=LANES, out_dtype=F32, name="dx_forget", res=dx)
    gr["w_qkv"] = _matmul(x2, dqkv, TN, tm=td, tn=tqkv, tk=tkt, out_dtype=F32, name="dw_qkv")
    gr["w_g"] = _matmul(x2, dgate, TN, tm=td, tn=tg2, tk=tkt, out_dtype=F32, name="dw_gate")
    gr["w_f"] = _matmul(x2, df, TN, tm=td, tn=LANES, tk=tkt, out_dtype=F32, name="dw_forget")
    gr["b_qkv"] = _colsum(dqkv, tm, "db_qkv")[0:1]
    gr["b_g"] = _colsum(dgate, tm, "db_gate")[0:1]
    gr["b_f"] = _colsum(df, tm, "db_forget")[0:1]
    return ls[0:1, 0:1], dx.reshape(b, s, d), gr


SHARDED = ("w_in", "w_proj_sb", "w_proj_fox", "w_out", "w_up", "w_conv", "w_down")
ROW_SHARDED = ("w_out", "w_down")
REPLICATED = ("b_in", "ln1_g", "ln1_b", "b_conv", "ln2_g", "ln2_b")
WEIGHTS = ("w_in", "b_in", "w_proj_sb", "w_proj_fox", "w_out", "ln1_g", "ln1_b", "w_up", "w_conv", "b_conv",
           "w_down", "ln2_g", "ln2_b")
MATMUL_OPERANDS = ("w_in", "w_proj_sb", "w_proj_fox", "w_out", "w_up", "w_down")


def _cut(full, name):
    r, c = full.shape
    if name in ROW_SHARDED:
        return full.reshape(N_DEV, r // N_DEV, c)
    cs = c // N_DEV
    return jnp.stack([full[:, j * cs:(j + 1) * cs] for j in range(N_DEV)], axis=0)


def _join(blocks, name):
    p, r, c = blocks.shape
    if name in ROW_SHARDED:
        return blocks.reshape(p * r, c)
    return jnp.concatenate([blocks[j] for j in range(p)], axis=1)


def kernel(x, w_in, b_in, w_proj_sb, w_proj_fox, w_out, ln1_g, ln1_b, w_up, w_conv, b_conv, w_down, ln2_g, ln2_b, loss_target, m_w_in, m_b_in, m_w_proj_sb, m_w_proj_fox, m_w_out, m_ln1_g, m_ln1_b, m_w_up, m_w_conv, m_b_conv, m_w_down, m_ln2_g, m_ln2_b, v_w_in, v_b_in, v_w_proj_sb, v_w_proj_fox, v_w_out, v_ln1_g, v_ln1_b, v_w_up, v_w_conv, v_b_conv, v_w_down, v_ln2_g, v_ln2_b):
    wts = dict(w_in=w_in, b_in=b_in, w_proj_sb=w_proj_sb, w_proj_fox=w_proj_fox, w_out=w_out, ln1_g=ln1_g, ln1_b=ln1_b,
               w_up=w_up, w_conv=w_conv, b_conv=b_conv, w_down=w_down, ln2_g=ln2_g, ln2_b=ln2_b)
    mom = dict(w_in=m_w_in, b_in=m_b_in, w_proj_sb=m_w_proj_sb, w_proj_fox=m_w_proj_fox, w_out=m_w_out, ln1_g=m_ln1_g,
               ln1_b=m_ln1_b, w_up=m_w_up, w_conv=m_w_conv, b_conv=m_b_conv, w_down=m_w_down, ln2_g=m_ln2_g, ln2_b=m_ln2_b)
    var = dict(w_in=v_w_in, b_in=v_b_in, w_proj_sb=v_w_proj_sb, w_proj_fox=v_w_proj_fox, w_out=v_w_out, ln1_g=v_ln1_g,
               ln1_b=v_ln1_b, w_up=v_w_up, w_conv=v_w_conv, b_conv=v_b_conv, w_down=v_w_down, ln2_g=v_ln2_g, ln2_b=v_ln2_b)
    shard = {n: wts[n].reshape(wts[n].shape[-2:]) for n in WEIGHTS}

    gathered = _all_gather([shard[n].astype(BF16) if n in MATMUL_OPERANDS else shard[n] for n in SHARDED],
                           "gather_weights")
    full = {n: _join(g, n) for n, g in zip(SHARDED, gathered)}
    e = full["w_proj_sb"].shape[0]
    h = e // HEAD_DIM
    d = full["w_out"].shape[0]
    nq = 6 * e

    def cut_in(a, pad):
        fcols = a[:, nq:nq + h]
        if pad:
            fcols = jnp.pad(fcols, ((0, 0), (0, LANES - h)))
        return a[:, :nq], a[:, nq + h:], fcols

    w_qkv, w_g, w_f = cut_in(full["w_in"], True)
    b_qkv, b_g, b_f = cut_in(shard["b_in"], True)
    w = dict(w_qkv=w_qkv, w_g=w_g, w_f=w_f, b_qkv=b_qkv, b_g=b_g, b_f=b_f,
             wp_sb=full["w_proj_sb"], wp_fx=full["w_proj_fox"], w_out=full["w_out"], w_up=full["w_up"],
             w_down=full["w_down"], w_conv=full["w_conv"], b_conv=shard["b_conv"], ln1_g=shard["ln1_g"], ln1_b=shard["ln1_b"],
             ln2_g=shard["ln2_g"], ln2_b=shard["ln2_b"])

    loss_local, grad_x, gr = _layer_step(x, loss_target, w, min(256, x.shape[1]))
    loss = lax.psum(loss_local[0, 0], ("x", "y", "c"))

    local = dict(
        w_in=jnp.concatenate([gr["w_qkv"], gr["w_f"][:, :h], gr["w_g"]], axis=1),
        b_in=jnp.concatenate([gr["b_qkv"], gr["b_f"][:, :h], gr["b_g"]], axis=1),
        w_proj_sb=gr["wp_sb"], w_proj_fox=gr["wp_fx"], w_out=gr["w_out"], w_up=gr["w_up"], w_conv=gr["w_conv"],
        w_down=gr["w_down"], ln1_g=gr["ln1_g"], ln1_b=gr["ln1_b"], b_conv=gr["b_conv"], ln2_g=gr["ln2_g"],
        ln2_b=gr["ln2_b"])

    landed = _exchange([_cut(local[n], n).astype(BF16 if n in MATMUL_OPERANDS else F32) for n in SHARDED],
                       "exchange_grads")
    gsum = dict(zip(SHARDED, landed))

    parts = _all_gather([jnp.concatenate([local[n] for n in REPLICATED], axis=1)], "gather_small_grads")[0]
    off = 0
    for n in REPLICATED:
        gsum[n] = parts[:, :, off:off + shard[n].size]
        off += shard[n].size

    grads, deltas, new_m, new_v = [], [], [], []
    for n in WEIGHTS:
        shp = wts[n].shape
        g, dl, mn, vn = _adamw(shard[n], gsum[n], mom[n].reshape(shard[n].shape), var[n].reshape(shard[n].shape),
                               "adamw_" + n)
        grads.append(g.reshape(shp))
        deltas.append(dl.reshape(shp))
        new_m.append(mn.reshape(shp))
        new_v.append(vn.reshape(shp))
    return (loss, grad_x, *grads, *deltas, *new_m, *new_v)
```

```python
import functools
import math

import jax
import jax.numpy as jnp
from jax import lax
from jax.experimental import pallas as pl
from jax.experimental.pallas import tpu as pltpu

F32 = jnp.float32
BF16 = jnp.bfloat16

HEAD_DIM = 64
LN_EPS = 1e-5
DEPTH = 1
ALPHA = (2.0 * DEPTH) ** 0.25
ADAM_LR, ADAM_B1, ADAM_B2, ADAM_EPS, ADAM_WD, ADAM_STEP = 0.001, 0.9, 0.999, 1e-08, 0.01, 10
N_DEV = 8
LANES = 128
SUBLANES = 8
VMEM_LIMIT = 56 * 1024 * 1024

NN = ((1,), (0,))
NT = ((1,), (1,))
TN = ((0,), (0,))


def _dot(a, b, dims):
    return lax.dot_general(a, b, (dims, ((), ())), preferred_element_type=F32)


def _params(*sem):
    return pltpu.CompilerParams(dimension_semantics=sem, vmem_limit_bytes=VMEM_LIMIT)


def _iotas(blk):
    row = lax.broadcasted_iota(jnp.int32, (blk, blk), 0)
    col = lax.broadcasted_iota(jnp.int32, (blk, blk), 1)
    return row, col


def _sb_terms(z):
    e = jnp.exp(-jnp.abs(z))
    lb = jnp.minimum(z, 0.0) - jnp.log(1.0 + e)
    return lb, lb - z, e


def _pair_specs(s, blk, e, pp, branch):
    w = pp * LANES
    nq = s // blk
    ng = e // w
    base = 3 * branch * ng
    q_in = pl.BlockSpec((blk, w), lambda b, g, i: (b * nq + i, base + g))
    k_in = pl.BlockSpec((s, w), lambda b, g, i: (b, base + ng + g))
    v_in = pl.BlockSpec((s, w), lambda b, g, i: (b, base + 2 * ng + g))
    q_out = pl.BlockSpec((blk, w), lambda b, g, i: (b * nq + i, g))
    kv_out = pl.BlockSpec((s, w), lambda b, g, i: (b, g))
    rows = pl.BlockSpec((2 * pp, blk, 1), lambda b, g, i: (b * ng + g, i, 0))
    krow = pl.BlockSpec((2 * pp, 1, s), lambda b, g, i: (b * ng + g, 0, 0))
    return q_in, k_in, v_in, q_out, kv_out, rows, krow


def _half_masks(x):
    low = lax.broadcasted_iota(jnp.int32, x.shape, 1) < HEAD_DIM
    zero = jnp.zeros_like(x)
    return jnp.concatenate([jnp.where(low, x, zero), jnp.where(low, zero, x)], axis=0)


def _tri_sums(xs, tri):
    hi = [x.astype(BF16) for x in xs]
    lo = [(x - h.astype(F32)).astype(BF16) for x, h in zip(xs, hi)]
    n = len(xs)
    blk = xs[0].shape[0]
    r = _dot(jnp.concatenate(hi + lo, axis=0), tri, NN)
    return [r[i * blk:(i + 1) * blk] + r[(n + i) * blk:(n + i + 1) * blk] for i in range(n)]


def _sb_fwd(qkv, b, s, e, blk, pp):
    scale = HEAD_DIM ** -0.5
    nh = 2 * pp
    t = b * s

    def body(q_ref, k_ref, v_ref, o_ref, tot_ref):
        qi = pl.program_id(2)
        qm = [_half_masks((q_ref[:, p * LANES:(p + 1) * LANES] * scale).astype(BF16)) for p in range(pp)]
        row, col = _iotas(blk)
        strict = col < row
        after = (row > col).astype(BF16)

        def block(j, carry, diag):
            off = pl.multiple_of(j * blk, blk)
            o_acc, run = carry
            zz = [_dot(qm[p], k_ref[pl.ds(off, blk), p * LANES:(p + 1) * LANES], NT) for p in range(pp)]
            z = [zz[h // 2][(h % 2) * blk:(h % 2 + 1) * blk] for h in range(nh)]
            terms = [_sb_terms(z[h]) for h in range(nh)]
            lom = [jnp.where(strict, terms[h][1], 0.0) if diag else terms[h][1] for h in range(nh)]
            sfx = _tri_sums(lom, after)
            a = [jnp.exp(terms[h][0] + sfx[h] + run[h]) for h in range(nh)]
            if diag:
                a = [jnp.where(strict, a[h], 0.0) for h in range(nh)]
            ab = [a[h].astype(BF16) for h in range(nh)]
            o_new = tuple(
                o_acc[p] + _dot(jnp.concatenate([ab[2 * p], ab[2 * p + 1]], axis=1),
                                _half_masks(v_ref[pl.ds(off, blk), p * LANES:(p + 1) * LANES]), NN)
                for p in range(pp))
            return o_new, tuple(run[h] + sfx[h][:, 0:1] + lom[h][:, 0:1] for h in range(nh))

        carry = block(qi, ((jnp.zeros((blk, LANES), F32),) * pp, (jnp.zeros((blk, 1), F32),) * nh), True)
        o_acc, run = lax.fori_loop(0, qi, lambda it, c: block(qi - 1 - it, c, False), carry)
        for p in range(pp):
            o_ref[:, p * LANES:(p + 1) * LANES] = o_acc[p].astype(o_ref.dtype)
        for h in range(nh):
            tot_ref[h] = run[h]

    q_in, k_in, v_in, q_out, _, rows, _ = _pair_specs(s, blk, e, pp, 0)
    return pl.pallas_call(
        body, name="sb_fwd", grid=(b, e // (pp * LANES), s // blk),
        in_specs=[q_in, k_in, v_in], out_specs=[q_out, rows],
        out_shape=[jax.ShapeDtypeStruct((t, e), BF16), jax.ShapeDtypeStruct((b * e // HEAD_DIM, s, 1), F32)],
        compiler_params=_params("parallel", "parallel", "arbitrary"),
    )(qkv, qkv, qkv)


def _sb_bwd(qkv, do, tot, b, s, e, blk, pp):
    scale = HEAD_DIM ** -0.5
    nh = 2 * pp
    t = b * s
    nq = s // blk

    def body(q_ref, k_ref, v_ref, do_ref, tot_ref, dq_ref, dk_ref, dv_ref, dk_acc, dv_acc):
        qi = pl.program_id(2)

        @pl.when(qi == 0)
        def _():
            dk_acc[...] = jnp.zeros_like(dk_acc)
            dv_acc[...] = jnp.zeros_like(dv_acc)

        qm = [_half_masks((q_ref[:, p * LANES:(p + 1) * LANES] * scale).astype(BF16)) for p in range(pp)]
        dom = [_half_masks(do_ref[:, p * LANES:(p + 1) * LANES].astype(BF16)) for p in range(pp)]
        tot_t = [tot_ref[h] for h in range(nh)]
        row, col = _iotas(blk)
        strict = col < row
        upto = (row <= col).astype(BF16)
        before = (row < col).astype(BF16)

        def block(j, carry, diag):
            off = pl.multiple_of(j * blk, blk)
            dq_acc, cl, cg = carry
            kp = [k_ref[pl.ds(off, blk), p * LANES:(p + 1) * LANES] for p in range(pp)]
            vp = [v_ref[pl.ds(off, blk), p * LANES:(p + 1) * LANES] for p in range(pp)]
            zz = [_dot(qm[p], kp[p], NT) for p in range(pp)]
            dd = [_dot(dom[p], vp[p], NT) for p in range(pp)]
            z = [zz[h // 2][(h % 2) * blk:(h % 2 + 1) * blk] for h in range(nh)]
            da = [dd[h // 2][(h % 2) * blk:(h % 2 + 1) * blk] for h in range(nh)]
            terms = [_sb_terms(z[h]) for h in range(nh)]
            lom = [jnp.where(strict, terms[h][1], 0.0) if diag else terms[h][1] for h in range(nh)]
            pre = _tri_sums(lom, upto)
            a = [jnp.exp(terms[h][0] + (tot_t[h] - cl[h] - pre[h])) for h in range(nh)]
            if diag:
                a = [jnp.where(strict, a[h], 0.0) for h in range(nh)]
            g = [a[h] * da[h] for h in range(nh)]
            pw = _tri_sums(g, before)
            dzb = []
            for h in range(nh):
                ex = terms[h][2]
                r = 1.0 / (1.0 + ex)
                er = ex * r
                pos = z[h] >= 0.0
                dz = g[h] * jnp.where(pos, er, r) - (cg[h] + pw[h]) * jnp.where(pos, r, er)
                if diag:
                    dz = jnp.where(strict, dz, 0.0)
                dzb.append(dz.astype(BF16))
            ab = [a[h].astype(BF16) for h in range(nh)]
            for p in range(pp):
                cols = slice(p * LANES, (p + 1) * LANES)
                dk_acc[pl.ds(off, blk), cols] += _dot(jnp.concatenate([dzb[2 * p], dzb[2 * p + 1]], axis=0), qm[p], TN)
                dv_acc[pl.ds(off, blk), cols] += _dot(jnp.concatenate([ab[2 * p], ab[2 * p + 1]], axis=0), dom[p], TN)
            dq_new = tuple(dq_acc[p] + _dot(jnp.concatenate([dzb[2 * p], dzb[2 * p + 1]], axis=1), _half_masks(kp[p]), NN)
                           for p in range(pp))
            return (dq_new, tuple(cl[h] + pre[h][:, blk - 1:blk] for h in range(nh)),
                    tuple(cg[h] + pw[h][:, blk - 1:blk] + g[h][:, blk - 1:blk] for h in range(nh)))

        zero1 = (jnp.zeros((blk, 1), F32),) * nh
        carry = lax.fori_loop(0, qi, lambda j, c: block(j, c, False), ((jnp.zeros((blk, LANES), F32),) * pp, zero1, zero1))
        dq_acc, _, _ = block(qi, carry, True)
        for p in range(pp):
            dq_ref[:, p * LANES:(p + 1) * LANES] = (dq_acc[p] * scale).astype(BF16)

        @pl.when(qi == nq - 1)
        def _():
            dk_ref[...] = dk_acc[...].astype(BF16)
            dv_ref[...] = dv_acc[...].astype(BF16)

    q_in, k_in, v_in, q_out, kv_out, rows, _ = _pair_specs(s, blk, e, pp, 0)
    w = pp * LANES
    return pl.pallas_call(
        body, name="sb_bwd", grid=(b, e // w, nq),
        in_specs=[q_in, k_in, v_in, q_out, rows], out_specs=[q_out, kv_out, kv_out],
        out_shape=[jax.ShapeDtypeStruct((t, e), BF16)] * 3,
        scratch_shapes=[pltpu.VMEM((s, w), F32), pltpu.VMEM((s, w), F32)],
        compiler_params=_params("parallel", "parallel", "arbitrary"),
    )(qkv, qkv, qkv, do, tot)


NEG = -1e30


def _fox_fwd(qkv, c, ct, b, s, e, blk, pp):
    scale = HEAD_DIM ** -0.5
    nh = 2 * pp
    t = b * s

    def body(q_ref, k_ref, v_ref, c_ref, ct_ref, o_ref, lse_ref):
        qi = pl.program_id(2)
        qm = [_half_masks((q_ref[:, p * LANES:(p + 1) * LANES] * scale).astype(BF16)) for p in range(pp)]
        cq = [c_ref[h] for h in range(nh)]
        row, col = _iotas(blk)
        causal = col <= row
        low = lax.broadcasted_iota(jnp.int32, (blk, LANES), 1) < HEAD_DIM

        def block(j, carry, diag):
            off = pl.multiple_of(j * blk, blk)
            m, l, acc = carry
            zz = [_dot(qm[p], k_ref[pl.ds(off, blk), p * LANES:(p + 1) * LANES], NT) for p in range(pp)]
            z = [zz[h // 2][(h % 2) * blk:(h % 2 + 1) * blk] + (cq[h] - ct_ref[h, :, pl.ds(off, blk)]) for h in range(nh)]
            if diag:
                z = [jnp.where(causal, z[h], NEG) for h in range(nh)]
            m_new = tuple(jnp.maximum(m[h], jnp.max(z[h], axis=1, keepdims=True)) for h in range(nh))
            w = [jnp.exp(m[h] - m_new[h]) for h in range(nh)]
            pr = [jnp.exp(z[h] - m_new[h]) for h in range(nh)]
            pb = [pr[h].astype(BF16) for h in range(nh)]
            pv = [_dot(jnp.concatenate([pb[2 * p], pb[2 * p + 1]], axis=1),
                       _half_masks(v_ref[pl.ds(off, blk), p * LANES:(p + 1) * LANES]), NN) for p in range(pp)]
            acc_new = tuple(jnp.where(low, w[2 * p], w[2 * p + 1]) * acc[p] + pv[p] for p in range(pp))
            l_new = tuple(w[h] * l[h] + jnp.sum(pr[h], axis=1, keepdims=True) for h in range(nh))
            return m_new, l_new, acc_new

        zero = ((jnp.full((blk, 1), NEG, F32),) * nh, (jnp.zeros((blk, 1), F32),) * nh, (jnp.zeros((blk, LANES), F32),) * pp)
        carry = block(qi, zero, True)
        m, l, acc = lax.fori_loop(0, qi, lambda j, c_: block(j, c_, False), carry)
        for p in range(pp):
            o_ref[:, p * LANES:(p + 1) * LANES] = acc[p] / jnp.where(low, l[2 * p], l[2 * p + 1])
        for h in range(nh):
            lse_ref[h] = m[h] + jnp.log(l[h])

    q_in, k_in, v_in, q_out, _, rows, krow = _pair_specs(s, blk, e, pp, 1)
    return pl.pallas_call(
        body, name="fox_fwd", grid=(b, e // (pp * LANES), s // blk),
        in_specs=[q_in, k_in, v_in, rows, krow], out_specs=[q_out, rows],
        out_shape=[jax.ShapeDtypeStruct((t, e), F32), jax.ShapeDtypeStruct((b * e // HEAD_DIM, s, 1), F32)],
        compiler_params=_params("parallel", "parallel", "arbitrary"),
    )(qkv, qkv, qkv, c, ct)


def _fox_bwd(qkv, c, ct, do, o, lse, b, s, e, blk, pp):
    scale = HEAD_DIM ** -0.5
    nh = 2 * pp
    t = b * s
    nq = s // blk

    def body(q_ref, k_ref, v_ref, c_ref, ct_ref, do_ref, o_ref, lse_ref, dq_ref, dk_ref, dv_ref, dct_ref, dk_acc, dv_acc):
        qi = pl.program_id(2)

        @pl.when(qi == 0)
        def _():
            dk_acc[...] = jnp.zeros_like(dk_acc)
            dv_acc[...] = jnp.zeros_like(dv_acc)
            dct_ref[...] = jnp.zeros_like(dct_ref)

        qm = [_half_masks((q_ref[:, p * LANES:(p + 1) * LANES] * scale).astype(BF16)) for p in range(pp)]
        dob = [do_ref[:, p * LANES:(p + 1) * LANES].astype(BF16) for p in range(pp)]
        dom = [_half_masks(dob[p]) for p in range(pp)]
        low = lax.broadcasted_iota(jnp.int32, (blk, LANES), 1) < HEAD_DIM
        delta = []
        for p in range(pp):
            prod = dob[p].astype(F32) * o_ref[:, p * LANES:(p + 1) * LANES]
            delta.append(jnp.sum(jnp.where(low, prod, 0.0), axis=1, keepdims=True))
            delta.append(jnp.sum(jnp.where(low, 0.0, prod), axis=1, keepdims=True))
        cq = [c_ref[h] for h in range(nh)]
        lse_t = [lse_ref[h] for h in range(nh)]
        row, col = _iotas(blk)
        causal = col <= row

        def block(j, dq_acc, diag):
            off = pl.multiple_of(j * blk, blk)
            kp = [k_ref[pl.ds(off, blk), p * LANES:(p + 1) * LANES] for p in range(pp)]
            zz = [_dot(qm[p], kp[p], NT) for p in range(pp)]
            dd = [_dot(dom[p], v_ref[pl.ds(off, blk), p * LANES:(p + 1) * LANES], NT) for p in range(pp)]
            z = [zz[h // 2][(h % 2) * blk:(h % 2 + 1) * blk] + (cq[h] - ct_ref[h, :, pl.ds(off, blk)]) for h in range(nh)]
            pr = [jnp.exp(z[h] - lse_t[h]) for h in range(nh)]
            if diag:
                pr = [jnp.where(causal, pr[h], 0.0) for h in range(nh)]
            ds = [pr[h] * (dd[h // 2][(h % 2) * blk:(h % 2 + 1) * blk] - delta[h]) for h in range(nh)]
            dsb = [ds[h].astype(BF16) for h in range(nh)]
            pb = [pr[h].astype(BF16) for h in range(nh)]
            for p in range(pp):
                cols = slice(p * LANES, (p + 1) * LANES)
                dk_acc[pl.ds(off, blk), cols] += _dot(jnp.concatenate([dsb[2 * p], dsb[2 * p + 1]], axis=0), qm[p], TN)
                dv_acc[pl.ds(off, blk), cols] += _dot(jnp.concatenate([pb[2 * p], pb[2 * p + 1]], axis=0), dom[p], TN)
            for h in range(nh):
                dct_ref[h, :, pl.ds(off, blk)] -= jnp.sum(ds[h], axis=0, keepdims=True)
            return tuple(dq_acc[p] + _dot(jnp.concatenate([dsb[2 * p], dsb[2 * p + 1]], axis=1), _half_masks(kp[p]), NN)
                         for p in range(pp))

        dq_acc = lax.fori_loop(0, qi, lambda j, a: block(j, a, False), (jnp.zeros((blk, LANES), F32),) * pp)
        dq_acc = block(qi, dq_acc, True)
        for p in range(pp):
            dq_ref[:, p * LANES:(p + 1) * LANES] = (dq_acc[p] * scale).astype(BF16)

        @pl.when(qi == nq - 1)
        def _():
            dk_ref[...] = dk_acc[...].astype(BF16)
            dv_ref[...] = dv_acc[...].astype(BF16)

    q_in, k_in, v_in, q_out, kv_out, rows, krow = _pair_specs(s, blk, e, pp, 1)
    w = pp * LANES
    return pl.pallas_call(
        body, name="fox_bwd", grid=(b, e // w, nq),
        in_specs=[q_in, k_in, v_in, rows, krow, q_out, q_out, rows], out_specs=[q_out, kv_out, kv_out, krow],
        out_shape=[jax.ShapeDtypeStruct((t, e), BF16)] * 3 + [jax.ShapeDtypeStruct((b * e // HEAD_DIM, 1, s), F32)],
        scratch_shapes=[pltpu.VMEM((s, w), F32), pltpu.VMEM((s, w), F32)],
        compiler_params=_params("parallel", "parallel", "arbitrary"),
    )(qkv, qkv, qkv, c, ct, do, o, lse)


def _scan_rows(f2, group, mode, d2=None):
    n = f2.shape[0]

    def body(*refs):
        f_ref, o_ref = refs[0], refs[-1]
        f = f_ref[...]
        row, col = _iotas(LANES)
        grow = lax.broadcasted_iota(jnp.int32, (n, n), 0)
        gcol = lax.broadcasted_iota(jnp.int32, (n, n), 1)
        same = (grow // group) == (gcol // group)
        e = jnp.exp(-jnp.abs(f))
        if mode == "fwd":
            x = jnp.minimum(f, 0.0) - jnp.log1p(e)
            within = (row <= col).astype(F32)
            earlier = (same & (gcol < grow)).astype(F32)
        else:
            x = refs[1][...]
            within = (row >= col).astype(F32)
            earlier = (same & (gcol > grow)).astype(F32)
        y = jnp.dot(x, within, preferred_element_type=F32, precision=lax.Precision.HIGHEST)
        tot = jnp.sum(x, axis=1, keepdims=True)
        y = y + jnp.dot(earlier, tot, preferred_element_type=F32, precision=lax.Precision.HIGHEST)
        if mode == "bwd":
            r = 1.0 / (1.0 + e)
            y = y * jnp.where(f >= 0.0, e * r, r)
        o_ref[...] = y

    args = (f2,) if mode == "fwd" else (f2, d2)
    return pl.pallas_call(body, name="logf_" + mode, out_shape=jax.ShapeDtypeStruct(f2.shape, F32),
                          compiler_params=_params())(*args)


def _matmul(a, b, dims, *, tm, tn, tk, out_dtype, name, bias=None, res=None, res_scale=1.0):
    if dims == NN:
        (m, kk), n = a.shape, b.shape[1]
        a_spec = pl.BlockSpec((tm, tk), lambda i, j, k: (i, k))
        b_spec = pl.BlockSpec((tk, tn), lambda i, j, k: (k, j))
    elif dims == NT:
        (m, kk), n = a.shape, b.shape[0]
        a_spec = pl.BlockSpec((tm, tk), lambda i, j, k: (i, k))
        b_spec = pl.BlockSpec((tn, tk), lambda i, j, k: (j, k))
    else:
        (kk, m), n = a.shape, b.shape[1]
        a_spec = pl.BlockSpec((tk, tm), lambda i, j, k: (k, i))
        b_spec = pl.BlockSpec((tk, tn), lambda i, j, k: (k, j))
    assert m % tm == 0 and n % tn == 0 and kk % tk == 0, (name, m, n, kk, tm, tn, tk)
    nk = kk // tk
    extras, extra_specs = [], []
    if bias is not None:
        extras.append(bias)
        extra_specs.append(pl.BlockSpec((1, tn), lambda i, j, k: (0, j)))
    if res is not None:
        extras.append(res)
        extra_specs.append(pl.BlockSpec((tm, tn), lambda i, j, k: (i, j)))

    def body(a_ref, b_ref, *rest):
        o_ref, acc_ref = rest[-2], rest[-1]
        k = pl.program_id(2)
        part = _dot(a_ref[...].astype(BF16), b_ref[...].astype(BF16), dims)

        @pl.when(k == 0)
        def _():
            acc_ref[...] = part

        @pl.when(k > 0)
        def _():
            acc_ref[...] += part

        @pl.when(k == nk - 1)
        def _():
            out = acc_ref[...]
            idx = 0
            if bias is not None:
                out = out + rest[idx][...]
                idx += 1
            if res is not None:
                out = out + res_scale * rest[idx][...]
            o_ref[...] = out.astype(o_ref.dtype)

    return pl.pallas_call(
        body, name=name, grid=(m // tm, n // tn, nk),
        in_specs=[a_spec, b_spec] + extra_specs,
        out_specs=pl.BlockSpec((tm, tn), lambda i, j, k: (i, j)),
        out_shape=jax.ShapeDtypeStruct((m, n), out_dtype),
        scratch_shapes=[pltpu.VMEM((tm, tn), F32)],
        compiler_params=_params("parallel", "parallel", "arbitrary"),
    )(a, b, *extras)


def _sigmoid(x):
    e = jnp.exp(-jnp.abs(x))
    r = 1.0 / (1.0 + e)
    return jnp.where(x >= 0.0, r, e * r)


def _proj_gate_fwd(o_sb, o_fx, wp_sb, wp_fx, g, tm):
    t, e = o_sb.shape
    d = wp_sb.shape[1]

    def body(osb_ref, ofx_ref, wsb_ref, wfx_ref, gsb_ref, gfx_ref, mg_ref, ysb_ref, yfx_ref):
        ysb = _dot(osb_ref[...].astype(BF16), wsb_ref[...], NN)
        yfx = _dot(ofx_ref[...].astype(BF16), wfx_ref[...], NN)
        ysb_ref[...] = ysb
        yfx_ref[...] = yfx
        mg_ref[...] = (_sigmoid(gsb_ref[...]) * ysb + _sigmoid(gfx_ref[...]) * yfx).astype(BF16)

    rows_e = pl.BlockSpec((tm, e), lambda i: (i, 0))
    rows_d = pl.BlockSpec((tm, d), lambda i: (i, 0))
    w_spec = pl.BlockSpec((e, d), lambda i: (0, 0))
    return pl.pallas_call(
        body, name="proj_gate_fwd", grid=(t // tm,),
        in_specs=[rows_e, rows_e, w_spec, w_spec, rows_d, pl.BlockSpec((tm, d), lambda i: (i, 1))],
        out_specs=[rows_d, rows_d, rows_d],
        out_shape=[jax.ShapeDtypeStruct((t, d), BF16), jax.ShapeDtypeStruct((t, d), F32), jax.ShapeDtypeStruct((t, d), F32)],
        compiler_params=_params("parallel"),
    )(o_sb, o_fx, wp_sb, wp_fx, g, g)


def _gate_bwd(dmg, y_sb, y_fx, g, tm):
    t, d = dmg.shape

    def body(dm_ref, ysb_ref, yfx_ref, gsb_ref, gfx_ref, dysb_ref, dyfx_ref, dg_ref):
        dm = dm_ref[...]
        ssb = _sigmoid(gsb_ref[...])
        sfx = _sigmoid(gfx_ref[...])
        dysb_ref[...] = (dm * ssb).astype(BF16)
        dyfx_ref[...] = (dm * sfx).astype(BF16)
        dg_ref[:, 0:d] = (dm * ysb_ref[...] * ssb * (1.0 - ssb)).astype(BF16)
        dg_ref[:, d:2 * d] = (dm * yfx_ref[...] * sfx * (1.0 - sfx)).astype(BF16)

    rows = pl.BlockSpec((tm, d), lambda i: (i, 0))
    rows1 = pl.BlockSpec((tm, d), lambda i: (i, 1))
    return pl.pallas_call(
        body, name="gate_bwd", grid=(t // tm,),
        in_specs=[rows, rows, rows, rows, rows1],
        out_specs=[rows, rows, pl.BlockSpec((tm, 2 * d), lambda i: (i, 0))],
        out_shape=[jax.ShapeDtypeStruct((t, d), BF16)] * 2 + [jax.ShapeDtypeStruct((t, 2 * d), BF16)],
        compiler_params=_params("parallel"),
    )(dmg, y_sb, y_fx, g, g)


def _mm_res_ln(a, w, xres, gamma, beta, tm, name):
    t, kk = a.shape
    d = w.shape[1]

    def body(a_ref, w_ref, x_ref, g_ref, b_ref, xn_ref, xh_ref, rs_ref):
        r = ALPHA * x_ref[...] + _dot(a_ref[...].astype(BF16), w_ref[...], NN)
        mean = jnp.mean(r, axis=1, keepdims=True)
        cen = r - mean
        rstd = lax.rsqrt(jnp.mean(cen * cen, axis=1, keepdims=True) + LN_EPS)
        xh = cen * rstd
        xh_ref[...] = xh
        xn_ref[...] = xh * g_ref[...] + b_ref[...]
        rs_ref[...] = rstd

    rows_d = pl.BlockSpec((tm, d), lambda i: (i, 0))
    vec = pl.BlockSpec((1, d), lambda i: (0, 0))
    return pl.pallas_call(
        body, name=name, grid=(t // tm,),
        in_specs=[pl.BlockSpec((tm, kk), lambda i: (i, 0)), pl.BlockSpec((kk, d), lambda i: (0, 0)), rows_d, vec, vec],
        out_specs=[rows_d, rows_d, pl.BlockSpec((tm, 1), lambda i: (i, 0))],
        out_shape=[jax.ShapeDtypeStruct((t, d), F32), jax.ShapeDtypeStruct((t, d), F32), jax.ShapeDtypeStruct((t, 1), F32)],
        compiler_params=_params("parallel"),
    )(a, w, xres, gamma, beta)


def _ln_bwd_math(dy, xh, rstd, gamma):
    dxh = dy * gamma
    m1 = jnp.mean(dxh, axis=1, keepdims=True)
    m2 = jnp.mean(dxh * xh, axis=1, keepdims=True)
    return rstd * (dxh - m1 - xh * m2)


def _rowsum8(x):
    tm, n = x.shape
    return jnp.sum(x.reshape(tm // SUBLANES, SUBLANES, n), axis=0)


def _fold8(ref):
    ref[0:1, :] = jnp.sum(ref[...], axis=0, keepdims=True)


def _loss_ln_bwd(x2, xh, rstd, gamma, target, tm):
    t, d = x2.shape

    def body(x_ref, xh_ref, rs_ref, g_ref, tg_ref, dr_ref, dg_ref, db_ref, ls_ref):
        @pl.when(pl.program_id(0) == 0)
        def _():
            dg_ref[...] = jnp.zeros_like(dg_ref)
            db_ref[...] = jnp.zeros_like(db_ref)
            ls_ref[...] = jnp.zeros_like(ls_ref)

        err = x_ref[...] - tg_ref[...]
        xh = xh_ref[...]
        dy = err * (1.0 / d)
        dr_ref[...] = _ln_bwd_math(dy, xh, rs_ref[...], g_ref[...])
        dg_ref[...] += _rowsum8(dy * xh)
        db_ref[...] += _rowsum8(dy)
        sq = _rowsum8(err * err)
        part = sq[:, 0:LANES]
        for j in range(1, d // LANES):
            part = part + sq[:, j * LANES:(j + 1) * LANES]
        ls_ref[...] += part * (0.5 / d)

        @pl.when(pl.program_id(0) == t // tm - 1)
        def _():
            _fold8(dg_ref)
            _fold8(db_ref)
            ls_ref[0:1, 0:1] = jnp.sum(jnp.sum(ls_ref[...], axis=0, keepdims=True), axis=1, keepdims=True)

    rows = pl.BlockSpec((tm, d), lambda i: (i, 0))
    acc = pl.BlockSpec((SUBLANES, d), lambda i: (0, 0))
    return pl.pallas_call(
        body, name="loss_ln_bwd", grid=(t // tm,),
        in_specs=[rows, rows, pl.BlockSpec((tm, 1), lambda i: (i, 0)), pl.BlockSpec((1, d), lambda i: (0, 0)), rows],
        out_specs=[rows, acc, acc, pl.BlockSpec((SUBLANES, LANES), lambda i: (0, 0))],
        out_shape=[jax.ShapeDtypeStruct((t, d), F32), jax.ShapeDtypeStruct((SUBLANES, d), F32),
                   jax.ShapeDtypeStruct((SUBLANES, d), F32), jax.ShapeDtypeStruct((SUBLANES, LANES), F32)],
        compiler_params=_params("arbitrary"),
    )(x2, xh, rstd, gamma, target)


def _ln_bwd(dr_next, dlin, xh, rstd, gamma, tm):
    t, d = xh.shape

    def body(dn_ref, dl_ref, xh_ref, rs_ref, g_ref, dr_ref, dg_ref, db_ref):
        @pl.when(pl.program_id(0) == 0)
        def _():
            dg_ref[...] = jnp.zeros_like(dg_ref)
            db_ref[...] = jnp.zeros_like(db_ref)

        dy = ALPHA * dn_ref[...] + dl_ref[...]
        xh = xh_ref[...]
        dr_ref[...] = _ln_bwd_math(dy, xh, rs_ref[...], g_ref[...])
        dg_ref[...] += _rowsum8(dy * xh)
        db_ref[...] += _rowsum8(dy)

        @pl.when(pl.program_id(0) == t // tm - 1)
        def _():
            _fold8(dg_ref)
            _fold8(db_ref)

    rows = pl.BlockSpec((tm, d), lambda i: (i, 0))
    acc = pl.BlockSpec((SUBLANES, d), lambda i: (0, 0))
    return pl.pallas_call(
        body, name="ln_bwd", grid=(t // tm,),
        in_specs=[rows, rows, rows, pl.BlockSpec((tm, 1), lambda i: (i, 0)), pl.BlockSpec((1, d), lambda i: (0, 0))],
        out_specs=[rows, acc, acc],
        out_shape=[jax.ShapeDtypeStruct((t, d), F32), jax.ShapeDtypeStruct((SUBLANES, d), F32),
                   jax.ShapeDtypeStruct((SUBLANES, d), F32)],
        compiler_params=_params("arbitrary"),
    )(dr_next, dlin, xh, rstd, gamma)


def _shift_rows(x, halo, shift, row):
    out = pltpu.roll(x, shift, 0)
    for r in range(shift):
        out = jnp.where(row == r, halo[SUBLANES - shift + r:SUBLANES - shift + r + 1, :], out)
    return out


def _unshift_rows(x, halo, shift, row, tm):
    out = pltpu.roll(x, tm - shift, 0)
    for r in range(shift):
        out = jnp.where(row == tm - shift + r, halo[r:r + 1, :], out)
    return out


def _conv_pre(ug_ref, halo_ref, wc_ref, bc_ref, first, tm):
    ug = ug_ref[...]
    halo = jnp.where(first, 0.0, halo_ref[...])
    row = lax.broadcasted_iota(jnp.int32, ug.shape, 0)
    wc = wc_ref[...]
    um1 = _shift_rows(ug, halo, 1, row)
    um2 = _shift_rows(ug, halo, 2, row)
    c = bc_ref[...] + wc[2:3, :] * ug + wc[1:2, :] * um1 + wc[0:1, :] * um2
    return c, ug, um1, um2


INV_SQRT2 = 1.0 / math.sqrt(2.0)
INV_SQRT2PI = 1.0 / math.sqrt(2.0 * math.pi)


def _conv_glu_fwd(u, wc, bc, seq, tm):
    t, f2 = u.shape
    f = f2 // 2
    per_seq = seq // tm
    hb = tm // SUBLANES

    def body(ug_ref, halo_ref, uv_ref, wc_ref, bc_ref, a_ref):
        first = (pl.program_id(0) % per_seq) == 0
        c, _, _, _ = _conv_pre(ug_ref, halo_ref, wc_ref, bc_ref, first, tm)
        gelu = 0.5 * c * (1.0 + lax.erf(c * INV_SQRT2))
        a_ref[...] = (gelu * uv_ref[...]).astype(BF16)

    return pl.pallas_call(
        body, name="conv_glu_fwd", grid=(t // tm,),
        in_specs=[pl.BlockSpec((tm, f), lambda i: (i, 0)),
                  pl.BlockSpec((SUBLANES, f), lambda i: (jnp.maximum(i * hb - 1, 0), 0)),
                  pl.BlockSpec((tm, f), lambda i: (i, 1)),
                  pl.BlockSpec((3, f), lambda i: (0, 0)), pl.BlockSpec((1, f), lambda i: (0, 0))],
        out_specs=pl.BlockSpec((tm, f), lambda i: (i, 0)),
        out_shape=jax.ShapeDtypeStruct((t, f), BF16),
        compiler_params=_params("parallel"),
    )(u, u, u, wc, bc)


def _conv_glu_bwd1(u, da, wc, bc, seq, tm):
    t, f2 = u.shape
    f = f2 // 2
    per_seq = seq // tm
    hb = tm // SUBLANES

    def body(ug_ref, halo_ref, uv_ref, da_ref, wc_ref, bc_ref, dc_ref, duv_ref):
        first = (pl.program_id(0) % per_seq) == 0
        c, _, _, _ = _conv_pre(ug_ref, halo_ref, wc_ref, bc_ref, first, tm)
        cdf = 0.5 * (1.0 + lax.erf(c * INV_SQRT2))
        pdf = jnp.exp(-0.5 * c * c) * INV_SQRT2PI
        da = da_ref[...]
        duv_ref[...] = (da * (c * cdf)).astype(BF16)
        dc_ref[...] = da * uv_ref[...] * (cdf + c * pdf)

    rows = pl.BlockSpec((tm, f), lambda i: (i, 0))
    return pl.pallas_call(
        body, name="conv_glu_bwd1", grid=(t // tm,),
        in_specs=[rows, pl.BlockSpec((SUBLANES, f), lambda i: (jnp.maximum(i * hb - 1, 0), 0)),
                  pl.BlockSpec((tm, f), lambda i: (i, 1)), rows,
                  pl.BlockSpec((3, f), lambda i: (0, 0)), pl.BlockSpec((1, f), lambda i: (0, 0))],
        out_specs=[rows, rows],
        out_shape=[jax.ShapeDtypeStruct((t, f), F32), jax.ShapeDtypeStruct((t, f), BF16)],
        compiler_params=_params("parallel"),
    )(u, u, u, da, wc, bc)


def _conv_glu_bwd2(u, dc, wc, seq, tm):
    t, f2 = u.shape
    f = f2 // 2
    per_seq = seq // tm
    hb = tm // SUBLANES
    nblk = t // SUBLANES

    def body(ug_ref, halo_ref, dc_ref, nxt_ref, wc_ref, dug_ref, w0_ref, w1_ref, w2_ref, b_ref):
        i = pl.program_id(0)

        @pl.when(i == 0)
        def _():
            for r in (w0_ref, w1_ref, w2_ref, b_ref):
                r[...] = jnp.zeros_like(r)

        first = (i % per_seq) == 0
        last = (i % per_seq) == per_seq - 1
        ug = ug_ref[...]
        halo = jnp.where(first, 0.0, halo_ref[...])
        nxt = jnp.where(last, 0.0, nxt_ref[...])
        row = lax.broadcasted_iota(jnp.int32, ug.shape, 0)
        dc = dc_ref[...]
        wc = wc_ref[...]
        dp1 = _unshift_rows(dc, nxt, 1, row, tm)
        dp2 = _unshift_rows(dc, nxt, 2, row, tm)
        dug_ref[...] = (wc[2:3, :] * dc + wc[1:2, :] * dp1 + wc[0:1, :] * dp2).astype(BF16)
        w2_ref[...] += _rowsum8(dc * ug)
        w1_ref[...] += _rowsum8(dc * _shift_rows(ug, halo, 1, row))
        w0_ref[...] += _rowsum8(dc * _shift_rows(ug, halo, 2, row))
        b_ref[...] += _rowsum8(dc)

        @pl.when(i == t // tm - 1)
        def _():
            for r in (w0_ref, w1_ref, w2_ref, b_ref):
                _fold8(r)

    rows = pl.BlockSpec((tm, f), lambda i: (i, 0))
    acc = pl.BlockSpec((SUBLANES, f), lambda i: (0, 0))
    return pl.pallas_call(
        body, name="conv_glu_bwd2", grid=(t // tm,),
        in_specs=[rows, pl.BlockSpec((SUBLANES, f), lambda i: (jnp.maximum(i * hb - 1, 0), 0)),
                  rows, pl.BlockSpec((SUBLANES, f), lambda i: (jnp.minimum((i + 1) * hb, nblk - 1), 0)),
                  pl.BlockSpec((3, f), lambda i: (0, 0))],
        out_specs=[rows, acc, acc, acc, acc],
        out_shape=[jax.ShapeDtypeStruct((t, f), BF16)] + [jax.ShapeDtypeStruct((SUBLANES, f), F32)] * 4,
        compiler_params=_params("arbitrary"),
    )(u, u, dc, dc, wc)


def _colsum(x, tm, name):
    t, n = x.shape

    def body(x_ref, o_ref):
        @pl.when(pl.program_id(0) == 0)
        def _():
            o_ref[...] = jnp.zeros_like(o_ref)

        o_ref[...] += _rowsum8(x_ref[...].astype(F32))

        @pl.when(pl.program_id(0) == t // tm - 1)
        def _():
            _fold8(o_ref)

    return pl.pallas_call(
        body, name=name, grid=(t // tm,),
        in_specs=[pl.BlockSpec((tm, n), lambda i: (i, 0))],
        out_specs=pl.BlockSpec((SUBLANES, n), lambda i: (0, 0)),
        out_shape=jax.ShapeDtypeStruct((SUBLANES, n), F32),
        compiler_params=_params("arbitrary"),
    )(x)


def _adamw(w, gparts, m, v, name):
    p, r, c = gparts.shape
    tr = r
    for cand in (512, 256, 128, 64, 32, 16, 8):
        if cand * p <= 1024 and r % cand == 0 and r > cand:
            tr = cand
            break
    c1 = 1.0 - ADAM_B1 ** ADAM_STEP
    c2 = 1.0 - ADAM_B2 ** ADAM_STEP

    def body(w_ref, g_ref, m_ref, v_ref, go_ref, d_ref, mo_ref, vo_ref):
        g = g_ref[0].astype(F32)
        for i in range(1, p):
            g = g + g_ref[i].astype(F32)
        mn = ADAM_B1 * m_ref[...] + (1.0 - ADAM_B1) * g
        vn = ADAM_B2 * v_ref[...] + (1.0 - ADAM_B2) * (g * g)
        go_ref[...] = g
        mo_ref[...] = mn
        vo_ref[...] = vn
        d_ref[...] = -ADAM_LR * ((mn / c1) / (jnp.sqrt(vn / c2) + ADAM_EPS) + ADAM_WD * w_ref[...])

    blk = pl.BlockSpec((tr, c), lambda i: (i, 0))
    return pl.pallas_call(
        body, name=name, grid=(r // tr,),
        in_specs=[blk, pl.BlockSpec((p, tr, c), lambda i: (0, i, 0)), blk, blk],
        out_specs=[blk] * 4,
        out_shape=[jax.ShapeDtypeStruct((r, c), F32)] * 4,
        compiler_params=_params("parallel"),
    )(w, gparts, m, v)


MESH = pl.DeviceIdType.MESH
ANY = pl.BlockSpec(memory_space=pl.ANY)


def _all_gather(xs, name):
    n = len(xs)

    def body(*refs):
        x_refs, out_refs = refs[:n], refs[n:2 * n]
        send_sems, recv_sems, local_sems = refs[2 * n:]
        x, y, c = lax.axis_index("x"), lax.axis_index("y"), lax.axis_index("c")
        me, sibling = (x, y, c), (x, y, 1 - c)
        chips = [(1 - x, y), (x, 1 - y), (1 - x, 1 - y)]

        def slot(a, px, py, pc):
            return out_refs[a].at[4 * px + 2 * py + pc]

        def copy(a, k, block, to, src=None):
            return pltpu.make_async_remote_copy(
                src_ref=slot(a, *block) if src is None else src, dst_ref=slot(a, *block),
                send_sem=send_sems.at[k * n + a], recv_sem=recv_sems.at[k * n + a], device_id=to, device_id_type=MESH)

        arrays = range(n)
        mine = [pltpu.make_async_copy(x_refs[a], slot(a, *me), local_sems.at[a]) for a in arrays]
        first = [copy(a, 0, me, sibling, src=x_refs[a]) for a in arrays]
        first += [copy(a, 1 + j, me, (*chip, c), src=x_refs[a]) for j, chip in enumerate(chips) for a in arrays]
        for cp in mine + first:
            cp.start()
        passed = []
        for j, chip in enumerate(chips):
            for a in arrays:
                copy(a, 1 + j, (*chip, c), me).wait_recv()
                passed.append(copy(a, 4 + j, (*chip, c), sibling))
                passed[-1].start()
        for a in arrays:
            copy(a, 0, sibling, me).wait_recv()
        for j, chip in enumerate(chips):
            for a in arrays:
                copy(a, 4 + j, (*chip, 1 - c), me).wait_recv()
        for cp in first + passed:
            cp.wait_send()
        for cp in mine:
            cp.wait()

    return pl.pallas_call(
        body, name=name,
        out_shape=[jax.ShapeDtypeStruct((N_DEV,) + x.shape, x.dtype) for x in xs],
        in_specs=[ANY] * n, out_specs=[ANY] * n,
        scratch_shapes=[pltpu.SemaphoreType.DMA((7 * n,)), pltpu.SemaphoreType.DMA((7 * n,)),
                        pltpu.SemaphoreType.DMA((n,))],
    )(*xs)


def _exchange(gs, name):
    n = len(gs)

    def body(*refs):
        g_refs, land_refs = refs[:n], refs[n:2 * n]
        send_sems, recv_sems, local_sems = refs[2 * n:]
        x, y, c = lax.axis_index("x"), lax.axis_index("y"), lax.axis_index("c")
        mine = 4 * x + 2 * y + c
        own = [pltpu.make_async_copy(g_refs[a].at[mine], land_refs[a].at[mine], local_sems.at[a]) for a in range(n)]
        copies = []
        for k in range(1, N_DEV):
            px = 1 - x if k & 4 else x
            py = 1 - y if k & 2 else y
            pc = 1 - c if k & 1 else c
            for a in range(n):
                copies.append(pltpu.make_async_remote_copy(
                    src_ref=g_refs[a].at[4 * px + 2 * py + pc], dst_ref=land_refs[a].at[mine],
                    send_sem=send_sems.at[(k - 1) * n + a], recv_sem=recv_sems.at[(k - 1) * n + a],
                    device_id=(px, py, pc), device_id_type=MESH))
        for cp in own + copies:
            cp.start()
        for cp in copies + own:
            cp.wait()

    return pl.pallas_call(
        body, name=name, out_shape=[jax.ShapeDtypeStruct(g.shape, g.dtype) for g in gs],
        in_specs=[ANY] * n, out_specs=[ANY] * n,
        scratch_shapes=[pltpu.SemaphoreType.DMA((7 * n,)), pltpu.SemaphoreType.DMA((7 * n,)),
                        pltpu.SemaphoreType.DMA((n,))],
    )(*gs)


def _tile(n, pref, unit=LANES):
    if n <= pref:
        return n
    best = None
    for cand in range(unit, pref + 1, unit):
        if n % cand == 0:
            best = cand
    assert best is not None, (n, pref, unit)
    return best


def _layer_step(x, target, w, attn_blk):
    b, s, d = x.shape
    t = b * s
    e = w["wp_sb"].shape[0]
    h = e // HEAD_DIM
    f = w["w_down"].shape[0]
    x2 = x.reshape(t, d)
    tg = target.reshape(t, d)
    pp = 2 if (e // LANES) % 2 == 0 else 1
    tm = _tile(t, 512, SUBLANES)
    tmc = _tile(s, 256, SUBLANES)
    tkt = _tile(t, 512, SUBLANES)
    td = _tile(d, 1024)
    tf = _tile(f, 1408)
    t2f = _tile(2 * f, 1408)
    tqkv = _tile(6 * e, 1024)
    tg2 = _tile(2 * d, 1024)

    qkv = _matmul(x2, w["w_qkv"], NN, tm=tm, tn=tqkv, tk=d, out_dtype=BF16, name="in_qkv", bias=w["b_qkv"])
    gate = _matmul(x2, w["w_g"], NN, tm=tm, tn=tg2, tk=d, out_dtype=F32, name="in_gate", bias=w["b_g"])
    fl = _matmul(x2, w["w_f"], NN, tm=tm, tn=LANES, tk=d, out_dtype=F32, name="in_forget", bias=w["b_f"])
    nr = s // LANES
    f2 = fl.reshape(b, s, LANES)[:, :, :h].transpose(0, 2, 1).reshape(b * h * nr, LANES)
    c2 = _scan_rows(f2, nr, "fwd")
    c = c2.reshape(b * h, s, 1)
    ct = c2.reshape(b * h, 1, s)
    o_sb, tot = _sb_fwd(qkv, b, s, e, attn_blk, pp)
    o_fx, lse = _fox_fwd(qkv, c, ct, b, s, e, attn_blk, pp)
    merged, y_sb, y_fx = _proj_gate_fwd(o_sb, o_fx, w["wp_sb"], w["wp_fx"], gate, tm)
    x1, xh1, rs1 = _mm_res_ln(merged, w["w_out"], x2, w["ln1_g"], w["ln1_b"], tm, "out_ln1")
    u = _matmul(x1, w["w_up"], NN, tm=tm, tn=t2f, tk=d, out_dtype=F32, name="ffn_up")
    act = _conv_glu_fwd(u, w["w_conv"], w["b_conv"], s, tmc)
    xo, xh2, rs2 = _mm_res_ln(act, w["w_down"], x1, w["ln2_g"], w["ln2_b"], tm, "down_ln2")

    gr = {}
    dr2, dg2, db2, ls = _loss_ln_bwd(xo, xh2, rs2, w["ln2_g"], tg, tm)
    gr["ln2_g"], gr["ln2_b"] = dg2[0:1], db2[0:1]
    da = _matmul(dr2, w["w_down"], NT, tm=tm, tn=tf, tk=d, out_dtype=F32, name="d_act")
    gr["w_down"] = _matmul(act, dr2, TN, tm=tf, tn=td, tk=tkt, out_dtype=F32, name="dw_down")
    dc, du_v = _conv_glu_bwd1(u, da, w["w_conv"], w["b_conv"], s, tmc)
    du_g, gw0, gw1, gw2, gbc = _conv_glu_bwd2(u, dc, w["w_conv"], s, tmc)
    gr["w_conv"] = jnp.concatenate([gw0[0:1], gw1[0:1], gw2[0:1]], axis=0)
    gr["b_conv"] = gbc[0:1]
    du = jnp.concatenate([du_g, du_v], axis=1)
    dlin1 = _matmul(du, w["w_up"], NT, tm=tm, tn=td, tk=t2f, out_dtype=F32, name="d_x1")
    gr["w_up"] = _matmul(x1, du, TN, tm=td, tn=t2f, tk=tkt, out_dtype=F32, name="dw_up")
    dr1, dg1, db1 = _ln_bwd(dr2, dlin1, xh1, rs1, w["ln1_g"], tm)
    gr["ln1_g"], gr["ln1_b"] = dg1[0:1], db1[0:1]
    dmg = _matmul(dr1, w["w_out"], NT, tm=tm, tn=td, tk=td, out_dtype=F32, name="d_merged")
    gr["w_out"] = _matmul(merged, dr1, TN, tm=td, tn=td, tk=tkt, out_dtype=F32, name="dw_out")
    dy_sb, dy_fx, dgate = _gate_bwd(dmg, y_sb, y_fx, gate, tm)
    do_sb = _matmul(dy_sb, w["wp_sb"], NT, tm=tm, tn=e, tk=td, out_dtype=BF16, name="d_o_sb")
    do_fx = _matmul(dy_fx, w["wp_fx"], NT, tm=tm, tn=e, tk=td, out_dtype=BF16, name="d_o_fx")
    gr["wp_sb"] = _matmul(o_sb, dy_sb, TN, tm=e, tn=td, tk=tkt, out_dtype=F32, name="dwp_sb")
    gr["wp_fx"] = _matmul(o_fx, dy_fx, TN, tm=e, tn=td, tk=tkt, out_dtype=F32, name="dwp_fx")
    dq_sb, dk_sb, dv_sb = _sb_bwd(qkv, do_sb, tot, b, s, e, attn_blk, pp)
    dq_fx, dk_fx, dv_fx, dct = _fox_bwd(qkv, c, ct, do_fx, o_fx, lse, b, s, e, attn_blk, pp)
    dqkv = jnp.concatenate([dq_sb, dk_sb, dv_sb, dq_fx, dk_fx, dv_fx], axis=1)
    df2 = _scan_rows(f2, nr, "bwd", dct.reshape(b * h * nr, LANES))
    df = jnp.pad(df2.reshape(b, h, s).transpose(0, 2, 1).reshape(t, h), ((0, 0), (0, LANES - h))).astype(BF16)
    dx = _matmul(dqkv, w["w_qkv"], NT, tm=tm, tn=td, tk=tqkv, out_dtype=F32, name="dx_qkv", res=dr1, res_scale=ALPHA)
    dx = _matmul(dgate, w["w_g"], NT, tm=tm, tn=td, tk=tg2, out_dtype=F32, name="dx_gate", res=dx)
    dx = _matmul(df, w["w_f"], NT, tm=tm, tn=td, tk=LANES, out_dtype=F32, name="dx_forget", res=dx)
    gr["w_qkv"] = _matmul(x2, dqkv, TN, tm=td, tn=tqkv, tk=tkt, out_dtype=F32, name="dw_qkv")
    gr["w_g"] = _matmul(x2, dgate, TN, tm=td, tn=tg2, tk=tkt, out_dtype=F32, name="dw_gate")
    gr["w_f"] = _matmul(x2, df, TN, tm=td, tn=LANES, tk=tkt, out_dtype=F32, name="dw_forget")
    gr["b_qkv"] = _colsum(dqkv, tm, "db_qkv")[0:1]
    gr["b_g"] = _colsum(dgate, tm, "db_gate")[0:1]
    gr["b_f"] = _colsum(df, tm, "db_forget")[0:1]
    return ls[0:1, 0:1], dx.reshape(b, s, d), gr


SHARDED = ("w_in", "w_proj_sb", "w_proj_fox", "w_out", "w_up", "w_conv", "w_down")
ROW_SHARDED = ("w_out", "w_down")
REPLICATED = ("b_in", "ln1_g", "ln1_b", "b_conv", "ln2_g", "ln2_b")
WEIGHTS = ("w_in", "b_in", "w_proj_sb", "w_proj_fox", "w_out", "ln1_g", "ln1_b", "w_up", "w_conv", "b_conv",
           "w_down", "ln2_g", "ln2_b")
MATMUL_OPERANDS = ("w_in", "w_proj_sb", "w_proj_fox", "w_out", "w_up", "w_down")


def _cut(full, name):
    r, c = full.shape
    if name in ROW_SHARDED:
        return full.reshape(N_DEV, r // N_DEV, c)
    cs = c // N_DEV
    return jnp.stack([full[:, j * cs:(j + 1) * cs] for j in range(N_DEV)], axis=0)


def _join(blocks, name):
    p, r, c = blocks.shape
    if name in ROW_SHARDED:
        return blocks.reshape(p * r, c)
    return jnp.concatenate([blocks[j] for j in range(p)], axis=1)


def kernel(x, w_in, b_in, w_proj_sb, w_proj_fox, w_out, ln1_g, ln1_b, w_up, w_conv, b_conv, w_down, ln2_g, ln2_b, loss_target, m_w_in, m_b_in, m_w_proj_sb, m_w_proj_fox, m_w_out, m_ln1_g, m_ln1_b, m_w_up, m_w_conv, m_b_conv, m_w_down, m_ln2_g, m_ln2_b, v_w_in, v_b_in, v_w_proj_sb, v_w_proj_fox, v_w_out, v_ln1_g, v_ln1_b, v_w_up, v_w_conv, v_b_conv, v_w_down, v_ln2_g, v_ln2_b):
    wts = dict(w_in=w_in, b_in=b_in, w_proj_sb=w_proj_sb, w_proj_fox=w_proj_fox, w_out=w_out, ln1_g=ln1_g, ln1_b=ln1_b,
               w_up=w_up, w_conv=w_conv, b_conv=b_conv, w_down=w_down, ln2_g=ln2_g, ln2_b=ln2_b)
    mom = dict(w_in=m_w_in, b_in=m_b_in, w_proj_sb=m_w_proj_sb, w_proj_fox=m_w_proj_fox, w_out=m_w_out, ln1_g=m_ln1_g,
               ln1_b=m_ln1_b, w_up=m_w_up, w_conv=m_w_conv, b_conv=m_b_conv, w_down=m_w_down, ln2_g=m_ln2_g, ln2_b=m_ln2_b)
    var = dict(w_in=v_w_in, b_in=v_b_in, w_proj_sb=v_w_proj_sb, w_proj_fox=v_w_proj_fox, w_out=v_w_out, ln1_g=v_ln1_g,
               ln1_b=v_ln1_b, w_up=v_w_up, w_conv=v_w_conv, b_conv=v_b_conv, w_down=v_w_down, ln2_g=v_ln2_g, ln2_b=v_ln2_b)
    shard = {n: wts[n].reshape(wts[n].shape[-2:]) for n in WEIGHTS}

    gathered = _all_gather([shard[n].astype(BF16) if n in MATMUL_OPERANDS else shard[n] for n in SHARDED],
                           "gather_weights")
    full = {n: _join(g, n) for n, g in zip(SHARDED, gathered)}
    e = full["w_proj_sb"].shape[0]
    h = e // HEAD_DIM
    d = full["w_out"].shape[0]
    nq = 6 * e

    def cut_in(a, pad):
        fcols = a[:, nq:nq + h]
        if pad:
            fcols = jnp.pad(fcols, ((0, 0), (0, LANES - h)))
        return a[:, :nq], a[:, nq + h:], fcols

    w_qkv, w_g, w_f = cut_in(full["w_in"], True)
    b_qkv, b_g, b_f = cut_in(shard["b_in"], True)
    w = dict(w_qkv=w_qkv, w_g=w_g, w_f=w_f, b_qkv=b_qkv, b_g=b_g, b_f=b_f,
             wp_sb=full["w_proj_sb"], wp_fx=full["w_proj_fox"], w_out=full["w_out"], w_up=full["w_up"],
             w_down=full["w_down"], w_conv=full["w_conv"], b_conv=shard["b_conv"], ln1_g=shard["ln1_g"], ln1_b=shard["ln1_b"],
             ln2_g=shard["ln2_g"], ln2_b=shard["ln2_b"])

    loss_local, grad_x, gr = _layer_step(x, loss_target, w, min(256, x.shape[1]))
    loss = lax.psum(loss_local[0, 0], ("x", "y", "c"))

    local = dict(
        w_in=jnp.concatenate([gr["w_qkv"], gr["w_f"][:, :h], gr["w_g"]], axis=1),
        b_in=jnp.concatenate([gr["b_qkv"], gr["b_f"][:, :h], gr["b_g"]], axis=1),
        w_proj_sb=gr["wp_sb"], w_proj_fox=gr["wp_fx"], w_out=gr["w_out"], w_up=gr["w_up"], w_conv=gr["w_conv"],
        w_down=gr["w_down"], ln1_g=gr["ln1_g"], ln1_b=gr["ln1_b"], b_conv=gr["b_conv"], ln2_g=gr["ln2_g"],
        ln2_b=gr["ln2_b"])

    landed = _exchange([_cut(local[n], n).astype(BF16 if n in MATMUL_OPERANDS else F32) for n in SHARDED],
                       "exchange_grads")
    gsum = dict(zip(SHARDED, landed))

    parts = _all_gather([jnp.concatenate([local[n] for n in REPLICATED], axis=1)], "gather_small_grads")[0]
    off = 0
    for n in REPLICATED:
        gsum[n] = parts[:, :, off:off + shard[n].size]
        off += shard[n].size

    grads, deltas, new_m, new_v = [], [], [], []
    for n in WEIGHTS:
        shp = wts[n].shape
        g, dl, mn, vn = _adamw(shard[n], gsum[n], mom[n].reshape(shard[n].shape), var[n].reshape(shard[n].shape),
                               "adamw_" + n)
        grads.append(g.reshape(shp))
        deltas.append(dl.reshape(shp))
        new_m.append(mn.reshape(shp))
        new_v.append(vn.reshape(shp))
    return (loss, grad_x, *grads, *deltas, *new_m, *new_v)
```

```python
import functools
import math

import jax
import jax.numpy as jnp
from jax import lax
from jax.experimental import pallas as pl
from jax.experimental.pallas import tpu as pltpu

F32 = jnp.float32
BF16 = jnp.bfloat16

HEAD_DIM = 64
LN_EPS = 1e-5
DEPTH = 1
ALPHA = (2.0 * DEPTH) ** 0.25
ADAM_LR, ADAM_B1, ADAM_B2, ADAM_EPS, ADAM_WD, ADAM_STEP = 0.001, 0.9, 0.999, 1e-08, 0.01, 10
N_DEV = 8
LANES = 128
SUBLANES = 8
VMEM_LIMIT = 56 * 1024 * 1024

NN = ((1,), (0,))
NT = ((1,), (1,))
TN = ((0,), (0,))


def _dot(a, b, dims):
    return lax.dot_general(a, b, (dims, ((), ())), preferred_element_type=F32)


def _params(*sem):
    return pltpu.CompilerParams(dimension_semantics=sem, vmem_limit_bytes=VMEM_LIMIT)


def _iotas(blk):
    row = lax.broadcasted_iota(jnp.int32, (blk, blk), 0)
    col = lax.broadcasted_iota(jnp.int32, (blk, blk), 1)
    return row, col


def _sb_terms(z):
    e = jnp.exp(-jnp.abs(z))
    lb = jnp.minimum(z, 0.0) - jnp.log(1.0 + e)
    return lb, lb - z, e


def _pair_specs(s, blk, e, pp, branch):
    w = pp * LANES
    nq = s // blk
    ng = e // w
    base = 3 * branch * ng
    q_in = pl.BlockSpec((blk, w), lambda b, g, i: (b * nq + i, base + g))
    k_in = pl.BlockSpec((s, w), lambda b, g, i: (b, base + ng + g))
    v_in = pl.BlockSpec((s, w), lambda b, g, i: (b, base + 2 * ng + g))
    q_out = pl.BlockSpec((blk, w), lambda b, g, i: (b * nq + i, g))
    kv_out = pl.BlockSpec((s, w), lambda b, g, i: (b, g))
    rows = pl.BlockSpec((2 * pp, blk, 1), lambda b, g, i: (b * ng + g, i, 0))
    krow = pl.BlockSpec((2 * pp, 1, s), lambda b, g, i: (b * ng + g, 0, 0))
    return q_in, k_in, v_in, q_out, kv_out, rows, krow


def _half_masks(x):
    low = lax.broadcasted_iota(jnp.int32, x.shape, 1) < HEAD_DIM
    zero = jnp.zeros_like(x)
    return jnp.concatenate([jnp.where(low, x, zero), jnp.where(low, zero, x)], axis=0)


def _tri_sums(xs, tri):
    hi = [x.astype(BF16) for x in xs]
    lo = [(x - h.astype(F32)).astype(BF16) for x, h in zip(xs, hi)]
    n = len(xs)
    blk = xs[0].shape[0]
    r = _dot(jnp.concatenate(hi + lo, axis=0), tri, NN)
    return [r[i * blk:(i + 1) * blk] + r[(n + i) * blk:(n + i + 1) * blk] for i in range(n)]


def _sb_fwd(qkv, b, s, e, blk, pp):
    scale = HEAD_DIM ** -0.5
    nh = 2 * pp
    t = b * s

    def body(q_ref, k_ref, v_ref, o_ref, tot_ref, first_ref):
        qi = pl.program_id(2)
        qm = [_half_masks((q_ref[:, p * LANES:(p + 1) * LANES] * scale).astype(BF16)) for p in range(pp)]
        row, col = _iotas(blk)
        strict = col < row
        after = (row > col).astype(BF16)

        def block(j, carry, diag):
            off = pl.multiple_of(j * blk, blk)
            o_acc, run = carry
            zz = [_dot(qm[p], k_ref[pl.ds(off, blk), p * LANES:(p + 1) * LANES], NT) for p in range(pp)]
            z = [zz[h // 2][(h % 2) * blk:(h % 2 + 1) * blk] for h in range(nh)]
            terms = [_sb_terms(z[h]) for h in range(nh)]
            lom = [jnp.where(strict, terms[h][1], 0.0) if diag else terms[h][1] for h in range(nh)]
            sfx = _tri_sums(lom, after)
            a = [jnp.exp(terms[h][0] + sfx[h] + run[h]) for h in range(nh)]
            if diag:
                a = [jnp.where(strict, a[h], 0.0) for h in range(nh)]
            ab = [a[h].astype(BF16) for h in range(nh)]
            o_new = tuple(
                o_acc[p] + _dot(jnp.concatenate([ab[2 * p], ab[2 * p + 1]], axis=1),
                                _half_masks(v_ref[pl.ds(off, blk), p * LANES:(p + 1) * LANES]), NN)
                for p in range(pp))
            return o_new, tuple(run[h] + sfx[h][:, 0:1] + lom[h][:, 0:1] for h in range(nh))

        def alive(run):
            m = run[0]
            for h in range(1, nh):
                m = jnp.maximum(m, run[h])
            return jnp.max(m) > DEAD

        o_acc, run = block(qi, ((jnp.zeros((blk, LANES), F32),) * pp, (jnp.zeros((blk, 1), F32),) * nh), True)

        def step(c):
            j, _, o_acc, run = c
            o_acc, run = block(j, (o_acc, run), False)
            return j - 1, alive(run), o_acc, run

        j, _, o_acc, run = lax.while_loop(lambda c: jnp.logical_and(c[0] >= 0, c[1]), step,
                                          (qi - 1, alive(run), o_acc, run))
        for p in range(pp):
            o_ref[:, p * LANES:(p + 1) * LANES] = o_acc[p].astype(o_ref.dtype)
        for h in range(nh):
            tot_ref[h] = run[h]
            first_ref[h] = jnp.zeros((blk, 1), F32) + (j + 1).astype(F32)

    q_in, k_in, v_in, q_out, _, rows, _ = _pair_specs(s, blk, e, pp, 0)
    return pl.pallas_call(
        body, name="sb_fwd", grid=(b, e // (pp * LANES), s // blk),
        in_specs=[q_in, k_in, v_in], out_specs=[q_out, rows, rows],
        out_shape=[jax.ShapeDtypeStruct((t, e), BF16)] + [jax.ShapeDtypeStruct((b * e // HEAD_DIM, s, 1), F32)] * 2,
        compiler_params=_params("parallel", "parallel", "arbitrary"),
    )(qkv, qkv, qkv)


def _sb_bwd(qkv, do, tot, first, b, s, e, blk, pp):
    scale = HEAD_DIM ** -0.5
    nh = 2 * pp
    t = b * s
    nq = s // blk

    def body(q_ref, k_ref, v_ref, do_ref, tot_ref, first_ref, dq_ref, dk_ref, dv_ref, dk_acc, dv_acc):
        qi = pl.program_id(2)

        @pl.when(qi == 0)
        def _():
            dk_acc[...] = jnp.zeros_like(dk_acc)
            dv_acc[...] = jnp.zeros_like(dv_acc)

        qm = [_half_masks((q_ref[:, p * LANES:(p + 1) * LANES] * scale).astype(BF16)) for p in range(pp)]
        dom = [_half_masks(do_ref[:, p * LANES:(p + 1) * LANES].astype(BF16)) for p in range(pp)]
        tot_t = [tot_ref[h] for h in range(nh)]
        row, col = _iotas(blk)
        strict = col < row
        upto = (row <= col).astype(BF16)
        before = (row < col).astype(BF16)

        def block(j, carry, diag):
            off = pl.multiple_of(j * blk, blk)
            dq_acc, cl, cg = carry
            kp = [k_ref[pl.ds(off, blk), p * LANES:(p + 1) * LANES] for p in range(pp)]
            vp = [v_ref[pl.ds(off, blk), p * LANES:(p + 1) * LANES] for p in range(pp)]
            zz = [_dot(qm[p], kp[p], NT) for p in range(pp)]
            dd = [_dot(dom[p], vp[p], NT) for p in range(pp)]
            z = [zz[h // 2][(h % 2) * blk:(h % 2 + 1) * blk] for h in range(nh)]
            da = [dd[h // 2][(h % 2) * blk:(h % 2 + 1) * blk] for h in range(nh)]
            terms = [_sb_terms(z[h]) for h in range(nh)]
            lom = [jnp.where(strict, terms[h][1], 0.0) if diag else terms[h][1] for h in range(nh)]
            pre = _tri_sums(lom, upto)
            a = [jnp.exp(terms[h][0] + (tot_t[h] - cl[h] - pre[h])) for h in range(nh)]
            if diag:
                a = [jnp.where(strict, a[h], 0.0) for h in range(nh)]
            g = [a[h] * da[h] for h in range(nh)]
            pw = _tri_sums(g, before)
            dzb = []
            for h in range(nh):
                ex = terms[h][2]
                r = 1.0 / (1.0 + ex)
                er = ex * r
                pos = z[h] >= 0.0
                dz = g[h] * jnp.where(pos, er, r) - (cg[h] + pw[h]) * jnp.where(pos, r, er)
                if diag:
                    dz = jnp.where(strict, dz, 0.0)
                dzb.append(dz.astype(BF16))
            ab = [a[h].astype(BF16) for h in range(nh)]
            for p in range(pp):
                cols = slice(p * LANES, (p + 1) * LANES)
                dk_acc[pl.ds(off, blk), cols] += _dot(jnp.concatenate([dzb[2 * p], dzb[2 * p + 1]], axis=0), qm[p], TN)
                dv_acc[pl.ds(off, blk), cols] += _dot(jnp.concatenate([ab[2 * p], ab[2 * p + 1]], axis=0), dom[p], TN)
            dq_new = tuple(dq_acc[p] + _dot(jnp.concatenate([dzb[2 * p], dzb[2 * p + 1]], axis=1), _half_masks(kp[p]), NN)
                           for p in range(pp))
            return (dq_new, tuple(cl[h] + pre[h][:, blk - 1:blk] for h in range(nh)),
                    tuple(cg[h] + pw[h][:, blk - 1:blk] + g[h][:, blk - 1:blk] for h in range(nh)))

        zero1 = (jnp.zeros((blk, 1), F32),) * nh
        j0 = jnp.clip(jnp.max(first_ref[0]).astype(jnp.int32), 0, qi)
        carry = lax.fori_loop(j0, qi, lambda j, c: block(j, c, False), ((jnp.zeros((blk, LANES), F32),) * pp, zero1, zero1))
        dq_acc, _, _ = block(qi, carry, True)
        for p in range(pp):
            dq_ref[:, p * LANES:(p + 1) * LANES] = (dq_acc[p] * scale).astype(BF16)

        @pl.when(qi == nq - 1)
        def _():
            dk_ref[...] = dk_acc[...].astype(BF16)
            dv_ref[...] = dv_acc[...].astype(BF16)

    q_in, k_in, v_in, q_out, kv_out, rows, _ = _pair_specs(s, blk, e, pp, 0)
    w = pp * LANES
    return pl.pallas_call(
        body, name="sb_bwd", grid=(b, e // w, nq),
        in_specs=[q_in, k_in, v_in, q_out, rows, rows], out_specs=[q_out, kv_out, kv_out],
        out_shape=[jax.ShapeDtypeStruct((t, e), BF16)] * 3,
        scratch_shapes=[pltpu.VMEM((s, w), F32), pltpu.VMEM((s, w), F32)],
        compiler_params=_params("parallel", "parallel", "arbitrary"),
    )(qkv, qkv, qkv, do, tot, first)


NEG = -1e30
DEAD = -110.0


def _fox_fwd(qkv, c, ct, b, s, e, blk, pp):
    scale = HEAD_DIM ** -0.5
    nh = 2 * pp
    t = b * s

    def body(q_ref, k_ref, v_ref, c_ref, ct_ref, o_ref, lse_ref):
        qi = pl.program_id(2)
        qm = [_half_masks((q_ref[:, p * LANES:(p + 1) * LANES] * scale).astype(BF16)) for p in range(pp)]
        cq = [c_ref[h] for h in range(nh)]
        row, col = _iotas(blk)
        causal = col <= row
        low = lax.broadcasted_iota(jnp.int32, (blk, LANES), 1) < HEAD_DIM

        def block(j, carry, diag):
            off = pl.multiple_of(j * blk, blk)
            m, l, acc = carry
            zz = [_dot(qm[p], k_ref[pl.ds(off, blk), p * LANES:(p + 1) * LANES], NT) for p in range(pp)]
            z = [zz[h // 2][(h % 2) * blk:(h % 2 + 1) * blk] + (cq[h] - ct_ref[h, :, pl.ds(off, blk)]) for h in range(nh)]
            if diag:
                z = [jnp.where(causal, z[h], NEG) for h in range(nh)]
            m_new = tuple(jnp.maximum(m[h], jnp.max(z[h], axis=1, keepdims=True)) for h in range(nh))
            w = [jnp.exp(m[h] - m_new[h]) for h in range(nh)]
            pr = [jnp.exp(z[h] - m_new[h]) for h in range(nh)]
            pb = [pr[h].astype(BF16) for h in range(nh)]
            pv = [_dot(jnp.concatenate([pb[2 * p], pb[2 * p + 1]], axis=1),
                       _half_masks(v_ref[pl.ds(off, blk), p * LANES:(p + 1) * LANES]), NN) for p in range(pp)]
            acc_new = tuple(jnp.where(low, w[2 * p], w[2 * p + 1]) * acc[p] + pv[p] for p in range(pp))
            l_new = tuple(w[h] * l[h] + jnp.sum(pr[h], axis=1, keepdims=True) for h in range(nh))
            return m_new, l_new, acc_new

        zero = ((jnp.full((blk, 1), NEG, F32),) * nh, (jnp.zeros((blk, 1), F32),) * nh, (jnp.zeros((blk, LANES), F32),) * pp)
        carry = block(qi, zero, True)
        m, l, acc = lax.fori_loop(0, qi, lambda j, c_: block(j, c_, False), carry)
        for p in range(pp):
            o_ref[:, p * LANES:(p + 1) * LANES] = acc[p] / jnp.where(low, l[2 * p], l[2 * p + 1])
        for h in range(nh):
            lse_ref[h] = m[h] + jnp.log(l[h])

    q_in, k_in, v_in, q_out, _, rows, krow = _pair_specs(s, blk, e, pp, 1)
    return pl.pallas_call(
        body, name="fox_fwd", grid=(b, e // (pp * LANES), s // blk),
        in_specs=[q_in, k_in, v_in, rows, krow], out_specs=[q_out, rows],
        out_shape=[jax.ShapeDtypeStruct((t, e), F32), jax.ShapeDtypeStruct((b * e // HEAD_DIM, s, 1), F32)],
        compiler_params=_params("parallel", "parallel", "arbitrary"),
    )(qkv, qkv, qkv, c, ct)


def _fox_bwd(qkv, c, ct, do, o, lse, b, s, e, blk, pp):
    scale = HEAD_DIM ** -0.5
    nh = 2 * pp
    t = b * s
    nq = s // blk

    def body(q_ref, k_ref, v_ref, c_ref, ct_ref, do_ref, o_ref, lse_ref, dq_ref, dk_ref, dv_ref, dct_ref, dk_acc, dv_acc):
        qi = pl.program_id(2)

        @pl.when(qi == 0)
        def _():
            dk_acc[...] = jnp.zeros_like(dk_acc)
            dv_acc[...] = jnp.zeros_like(dv_acc)
            dct_ref[...] = jnp.zeros_like(dct_ref)

        qm = [_half_masks((q_ref[:, p * LANES:(p + 1) * LANES] * scale).astype(BF16)) for p in range(pp)]
        dob = [do_ref[:, p * LANES:(p + 1) * LANES].astype(BF16) for p in range(pp)]
        dom = [_half_masks(dob[p]) for p in range(pp)]
        low = lax.broadcasted_iota(jnp.int32, (blk, LANES), 1) < HEAD_DIM
        delta = []
        for p in range(pp):
            prod = dob[p].astype(F32) * o_ref[:, p * LANES:(p + 1) * LANES]
            delta.append(jnp.sum(jnp.where(low, prod, 0.0), axis=1, keepdims=True))
            delta.append(jnp.sum(jnp.where(low, 0.0, prod), axis=1, keepdims=True))
        cq = [c_ref[h] for h in range(nh)]
        lse_t = [lse_ref[h] for h in range(nh)]
        row, col = _iotas(blk)
        causal = col <= row

        def block(j, dq_acc, diag):
            off = pl.multiple_of(j * blk, blk)
            kp = [k_ref[pl.ds(off, blk), p * LANES:(p + 1) * LANES] for p in range(pp)]
            zz = [_dot(qm[p], kp[p], NT) for p in range(pp)]
            dd = [_dot(dom[p], v_ref[pl.ds(off, blk), p * LANES:(p + 1) * LANES], NT) for p in range(pp)]
            z = [zz[h // 2][(h % 2) * blk:(h % 2 + 1) * blk] + (cq[h] - ct_ref[h, :, pl.ds(off, blk)]) for h in range(nh)]
            pr = [jnp.exp(z[h] - lse_t[h]) for h in range(nh)]
            if diag:
                pr = [jnp.where(causal, pr[h], 0.0) for h in range(nh)]
            ds = [pr[h] * (dd[h // 2][(h % 2) * blk:(h % 2 + 1) * blk] - delta[h]) for h in range(nh)]
            dsb = [ds[h].astype(BF16) for h in range(nh)]
            pb = [pr[h].astype(BF16) for h in range(nh)]
            for p in range(pp):
                cols = slice(p * LANES, (p + 1) * LANES)
                dk_acc[pl.ds(off, blk), cols] += _dot(jnp.concatenate([dsb[2 * p], dsb[2 * p + 1]], axis=0), qm[p], TN)
                dv_acc[pl.ds(off, blk), cols] += _dot(jnp.concatenate([pb[2 * p], pb[2 * p + 1]], axis=0), dom[p], TN)
            for h in range(nh):
                dct_ref[h, :, pl.ds(off, blk)] -= jnp.sum(ds[h], axis=0, keepdims=True)
            return tuple(dq_acc[p] + _dot(jnp.concatenate([dsb[2 * p], dsb[2 * p + 1]], axis=1), _half_masks(kp[p]), NN)
                         for p in range(pp))

        dq_acc = lax.fori_loop(0, qi, lambda j, a: block(j, a, False), (jnp.zeros((blk, LANES), F32),) * pp)
        dq_acc = block(qi, dq_acc, True)
        for p in range(pp):
            dq_ref[:, p * LANES:(p + 1) * LANES] = (dq_acc[p] * scale).astype(BF16)

        @pl.when(qi == nq - 1)
        def _():
            dk_ref[...] = dk_acc[...].astype(BF16)
            dv_ref[...] = dv_acc[...].astype(BF16)

    q_in, k_in, v_in, q_out, kv_out, rows, krow = _pair_specs(s, blk, e, pp, 1)
    w = pp * LANES
    return pl.pallas_call(
        body, name="fox_bwd", grid=(b, e // w, nq),
        in_specs=[q_in, k_in, v_in, rows, krow, q_out, q_out, rows], out_specs=[q_out, kv_out, kv_out, krow],
        out_shape=[jax.ShapeDtypeStruct((t, e), BF16)] * 3 + [jax.ShapeDtypeStruct((b * e // HEAD_DIM, 1, s), F32)],
        scratch_shapes=[pltpu.VMEM((s, w), F32), pltpu.VMEM((s, w), F32)],
        compiler_params=_params("parallel", "parallel", "arbitrary"),
    )(qkv, qkv, qkv, c, ct, do, o, lse)


def _scan_rows(f2, group, mode, d2=None):
    n = f2.shape[0]

    def body(*refs):
        f_ref, o_ref = refs[0], refs[-1]
        f = f_ref[...]
        row, col = _iotas(LANES)
        grow = lax.broadcasted_iota(jnp.int32, (n, n), 0)
        gcol = lax.broadcasted_iota(jnp.int32, (n, n), 1)
        same = (grow // group) == (gcol // group)
        e = jnp.exp(-jnp.abs(f))
        if mode == "fwd":
            x = jnp.minimum(f, 0.0) - jnp.log1p(e)
            within = (row <= col).astype(F32)
            earlier = (same & (gcol < grow)).astype(F32)
        else:
            x = refs[1][...]
            within = (row >= col).astype(F32)
            earlier = (same & (gcol > grow)).astype(F32)
        y = jnp.dot(x, within, preferred_element_type=F32, precision=lax.Precision.HIGHEST)
        tot = jnp.sum(x, axis=1, keepdims=True)
        y = y + jnp.dot(earlier, tot, preferred_element_type=F32, precision=lax.Precision.HIGHEST)
        if mode == "bwd":
            r = 1.0 / (1.0 + e)
            y = y * jnp.where(f >= 0.0, e * r, r)
        o_ref[...] = y

    args = (f2,) if mode == "fwd" else (f2, d2)
    return pl.pallas_call(body, name="logf_" + mode, out_shape=jax.ShapeDtypeStruct(f2.shape, F32),
                          compiler_params=_params())(*args)


def _matmul(a, b, dims, *, tm, tn, tk, out_dtype, name, bias=None, res=None, res_scale=1.0):
    if dims == NN:
        (m, kk), n = a.shape, b.shape[1]
        a_spec = pl.BlockSpec((tm, tk), lambda i, j, k: (i, k))
        b_spec = pl.BlockSpec((tk, tn), lambda i, j, k: (k, j))
    elif dims == NT:
        (m, kk), n = a.shape, b.shape[0]
        a_spec = pl.BlockSpec((tm, tk), lambda i, j, k: (i, k))
        b_spec = pl.BlockSpec((tn, tk), lambda i, j, k: (j, k))
    else:
        (kk, m), n = a.shape, b.shape[1]
        a_spec = pl.BlockSpec((tk, tm), lambda i, j, k: (k, i))
        b_spec = pl.BlockSpec((tk, tn), lambda i, j, k: (k, j))
    assert m % tm == 0 and n % tn == 0 and kk % tk == 0, (name, m, n, kk, tm, tn, tk)
    nk = kk // tk
    extras, extra_specs = [], []
    if bias is not None:
        extras.append(bias)
        extra_specs.append(pl.BlockSpec((1, tn), lambda i, j, k: (0, j)))
    if res is not None:
        extras.append(res)
        extra_specs.append(pl.BlockSpec((tm, tn), lambda i, j, k: (i, j)))

    def body(a_ref, b_ref, *rest):
        o_ref, acc_ref = rest[-2], rest[-1]
        k = pl.program_id(2)
        part = _dot(a_ref[...].astype(BF16), b_ref[...].astype(BF16), dims)

        @pl.when(k == 0)
        def _():
            acc_ref[...] = part

        @pl.when(k > 0)
        def _():
            acc_ref[...] += part

        @pl.when(k == nk - 1)
        def _():
            out = acc_ref[...]
            idx = 0
            if bias is not None:
                out = out + rest[idx][...]
                idx += 1
            if res is not None:
                out = out + res_scale * rest[idx][...]
            o_ref[...] = out.astype(o_ref.dtype)

    return pl.pallas_call(
        body, name=name, grid=(m // tm, n // tn, nk),
        in_specs=[a_spec, b_spec] + extra_specs,
        out_specs=pl.BlockSpec((tm, tn), lambda i, j, k: (i, j)),
        out_shape=jax.ShapeDtypeStruct((m, n), out_dtype),
        scratch_shapes=[pltpu.VMEM((tm, tn), F32)],
        compiler_params=_params("parallel", "parallel", "arbitrary"),
    )(a, b, *extras)


def _sigmoid(x):
    e = jnp.exp(-jnp.abs(x))
    r = 1.0 / (1.0 + e)
    return jnp.where(x >= 0.0, r, e * r)


def _proj_gate_fwd(o_sb, o_fx, wp_sb, wp_fx, g, tm):
    t, e = o_sb.shape
    d = wp_sb.shape[1]

    def body(osb_ref, ofx_ref, wsb_ref, wfx_ref, gsb_ref, gfx_ref, mg_ref, ysb_ref, yfx_ref):
        ysb = _dot(osb_ref[...].astype(BF16), wsb_ref[...], NN)
        yfx = _dot(ofx_ref[...].astype(BF16), wfx_ref[...], NN)
        ysb_ref[...] = ysb
        yfx_ref[...] = yfx
        mg_ref[...] = (_sigmoid(gsb_ref[...]) * ysb + _sigmoid(gfx_ref[...]) * yfx).astype(BF16)

    rows_e = pl.BlockSpec((tm, e), lambda i: (i, 0))
    rows_d = pl.BlockSpec((tm, d), lambda i: (i, 0))
    w_spec = pl.BlockSpec((e, d), lambda i: (0, 0))
    return pl.pallas_call(
        body, name="proj_gate_fwd", grid=(t // tm,),
        in_specs=[rows_e, rows_e, w_spec, w_spec, rows_d, pl.BlockSpec((tm, d), lambda i: (i, 1))],
        out_specs=[rows_d, rows_d, rows_d],
        out_shape=[jax.ShapeDtypeStruct((t, d), BF16), jax.ShapeDtypeStruct((t, d), F32), jax.ShapeDtypeStruct((t, d), F32)],
        compiler_params=_params("parallel"),
    )(o_sb, o_fx, wp_sb, wp_fx, g, g)


def _gate_bwd(dmg, y_sb, y_fx, g, tm):
    t, d = dmg.shape

    def body(dm_ref, ysb_ref, yfx_ref, gsb_ref, gfx_ref, dysb_ref, dyfx_ref, dg_ref):
        dm = dm_ref[...]
        ssb = _sigmoid(gsb_ref[...])
        sfx = _sigmoid(gfx_ref[...])
        dysb_ref[...] = (dm * ssb).astype(BF16)
        dyfx_ref[...] = (dm * sfx).astype(BF16)
        dg_ref[:, 0:d] = (dm * ysb_ref[...] * ssb * (1.0 - ssb)).astype(BF16)
        dg_ref[:, d:2 * d] = (dm * yfx_ref[...] * sfx * (1.0 - sfx)).astype(BF16)

    rows = pl.BlockSpec((tm, d), lambda i: (i, 0))
    rows1 = pl.BlockSpec((tm, d), lambda i: (i, 1))
    return pl.pallas_call(
        body, name="gate_bwd", grid=(t // tm,),
        in_specs=[rows, rows, rows, rows, rows1],
        out_specs=[rows, rows, pl.BlockSpec((tm, 2 * d), lambda i: (i, 0))],
        out_shape=[jax.ShapeDtypeStruct((t, d), BF16)] * 2 + [jax.ShapeDtypeStruct((t, 2 * d), BF16)],
        compiler_params=_params("parallel"),
    )(dmg, y_sb, y_fx, g, g)


def _mm_res_ln(a, w, xres, gamma, beta, tm, name):
    t, kk = a.shape
    d = w.shape[1]

    def body(a_ref, w_ref, x_ref, g_ref, b_ref, xn_ref, xh_ref, rs_ref):
        r = ALPHA * x_ref[...] + _dot(a_ref[...].astype(BF16), w_ref[...], NN)
        mean = jnp.mean(r, axis=1, keepdims=True)
        cen = r - mean
        rstd = lax.rsqrt(jnp.mean(cen * cen, axis=1, keepdims=True) + LN_EPS)
        xh = cen * rstd
        xh_ref[...] = xh
        xn_ref[...] = xh * g_ref[...] + b_ref[...]
        rs_ref[...] = rstd

    rows_d = pl.BlockSpec((tm, d), lambda i: (i, 0))
    vec = pl.BlockSpec((1, d), lambda i: (0, 0))
    return pl.pallas_call(
        body, name=name, grid=(t // tm,),
        in_specs=[pl.BlockSpec((tm, kk), lambda i: (i, 0)), pl.BlockSpec((kk, d), lambda i: (0, 0)), rows_d, vec, vec],
        out_specs=[rows_d, rows_d, pl.BlockSpec((tm, 1), lambda i: (i, 0))],
        out_shape=[jax.ShapeDtypeStruct((t, d), F32), jax.ShapeDtypeStruct((t, d), F32), jax.ShapeDtypeStruct((t, 1), F32)],
        compiler_params=_params("parallel"),
    )(a, w, xres, gamma, beta)


def _ln_bwd_math(dy, xh, rstd, gamma):
    dxh = dy * gamma
    m1 = jnp.mean(dxh, axis=1, keepdims=True)
    m2 = jnp.mean(dxh * xh, axis=1, keepdims=True)
    return rstd * (dxh - m1 - xh * m2)


def _rowsum8(x):
    tm, n = x.shape
    return jnp.sum(x.reshape(tm // SUBLANES, SUBLANES, n), axis=0)


def _fold8(ref):
    ref[0:1, :] = jnp.sum(ref[...], axis=0, keepdims=True)


def _loss_ln_bwd(x2, xh, rstd, gamma, target, tm):
    t, d = x2.shape

    def body(x_ref, xh_ref, rs_ref, g_ref, tg_ref, dr_ref, dg_ref, db_ref, ls_ref):
        @pl.when(pl.program_id(0) == 0)
        def _():
            dg_ref[...] = jnp.zeros_like(dg_ref)
            db_ref[...] = jnp.zeros_like(db_ref)
            ls_ref[...] = jnp.zeros_like(ls_ref)

        err = x_ref[...] - tg_ref[...]
        xh = xh_ref[...]
        dy = err * (1.0 / d)
        dr_ref[...] = _ln_bwd_math(dy, xh, rs_ref[...], g_ref[...])
        dg_ref[...] += _rowsum8(dy * xh)
        db_ref[...] += _rowsum8(dy)
        sq = _rowsum8(err * err)
        part = sq[:, 0:LANES]
        for j in range(1, d // LANES):
            part = part + sq[:, j * LANES:(j + 1) * LANES]
        ls_ref[...] += part * (0.5 / d)

        @pl.when(pl.program_id(0) == t // tm - 1)
        def _():
            _fold8(dg_ref)
            _fold8(db_ref)
            ls_ref[0:1, 0:1] = jnp.sum(jnp.sum(ls_ref[...], axis=0, keepdims=True), axis=1, keepdims=True)

    rows = pl.BlockSpec((tm, d), lambda i: (i, 0))
    acc = pl.BlockSpec((SUBLANES, d), lambda i: (0, 0))
    return pl.pallas_call(
        body, name="loss_ln_bwd", grid=(t // tm,),
        in_specs=[rows, rows, pl.BlockSpec((tm, 1), lambda i: (i, 0)), pl.BlockSpec((1, d), lambda i: (0, 0)), rows],
        out_specs=[rows, acc, acc, pl.BlockSpec((SUBLANES, LANES), lambda i: (0, 0))],
        out_shape=[jax.ShapeDtypeStruct((t, d), F32), jax.ShapeDtypeStruct((SUBLANES, d), F32),
                   jax.ShapeDtypeStruct((SUBLANES, d), F32), jax.ShapeDtypeStruct((SUBLANES, LANES), F32)],
        compiler_params=_params("arbitrary"),
    )(x2, xh, rstd, gamma, target)


def _ln_bwd(dr_next, dlin, xh, rstd, gamma, tm):
    t, d = xh.shape

    def body(dn_ref, dl_ref, xh_ref, rs_ref, g_ref, dr_ref, dg_ref, db_ref):
        @pl.when(pl.program_id(0) == 0)
        def _():
            dg_ref[...] = jnp.zeros_like(dg_ref)
            db_ref[...] = jnp.zeros_like(db_ref)

        dy = ALPHA * dn_ref[...] + dl_ref[...]
        xh = xh_ref[...]
        dr_ref[...] = _ln_bwd_math(dy, xh, rs_ref[...], g_ref[...])
        dg_ref[...] += _rowsum8(dy * xh)
        db_ref[...] += _rowsum8(dy)

        @pl.when(pl.program_id(0) == t // tm - 1)
        def _():
            _fold8(dg_ref)
            _fold8(db_ref)

    rows = pl.BlockSpec((tm, d), lambda i: (i, 0))
    acc = pl.BlockSpec((SUBLANES, d), lambda i: (0, 0))
    return pl.pallas_call(
        body, name="ln_bwd", grid=(t // tm,),
        in_specs=[rows, rows, rows, pl.BlockSpec((tm, 1), lambda i: (i, 0)), pl.BlockSpec((1, d), lambda i: (0, 0))],
        out_specs=[rows, acc, acc],
        out_shape=[jax.ShapeDtypeStruct((t, d), F32), jax.ShapeDtypeStruct((SUBLANES, d), F32),
                   jax.ShapeDtypeStruct((SUBLANES, d), F32)],
        compiler_params=_params("arbitrary"),
    )(dr_next, dlin, xh, rstd, gamma)


def _shift_rows(x, halo, shift, row):
    out = pltpu.roll(x, shift, 0)
    for r in range(shift):
        out = jnp.where(row == r, halo[SUBLANES - shift + r:SUBLANES - shift + r + 1, :], out)
    return out


def _unshift_rows(x, halo, shift, row, tm):
    out = pltpu.roll(x, tm - shift, 0)
    for r in range(shift):
        out = jnp.where(row == tm - shift + r, halo[r:r + 1, :], out)
    return out


def _conv_pre(ug_ref, halo_ref, wc_ref, bc_ref, first, tm):
    ug = ug_ref[...]
    halo = jnp.where(first, 0.0, halo_ref[...])
    row = lax.broadcasted_iota(jnp.int32, ug.shape, 0)
    wc = wc_ref[...]
    um1 = _shift_rows(ug, halo, 1, row)
    um2 = _shift_rows(ug, halo, 2, row)
    c = bc_ref[...] + wc[2:3, :] * ug + wc[1:2, :] * um1 + wc[0:1, :] * um2
    return c, ug, um1, um2


INV_SQRT2 = 1.0 / math.sqrt(2.0)
INV_SQRT2PI = 1.0 / math.sqrt(2.0 * math.pi)


def _conv_glu_fwd(u, wc, bc, seq, tm):
    t, f2 = u.shape
    f = f2 // 2
    per_seq = seq // tm
    hb = tm // SUBLANES

    def body(ug_ref, halo_ref, uv_ref, wc_ref, bc_ref, a_ref):
        first = (pl.program_id(0) % per_seq) == 0
        c, _, _, _ = _conv_pre(ug_ref, halo_ref, wc_ref, bc_ref, first, tm)
        gelu = 0.5 * c * (1.0 + lax.erf(c * INV_SQRT2))
        a_ref[...] = (gelu * uv_ref[...]).astype(BF16)

    return pl.pallas_call(
        body, name="conv_glu_fwd", grid=(t // tm,),
        in_specs=[pl.BlockSpec((tm, f), lambda i: (i, 0)),
                  pl.BlockSpec((SUBLANES, f), lambda i: (jnp.maximum(i * hb - 1, 0), 0)),
                  pl.BlockSpec((tm, f), lambda i: (i, 1)),
                  pl.BlockSpec((3, f), lambda i: (0, 0)), pl.BlockSpec((1, f), lambda i: (0, 0))],
        out_specs=pl.BlockSpec((tm, f), lambda i: (i, 0)),
        out_shape=jax.ShapeDtypeStruct((t, f), BF16),
        compiler_params=_params("parallel"),
    )(u, u, u, wc, bc)


def _conv_glu_bwd1(u, da, wc, bc, seq, tm):
    t, f2 = u.shape
    f = f2 // 2
    per_seq = seq // tm
    hb = tm // SUBLANES

    def body(ug_ref, halo_ref, uv_ref, da_ref, wc_ref, bc_ref, dc_ref, duv_ref):
        first = (pl.program_id(0) % per_seq) == 0
        c, _, _, _ = _conv_pre(ug_ref, halo_ref, wc_ref, bc_ref, first, tm)
        cdf = 0.5 * (1.0 + lax.erf(c * INV_SQRT2))
        pdf = jnp.exp(-0.5 * c * c) * INV_SQRT2PI
        da = da_ref[...]
        duv_ref[...] = (da * (c * cdf)).astype(BF16)
        dc_ref[...] = da * uv_ref[...] * (cdf + c * pdf)

    rows = pl.BlockSpec((tm, f), lambda i: (i, 0))
    return pl.pallas_call(
        body, name="conv_glu_bwd1", grid=(t // tm,),
        in_specs=[rows, pl.BlockSpec((SUBLANES, f), lambda i: (jnp.maximum(i * hb - 1, 0), 0)),
                  pl.BlockSpec((tm, f), lambda i: (i, 1)), rows,
                  pl.BlockSpec((3, f), lambda i: (0, 0)), pl.BlockSpec((1, f), lambda i: (0, 0))],
        out_specs=[rows, rows],
        out_shape=[jax.ShapeDtypeStruct((t, f), F32), jax.ShapeDtypeStruct((t, f), BF16)],
        compiler_params=_params("parallel"),
    )(u, u, u, da, wc, bc)


def _conv_glu_bwd2(u, dc, wc, seq, tm):
    t, f2 = u.shape
    f = f2 // 2
    per_seq = seq // tm
    hb = tm // SUBLANES
    nblk = t // SUBLANES

    def body(ug_ref, halo_ref, dc_ref, nxt_ref, wc_ref, dug_ref, w0_ref, w1_ref, w2_ref, b_ref):
        i = pl.program_id(0)

        @pl.when(i == 0)
        def _():
            for r in (w0_ref, w1_ref, w2_ref, b_ref):
                r[...] = jnp.zeros_like(r)

        first = (i % per_seq) == 0
        last = (i % per_seq) == per_seq - 1
        ug = ug_ref[...]
        halo = jnp.where(first, 0.0, halo_ref[...])
        nxt = jnp.where(last, 0.0, nxt_ref[...])
        row = lax.broadcasted_iota(jnp.int32, ug.shape, 0)
        dc = dc_ref[...]
        wc = wc_ref[...]
        dp1 = _unshift_rows(dc, nxt, 1, row, tm)
        dp2 = _unshift_rows(dc, nxt, 2, row, tm)
        dug_ref[...] = (wc[2:3, :] * dc + wc[1:2, :] * dp1 + wc[0:1, :] * dp2).astype(BF16)
        w2_ref[...] += _rowsum8(dc * ug)
        w1_ref[...] += _rowsum8(dc * _shift_rows(ug, halo, 1, row))
        w0_ref[...] += _rowsum8(dc * _shift_rows(ug, halo, 2, row))
        b_ref[...] += _rowsum8(dc)

        @pl.when(i == t // tm - 1)
        def _():
            for r in (w0_ref, w1_ref, w2_ref, b_ref):
                _fold8(r)

    rows = pl.BlockSpec((tm, f), lambda i: (i, 0))
    acc = pl.BlockSpec((SUBLANES, f), lambda i: (0, 0))
    return pl.pallas_call(
        body, name="conv_glu_bwd2", grid=(t // tm,),
        in_specs=[rows, pl.BlockSpec((SUBLANES, f), lambda i: (jnp.maximum(i * hb - 1, 0), 0)),
                  rows, pl.BlockSpec((SUBLANES, f), lambda i: (jnp.minimum((i + 1) * hb, nblk - 1), 0)),
                  pl.BlockSpec((3, f), lambda i: (0, 0))],
        out_specs=[rows, acc, acc, acc, acc],
        out_shape=[jax.ShapeDtypeStruct((t, f), BF16)] + [jax.ShapeDtypeStruct((SUBLANES, f), F32)] * 4,
        compiler_params=_params("arbitrary"),
    )(u, u, dc, dc, wc)


def _colsum(x, tm, name):
    t, n = x.shape

    def body(x_ref, o_ref):
        @pl.when(pl.program_id(0) == 0)
        def _():
            o_ref[...] = jnp.zeros_like(o_ref)

        o_ref[...] += _rowsum8(x_ref[...].astype(F32))

        @pl.when(pl.program_id(0) == t // tm - 1)
        def _():
            _fold8(o_ref)

    return pl.pallas_call(
        body, name=name, grid=(t // tm,),
        in_specs=[pl.BlockSpec((tm, n), lambda i: (i, 0))],
        out_specs=pl.BlockSpec((SUBLANES, n), lambda i: (0, 0)),
        out_shape=jax.ShapeDtypeStruct((SUBLANES, n), F32),
        compiler_params=_params("arbitrary"),
    )(x)


def _adamw(w, gparts, m, v, name):
    p, r, c = gparts.shape
    tr = r
    for cand in (512, 256, 128, 64, 32, 16, 8):
        if cand * p <= 1024 and r % cand == 0 and r > cand:
            tr = cand
            break
    c1 = 1.0 - ADAM_B1 ** ADAM_STEP
    c2 = 1.0 - ADAM_B2 ** ADAM_STEP

    def body(w_ref, g_ref, m_ref, v_ref, go_ref, d_ref, mo_ref, vo_ref):
        g = g_ref[0].astype(F32)
        for i in range(1, p):
            g = g + g_ref[i].astype(F32)
        mn = ADAM_B1 * m_ref[...] + (1.0 - ADAM_B1) * g
        vn = ADAM_B2 * v_ref[...] + (1.0 - ADAM_B2) * (g * g)
        go_ref[...] = g
        mo_ref[...] = mn
        vo_ref[...] = vn
        d_ref[...] = -ADAM_LR * ((mn / c1) / (jnp.sqrt(vn / c2) + ADAM_EPS) + ADAM_WD * w_ref[...])

    blk = pl.BlockSpec((tr, c), lambda i: (i, 0))
    return pl.pallas_call(
        body, name=name, grid=(r // tr,),
        in_specs=[blk, pl.BlockSpec((p, tr, c), lambda i: (0, i, 0)), blk, blk],
        out_specs=[blk] * 4,
        out_shape=[jax.ShapeDtypeStruct((r, c), F32)] * 4,
        compiler_params=_params("parallel"),
    )(w, gparts, m, v)


MESH = pl.DeviceIdType.MESH
ANY = pl.BlockSpec(memory_space=pl.ANY)


def _all_gather(xs, name):
    n = len(xs)

    def body(*refs):
        x_refs, out_refs = refs[:n], refs[n:2 * n]
        send_sems, recv_sems, local_sems = refs[2 * n:]
        x, y, c = lax.axis_index("x"), lax.axis_index("y"), lax.axis_index("c")
        me, sibling = (x, y, c), (x, y, 1 - c)
        chips = [(1 - x, y), (x, 1 - y), (1 - x, 1 - y)]

        def slot(a, px, py, pc):
            return out_refs[a].at[4 * px + 2 * py + pc]

        def copy(a, k, block, to, src=None):
            return pltpu.make_async_remote_copy(
                src_ref=slot(a, *block) if src is None else src, dst_ref=slot(a, *block),
                send_sem=send_sems.at[k * n + a], recv_sem=recv_sems.at[k * n + a], device_id=to, device_id_type=MESH)

        arrays = range(n)
        mine = [pltpu.make_async_copy(x_refs[a], slot(a, *me), local_sems.at[a]) for a in arrays]
        first = [copy(a, 0, me, sibling, src=x_refs[a]) for a in arrays]
        first += [copy(a, 1 + j, me, (*chip, c), src=x_refs[a]) for j, chip in enumerate(chips) for a in arrays]
        for cp in mine + first:
            cp.start()
        passed = []
        for j, chip in enumerate(chips):
            for a in arrays:
                copy(a, 1 + j, (*chip, c), me).wait_recv()
                passed.append(copy(a, 4 + j, (*chip, c), sibling))
                passed[-1].start()
        for a in arrays:
            copy(a, 0, sibling, me).wait_recv()
        for j, chip in enumerate(chips):
            for a in arrays:
                copy(a, 4 + j, (*chip, 1 - c), me).wait_recv()
        for cp in first + passed:
            cp.wait_send()
        for cp in mine:
            cp.wait()

    return pl.pallas_call(
        body, name=name,
        out_shape=[jax.ShapeDtypeStruct((N_DEV,) + x.shape, x.dtype) for x in xs],
        in_specs=[ANY] * n, out_specs=[ANY] * n,
        scratch_shapes=[pltpu.SemaphoreType.DMA((7 * n,)), pltpu.SemaphoreType.DMA((7 * n,)),
                        pltpu.SemaphoreType.DMA((n,))],
    )(*xs)


def _exchange(gs, name):
    n = len(gs)

    def body(*refs):
        g_refs, land_refs = refs[:n], refs[n:2 * n]
        send_sems, recv_sems, local_sems = refs[2 * n:]
        x, y, c = lax.axis_index("x"), lax.axis_index("y"), lax.axis_index("c")
        mine = 4 * x + 2 * y + c
        own = [pltpu.make_async_copy(g_refs[a].at[mine], land_refs[a].at[mine], local_sems.at[a]) for a in range(n)]
        copies = []
        for k in range(1, N_DEV):
            px = 1 - x if k & 4 else x
            py = 1 - y if k & 2 else y
            pc = 1 - c if k & 1 else c
            for a in range(n):
                copies.append(pltpu.make_async_remote_copy(
                    src_ref=g_refs[a].at[4 * px + 2 * py + pc], dst_ref=land_refs[a].at[mine],
                    send_sem=send_sems.at[(k - 1) * n + a], recv_sem=recv_sems.at[(k - 1) * n + a],
                    device_id=(px, py, pc), device_id_type=MESH))
        for cp in own + copies:
            cp.start()
        for cp in copies + own:
            cp.wait()

    return pl.pallas_call(
        body, name=name, out_shape=[jax.ShapeDtypeStruct(g.shape, g.dtype) for g in gs],
        in_specs=[ANY] * n, out_specs=[ANY] * n,
        scratch_shapes=[pltpu.SemaphoreType.DMA((7 * n,)), pltpu.SemaphoreType.DMA((7 * n,)),
                        pltpu.SemaphoreType.DMA((n,))],
    )(*gs)


def _tile(n, pref, unit=LANES):
    if n <= pref:
        return n
    best = None
    for cand in range(unit, pref + 1, unit):
        if n % cand == 0:
            best = cand
    assert best is not None, (n, pref, unit)
    return best


def _layer_step(x, target, w, attn_blk):
    b, s, d = x.shape
    t = b * s
    e = w["wp_sb"].shape[0]
    h = e // HEAD_DIM
    f = w["w_down"].shape[0]
    x2 = x.reshape(t, d)
    tg = target.reshape(t, d)
    pp = 2 if (e // LANES) % 2 == 0 else 1
    tm = _tile(t, 512, SUBLANES)
    tmc = _tile(s, 256, SUBLANES)
    tkt = _tile(t, 512, SUBLANES)
    td = _tile(d, 1024)
    tf = _tile(f, 1408)
    t2f = _tile(2 * f, 1408)
    tqkv = _tile(6 * e, 1024)
    tg2 = _tile(2 * d, 1024)

    qkv = _matmul(x2, w["w_qkv"], NN, tm=tm, tn=tqkv, tk=d, out_dtype=BF16, name="in_qkv", bias=w["b_qkv"])
    gate = _matmul(x2, w["w_g"], NN, tm=tm, tn=tg2, tk=d, out_dtype=F32, name="in_gate", bias=w["b_g"])
    fl = _matmul(x2, w["w_f"], NN, tm=tm, tn=LANES, tk=d, out_dtype=F32, name="in_forget", bias=w["b_f"])
    nr = s // LANES
    f2 = fl.reshape(b, s, LANES)[:, :, :h].transpose(0, 2, 1).reshape(b * h * nr, LANES)
    c2 = _scan_rows(f2, nr, "fwd")
    c = c2.reshape(b * h, s, 1)
    ct = c2.reshape(b * h, 1, s)
    o_sb, tot, first = _sb_fwd(qkv, b, s, e, attn_blk, pp)
    o_fx, lse = _fox_fwd(qkv, c, ct, b, s, e, attn_blk, pp)
    merged, y_sb, y_fx = _proj_gate_fwd(o_sb, o_fx, w["wp_sb"], w["wp_fx"], gate, tm)
    x1, xh1, rs1 = _mm_res_ln(merged, w["w_out"], x2, w["ln1_g"], w["ln1_b"], tm, "out_ln1")
    u = _matmul(x1, w["w_up"], NN, tm=tm, tn=t2f, tk=d, out_dtype=F32, name="ffn_up")
    act = _conv_glu_fwd(u, w["w_conv"], w["b_conv"], s, tmc)
    xo, xh2, rs2 = _mm_res_ln(act, w["w_down"], x1, w["ln2_g"], w["ln2_b"], tm, "down_ln2")

    gr = {}
    dr2, dg2, db2, ls = _loss_ln_bwd(xo, xh2, rs2, w["ln2_g"], tg, tm)
    gr["ln2_g"], gr["ln2_b"] = dg2[0:1], db2[0:1]
    da = _matmul(dr2, w["w_down"], NT, tm=tm, tn=tf, tk=d, out_dtype=F32, name="d_act")
    gr["w_down"] = _matmul(act, dr2, TN, tm=tf, tn=td, tk=tkt, out_dtype=F32, name="dw_down")
    dc, du_v = _conv_glu_bwd1(u, da, w["w_conv"], w["b_conv"], s, tmc)
    du_g, gw0, gw1, gw2, gbc = _conv_glu_bwd2(u, dc, w["w_conv"], s, tmc)
    gr["w_conv"] = jnp.concatenate([gw0[0:1], gw1[0:1], gw2[0:1]], axis=0)
    gr["b_conv"] = gbc[0:1]
    du = jnp.concatenate([du_g, du_v], axis=1)
    dlin1 = _matmul(du, w["w_up"], NT, tm=tm, tn=td, tk=t2f, out_dtype=F32, name="d_x1")
    gr["w_up"] = _matmul(x1, du, TN, tm=td, tn=t2f, tk=tkt, out_dtype=F32, name="dw_up")
    dr1, dg1, db1 = _ln_bwd(dr2, dlin1, xh1, rs1, w["ln1_g"], tm)
    gr["ln1_g"], gr["ln1_b"] = dg1[0:1], db1[0:1]
    dmg = _matmul(dr1, w["w_out"], NT, tm=tm, tn=td, tk=td, out_dtype=F32, name="d_merged")
    gr["w_out"] = _matmul(merged, dr1, TN, tm=td, tn=td, tk=tkt, out_dtype=F32, name="dw_out")
    dy_sb, dy_fx, dgate = _gate_bwd(dmg, y_sb, y_fx, gate, tm)
    do_sb = _matmul(dy_sb, w["wp_sb"], NT, tm=tm, tn=e, tk=td, out_dtype=BF16, name="d_o_sb")
    do_fx = _matmul(dy_fx, w["wp_fx"], NT, tm=tm, tn=e, tk=td, out_dtype=BF16, name="d_o_fx")
    gr["wp_sb"] = _matmul(o_sb, dy_sb, TN, tm=e, tn=td, tk=tkt, out_dtype=F32, name="dwp_sb")
    gr["wp_fx"] = _matmul(o_fx, dy_fx, TN, tm=e, tn=td, tk=tkt, out_dtype=F32, name="dwp_fx")
    dq_sb, dk_sb, dv_sb = _sb_bwd(qkv, do_sb, tot, first, b, s, e, attn_blk, pp)
    dq_fx, dk_fx, dv_fx, dct = _fox_bwd(qkv, c, ct, do_fx, o_fx, lse, b, s, e, attn_blk, pp)
    dqkv = jnp.concatenate([dq_sb, dk_sb, dv_sb, dq_fx, dk_fx, dv_fx], axis=1)
    df2 = _scan_rows(f2, nr, "bwd", dct.reshape(b * h * nr, LANES))
    df = jnp.pad(df2.reshape(b, h, s).transpose(0, 2, 1).reshape(t, h), ((0, 0), (0, LANES - h))).astype(BF16)
    dx = _matmul(dqkv, w["w_qkv"], NT, tm=tm, tn=td, tk=tqkv, out_dtype=F32, name="dx_qkv", res=dr1, res_scale=ALPHA)
    dx = _matmul(dgate, w["w_g"], NT, tm=tm, tn=td, tk=tg2, out_dtype=F32, name="dx_gate", res=dx)
    dx = _matmul(df, w["w_f"], NT, tm=tm, tn=td, tk=LANES, out_dtype=F32, name="dx_forget", res=dx)
    gr["w_qkv"] = _matmul(x2, dqkv, TN, tm=td, tn=tqkv, tk=tkt, out_dtype=F32, name="dw_qkv")
    gr["w_g"] = _matmul(x2, dgate, TN, tm=td, tn=tg2, tk=tkt, out_dtype=F32, name="dw_gate")
    gr["w_f"] = _matmul(x2, df, TN, tm=td, tn=LANES, tk=tkt, out_dtype=F32, name="dw_forget")
    gr["b_qkv"] = _colsum(dqkv, tm, "db_qkv")[0:1]
    gr["b_g"] = _colsum(dgate, tm, "db_gate")[0:1]
    gr["b_f"] = _colsum(df, tm, "db_forget")[0:1]
    return ls[0:1, 0:1], dx.reshape(b, s, d), gr


SHARDED = ("w_in", "w_proj_sb", "w_proj_fox", "w_out", "w_up", "w_conv", "w_down")
ROW_SHARDED = ("w_out", "w_down")
REPLICATED = ("b_in", "ln1_g", "ln1_b", "b_conv", "ln2_g", "ln2_b")
WEIGHTS = ("w_in", "b_in", "w_proj_sb", "w_proj_fox", "w_out", "ln1_g", "ln1_b", "w_up", "w_conv", "b_conv",
           "w_down", "ln2_g", "ln2_b")
MATMUL_OPERANDS = ("w_in", "w_proj_sb", "w_proj_fox", "w_out", "w_up", "w_down")


def _cut(full, name):
    r, c = full.shape
    if name in ROW_SHARDED:
        return full.reshape(N_DEV, r // N_DEV, c)
    cs = c // N_DEV
    return jnp.stack([full[:, j * cs:(j + 1) * cs] for j in range(N_DEV)], axis=0)


def _join(blocks, name):
    p, r, c = blocks.shape
    if name in ROW_SHARDED:
        return blocks.reshape(p * r, c)
    return jnp.concatenate([blocks[j] for j in range(p)], axis=1)


def kernel(x, w_in, b_in, w_proj_sb, w_proj_fox, w_out, ln1_g, ln1_b, w_up, w_conv, b_conv, w_down, ln2_g, ln2_b, loss_target, m_w_in, m_b_in, m_w_proj_sb, m_w_proj_fox, m_w_out, m_ln1_g, m_ln1_b, m_w_up, m_w_conv, m_b_conv, m_w_down, m_ln2_g, m_ln2_b, v_w_in, v_b_in, v_w_proj_sb, v_w_proj_fox, v_w_out, v_ln1_g, v_ln1_b, v_w_up, v_w_conv, v_b_conv, v_w_down, v_ln2_g, v_ln2_b):
    wts = dict(w_in=w_in, b_in=b_in, w_proj_sb=w_proj_sb, w_proj_fox=w_proj_fox, w_out=w_out, ln1_g=ln1_g, ln1_b=ln1_b,
               w_up=w_up, w_conv=w_conv, b_conv=b_conv, w_down=w_down, ln2_g=ln2_g, ln2_b=ln2_b)
    mom = dict(w_in=m_w_in, b_in=m_b_in, w_proj_sb=m_w_proj_sb, w_proj_fox=m_w_proj_fox, w_out=m_w_out, ln1_g=m_ln1_g,
               ln1_b=m_ln1_b, w_up=m_w_up, w_conv=m_w_conv, b_conv=m_b_conv, w_down=m_w_down, ln2_g=m_ln2_g, ln2_b=m_ln2_b)
    var = dict(w_in=v_w_in, b_in=v_b_in, w_proj_sb=v_w_proj_sb, w_proj_fox=v_w_proj_fox, w_out=v_w_out, ln1_g=v_ln1_g,
               ln1_b=v_ln1_b, w_up=v_w_up, w_conv=v_w_conv, b_conv=v_b_conv, w_down=v_w_down, ln2_g=v_ln2_g, ln2_b=v_ln2_b)
    shard = {n: wts[n].reshape(wts[n].shape[-2:]) for n in WEIGHTS}

    gathered = _all_gather([shard[n].astype(BF16) if n in MATMUL_OPERANDS else shard[n] for n in SHARDED],
                           "gather_weights")
    full = {n: _join(g, n) for n, g in zip(SHARDED, gathered)}
    e = full["w_proj_sb"].shape[0]
    h = e // HEAD_DIM
    d = full["w_out"].shape[0]
    nq = 6 * e

    def cut_in(a, pad):
        fcols = a[:, nq:nq + h]
        if pad:
            fcols = jnp.pad(fcols, ((0, 0), (0, LANES - h)))
        return a[:, :nq], a[:, nq + h:], fcols

    w_qkv, w_g, w_f = cut_in(full["w_in"], True)
    b_qkv, b_g, b_f = cut_in(shard["b_in"], True)
    w = dict(w_qkv=w_qkv, w_g=w_g, w_f=w_f, b_qkv=b_qkv, b_g=b_g, b_f=b_f,
             wp_sb=full["w_proj_sb"], wp_fx=full["w_proj_fox"], w_out=full["w_out"], w_up=full["w_up"],
             w_down=full["w_down"], w_conv=full["w_conv"], b_conv=shard["b_conv"], ln1_g=shard["ln1_g"], ln1_b=shard["ln1_b"],
             ln2_g=shard["ln2_g"], ln2_b=shard["ln2_b"])

    loss_local, grad_x, gr = _layer_step(x, loss_target, w, min(256, x.shape[1]))
    loss = lax.psum(loss_local[0, 0], ("x", "y", "c"))

    local = dict(
        w_in=jnp.concatenate([gr["w_qkv"], gr["w_f"][:, :h], gr["w_g"]], axis=1),
        b_in=jnp.concatenate([gr["b_qkv"], gr["b_f"][:, :h], gr["b_g"]], axis=1),
        w_proj_sb=gr["wp_sb"], w_proj_fox=gr["wp_fx"], w_out=gr["w_out"], w_up=gr["w_up"], w_conv=gr["w_conv"],
        w_down=gr["w_down"], ln1_g=gr["ln1_g"], ln1_b=gr["ln1_b"], b_conv=gr["b_conv"], ln2_g=gr["ln2_g"],
        ln2_b=gr["ln2_b"])

    landed = _exchange([_cut(local[n], n).astype(BF16 if n in MATMUL_OPERANDS else F32) for n in SHARDED],
                       "exchange_grads")
    gsum = dict(zip(SHARDED, landed))

    parts = _all_gather([jnp.concatenate([local[n] for n in REPLICATED], axis=1)], "gather_small_grads")[0]
    off = 0
    for n in REPLICATED:
        gsum[n] = parts[:, :, off:off + shard[n].size]
        off += shard[n].size

    grads, deltas, new_m, new_v = [], [], [], []
    for n in WEIGHTS:
        shp = wts[n].shape
        g, dl, mn, vn = _adamw(shard[n], gsum[n], mom[n].reshape(shard[n].shape), var[n].reshape(shard[n].shape),
                               "adamw_" + n)
        grads.append(g.reshape(shp))
        deltas.append(dl.reshape(shp))
        new_m.append(mn.reshape(shp))
        new_v.append(vn.reshape(shp))
    return (loss, grad_x, *grads, *deltas, *new_m, *new_v)
```

```python
import functools
import math

import jax
import jax.numpy as jnp
from jax import lax
from jax.experimental import pallas as pl
from jax.experimental.pallas import tpu as pltpu

F32 = jnp.float32
BF16 = jnp.bfloat16

HEAD_DIM = 64
LN_EPS = 1e-5
DEPTH = 1
ALPHA = (2.0 * DEPTH) ** 0.25
ADAM_LR, ADAM_B1, ADAM_B2, ADAM_EPS, ADAM_WD, ADAM_STEP = 0.001, 0.9, 0.999, 1e-08, 0.01, 10
N_DEV = 8
LANES = 128
SUBLANES = 8
HALO = 16
VMEM_LIMIT = 56 * 1024 * 1024

NN = ((1,), (0,))
NT = ((1,), (1,))
TN = ((0,), (0,))


def _dot(a, b, dims):
    return lax.dot_general(a, b, (dims, ((), ())), preferred_element_type=F32)


def _params(*sem):
    return pltpu.CompilerParams(dimension_semantics=sem, vmem_limit_bytes=VMEM_LIMIT)


def _iotas(blk):
    row = lax.broadcasted_iota(jnp.int32, (blk, blk), 0)
    col = lax.broadcasted_iota(jnp.int32, (blk, blk), 1)
    return row, col


def _sb_terms(z):
    e = jnp.exp(-jnp.abs(z))
    lb = jnp.minimum(z, 0.0) - jnp.log(1.0 + e)
    return lb, lb - z, e


def _pair_specs(s, blk, e, pp, branch):
    w = pp * LANES
    nq = s // blk
    ng = e // w
    base = 3 * branch * ng
    q_in = pl.BlockSpec((blk, w), lambda b, g, i: (b * nq + i, base + g))
    k_in = pl.BlockSpec((s, w), lambda b, g, i: (b, base + ng + g))
    v_in = pl.BlockSpec((s, w), lambda b, g, i: (b, base + 2 * ng + g))
    q_out = pl.BlockSpec((blk, w), lambda b, g, i: (b * nq + i, g))
    kv_out = pl.BlockSpec((s, w), lambda b, g, i: (b, g))
    rows = pl.BlockSpec((2 * pp, blk, 1), lambda b, g, i: (b * ng + g, i, 0))
    krow = pl.BlockSpec((2 * pp, 1, s), lambda b, g, i: (b * ng + g, 0, 0))
    return q_in, k_in, v_in, q_out, kv_out, rows, krow


def _half_masks(x):
    low = lax.broadcasted_iota(jnp.int32, x.shape, 1) < HEAD_DIM
    zero = jnp.zeros_like(x)
    return jnp.concatenate([jnp.where(low, x, zero), jnp.where(low, zero, x)], axis=0)


def _tri_sums(xs, tri):
    hi = [x.astype(BF16) for x in xs]
    lo = [(x - h.astype(F32)).astype(BF16) for x, h in zip(xs, hi)]
    n = len(xs)
    blk = xs[0].shape[0]
    r = _dot(jnp.concatenate(hi + lo, axis=0), tri, NN)
    return [r[i * blk:(i + 1) * blk] + r[(n + i) * blk:(n + i + 1) * blk] for i in range(n)]


def _sb_fwd(qkv, b, s, e, blk, pp):
    scale = HEAD_DIM ** -0.5
    nh = 2 * pp
    t = b * s

    def body(q_ref, k_ref, v_ref, o_ref, tot_ref, first_ref):
        qi = pl.program_id(2)
        qm = [_half_masks((q_ref[:, p * LANES:(p + 1) * LANES] * scale).astype(BF16)) for p in range(pp)]
        row, col = _iotas(blk)
        strict = col < row
        after = (row > col).astype(BF16)

        def block(j, carry, diag):
            off = pl.multiple_of(j * blk, blk)
            o_acc, run = carry
            zz = [_dot(qm[p], k_ref[pl.ds(off, blk), p * LANES:(p + 1) * LANES], NT) for p in range(pp)]
            z = [zz[h // 2][(h % 2) * blk:(h % 2 + 1) * blk] for h in range(nh)]
            terms = [_sb_terms(z[h]) for h in range(nh)]
            lom = [jnp.where(strict, terms[h][1], 0.0) if diag else terms[h][1] for h in range(nh)]
            sfx = _tri_sums(lom, after)
            a = [jnp.exp(terms[h][0] + sfx[h] + run[h]) for h in range(nh)]
            if diag:
                a = [jnp.where(strict, a[h], 0.0) for h in range(nh)]
            ab = [a[h].astype(BF16) for h in range(nh)]
            o_new = tuple(
                o_acc[p] + _dot(jnp.concatenate([ab[2 * p], ab[2 * p + 1]], axis=1),
                                _half_masks(v_ref[pl.ds(off, blk), p * LANES:(p + 1) * LANES]), NN)
                for p in range(pp))
            return o_new, tuple(run[h] + sfx[h][:, 0:1] + lom[h][:, 0:1] for h in range(nh))

        def alive(run):
            m = run[0]
            for h in range(1, nh):
                m = jnp.maximum(m, run[h])
            return jnp.max(m) > DEAD

        o_acc, run = block(qi, ((jnp.zeros((blk, LANES), F32),) * pp, (jnp.zeros((blk, 1), F32),) * nh), True)

        def step(c):
            j, _, o_acc, run = c
            o_acc, run = block(j, (o_acc, run), False)
            return j - 1, alive(run), o_acc, run

        j, _, o_acc, run = lax.while_loop(lambda c: jnp.logical_and(c[0] >= 0, c[1]), step,
                                          (qi - 1, alive(run), o_acc, run))
        for p in range(pp):
            o_ref[:, p * LANES:(p + 1) * LANES] = o_acc[p].astype(o_ref.dtype)
        for h in range(nh):
            tot_ref[h] = run[h]
            first_ref[h] = jnp.zeros((blk, 1), F32) + (j + 1).astype(F32)

    q_in, k_in, v_in, q_out, _, rows, _ = _pair_specs(s, blk, e, pp, 0)
    return pl.pallas_call(
        body, name="sb_fwd", grid=(b, e // (pp * LANES), s // blk),
        in_specs=[q_in, k_in, v_in], out_specs=[q_out, rows, rows],
        out_shape=[jax.ShapeDtypeStruct((t, e), BF16)] + [jax.ShapeDtypeStruct((b * e // HEAD_DIM, s, 1), F32)] * 2,
        compiler_params=_params("parallel", "parallel", "arbitrary"),
    )(qkv, qkv, qkv)


def _sb_bwd(qkv, do, tot, first, b, s, e, blk, pp):
    scale = HEAD_DIM ** -0.5
    nh = 2 * pp
    t = b * s
    nq = s // blk

    def body(q_ref, k_ref, v_ref, do_ref, tot_ref, first_ref, dq_ref, dk_ref, dv_ref, dk_acc, dv_acc):
        qi = pl.program_id(2)

        @pl.when(qi == 0)
        def _():
            dk_acc[...] = jnp.zeros_like(dk_acc)
            dv_acc[...] = jnp.zeros_like(dv_acc)

        qm = [_half_masks((q_ref[:, p * LANES:(p + 1) * LANES] * scale).astype(BF16)) for p in range(pp)]
        dom = [_half_masks(do_ref[:, p * LANES:(p + 1) * LANES].astype(BF16)) for p in range(pp)]
        tot_t = [tot_ref[h] for h in range(nh)]
        row, col = _iotas(blk)
        strict = col < row
        upto = (row <= col).astype(BF16)
        before = (row < col).astype(BF16)

        def block(j, carry, diag):
            off = pl.multiple_of(j * blk, blk)
            dq_acc, cl, cg = carry
            kp = [k_ref[pl.ds(off, blk), p * LANES:(p + 1) * LANES] for p in range(pp)]
            vp = [v_ref[pl.ds(off, blk), p * LANES:(p + 1) * LANES] for p in range(pp)]
            zz = [_dot(qm[p], kp[p], NT) for p in range(pp)]
            dd = [_dot(dom[p], vp[p], NT) for p in range(pp)]
            z = [zz[h // 2][(h % 2) * blk:(h % 2 + 1) * blk] for h in range(nh)]
            da = [dd[h // 2][(h % 2) * blk:(h % 2 + 1) * blk] for h in range(nh)]
            terms = [_sb_terms(z[h]) for h in range(nh)]
            lom = [jnp.where(strict, terms[h][1], 0.0) if diag else terms[h][1] for h in range(nh)]
            pre = _tri_sums(lom, upto)
            a = [jnp.exp(terms[h][0] + (tot_t[h] - cl[h] - pre[h])) for h in range(nh)]
            if diag:
                a = [jnp.where(strict, a[h], 0.0) for h in range(nh)]
            g = [a[h] * da[h] for h in range(nh)]
            pw = _tri_sums(g, before)
            dzb = []
            for h in range(nh):
                ex = terms[h][2]
                r = 1.0 / (1.0 + ex)
                er = ex * r
                pos = z[h] >= 0.0
                dz = g[h] * jnp.where(pos, er, r) - (cg[h] + pw[h]) * jnp.where(pos, r, er)
                if diag:
                    dz = jnp.where(strict, dz, 0.0)
                dzb.append(dz.astype(BF16))
            ab = [a[h].astype(BF16) for h in range(nh)]
            for p in range(pp):
                cols = slice(p * LANES, (p + 1) * LANES)
                dk_acc[pl.ds(off, blk), cols] += _dot(jnp.concatenate([dzb[2 * p], dzb[2 * p + 1]], axis=0), qm[p], TN)
                dv_acc[pl.ds(off, blk), cols] += _dot(jnp.concatenate([ab[2 * p], ab[2 * p + 1]], axis=0), dom[p], TN)
            dq_new = tuple(dq_acc[p] + _dot(jnp.concatenate([dzb[2 * p], dzb[2 * p + 1]], axis=1), _half_masks(kp[p]), NN)
                           for p in range(pp))
            return (dq_new, tuple(cl[h] + pre[h][:, blk - 1:blk] for h in range(nh)),
                    tuple(cg[h] + pw[h][:, blk - 1:blk] + g[h][:, blk - 1:blk] for h in range(nh)))

        zero1 = (jnp.zeros((blk, 1), F32),) * nh
        j0 = jnp.clip(jnp.max(first_ref[0]).astype(jnp.int32), 0, qi)
        carry = lax.fori_loop(j0, qi, lambda j, c: block(j, c, False), ((jnp.zeros((blk, LANES), F32),) * pp, zero1, zero1))
        dq_acc, _, _ = block(qi, carry, True)
        for p in range(pp):
            dq_ref[:, p * LANES:(p + 1) * LANES] = (dq_acc[p] * scale).astype(BF16)

        @pl.when(qi == nq - 1)
        def _():
            dk_ref[...] = dk_acc[...].astype(BF16)
            dv_ref[...] = dv_acc[...].astype(BF16)

    q_in, k_in, v_in, q_out, kv_out, rows, _ = _pair_specs(s, blk, e, pp, 0)
    w = pp * LANES
    return pl.pallas_call(
        body, name="sb_bwd", grid=(b, e // w, nq),
        in_specs=[q_in, k_in, v_in, q_out, rows, rows], out_specs=[q_out, kv_out, kv_out],
        out_shape=[jax.ShapeDtypeStruct((t, e), BF16)] * 3,
        scratch_shapes=[pltpu.VMEM((s, w), F32), pltpu.VMEM((s, w), F32)],
        compiler_params=_params("parallel", "parallel", "arbitrary"),
    )(qkv, qkv, qkv, do, tot, first)


NEG = -1e30
DEAD = -110.0


def _fox_fwd(qkv, c, ct, b, s, e, blk, pp):
    scale = HEAD_DIM ** -0.5
    nh = 2 * pp
    t = b * s

    def body(q_ref, k_ref, v_ref, c_ref, ct_ref, o_ref, lse_ref):
        qi = pl.program_id(2)
        qm = [_half_masks((q_ref[:, p * LANES:(p + 1) * LANES] * scale).astype(BF16)) for p in range(pp)]
        cq = [c_ref[h] for h in range(nh)]
        row, col = _iotas(blk)
        causal = col <= row
        low = lax.broadcasted_iota(jnp.int32, (blk, LANES), 1) < HEAD_DIM

        def block(j, carry, diag):
            off = pl.multiple_of(j * blk, blk)
            m, l, acc = carry
            zz = [_dot(qm[p], k_ref[pl.ds(off, blk), p * LANES:(p + 1) * LANES], NT) for p in range(pp)]
            z = [zz[h // 2][(h % 2) * blk:(h % 2 + 1) * blk] + (cq[h] - ct_ref[h, :, pl.ds(off, blk)]) for h in range(nh)]
            if diag:
                z = [jnp.where(causal, z[h], NEG) for h in range(nh)]
            m_new = tuple(jnp.maximum(m[h], jnp.max(z[h], axis=1, keepdims=True)) for h in range(nh))
            w = [jnp.exp(m[h] - m_new[h]) for h in range(nh)]
            pr = [jnp.exp(z[h] - m_new[h]) for h in range(nh)]
            pb = [pr[h].astype(BF16) for h in range(nh)]
            pv = [_dot(jnp.concatenate([pb[2 * p], pb[2 * p + 1]], axis=1),
                       _half_masks(v_ref[pl.ds(off, blk), p * LANES:(p + 1) * LANES]), NN) for p in range(pp)]
            acc_new = tuple(jnp.where(low, w[2 * p], w[2 * p + 1]) * acc[p] + pv[p] for p in range(pp))
            l_new = tuple(w[h] * l[h] + jnp.sum(pr[h], axis=1, keepdims=True) for h in range(nh))
            return m_new, l_new, acc_new

        zero = ((jnp.full((blk, 1), NEG, F32),) * nh, (jnp.zeros((blk, 1), F32),) * nh, (jnp.zeros((blk, LANES), F32),) * pp)
        carry = block(qi, zero, True)
        m, l, acc = lax.fori_loop(0, qi, lambda j, c_: block(j, c_, False), carry)
        for p in range(pp):
            o_ref[:, p * LANES:(p + 1) * LANES] = acc[p] / jnp.where(low, l[2 * p], l[2 * p + 1])
        for h in range(nh):
            lse_ref[h] = m[h] + jnp.log(l[h])

    q_in, k_in, v_in, q_out, _, rows, krow = _pair_specs(s, blk, e, pp, 1)
    return pl.pallas_call(
        body, name="fox_fwd", grid=(b, e // (pp * LANES), s // blk),
        in_specs=[q_in, k_in, v_in, rows, krow], out_specs=[q_out, rows],
        out_shape=[jax.ShapeDtypeStruct((t, e), F32), jax.ShapeDtypeStruct((b * e // HEAD_DIM, s, 1), F32)],
        compiler_params=_params("parallel", "parallel", "arbitrary"),
    )(qkv, qkv, qkv, c, ct)


def _fox_bwd(qkv, c, ct, do, o, lse, b, s, e, blk, pp):
    scale = HEAD_DIM ** -0.5
    nh = 2 * pp
    t = b * s
    nq = s // blk

    def body(q_ref, k_ref, v_ref, c_ref, ct_ref, do_ref, o_ref, lse_ref, dq_ref, dk_ref, dv_ref, dct_ref, dk_acc, dv_acc):
        qi = pl.program_id(2)

        @pl.when(qi == 0)
        def _():
            dk_acc[...] = jnp.zeros_like(dk_acc)
            dv_acc[...] = jnp.zeros_like(dv_acc)
            dct_ref[...] = jnp.zeros_like(dct_ref)

        qm = [_half_masks((q_ref[:, p * LANES:(p + 1) * LANES] * scale).astype(BF16)) for p in range(pp)]
        dob = [do_ref[:, p * LANES:(p + 1) * LANES].astype(BF16) for p in range(pp)]
        dom = [_half_masks(dob[p]) for p in range(pp)]
        low = lax.broadcasted_iota(jnp.int32, (blk, LANES), 1) < HEAD_DIM
        delta = []
        for p in range(pp):
            prod = dob[p].astype(F32) * o_ref[:, p * LANES:(p + 1) * LANES]
            delta.append(jnp.sum(jnp.where(low, prod, 0.0), axis=1, keepdims=True))
            delta.append(jnp.sum(jnp.where(low, 0.0, prod), axis=1, keepdims=True))
        cq = [c_ref[h] for h in range(nh)]
        lse_t = [lse_ref[h] for h in range(nh)]
        row, col = _iotas(blk)
        causal = col <= row

        def block(j, dq_acc, diag):
            off = pl.multiple_of(j * blk, blk)
            kp = [k_ref[pl.ds(off, blk), p * LANES:(p + 1) * LANES] for p in range(pp)]
            zz = [_dot(qm[p], kp[p], NT) for p in range(pp)]
            dd = [_dot(dom[p], v_ref[pl.ds(off, blk), p * LANES:(p + 1) * LANES], NT) for p in range(pp)]
            z = [zz[h // 2][(h % 2) * blk:(h % 2 + 1) * blk] + (cq[h] - ct_ref[h, :, pl.ds(off, blk)]) for h in range(nh)]
            pr = [jnp.exp(z[h] - lse_t[h]) for h in range(nh)]
            if diag:
                pr = [jnp.where(causal, pr[h], 0.0) for h in range(nh)]
            ds = [pr[h] * (dd[h // 2][(h % 2) * blk:(h % 2 + 1) * blk] - delta[h]) for h in range(nh)]
            dsb = [ds[h].astype(BF16) for h in range(nh)]
            pb = [pr[h].astype(BF16) for h in range(nh)]
            for p in range(pp):
                cols = slice(p * LANES, (p + 1) * LANES)
                dk_acc[pl.ds(off, blk), cols] += _dot(jnp.concatenate([dsb[2 * p], dsb[2 * p + 1]], axis=0), qm[p], TN)
                dv_acc[pl.ds(off, blk), cols] += _dot(jnp.concatenate([pb[2 * p], pb[2 * p + 1]], axis=0), dom[p], TN)
            for h in range(nh):
                dct_ref[h, :, pl.ds(off, blk)] -= jnp.sum(ds[h], axis=0, keepdims=True)
            return tuple(dq_acc[p] + _dot(jnp.concatenate([dsb[2 * p], dsb[2 * p + 1]], axis=1), _half_masks(kp[p]), NN)
                         for p in range(pp))

        dq_acc = lax.fori_loop(0, qi, lambda j, a: block(j, a, False), (jnp.zeros((blk, LANES), F32),) * pp)
        dq_acc = block(qi, dq_acc, True)
        for p in range(pp):
            dq_ref[:, p * LANES:(p + 1) * LANES] = (dq_acc[p] * scale).astype(BF16)

        @pl.when(qi == nq - 1)
        def _():
            dk_ref[...] = dk_acc[...].astype(BF16)
            dv_ref[...] = dv_acc[...].astype(BF16)

    q_in, k_in, v_in, q_out, kv_out, rows, krow = _pair_specs(s, blk, e, pp, 1)
    w = pp * LANES
    return pl.pallas_call(
        body, name="fox_bwd", grid=(b, e // w, nq),
        in_specs=[q_in, k_in, v_in, rows, krow, q_out, q_out, rows], out_specs=[q_out, kv_out, kv_out, krow],
        out_shape=[jax.ShapeDtypeStruct((t, e), BF16)] * 3 + [jax.ShapeDtypeStruct((b * e // HEAD_DIM, 1, s), F32)],
        scratch_shapes=[pltpu.VMEM((s, w), F32), pltpu.VMEM((s, w), F32)],
        compiler_params=_params("parallel", "parallel", "arbitrary"),
    )(qkv, qkv, qkv, c, ct, do, o, lse)


def _scan_rows(f2, group, mode, d2=None):
    n = f2.shape[0]

    def body(*refs):
        f_ref, o_ref = refs[0], refs[-1]
        f = f_ref[...]
        row, col = _iotas(LANES)
        grow = lax.broadcasted_iota(jnp.int32, (n, n), 0)
        gcol = lax.broadcasted_iota(jnp.int32, (n, n), 1)
        same = (grow // group) == (gcol // group)
        e = jnp.exp(-jnp.abs(f))
        if mode == "fwd":
            x = jnp.minimum(f, 0.0) - jnp.log1p(e)
            within = (row <= col).astype(F32)
            earlier = (same & (gcol < grow)).astype(F32)
        else:
            x = refs[1][...]
            within = (row >= col).astype(F32)
            earlier = (same & (gcol > grow)).astype(F32)
        y = jnp.dot(x, within, preferred_element_type=F32, precision=lax.Precision.HIGHEST)
        tot = jnp.sum(x, axis=1, keepdims=True)
        y = y + jnp.dot(earlier, tot, preferred_element_type=F32, precision=lax.Precision.HIGHEST)
        if mode == "bwd":
            r = 1.0 / (1.0 + e)
            y = y * jnp.where(f >= 0.0, e * r, r)
        o_ref[...] = y

    args = (f2,) if mode == "fwd" else (f2, d2)
    return pl.pallas_call(body, name="logf_" + mode, out_shape=jax.ShapeDtypeStruct(f2.shape, F32),
                          compiler_params=_params())(*args)


def _matmul(a, b, dims, *, tm, tn, tk, out_dtype, name, bias=None, res=None, res_scale=1.0, b_outer=False):
    def ij(g0, g1):
        return (g1, g0) if b_outer else (g0, g1)

    if dims == NN:
        (m, kk), n = a.shape, b.shape[1]
        a_spec = pl.BlockSpec((tm, tk), lambda g0, g1, k: (ij(g0, g1)[0], k))
        b_spec = pl.BlockSpec((tk, tn), lambda g0, g1, k: (k, ij(g0, g1)[1]))
    elif dims == NT:
        (m, kk), n = a.shape, b.shape[0]
        a_spec = pl.BlockSpec((tm, tk), lambda g0, g1, k: (ij(g0, g1)[0], k))
        b_spec = pl.BlockSpec((tn, tk), lambda g0, g1, k: (ij(g0, g1)[1], k))
    else:
        (kk, m), n = a.shape, b.shape[1]
        a_spec = pl.BlockSpec((tk, tm), lambda g0, g1, k: (k, ij(g0, g1)[0]))
        b_spec = pl.BlockSpec((tk, tn), lambda g0, g1, k: (k, ij(g0, g1)[1]))
    assert m % tm == 0 and n % tn == 0 and kk % tk == 0, (name, m, n, kk, tm, tn, tk)
    nk = kk // tk
    extras, extra_specs = [], []
    if bias is not None:
        extras.append(bias)
        extra_specs.append(pl.BlockSpec((1, tn), lambda g0, g1, k: (0, ij(g0, g1)[1])))
    if res is not None:
        extras.append(res)
        extra_specs.append(pl.BlockSpec((tm, tn), lambda g0, g1, k: ij(g0, g1)))

    def finish(out, rest, o_ref):
        idx = 0
        if bias is not None:
            out = out + rest[idx][...]
            idx += 1
        if res is not None:
            out = out + res_scale * rest[idx][...]
        o_ref[...] = out.astype(o_ref.dtype)

    def body_single(a_ref, b_ref, *rest):
        finish(_dot(a_ref[...].astype(BF16), b_ref[...].astype(BF16), dims), rest, rest[-1])

    def body_acc(a_ref, b_ref, *rest):
        o_ref, acc_ref = rest[-2], rest[-1]
        k = pl.program_id(2)
        part = _dot(a_ref[...].astype(BF16), b_ref[...].astype(BF16), dims)

        @pl.when(k == 0)
        def _():
            acc_ref[...] = part

        @pl.when(k > 0)
        def _():
            acc_ref[...] += part

        @pl.when(k == nk - 1)
        def _():
            finish(acc_ref[...], rest, o_ref)

    grid = (n // tn, m // tm, nk) if b_outer else (m // tm, n // tn, nk)
    return pl.pallas_call(
        body_single if nk == 1 else body_acc, name=name, grid=grid,
        in_specs=[a_spec, b_spec] + extra_specs,
        out_specs=pl.BlockSpec((tm, tn), lambda g0, g1, k: ij(g0, g1)),
        out_shape=jax.ShapeDtypeStruct((m, n), out_dtype),
        scratch_shapes=[] if nk == 1 else [pltpu.VMEM((tm, tn), F32)],
        compiler_params=_params("parallel", "parallel", "arbitrary"),
    )(a, b, *extras)


def _sigmoid(x):
    e = jnp.exp(-jnp.abs(x))
    r = 1.0 / (1.0 + e)
    return jnp.where(x >= 0.0, r, e * r)


def _proj_gate_fwd(o_sb, o_fx, wp_sb, wp_fx, g, tm):
    t, e = o_sb.shape
    d = wp_sb.shape[1]

    def body(osb_ref, ofx_ref, wsb_ref, wfx_ref, gsb_ref, gfx_ref, mg_ref, ysb_ref, yfx_ref):
        ysb = _dot(osb_ref[...].astype(BF16), wsb_ref[...], NN)
        yfx = _dot(ofx_ref[...].astype(BF16), wfx_ref[...], NN)
        ysb_ref[...] = ysb
        yfx_ref[...] = yfx
        mg_ref[...] = (_sigmoid(gsb_ref[...]) * ysb + _sigmoid(gfx_ref[...]) * yfx).astype(BF16)

    rows_e = pl.BlockSpec((tm, e), lambda i: (i, 0))
    rows_d = pl.BlockSpec((tm, d), lambda i: (i, 0))
    w_spec = pl.BlockSpec((e, d), lambda i: (0, 0))
    return pl.pallas_call(
        body, name="proj_gate_fwd", grid=(t // tm,),
        in_specs=[rows_e, rows_e, w_spec, w_spec, rows_d, pl.BlockSpec((tm, d), lambda i: (i, 1))],
        out_specs=[rows_d, rows_d, rows_d],
        out_shape=[jax.ShapeDtypeStruct((t, d), BF16), jax.ShapeDtypeStruct((t, d), F32), jax.ShapeDtypeStruct((t, d), F32)],
        compiler_params=_params("parallel"),
    )(o_sb, o_fx, wp_sb, wp_fx, g, g)


def _gate_bwd(dmg, y_sb, y_fx, g, tm):
    t, d = dmg.shape

    def body(dm_ref, ysb_ref, yfx_ref, gsb_ref, gfx_ref, dysb_ref, dyfx_ref, dg_ref):
        dm = dm_ref[...]
        ssb = _sigmoid(gsb_ref[...])
        sfx = _sigmoid(gfx_ref[...])
        dysb_ref[...] = (dm * ssb).astype(BF16)
        dyfx_ref[...] = (dm * sfx).astype(BF16)
        dg_ref[:, 0:d] = (dm * ysb_ref[...] * ssb * (1.0 - ssb)).astype(BF16)
        dg_ref[:, d:2 * d] = (dm * yfx_ref[...] * sfx * (1.0 - sfx)).astype(BF16)

    rows = pl.BlockSpec((tm, d), lambda i: (i, 0))
    rows1 = pl.BlockSpec((tm, d), lambda i: (i, 1))
    return pl.pallas_call(
        body, name="gate_bwd", grid=(t // tm,),
        in_specs=[rows, rows, rows, rows, rows1],
        out_specs=[rows, rows, pl.BlockSpec((tm, 2 * d), lambda i: (i, 0))],
        out_shape=[jax.ShapeDtypeStruct((t, d), BF16)] * 2 + [jax.ShapeDtypeStruct((t, 2 * d), BF16)],
        compiler_params=_params("parallel"),
    )(dmg, y_sb, y_fx, g, g)


def _mm_res_ln(a, w, xres, gamma, beta, tm, name):
    t, kk = a.shape
    d = w.shape[1]

    def body(a_ref, w_ref, x_ref, g_ref, b_ref, xn_ref, xh_ref, rs_ref, xb_ref):
        r = ALPHA * x_ref[...] + _dot(a_ref[...].astype(BF16), w_ref[...], NN)
        mean = jnp.mean(r, axis=1, keepdims=True)
        cen = r - mean
        rstd = lax.rsqrt(jnp.mean(cen * cen, axis=1, keepdims=True) + LN_EPS)
        xh = cen * rstd
        xn = xh * g_ref[...] + b_ref[...]
        xh_ref[...] = xh
        xn_ref[...] = xn
        xb_ref[...] = xn.astype(BF16)
        rs_ref[...] = rstd

    rows_d = pl.BlockSpec((tm, d), lambda i: (i, 0))
    vec = pl.BlockSpec((1, d), lambda i: (0, 0))
    return pl.pallas_call(
        body, name=name, grid=(t // tm,),
        in_specs=[pl.BlockSpec((tm, kk), lambda i: (i, 0)), pl.BlockSpec((kk, d), lambda i: (0, 0)), rows_d, vec, vec],
        out_specs=[rows_d, rows_d, pl.BlockSpec((tm, 1), lambda i: (i, 0)), rows_d],
        out_shape=[jax.ShapeDtypeStruct((t, d), F32), jax.ShapeDtypeStruct((t, d), F32), jax.ShapeDtypeStruct((t, 1), F32),
                   jax.ShapeDtypeStruct((t, d), BF16)],
        compiler_params=_params("parallel"),
    )(a, w, xres, gamma, beta)


def _ln_bwd_math(dy, xh, rstd, gamma):
    dxh = dy * gamma
    m1 = jnp.mean(dxh, axis=1, keepdims=True)
    m2 = jnp.mean(dxh * xh, axis=1, keepdims=True)
    return rstd * (dxh - m1 - xh * m2)


def _rowsum8(x):
    tm, n = x.shape
    return jnp.sum(x.reshape(tm // SUBLANES, SUBLANES, n), axis=0)


def _fold8(ref):
    ref[0:1, :] = jnp.sum(ref[...], axis=0, keepdims=True)


def _loss_ln_bwd(x2, xh, rstd, gamma, target, tm):
    t, d = x2.shape

    def body(x_ref, xh_ref, rs_ref, g_ref, tg_ref, dr_ref, dg_ref, db_ref, ls_ref, drb_ref):
        @pl.when(pl.program_id(0) == 0)
        def _():
            dg_ref[...] = jnp.zeros_like(dg_ref)
            db_ref[...] = jnp.zeros_like(db_ref)
            ls_ref[...] = jnp.zeros_like(ls_ref)

        err = x_ref[...] - tg_ref[...]
        xh = xh_ref[...]
        dy = err * (1.0 / d)
        dr = _ln_bwd_math(dy, xh, rs_ref[...], g_ref[...])
        dr_ref[...] = dr
        drb_ref[...] = dr.astype(BF16)
        dg_ref[...] += _rowsum8(dy * xh)
        db_ref[...] += _rowsum8(dy)
        sq = _rowsum8(err * err)
        part = sq[:, 0:LANES]
        for j in range(1, d // LANES):
            part = part + sq[:, j * LANES:(j + 1) * LANES]
        ls_ref[...] += part * (0.5 / d)

        @pl.when(pl.program_id(0) == t // tm - 1)
        def _():
            _fold8(dg_ref)
            _fold8(db_ref)
            ls_ref[0:1, 0:1] = jnp.sum(jnp.sum(ls_ref[...], axis=0, keepdims=True), axis=1, keepdims=True)

    rows = pl.BlockSpec((tm, d), lambda i: (i, 0))
    acc = pl.BlockSpec((SUBLANES, d), lambda i: (0, 0))
    return pl.pallas_call(
        body, name="loss_ln_bwd", grid=(t // tm,),
        in_specs=[rows, rows, pl.BlockSpec((tm, 1), lambda i: (i, 0)), pl.BlockSpec((1, d), lambda i: (0, 0)), rows],
        out_specs=[rows, acc, acc, pl.BlockSpec((SUBLANES, LANES), lambda i: (0, 0)), rows],
        out_shape=[jax.ShapeDtypeStruct((t, d), F32), jax.ShapeDtypeStruct((SUBLANES, d), F32),
                   jax.ShapeDtypeStruct((SUBLANES, d), F32), jax.ShapeDtypeStruct((SUBLANES, LANES), F32),
                   jax.ShapeDtypeStruct((t, d), BF16)],
        compiler_params=_params("arbitrary"),
    )(x2, xh, rstd, gamma, target)


def _ln_bwd(dr_next, dlin, xh, rstd, gamma, tm):
    t, d = xh.shape

    def body(dn_ref, dl_ref, xh_ref, rs_ref, g_ref, dr_ref, dg_ref, db_ref, drb_ref):
        @pl.when(pl.program_id(0) == 0)
        def _():
            dg_ref[...] = jnp.zeros_like(dg_ref)
            db_ref[...] = jnp.zeros_like(db_ref)

        dy = ALPHA * dn_ref[...] + dl_ref[...]
        xh = xh_ref[...]
        dr = _ln_bwd_math(dy, xh, rs_ref[...], g_ref[...])
        dr_ref[...] = dr
        drb_ref[...] = dr.astype(BF16)
        dg_ref[...] += _rowsum8(dy * xh)
        db_ref[...] += _rowsum8(dy)

        @pl.when(pl.program_id(0) == t // tm - 1)
        def _():
            _fold8(dg_ref)
            _fold8(db_ref)

    rows = pl.BlockSpec((tm, d), lambda i: (i, 0))
    acc = pl.BlockSpec((SUBLANES, d), lambda i: (0, 0))
    return pl.pallas_call(
        body, name="ln_bwd", grid=(t // tm,),
        in_specs=[rows, rows, rows, pl.BlockSpec((tm, 1), lambda i: (i, 0)), pl.BlockSpec((1, d), lambda i: (0, 0))],
        out_specs=[rows, acc, acc, rows],
        out_shape=[jax.ShapeDtypeStruct((t, d), F32), jax.ShapeDtypeStruct((SUBLANES, d), F32),
                   jax.ShapeDtypeStruct((SUBLANES, d), F32), jax.ShapeDtypeStruct((t, d), BF16)],
        compiler_params=_params("arbitrary"),
    )(dr_next, dlin, xh, rstd, gamma)


def _shift_rows(x, halo, shift, row):
    out = pltpu.roll(x, shift, 0)
    n = halo.shape[0]
    for r in range(shift):
        out = jnp.where(row == r, halo[n - shift + r:n - shift + r + 1, :], out)
    return out


def _unshift_rows(x, halo, shift, row, tm):
    out = pltpu.roll(x, tm - shift, 0)
    for r in range(shift):
        out = jnp.where(row == tm - shift + r, halo[r:r + 1, :], out)
    return out


def _conv_pre(ug_ref, halo_ref, wc_ref, bc_ref, first, tm):
    ug = ug_ref[...].astype(F32)
    halo = jnp.where(first, 0.0, halo_ref[...].astype(F32))
    row = lax.broadcasted_iota(jnp.int32, ug.shape, 0)
    wc = wc_ref[...]
    um1 = _shift_rows(ug, halo, 1, row)
    um2 = _shift_rows(ug, halo, 2, row)
    c = bc_ref[...] + wc[2:3, :] * ug + wc[1:2, :] * um1 + wc[0:1, :] * um2
    return c, ug, um1, um2


INV_SQRT2 = 1.0 / math.sqrt(2.0)
INV_SQRT2PI = 1.0 / math.sqrt(2.0 * math.pi)


def _conv_glu_fwd(u, wc, bc, seq, tm):
    t, f2 = u.shape
    f = f2 // 2
    per_seq = seq // tm
    hb = tm // HALO

    def body(ug_ref, halo_ref, uv_ref, wc_ref, bc_ref, a_ref):
        first = (pl.program_id(0) % per_seq) == 0
        c, _, _, _ = _conv_pre(ug_ref, halo_ref, wc_ref, bc_ref, first, tm)
        gelu = 0.5 * c * (1.0 + lax.erf(c * INV_SQRT2))
        a_ref[...] = (gelu * uv_ref[...].astype(F32)).astype(BF16)

    return pl.pallas_call(
        body, name="conv_glu_fwd", grid=(t // tm,),
        in_specs=[pl.BlockSpec((tm, f), lambda i: (i, 0)),
                  pl.BlockSpec((HALO, f), lambda i: (jnp.maximum(i * hb - 1, 0), 0)),
                  pl.BlockSpec((tm, f), lambda i: (i, 1)),
                  pl.BlockSpec((3, f), lambda i: (0, 0)), pl.BlockSpec((1, f), lambda i: (0, 0))],
        out_specs=pl.BlockSpec((tm, f), lambda i: (i, 0)),
        out_shape=jax.ShapeDtypeStruct((t, f), BF16),
        compiler_params=_params("parallel"),
    )(u, u, u, wc, bc)


def _conv_glu_bwd1(u, da, wc, bc, seq, tm):
    t, f2 = u.shape
    f = f2 // 2
    per_seq = seq // tm
    hb = tm // HALO

    def body(ug_ref, halo_ref, uv_ref, da_ref, wc_ref, bc_ref, dc_ref, duv_ref):
        first = (pl.program_id(0) % per_seq) == 0
        c, _, _, _ = _conv_pre(ug_ref, halo_ref, wc_ref, bc_ref, first, tm)
        cdf = 0.5 * (1.0 + lax.erf(c * INV_SQRT2))
        pdf = jnp.exp(-0.5 * c * c) * INV_SQRT2PI
        da = da_ref[...].astype(F32)
        duv_ref[...] = (da * (c * cdf)).astype(BF16)
        dc_ref[...] = da * uv_ref[...].astype(F32) * (cdf + c * pdf)

    rows = pl.BlockSpec((tm, f), lambda i: (i, 0))
    return pl.pallas_call(
        body, name="conv_glu_bwd1", grid=(t // tm,),
        in_specs=[rows, pl.BlockSpec((HALO, f), lambda i: (jnp.maximum(i * hb - 1, 0), 0)),
                  pl.BlockSpec((tm, f), lambda i: (i, 1)), rows,
                  pl.BlockSpec((3, f), lambda i: (0, 0)), pl.BlockSpec((1, f), lambda i: (0, 0))],
        out_specs=[rows, rows],
        out_shape=[jax.ShapeDtypeStruct((t, f), F32), jax.ShapeDtypeStruct((t, f), BF16)],
        compiler_params=_params("parallel"),
    )(u, u, u, da, wc, bc)


def _conv_glu_bwd2(u, dc, wc, seq, tm):
    t, f2 = u.shape
    f = f2 // 2
    per_seq = seq // tm
    hb = tm // HALO
    nblk = t // SUBLANES

    def body(ug_ref, halo_ref, dc_ref, nxt_ref, wc_ref, dug_ref, w0_ref, w1_ref, w2_ref, b_ref):
        i = pl.program_id(0)

        @pl.when(i == 0)
        def _():
            for r in (w0_ref, w1_ref, w2_ref, b_ref):
                r[...] = jnp.zeros_like(r)

        first = (i % per_seq) == 0
        last = (i % per_seq) == per_seq - 1
        ug = ug_ref[...].astype(F32)
        halo = jnp.where(first, 0.0, halo_ref[...].astype(F32))
        nxt = jnp.where(last, 0.0, nxt_ref[...])
        row = lax.broadcasted_iota(jnp.int32, ug.shape, 0)
        dc = dc_ref[...]
        wc = wc_ref[...]
        dp1 = _unshift_rows(dc, nxt, 1, row, tm)
        dp2 = _unshift_rows(dc, nxt, 2, row, tm)
        dug_ref[...] = (wc[2:3, :] * dc + wc[1:2, :] * dp1 + wc[0:1, :] * dp2).astype(BF16)
        w2_ref[...] += _rowsum8(dc * ug)
        w1_ref[...] += _rowsum8(dc * _shift_rows(ug, halo, 1, row))
        w0_ref[...] += _rowsum8(dc * _shift_rows(ug, halo, 2, row))
        b_ref[...] += _rowsum8(dc)

        @pl.when(i == t // tm - 1)
        def _():
            for r in (w0_ref, w1_ref, w2_ref, b_ref):
                _fold8(r)

    rows = pl.BlockSpec((tm, f), lambda i: (i, 0))
    acc = pl.BlockSpec((SUBLANES, f), lambda i: (0, 0))
    return pl.pallas_call(
        body, name="conv_glu_bwd2", grid=(t // tm,),
        in_specs=[rows, pl.BlockSpec((HALO, f), lambda i: (jnp.maximum(i * hb - 1, 0), 0)),
                  rows, pl.BlockSpec((SUBLANES, f), lambda i: (jnp.minimum((i + 1) * (tm // SUBLANES), nblk - 1), 0)),
                  pl.BlockSpec((3, f), lambda i: (0, 0))],
        out_specs=[rows, acc, acc, acc, acc],
        out_shape=[jax.ShapeDtypeStruct((t, f), BF16)] + [jax.ShapeDtypeStruct((SUBLANES, f), F32)] * 4,
        compiler_params=_params("arbitrary"),
    )(u, u, dc, dc, wc)


def _colsum(x, tm, name):
    t, n = x.shape

    def body(x_ref, o_ref):
        @pl.when(pl.program_id(0) == 0)
        def _():
            o_ref[...] = jnp.zeros_like(o_ref)

        o_ref[...] += _rowsum8(x_ref[...].astype(F32))

        @pl.when(pl.program_id(0) == t // tm - 1)
        def _():
            _fold8(o_ref)

    return pl.pallas_call(
        body, name=name, grid=(t // tm,),
        in_specs=[pl.BlockSpec((tm, n), lambda i: (i, 0))],
        out_specs=pl.BlockSpec((SUBLANES, n), lambda i: (0, 0)),
        out_shape=jax.ShapeDtypeStruct((SUBLANES, n), F32),
        compiler_params=_params("arbitrary"),
    )(x)


def _adamw(w, gparts, m, v, name):
    p, r, c = gparts.shape
    tr = r
    for cand in (512, 256, 128, 64, 32, 16, 8):
        if cand * p <= 1024 and r % cand == 0 and r > cand:
            tr = cand
            break
    c1 = 1.0 - ADAM_B1 ** ADAM_STEP
    c2 = 1.0 - ADAM_B2 ** ADAM_STEP

    def body(w_ref, g_ref, m_ref, v_ref, go_ref, d_ref, mo_ref, vo_ref):
        g = g_ref[0].astype(F32)
        for i in range(1, p):
            g = g + g_ref[i].astype(F32)
        mn = ADAM_B1 * m_ref[...] + (1.0 - ADAM_B1) * g
        vn = ADAM_B2 * v_ref[...] + (1.0 - ADAM_B2) * (g * g)
        go_ref[...] = g
        mo_ref[...] = mn
        vo_ref[...] = vn
        d_ref[...] = -ADAM_LR * ((mn / c1) / (jnp.sqrt(vn / c2) + ADAM_EPS) + ADAM_WD * w_ref[...])

    blk = pl.BlockSpec((tr, c), lambda i: (i, 0))
    return pl.pallas_call(
        body, name=name, grid=(r // tr,),
        in_specs=[blk, pl.BlockSpec((p, tr, c), lambda i: (0, i, 0)), blk, blk],
        out_specs=[blk] * 4,
        out_shape=[jax.ShapeDtypeStruct((r, c), F32)] * 4,
        compiler_params=_params("parallel"),
    )(w, gparts, m, v)


MESH = pl.DeviceIdType.MESH
ANY = pl.BlockSpec(memory_space=pl.ANY)


def _all_gather(xs, name):
    n = len(xs)

    def body(*refs):
        x_refs, out_refs = refs[:n], refs[n:2 * n]
        send_sems, recv_sems, local_sems = refs[2 * n:]
        x, y, c = lax.axis_index("x"), lax.axis_index("y"), lax.axis_index("c")
        me, sibling = (x, y, c), (x, y, 1 - c)
        chips = [(1 - x, y), (x, 1 - y), (1 - x, 1 - y)]

        def slot(a, px, py, pc):
            return out_refs[a].at[4 * px + 2 * py + pc]

        def copy(a, k, block, to, src=None):
            return pltpu.make_async_remote_copy(
                src_ref=slot(a, *block) if src is None else src, dst_ref=slot(a, *block),
                send_sem=send_sems.at[k * n + a], recv_sem=recv_sems.at[k * n + a], device_id=to, device_id_type=MESH)

        arrays = range(n)
        mine = [pltpu.make_async_copy(x_refs[a], slot(a, *me), local_sems.at[a]) for a in arrays]
        first = [copy(a, 0, me, sibling, src=x_refs[a]) for a in arrays]
        first += [copy(a, 1 + j, me, (*chip, c), src=x_refs[a]) for j, chip in enumerate(chips) for a in arrays]
        for cp in mine + first:
            cp.start()
        passed = []
        for j, chip in enumerate(chips):
            for a in arrays:
                copy(a, 1 + j, (*chip, c), me).wait_recv()
                passed.append(copy(a, 4 + j, (*chip, c), sibling))
                passed[-1].start()
        for a in arrays:
            copy(a, 0, sibling, me).wait_recv()
        for j, chip in enumerate(chips):
            for a in arrays:
                copy(a, 4 + j, (*chip, 1 - c), me).wait_recv()
        for cp in first + passed:
            cp.wait_send()
        for cp in mine:
            cp.wait()

    return pl.pallas_call(
        body, name=name,
        out_shape=[jax.ShapeDtypeStruct((N_DEV,) + x.shape, x.dtype) for x in xs],
        in_specs=[ANY] * n, out_specs=[ANY] * n,
        scratch_shapes=[pltpu.SemaphoreType.DMA((7 * n,)), pltpu.SemaphoreType.DMA((7 * n,)),
                        pltpu.SemaphoreType.DMA((n,))],
    )(*xs)


def _exchange(gs, name):
    n = len(gs)

    def body(*refs):
        g_refs, land_refs = refs[:n], refs[n:2 * n]
        send_sems, recv_sems, local_sems = refs[2 * n:]
        x, y, c = lax.axis_index("x"), lax.axis_index("y"), lax.axis_index("c")
        mine = 4 * x + 2 * y + c
        own = [pltpu.make_async_copy(g_refs[a].at[mine], land_refs[a].at[mine], local_sems.at[a]) for a in range(n)]
        copies = []
        for k in range(1, N_DEV):
            px = 1 - x if k & 4 else x
            py = 1 - y if k & 2 else y
            pc = 1 - c if k & 1 else c
            for a in range(n):
                copies.append(pltpu.make_async_remote_copy(
                    src_ref=g_refs[a].at[4 * px + 2 * py + pc], dst_ref=land_refs[a].at[mine],
                    send_sem=send_sems.at[(k - 1) * n + a], recv_sem=recv_sems.at[(k - 1) * n + a],
                    device_id=(px, py, pc), device_id_type=MESH))
        for cp in own + copies:
            cp.start()
        for cp in copies + own:
            cp.wait()

    return pl.pallas_call(
        body, name=name, out_shape=[jax.ShapeDtypeStruct(g.shape, g.dtype) for g in gs],
        in_specs=[ANY] * n, out_specs=[ANY] * n,
        scratch_shapes=[pltpu.SemaphoreType.DMA((7 * n,)), pltpu.SemaphoreType.DMA((7 * n,)),
                        pltpu.SemaphoreType.DMA((n,))],
    )(*gs)


def _tile(n, pref, unit=LANES):
    if n <= pref:
        return n
    best = None
    for cand in range(unit, pref + 1, unit):
        if n % cand == 0:
            best = cand
    assert best is not None, (n, pref, unit)
    return best


def _layer_step(x, target, w, attn_blk):
    b, s, d = x.shape
    t = b * s
    e = w["wp_sb"].shape[0]
    h = e // HEAD_DIM
    f = w["w_down"].shape[0]
    x2 = x.reshape(t, d)
    tg = target.reshape(t, d)
    pp = 2 if (e // LANES) % 2 == 0 else 1
    tm = _tile(t, 512, HALO)
    tmc = _tile(s, 256, HALO)
    tkt = _tile(t, 2048, HALO)
    td = _tile(d, 1024)
    tf = _tile(f, 1408)
    t2f = _tile(2 * f, 1408)
    tqkv = _tile(6 * e, 1024)
    tg2 = _tile(2 * d, 1024)
    xb = x2.astype(BF16)

    qkv = _matmul(xb, w["w_qkv"], NN, tm=tm, tn=tqkv, tk=d, out_dtype=BF16, name="in_qkv", bias=w["b_qkv"], b_outer=True)
    gate = _matmul(xb, w["w_g"], NN, tm=tm, tn=tg2, tk=d, out_dtype=F32, name="in_gate", bias=w["b_g"], b_outer=True)
    fl = _matmul(xb, w["w_f"], NN, tm=tm, tn=LANES, tk=d, out_dtype=F32, name="in_forget", bias=w["b_f"])
    nr = s // LANES
    f2 = fl.reshape(b, s, LANES)[:, :, :h].transpose(0, 2, 1).reshape(b * h * nr, LANES)
    c2 = _scan_rows(f2, nr, "fwd")
    c = c2.reshape(b * h, s, 1)
    ct = c2.reshape(b * h, 1, s)
    o_sb, tot, first = _sb_fwd(qkv, b, s, e, attn_blk, pp)
    o_fx, lse = _fox_fwd(qkv, c, ct, b, s, e, attn_blk, pp)
    merged, y_sb, y_fx = _proj_gate_fwd(o_sb, o_fx, w["wp_sb"], w["wp_fx"], gate, tm)
    x1, xh1, rs1, x1b = _mm_res_ln(merged, w["w_out"], x2, w["ln1_g"], w["ln1_b"], tm, "out_ln1")
    u = _matmul(x1b, w["w_up"], NN, tm=tm, tn=t2f, tk=d, out_dtype=BF16, name="ffn_up", b_outer=True)
    act = _conv_glu_fwd(u, w["w_conv"], w["b_conv"], s, tmc)
    xo, xh2, rs2, _ = _mm_res_ln(act, w["w_down"], x1, w["ln2_g"], w["ln2_b"], tm, "down_ln2")

    gr = {}
    dr2, dg2, db2, ls, dr2b = _loss_ln_bwd(xo, xh2, rs2, w["ln2_g"], tg, tm)
    gr["ln2_g"], gr["ln2_b"] = dg2[0:1], db2[0:1]
    da = _matmul(dr2b, w["w_down"], NT, tm=tm, tn=tf, tk=d, out_dtype=BF16, name="d_act", b_outer=True)
    gr["w_down"] = _matmul(act, dr2b, TN, tm=tf, tn=td, tk=tkt, out_dtype=F32, name="dw_down")
    dc, du_v = _conv_glu_bwd1(u, da, w["w_conv"], w["b_conv"], s, tmc)
    du_g, gw0, gw1, gw2, gbc = _conv_glu_bwd2(u, dc, w["w_conv"], s, tmc)
    gr["w_conv"] = jnp.concatenate([gw0[0:1], gw1[0:1], gw2[0:1]], axis=0)
    gr["b_conv"] = gbc[0:1]
    du = jnp.concatenate([du_g, du_v], axis=1)
    dlin1 = _matmul(du, w["w_up"], NT, tm=tm, tn=td, tk=2 * f, out_dtype=F32, name="d_x1")
    gr["w_up"] = _matmul(x1b, du, TN, tm=td, tn=t2f, tk=tkt, out_dtype=F32, name="dw_up")
    dr1, dg1, db1, dr1b = _ln_bwd(dr2, dlin1, xh1, rs1, w["ln1_g"], tm)
    gr["ln1_g"], gr["ln1_b"] = dg1[0:1], db1[0:1]
    dmg = _matmul(dr1b, w["w_out"], NT, tm=tm, tn=td, tk=d, out_dtype=F32, name="d_merged")
    gr["w_out"] = _matmul(merged, dr1b, TN, tm=td, tn=td, tk=tkt, out_dtype=F32, name="dw_out")
    dy_sb, dy_fx, dgate = _gate_bwd(dmg, y_sb, y_fx, gate, tm)
    do_sb = _matmul(dy_sb, w["wp_sb"], NT, tm=tm, tn=e, tk=d, out_dtype=BF16, name="d_o_sb")
    do_fx = _matmul(dy_fx, w["wp_fx"], NT, tm=tm, tn=e, tk=d, out_dtype=BF16, name="d_o_fx")
    gr["wp_sb"] = _matmul(o_sb, dy_sb, TN, tm=e, tn=td, tk=tkt, out_dtype=F32, name="dwp_sb")
    gr["wp_fx"] = _matmul(o_fx, dy_fx, TN, tm=e, tn=td, tk=tkt, out_dtype=F32, name="dwp_fx")
    dq_sb, dk_sb, dv_sb = _sb_bwd(qkv, do_sb, tot, first, b, s, e, attn_blk, pp)
    dq_fx, dk_fx, dv_fx, dct = _fox_bwd(qkv, c, ct, do_fx, o_fx, lse, b, s, e, attn_blk, pp)
    dqkv = jnp.concatenate([dq_sb, dk_sb, dv_sb, dq_fx, dk_fx, dv_fx], axis=1)
    df2 = _scan_rows(f2, nr, "bwd", dct.reshape(b * h * nr, LANES))
    df = jnp.pad(df2.reshape(b, h, s).transpose(0, 2, 1).reshape(t, h), ((0, 0), (0, LANES - h))).astype(BF16)
    dx = _matmul(dqkv, w["w_qkv"], NT, tm=tm, tn=td, tk=6 * e, out_dtype=F32, name="dx_qkv", res=dr1, res_scale=ALPHA)
    dx = _matmul(dgate, w["w_g"], NT, tm=tm, tn=td, tk=2 * d, out_dtype=F32, name="dx_gate", res=dx)
    dx = _matmul(df, w["w_f"], NT, tm=tm, tn=td, tk=LANES, out_dtype=F32, name="dx_forget", res=dx)
    gr["w_qkv"] = _matmul(xb, dqkv, TN, tm=td, tn=tqkv, tk=tkt, out_dtype=F32, name="dw_qkv")
    gr["w_g"] = _matmul(xb, dgate, TN, tm=td, tn=tg2, tk=tkt, out_dtype=F32, name="dw_gate")
    gr["w_f"] = _matmul(xb, df, TN, tm=td, tn=LANES, tk=tkt, out_dtype=F32, name="dw_forget")
    gr["b_qkv"] = _colsum(dqkv, tm, "db_qkv")[0:1]
    gr["b_g"] = _colsum(dgate, tm, "db_gate")[0:1]
    gr["b_f"] = _colsum(df, tm, "db_forget")[0:1]
    return ls[0:1, 0:1], dx.reshape(b, s, d), gr


SHARDED = ("w_in", "w_proj_sb", "w_proj_fox", "w_out", "w_up", "w_conv", "w_down")
ROW_SHARDED = ("w_out", "w_down")
REPLICATED = ("b_in", "ln1_g", "ln1_b", "b_conv", "ln2_g", "ln2_b")
WEIGHTS = ("w_in", "b_in", "w_proj_sb", "w_proj_fox", "w_out", "ln1_g", "ln1_b", "w_up", "w_conv", "b_conv",
           "w_down", "ln2_g", "ln2_b")
MATMUL_OPERANDS = ("w_in", "w_proj_sb", "w_proj_fox", "w_out", "w_up", "w_down")


def _cut(full, name):
    r, c = full.shape
    if name in ROW_SHARDED:
        return full.reshape(N_DEV, r // N_DEV, c)
    cs = c // N_DEV
    return jnp.stack([full[:, j * cs:(j + 1) * cs] for j in range(N_DEV)], axis=0)


def _join(blocks, name):
    p, r, c = blocks.shape
    if name in ROW_SHARDED:
        return blocks.reshape(p * r, c)
    return jnp.concatenate([blocks[j] for j in range(p)], axis=1)


def kernel(x, w_in, b_in, w_proj_sb, w_proj_fox, w_out, ln1_g, ln1_b, w_up, w_conv, b_conv, w_down, ln2_g, ln2_b, loss_target, m_w_in, m_b_in, m_w_proj_sb, m_w_proj_fox, m_w_out, m_ln1_g, m_ln1_b, m_w_up, m_w_conv, m_b_conv, m_w_down, m_ln2_g, m_ln2_b, v_w_in, v_b_in, v_w_proj_sb, v_w_proj_fox, v_w_out, v_ln1_g, v_ln1_b, v_w_up, v_w_conv, v_b_conv, v_w_down, v_ln2_g, v_ln2_b):
    wts = dict(w_in=w_in, b_in=b_in, w_proj_sb=w_proj_sb, w_proj_fox=w_proj_fox, w_out=w_out, ln1_g=ln1_g, ln1_b=ln1_b,
               w_up=w_up, w_conv=w_conv, b_conv=b_conv, w_down=w_down, ln2_g=ln2_g, ln2_b=ln2_b)
    mom = dict(w_in=m_w_in, b_in=m_b_in, w_proj_sb=m_w_proj_sb, w_proj_fox=m_w_proj_fox, w_out=m_w_out, ln1_g=m_ln1_g,
               ln1_b=m_ln1_b, w_up=m_w_up, w_conv=m_w_conv, b_conv=m_b_conv, w_down=m_w_down, ln2_g=m_ln2_g, ln2_b=m_ln2_b)
    var = dict(w_in=v_w_in, b_in=v_b_in, w_proj_sb=v_w_proj_sb, w_proj_fox=v_w_proj_fox, w_out=v_w_out, ln1_g=v_ln1_g,
               ln1_b=v_ln1_b, w_up=v_w_up, w_conv=v_w_conv, b_conv=v_b_conv, w_down=v_w_down, ln2_g=v_ln2_g, ln2_b=v_ln2_b)
    shard = {n: wts[n].reshape(wts[n].shape[-2:]) for n in WEIGHTS}

    gathered = _all_gather([shard[n].astype(BF16) if n in MATMUL_OPERANDS else shard[n] for n in SHARDED],
                           "gather_weights")
    full = {n: _join(g, n) for n, g in zip(SHARDED, gathered)}
    e = full["w_proj_sb"].shape[0]
    h = e // HEAD_DIM
    d = full["w_out"].shape[0]
    nq = 6 * e

    def cut_in(a, pad):
        fcols = a[:, nq:nq + h]
        if pad:
            fcols = jnp.pad(fcols, ((0, 0), (0, LANES - h)))
        return a[:, :nq], a[:, nq + h:], fcols

    w_qkv, w_g, w_f = cut_in(full["w_in"], True)
    b_qkv, b_g, b_f = cut_in(shard["b_in"], True)
    w = dict(w_qkv=w_qkv, w_g=w_g, w_f=w_f, b_qkv=b_qkv, b_g=b_g, b_f=b_f,
             wp_sb=full["w_proj_sb"], wp_fx=full["w_proj_fox"], w_out=full["w_out"], w_up=full["w_up"],
             w_down=full["w_down"], w_conv=full["w_conv"], b_conv=shard["b_conv"], ln1_g=shard["ln1_g"], ln1_b=shard["ln1_b"],
             ln2_g=shard["ln2_g"], ln2_b=shard["ln2_b"])

    loss_local, grad_x, gr = _layer_step(x, loss_target, w, min(256, x.shape[1]))
    loss = lax.psum(loss_local[0, 0], ("x", "y", "c"))

    local = dict(
        w_in=jnp.concatenate([gr["w_qkv"], gr["w_f"][:, :h], gr["w_g"]], axis=1),
        b_in=jnp.concatenate([gr["b_qkv"], gr["b_f"][:, :h], gr["b_g"]], axis=1),
        w_proj_sb=gr["wp_sb"], w_proj_fox=gr["wp_fx"], w_out=gr["w_out"], w_up=gr["w_up"], w_conv=gr["w_conv"],
        w_down=gr["w_down"], ln1_g=gr["ln1_g"], ln1_b=gr["ln1_b"], b_conv=gr["b_conv"], ln2_g=gr["ln2_g"],
        ln2_b=gr["ln2_b"])

    landed = _exchange([_cut(local[n], n).astype(BF16 if n in MATMUL_OPERANDS else F32) for n in SHARDED],
                       "exchange_grads")
    gsum = dict(zip(SHARDED, landed))

    parts = _all_gather([jnp.concatenate([local[n] for n in REPLICATED], axis=1)], "gather_small_grads")[0]
    off = 0
    for n in REPLICATED:
        gsum[n] = parts[:, :, off:off + shard[n].size]
        off += shard[n].size

    grads, deltas, new_m, new_v = [], [], [], []
    for n in WEIGHTS:
        shp = wts[n].shape
        g, dl, mn, vn = _adamw(shard[n], gsum[n], mom[n].reshape(shard[n].shape), var[n].reshape(shard[n].shape),
                               "adamw_" + n)
        grads.append(g.reshape(shp))
        deltas.append(dl.reshape(shp))
        new_m.append(mn.reshape(shp))
        new_v.append(vn.reshape(shp))
    return (loss, grad_x, *grads, *deltas, *new_m, *new_v)
```

```python
import functools
import math

import jax
import jax.numpy as jnp
from jax import lax
from jax.experimental import pallas as pl
from jax.experimental.pallas import tpu as pltpu

F32 = jnp.float32
BF16 = jnp.bfloat16

HEAD_DIM = 64
LN_EPS = 1e-5
DEPTH = 1
ALPHA = (2.0 * DEPTH) ** 0.25
ADAM_LR, ADAM_B1, ADAM_B2, ADAM_EPS, ADAM_WD, ADAM_STEP = 0.001, 0.9, 0.999, 1e-08, 0.01, 10
N_DEV = 8
LANES = 128
SUBLANES = 8
HALO = 16
VMEM_LIMIT = 56 * 1024 * 1024

NN = ((1,), (0,))
NT = ((1,), (1,))
TN = ((0,), (0,))


def _dot(a, b, dims):
    return lax.dot_general(a, b, (dims, ((), ())), preferred_element_type=F32)


def _params(*sem):
    return pltpu.CompilerParams(dimension_semantics=sem, vmem_limit_bytes=VMEM_LIMIT)


def _iotas(blk):
    row = lax.broadcasted_iota(jnp.int32, (blk, blk), 0)
    col = lax.broadcasted_iota(jnp.int32, (blk, blk), 1)
    return row, col


def _sb_terms(z):
    e = jnp.exp(-jnp.abs(z))
    lb = jnp.minimum(z, 0.0) - jnp.log(1.0 + e)
    return lb, lb - z, e


def _pair_specs(s, blk, e, pp, branch):
    w = pp * LANES
    nq = s // blk
    ng = e // w
    base = 3 * branch * ng
    q_in = pl.BlockSpec((blk, w), lambda b, g, i: (b * nq + i, base + g))
    k_in = pl.BlockSpec((s, w), lambda b, g, i: (b, base + ng + g))
    v_in = pl.BlockSpec((s, w), lambda b, g, i: (b, base + 2 * ng + g))
    q_out = pl.BlockSpec((blk, w), lambda b, g, i: (b * nq + i, g))
    kv_out = pl.BlockSpec((s, w), lambda b, g, i: (b, g))
    rows = pl.BlockSpec((2 * pp, blk, 1), lambda b, g, i: (b * ng + g, i, 0))
    krow = pl.BlockSpec((2 * pp, 1, s), lambda b, g, i: (b * ng + g, 0, 0))
    return q_in, k_in, v_in, q_out, kv_out, rows, krow


def _half_masks(x):
    low = lax.broadcasted_iota(jnp.int32, x.shape, 1) < HEAD_DIM
    zero = jnp.zeros_like(x)
    return jnp.concatenate([jnp.where(low, x, zero), jnp.where(low, zero, x)], axis=0)


def _tri_sums(xs, tri):
    hi = [x.astype(BF16) for x in xs]
    lo = [(x - h.astype(F32)).astype(BF16) for x, h in zip(xs, hi)]
    n = len(xs)
    blk = xs[0].shape[0]
    r = _dot(jnp.concatenate(hi + lo, axis=0), tri, NN)
    return [r[i * blk:(i + 1) * blk] + r[(n + i) * blk:(n + i + 1) * blk] for i in range(n)]


def _sb_fwd(qkv, b, s, e, blk, pp):
    scale = HEAD_DIM ** -0.5
    nh = 2 * pp
    t = b * s

    def body(q_ref, k_ref, v_ref, o_ref, tot_ref, first_ref):
        qi = pl.program_id(2)
        qm = [_half_masks((q_ref[:, p * LANES:(p + 1) * LANES] * scale).astype(BF16)) for p in range(pp)]
        row, col = _iotas(blk)
        strict = col < row
        after = (row > col).astype(BF16)

        def block(j, carry, diag):
            off = pl.multiple_of(j * blk, blk)
            o_acc, run = carry
            zz = [_dot(qm[p], k_ref[pl.ds(off, blk), p * LANES:(p + 1) * LANES], NT) for p in range(pp)]
            z = [zz[h // 2][(h % 2) * blk:(h % 2 + 1) * blk] for h in range(nh)]
            terms = [_sb_terms(z[h]) for h in range(nh)]
            lom = [jnp.where(strict, terms[h][1], 0.0) if diag else terms[h][1] for h in range(nh)]
            sfx = _tri_sums(lom, after)
            a = [jnp.exp(terms[h][0] + sfx[h] + run[h]) for h in range(nh)]
            if diag:
                a = [jnp.where(strict, a[h], 0.0) for h in range(nh)]
            ab = [a[h].astype(BF16) for h in range(nh)]
            o_new = tuple(
                o_acc[p] + _dot(jnp.concatenate([ab[2 * p], ab[2 * p + 1]], axis=1),
                                _half_masks(v_ref[pl.ds(off, blk), p * LANES:(p + 1) * LANES]), NN)
                for p in range(pp))
            return o_new, tuple(run[h] + sfx[h][:, 0:1] + lom[h][:, 0:1] for h in range(nh))

        def alive(run):
            m = run[0]
            for h in range(1, nh):
                m = jnp.maximum(m, run[h])
            return jnp.max(m) > DEAD

        o_acc, run = block(qi, ((jnp.zeros((blk, LANES), F32),) * pp, (jnp.zeros((blk, 1), F32),) * nh), True)

        def step(c):
            j, _, o_acc, run = c
            o_acc, run = block(j, (o_acc, run), False)
            return j - 1, alive(run), o_acc, run

        j, _, o_acc, run = lax.while_loop(lambda c: jnp.logical_and(c[0] >= 0, c[1]), step,
                                          (qi - 1, alive(run), o_acc, run))
        for p in range(pp):
            o_ref[:, p * LANES:(p + 1) * LANES] = o_acc[p].astype(o_ref.dtype)
        for h in range(nh):
            tot_ref[h] = run[h]
            first_ref[h] = jnp.zeros((blk, 1), F32) + (j + 1).astype(F32)

    q_in, k_in, v_in, q_out, _, rows, _ = _pair_specs(s, blk, e, pp, 0)
    return pl.pallas_call(
        body, name="sb_fwd", grid=(b, e // (pp * LANES), s // blk),
        in_specs=[q_in, k_in, v_in], out_specs=[q_out, rows, rows],
        out_shape=[jax.ShapeDtypeStruct((t, e), BF16)] + [jax.ShapeDtypeStruct((b * e // HEAD_DIM, s, 1), F32)] * 2,
        compiler_params=_params("parallel", "parallel", "arbitrary"),
    )(qkv, qkv, qkv)


def _sb_bwd(qkv, do, tot, first, b, s, e, blk, pp):
    scale = HEAD_DIM ** -0.5
    nh = 2 * pp
    t = b * s
    nq = s // blk

    def body(q_ref, k_ref, v_ref, do_ref, tot_ref, first_ref, dq_ref, dk_ref, dv_ref, dk_acc, dv_acc):
        qi = pl.program_id(2)

        @pl.when(qi == 0)
        def _():
            dk_acc[...] = jnp.zeros_like(dk_acc)
            dv_acc[...] = jnp.zeros_like(dv_acc)

        qm = [_half_masks((q_ref[:, p * LANES:(p + 1) * LANES] * scale).astype(BF16)) for p in range(pp)]
        dom = [_half_masks(do_ref[:, p * LANES:(p + 1) * LANES].astype(BF16)) for p in range(pp)]
        tot_t = [tot_ref[h] for h in range(nh)]
        row, col = _iotas(blk)
        strict = col < row
        upto = (row <= col).astype(BF16)
        before = (row < col).astype(BF16)

        def block(j, carry, diag):
            off = pl.multiple_of(j * blk, blk)
            dq_acc, cl, cg = carry
            kp = [k_ref[pl.ds(off, blk), p * LANES:(p + 1) * LANES] for p in range(pp)]
            vp = [v_ref[pl.ds(off, blk), p * LANES:(p + 1) * LANES] for p in range(pp)]
            zz = [_dot(qm[p], kp[p], NT) for p in range(pp)]
            dd = [_dot(dom[p], vp[p], NT) for p in range(pp)]
            z = [zz[h // 2][(h % 2) * blk:(h % 2 + 1) * blk] for h in range(nh)]
            da = [dd[h // 2][(h % 2) * blk:(h % 2 + 1) * blk] for h in range(nh)]
            terms = [_sb_terms(z[h]) for h in range(nh)]
            lom = [jnp.where(strict, terms[h][1], 0.0) if diag else terms[h][1] for h in range(nh)]
            pre = _tri_sums(lom, upto)
            a = [jnp.exp(terms[h][0] + (tot_t[h] - cl[h] - pre[h])) for h in range(nh)]
            if diag:
                a = [jnp.where(strict, a[h], 0.0) for h in range(nh)]
            g = [a[h] * da[h] for h in range(nh)]
            pw = _tri_sums(g, before)
            dzb = []
            for h in range(nh):
                ex = terms[h][2]
                r = 1.0 / (1.0 + ex)
                er = ex * r
                pos = z[h] >= 0.0
                dz = g[h] * jnp.where(pos, er, r) - (cg[h] + pw[h]) * jnp.where(pos, r, er)
                if diag:
                    dz = jnp.where(strict, dz, 0.0)
                dzb.append(dz.astype(BF16))
            ab = [a[h].astype(BF16) for h in range(nh)]
            for p in range(pp):
                cols = slice(p * LANES, (p + 1) * LANES)
                dk_acc[pl.ds(off, blk), cols] += _dot(jnp.concatenate([dzb[2 * p], dzb[2 * p + 1]], axis=0), qm[p], TN)
                dv_acc[pl.ds(off, blk), cols] += _dot(jnp.concatenate([ab[2 * p], ab[2 * p + 1]], axis=0), dom[p], TN)
            dq_new = tuple(dq_acc[p] + _dot(jnp.concatenate([dzb[2 * p], dzb[2 * p + 1]], axis=1), _half_masks(kp[p]), NN)
                           for p in range(pp))
            return (dq_new, tuple(cl[h] + pre[h][:, blk - 1:blk] for h in range(nh)),
                    tuple(cg[h] + pw[h][:, blk - 1:blk] + g[h][:, blk - 1:blk] for h in range(nh)))

        zero1 = (jnp.zeros((blk, 1), F32),) * nh
        j0 = jnp.clip(jnp.max(first_ref[0]).astype(jnp.int32), 0, qi)
        carry = lax.fori_loop(j0, qi, lambda j, c: block(j, c, False), ((jnp.zeros((blk, LANES), F32),) * pp, zero1, zero1))
        dq_acc, _, _ = block(qi, carry, True)
        for p in range(pp):
            dq_ref[:, p * LANES:(p + 1) * LANES] = (dq_acc[p] * scale).astype(BF16)

        @pl.when(qi == nq - 1)
        def _():
            dk_ref[...] = dk_acc[...].astype(BF16)
            dv_ref[...] = dv_acc[...].astype(BF16)

    q_in, k_in, v_in, q_out, kv_out, rows, _ = _pair_specs(s, blk, e, pp, 0)
    w = pp * LANES
    return pl.pallas_call(
        body, name="sb_bwd", grid=(b, e // w, nq),
        in_specs=[q_in, k_in, v_in, q_out, rows, rows], out_specs=[q_out, kv_out, kv_out],
        out_shape=[jax.ShapeDtypeStruct((t, e), BF16)] * 3,
        scratch_shapes=[pltpu.VMEM((s, w), F32), pltpu.VMEM((s, w), F32)],
        compiler_params=_params("parallel", "parallel", "arbitrary"),
    )(qkv, qkv, qkv, do, tot, first)


NEG = -1e30
DEAD = -110.0


def _ride_along(comm, src_refs, dst_refs, sems, first, last):
    if comm is None:
        return lambda: None

    @pl.when(first)
    def _():
        for cp in _peer_copies(comm[0], src_refs, dst_refs, *sems):
            cp.start()

    def finish():
        @pl.when(last)
        def _():
            for cp in _peer_copies(comm[0], src_refs, dst_refs, *sems):
                cp.wait()

    return finish


def _grid_ends(grid):
    ids = [pl.program_id(a) for a in range(len(grid))]
    first = functools.reduce(jnp.logical_and, [i == 0 for i in ids])
    last = functools.reduce(jnp.logical_and, [i == g - 1 for i, g in zip(ids, grid)])
    return first, last


def _fox_fwd(qkv, c, ct, b, s, e, blk, pp, comm=None):
    scale = HEAD_DIM ** -0.5
    nh = 2 * pp
    t = b * s
    nc = len(comm[1]) if comm else 0
    grid = (b, e // (pp * LANES), s // blk)

    def body(*refs):
        q_ref, k_ref, v_ref, c_ref, ct_ref = refs[:5]
        o_ref, lse_ref = refs[5 + nc:7 + nc]
        finish = _ride_along(comm, refs[5:5 + nc], refs[7 + nc:7 + 2 * nc], refs[7 + 2 * nc:], *_grid_ends(grid))
        qi = pl.program_id(2)
        qm = [_half_masks((q_ref[:, p * LANES:(p + 1) * LANES] * scale).astype(BF16)) for p in range(pp)]
        cq = [c_ref[h] for h in range(nh)]
        row, col = _iotas(blk)
        causal = col <= row
        low = lax.broadcasted_iota(jnp.int32, (blk, LANES), 1) < HEAD_DIM

        def block(j, carry, diag):
            off = pl.multiple_of(j * blk, blk)
            m, l, acc = carry
            zz = [_dot(qm[p], k_ref[pl.ds(off, blk), p * LANES:(p + 1) * LANES], NT) for p in range(pp)]
            z = [zz[h // 2][(h % 2) * blk:(h % 2 + 1) * blk] + (cq[h] - ct_ref[h, :, pl.ds(off, blk)]) for h in range(nh)]
            if diag:
                z = [jnp.where(causal, z[h], NEG) for h in range(nh)]
            m_new = tuple(jnp.maximum(m[h], jnp.max(z[h], axis=1, keepdims=True)) for h in range(nh))
            w = [jnp.exp(m[h] - m_new[h]) for h in range(nh)]
            pr = [jnp.exp(z[h] - m_new[h]) for h in range(nh)]
            pb = [pr[h].astype(BF16) for h in range(nh)]
            pv = [_dot(jnp.concatenate([pb[2 * p], pb[2 * p + 1]], axis=1),
                       _half_masks(v_ref[pl.ds(off, blk), p * LANES:(p + 1) * LANES]), NN) for p in range(pp)]
            acc_new = tuple(jnp.where(low, w[2 * p], w[2 * p + 1]) * acc[p] + pv[p] for p in range(pp))
            l_new = tuple(w[h] * l[h] + jnp.sum(pr[h], axis=1, keepdims=True) for h in range(nh))
            return m_new, l_new, acc_new

        zero = ((jnp.full((blk, 1), NEG, F32),) * nh, (jnp.zeros((blk, 1), F32),) * nh, (jnp.zeros((blk, LANES), F32),) * pp)
        carry = block(qi, zero, True)
        m, l, acc = lax.fori_loop(0, qi, lambda j, c_: block(j, c_, False), carry)
        for p in range(pp):
            o_ref[:, p * LANES:(p + 1) * LANES] = acc[p] / jnp.where(low, l[2 * p], l[2 * p + 1])
        for h in range(nh):
            lse_ref[h] = m[h] + jnp.log(l[h])
        finish()

    q_in, k_in, v_in, q_out, _, rows, krow = _pair_specs(s, blk, e, pp, 1)
    outs, sems = _peer_shapes(comm[0], comm[1]) if comm else ([], [])
    return pl.pallas_call(
        body, name="fox_fwd", grid=grid,
        in_specs=[q_in, k_in, v_in, rows, krow] + [ANY] * nc, out_specs=[q_out, rows] + [ANY] * nc,
        out_shape=[jax.ShapeDtypeStruct((t, e), F32), jax.ShapeDtypeStruct((b * e // HEAD_DIM, s, 1), F32)] + outs,
        scratch_shapes=sems,
        compiler_params=_params("arbitrary", "arbitrary", "arbitrary"),
    )(qkv, qkv, qkv, c, ct, *(comm[1] if comm else ()))


def _fox_bwd(qkv, c, ct, do, o, lse, b, s, e, blk, pp, comm=None):
    scale = HEAD_DIM ** -0.5
    nh = 2 * pp
    t = b * s
    nq = s // blk
    nc = len(comm[1]) if comm else 0
    w = pp * LANES
    grid = (b, e // w, nq)

    def body(*refs):
        q_ref, k_ref, v_ref, c_ref, ct_ref, do_ref, o_ref, lse_ref = refs[:8]
        dq_ref, dk_ref, dv_ref, dct_ref = refs[8 + nc:12 + nc]
        dk_acc, dv_acc = refs[12 + 2 * nc:14 + 2 * nc]
        finish = _ride_along(comm, refs[8:8 + nc], refs[12 + nc:12 + 2 * nc], refs[14 + 2 * nc:], *_grid_ends(grid))
        qi = pl.program_id(2)

        @pl.when(qi == 0)
        def _():
            dk_acc[...] = jnp.zeros_like(dk_acc)
            dv_acc[...] = jnp.zeros_like(dv_acc)
            dct_ref[...] = jnp.zeros_like(dct_ref)

        qm = [_half_masks((q_ref[:, p * LANES:(p + 1) * LANES] * scale).astype(BF16)) for p in range(pp)]
        dob = [do_ref[:, p * LANES:(p + 1) * LANES].astype(BF16) for p in range(pp)]
        dom = [_half_masks(dob[p]) for p in range(pp)]
        low = lax.broadcasted_iota(jnp.int32, (blk, LANES), 1) < HEAD_DIM
        delta = []
        for p in range(pp):
            prod = dob[p].astype(F32) * o_ref[:, p * LANES:(p + 1) * LANES]
            delta.append(jnp.sum(jnp.where(low, prod, 0.0), axis=1, keepdims=True))
            delta.append(jnp.sum(jnp.where(low, 0.0, prod), axis=1, keepdims=True))
        cq = [c_ref[h] for h in range(nh)]
        lse_t = [lse_ref[h] for h in range(nh)]
        row, col = _iotas(blk)
        causal = col <= row

        def block(j, dq_acc, diag):
            off = pl.multiple_of(j * blk, blk)
            kp = [k_ref[pl.ds(off, blk), p * LANES:(p + 1) * LANES] for p in range(pp)]
            zz = [_dot(qm[p], kp[p], NT) for p in range(pp)]
            dd = [_dot(dom[p], v_ref[pl.ds(off, blk), p * LANES:(p + 1) * LANES], NT) for p in range(pp)]
            z = [zz[h // 2][(h % 2) * blk:(h % 2 + 1) * blk] + (cq[h] - ct_ref[h, :, pl.ds(off, blk)]) for h in range(nh)]
            pr = [jnp.exp(z[h] - lse_t[h]) for h in range(nh)]
            if diag:
                pr = [jnp.where(causal, pr[h], 0.0) for h in range(nh)]
            ds = [pr[h] * (dd[h // 2][(h % 2) * blk:(h % 2 + 1) * blk] - delta[h]) for h in range(nh)]
            dsb = [ds[h].astype(BF16) for h in range(nh)]
            pb = [pr[h].astype(BF16) for h in range(nh)]
            for p in range(pp):
                cols = slice(p * LANES, (p + 1) * LANES)
                dk_acc[pl.ds(off, blk), cols] += _dot(jnp.concatenate([dsb[2 * p], dsb[2 * p + 1]], axis=0), qm[p], TN)
                dv_acc[pl.ds(off, blk), cols] += _dot(jnp.concatenate([pb[2 * p], pb[2 * p + 1]], axis=0), dom[p], TN)
            for h in range(nh):
                dct_ref[h, :, pl.ds(off, blk)] -= jnp.sum(ds[h], axis=0, keepdims=True)
            return tuple(dq_acc[p] + _dot(jnp.concatenate([dsb[2 * p], dsb[2 * p + 1]], axis=1), _half_masks(kp[p]), NN)
                         for p in range(pp))

        dq_acc = lax.fori_loop(0, qi, lambda j, a: block(j, a, False), (jnp.zeros((blk, LANES), F32),) * pp)
        dq_acc = block(qi, dq_acc, True)
        for p in range(pp):
            dq_ref[:, p * LANES:(p + 1) * LANES] = (dq_acc[p] * scale).astype(BF16)

        @pl.when(qi == nq - 1)
        def _():
            dk_ref[...] = dk_acc[...].astype(BF16)
            dv_ref[...] = dv_acc[...].astype(BF16)

        finish()

    q_in, k_in, v_in, q_out, kv_out, rows, krow = _pair_specs(s, blk, e, pp, 1)
    outs, sems = _peer_shapes(comm[0], comm[1]) if comm else ([], [])
    return pl.pallas_call(
        body, name="fox_bwd", grid=grid,
        in_specs=[q_in, k_in, v_in, rows, krow, q_out, q_out, rows] + [ANY] * nc,
        out_specs=[q_out, kv_out, kv_out, krow] + [ANY] * nc,
        out_shape=[jax.ShapeDtypeStruct((t, e), BF16)] * 3 + [jax.ShapeDtypeStruct((b * e // HEAD_DIM, 1, s), F32)] + outs,
        scratch_shapes=[pltpu.VMEM((s, w), F32), pltpu.VMEM((s, w), F32)] + sems,
        compiler_params=_params("arbitrary", "arbitrary", "arbitrary"),
    )(qkv, qkv, qkv, c, ct, do, o, lse, *(comm[1] if comm else ()))


def _scan_rows(f2, group, mode, d2=None):
    n = f2.shape[0]

    def body(*refs):
        f_ref, o_ref = refs[0], refs[-1]
        f = f_ref[...]
        row, col = _iotas(LANES)
        grow = lax.broadcasted_iota(jnp.int32, (n, n), 0)
        gcol = lax.broadcasted_iota(jnp.int32, (n, n), 1)
        same = (grow // group) == (gcol // group)
        e = jnp.exp(-jnp.abs(f))
        if mode == "fwd":
            x = jnp.minimum(f, 0.0) - jnp.log1p(e)
            within = (row <= col).astype(F32)
            earlier = (same & (gcol < grow)).astype(F32)
        else:
            x = refs[1][...]
            within = (row >= col).astype(F32)
            earlier = (same & (gcol > grow)).astype(F32)
        y = jnp.dot(x, within, preferred_element_type=F32, precision=lax.Precision.HIGHEST)
        tot = jnp.sum(x, axis=1, keepdims=True)
        y = y + jnp.dot(earlier, tot, preferred_element_type=F32, precision=lax.Precision.HIGHEST)
        if mode == "bwd":
            r = 1.0 / (1.0 + e)
            y = y * jnp.where(f >= 0.0, e * r, r)
        o_ref[...] = y

    args = (f2,) if mode == "fwd" else (f2, d2)
    return pl.pallas_call(body, name="logf_" + mode, out_shape=jax.ShapeDtypeStruct(f2.shape, F32),
                          compiler_params=_params())(*args)


def _matmul(a, b, dims, *, tm, tn, tk, out_dtype, name, bias=None, res=None, res_scale=1.0, b_outer=False):
    def ij(g0, g1):
        return (g1, g0) if b_outer else (g0, g1)

    if dims == NN:
        (m, kk), n = a.shape, b.shape[1]
        a_spec = pl.BlockSpec((tm, tk), lambda g0, g1, k: (ij(g0, g1)[0], k))
        b_spec = pl.BlockSpec((tk, tn), lambda g0, g1, k: (k, ij(g0, g1)[1]))
    elif dims == NT:
        (m, kk), n = a.shape, b.shape[0]
        a_spec = pl.BlockSpec((tm, tk), lambda g0, g1, k: (ij(g0, g1)[0], k))
        b_spec = pl.BlockSpec((tn, tk), lambda g0, g1, k: (ij(g0, g1)[1], k))
    else:
        (kk, m), n = a.shape, b.shape[1]
        a_spec = pl.BlockSpec((tk, tm), lambda g0, g1, k: (k, ij(g0, g1)[0]))
        b_spec = pl.BlockSpec((tk, tn), lambda g0, g1, k: (k, ij(g0, g1)[1]))
    assert m % tm == 0 and n % tn == 0 and kk % tk == 0, (name, m, n, kk, tm, tn, tk)
    nk = kk // tk
    extras, extra_specs = [], []
    if bias is not None:
        extras.append(bias)
        extra_specs.append(pl.BlockSpec((1, tn), lambda g0, g1, k: (0, ij(g0, g1)[1])))
    if res is not None:
        extras.append(res)
        extra_specs.append(pl.BlockSpec((tm, tn), lambda g0, g1, k: ij(g0, g1)))

    def finish(out, rest, o_ref):
        idx = 0
        if bias is not None:
            out = out + rest[idx][...]
            idx += 1
        if res is not None:
            out = out + res_scale * rest[idx][...]
        o_ref[...] = out.astype(o_ref.dtype)

    def body_single(a_ref, b_ref, *rest):
        finish(_dot(a_ref[...].astype(BF16), b_ref[...].astype(BF16), dims), rest, rest[-1])

    def body_acc(a_ref, b_ref, *rest):
        o_ref, acc_ref = rest[-2], rest[-1]
        k = pl.program_id(2)
        part = _dot(a_ref[...].astype(BF16), b_ref[...].astype(BF16), dims)

        @pl.when(k == 0)
        def _():
            acc_ref[...] = part

        @pl.when(k > 0)
        def _():
            acc_ref[...] += part

        @pl.when(k == nk - 1)
        def _():
            finish(acc_ref[...], rest, o_ref)

    grid = (n // tn, m // tm, nk) if b_outer else (m // tm, n // tn, nk)
    return pl.pallas_call(
        body_single if nk == 1 else body_acc, name=name, grid=grid,
        in_specs=[a_spec, b_spec] + extra_specs,
        out_specs=pl.BlockSpec((tm, tn), lambda g0, g1, k: ij(g0, g1)),
        out_shape=jax.ShapeDtypeStruct((m, n), out_dtype),
        scratch_shapes=[] if nk == 1 else [pltpu.VMEM((tm, tn), F32)],
        compiler_params=_params("parallel", "parallel", "arbitrary"),
    )(a, b, *extras)


def _sigmoid(x):
    e = jnp.exp(-jnp.abs(x))
    r = 1.0 / (1.0 + e)
    return jnp.where(x >= 0.0, r, e * r)


def _proj_gate_fwd(o_sb, o_fx, wp_sb, wp_fx, g, tm):
    t, e = o_sb.shape
    d = wp_sb.shape[1]

    def body(osb_ref, ofx_ref, wsb_ref, wfx_ref, gsb_ref, gfx_ref, mg_ref, ysb_ref, yfx_ref):
        ysb = _dot(osb_ref[...].astype(BF16), wsb_ref[...], NN)
        yfx = _dot(ofx_ref[...].astype(BF16), wfx_ref[...], NN)
        ysb_ref[...] = ysb
        yfx_ref[...] = yfx
        mg_ref[...] = (_sigmoid(gsb_ref[...]) * ysb + _sigmoid(gfx_ref[...]) * yfx).astype(BF16)

    rows_e = pl.BlockSpec((tm, e), lambda i: (i, 0))
    rows_d = pl.BlockSpec((tm, d), lambda i: (i, 0))
    w_spec = pl.BlockSpec((e, d), lambda i: (0, 0))
    return pl.pallas_call(
        body, name="proj_gate_fwd", grid=(t // tm,),
        in_specs=[rows_e, rows_e, w_spec, w_spec, rows_d, pl.BlockSpec((tm, d), lambda i: (i, 1))],
        out_specs=[rows_d, rows_d, rows_d],
        out_shape=[jax.ShapeDtypeStruct((t, d), BF16), jax.ShapeDtypeStruct((t, d), F32), jax.ShapeDtypeStruct((t, d), F32)],
        compiler_params=_params("parallel"),
    )(o_sb, o_fx, wp_sb, wp_fx, g, g)


def _gate_bwd(dmg, y_sb, y_fx, g, tm):
    t, d = dmg.shape

    def body(dm_ref, ysb_ref, yfx_ref, gsb_ref, gfx_ref, dysb_ref, dyfx_ref, dg_ref):
        dm = dm_ref[...]
        ssb = _sigmoid(gsb_ref[...])
        sfx = _sigmoid(gfx_ref[...])
        dysb_ref[...] = (dm * ssb).astype(BF16)
        dyfx_ref[...] = (dm * sfx).astype(BF16)
        dg_ref[:, 0:d] = (dm * ysb_ref[...] * ssb * (1.0 - ssb)).astype(BF16)
        dg_ref[:, d:2 * d] = (dm * yfx_ref[...] * sfx * (1.0 - sfx)).astype(BF16)

    rows = pl.BlockSpec((tm, d), lambda i: (i, 0))
    rows1 = pl.BlockSpec((tm, d), lambda i: (i, 1))
    return pl.pallas_call(
        body, name="gate_bwd", grid=(t // tm,),
        in_specs=[rows, rows, rows, rows, rows1],
        out_specs=[rows, rows, pl.BlockSpec((tm, 2 * d), lambda i: (i, 0))],
        out_shape=[jax.ShapeDtypeStruct((t, d), BF16)] * 2 + [jax.ShapeDtypeStruct((t, 2 * d), BF16)],
        compiler_params=_params("parallel"),
    )(dmg, y_sb, y_fx, g, g)


def _mm_res_ln(a, w, xres, gamma, beta, tm, name):
    t, kk = a.shape
    d = w.shape[1]

    def body(a_ref, w_ref, x_ref, g_ref, b_ref, xn_ref, xh_ref, rs_ref, xb_ref):
        r = ALPHA * x_ref[...] + _dot(a_ref[...].astype(BF16), w_ref[...], NN)
        mean = jnp.mean(r, axis=1, keepdims=True)
        cen = r - mean
        rstd = lax.rsqrt(jnp.mean(cen * cen, axis=1, keepdims=True) + LN_EPS)
        xh = cen * rstd
        xn = xh * g_ref[...] + b_ref[...]
        xh_ref[...] = xh
        xn_ref[...] = xn
        xb_ref[...] = xn.astype(BF16)
        rs_ref[...] = rstd

    rows_d = pl.BlockSpec((tm, d), lambda i: (i, 0))
    vec = pl.BlockSpec((1, d), lambda i: (0, 0))
    return pl.pallas_call(
        body, name=name, grid=(t // tm,),
        in_specs=[pl.BlockSpec((tm, kk), lambda i: (i, 0)), pl.BlockSpec((kk, d), lambda i: (0, 0)), rows_d, vec, vec],
        out_specs=[rows_d, rows_d, pl.BlockSpec((tm, 1), lambda i: (i, 0)), rows_d],
        out_shape=[jax.ShapeDtypeStruct((t, d), F32), jax.ShapeDtypeStruct((t, d), F32), jax.ShapeDtypeStruct((t, 1), F32),
                   jax.ShapeDtypeStruct((t, d), BF16)],
        compiler_params=_params("parallel"),
    )(a, w, xres, gamma, beta)


def _ln_bwd_math(dy, xh, rstd, gamma):
    dxh = dy * gamma
    m1 = jnp.mean(dxh, axis=1, keepdims=True)
    m2 = jnp.mean(dxh * xh, axis=1, keepdims=True)
    return rstd * (dxh - m1 - xh * m2)


def _rowsum8(x):
    tm, n = x.shape
    return jnp.sum(x.reshape(tm // SUBLANES, SUBLANES, n), axis=0)


def _fold8(ref):
    ref[0:1, :] = jnp.sum(ref[...], axis=0, keepdims=True)


def _loss_ln_bwd(x2, xh, rstd, gamma, target, tm):
    t, d = x2.shape

    def body(x_ref, xh_ref, rs_ref, g_ref, tg_ref, dr_ref, dg_ref, db_ref, ls_ref, drb_ref):
        @pl.when(pl.program_id(0) == 0)
        def _():
            dg_ref[...] = jnp.zeros_like(dg_ref)
            db_ref[...] = jnp.zeros_like(db_ref)
            ls_ref[...] = jnp.zeros_like(ls_ref)

        err = x_ref[...] - tg_ref[...]
        xh = xh_ref[...]
        dy = err * (1.0 / d)
        dr = _ln_bwd_math(dy, xh, rs_ref[...], g_ref[...])
        dr_ref[...] = dr
        drb_ref[...] = dr.astype(BF16)
        dg_ref[...] += _rowsum8(dy * xh)
        db_ref[...] += _rowsum8(dy)
        sq = _rowsum8(err * err)
        part = sq[:, 0:LANES]
        for j in range(1, d // LANES):
            part = part + sq[:, j * LANES:(j + 1) * LANES]
        ls_ref[...] += part * (0.5 / d)

        @pl.when(pl.program_id(0) == t // tm - 1)
        def _():
            _fold8(dg_ref)
            _fold8(db_ref)
            ls_ref[0:1, 0:1] = jnp.sum(jnp.sum(ls_ref[...], axis=0, keepdims=True), axis=1, keepdims=True)

    rows = pl.BlockSpec((tm, d), lambda i: (i, 0))
    acc = pl.BlockSpec((SUBLANES, d), lambda i: (0, 0))
    return pl.pallas_call(
        body, name="loss_ln_bwd", grid=(t // tm,),
        in_specs=[rows, rows, pl.BlockSpec((tm, 1), lambda i: (i, 0)), pl.BlockSpec((1, d), lambda i: (0, 0)), rows],
        out_specs=[rows, acc, acc, pl.BlockSpec((SUBLANES, LANES), lambda i: (0, 0)), rows],
        out_shape=[jax.ShapeDtypeStruct((t, d), F32), jax.ShapeDtypeStruct((SUBLANES, d), F32),
                   jax.ShapeDtypeStruct((SUBLANES, d), F32), jax.ShapeDtypeStruct((SUBLANES, LANES), F32),
                   jax.ShapeDtypeStruct((t, d), BF16)],
        compiler_params=_params("arbitrary"),
    )(x2, xh, rstd, gamma, target)


def _ln_bwd(dr_next, dlin, xh, rstd, gamma, tm):
    t, d = xh.shape

    def body(dn_ref, dl_ref, xh_ref, rs_ref, g_ref, dr_ref, dg_ref, db_ref, drb_ref):
        @pl.when(pl.program_id(0) == 0)
        def _():
            dg_ref[...] = jnp.zeros_like(dg_ref)
            db_ref[...] = jnp.zeros_like(db_ref)

        dy = ALPHA * dn_ref[...] + dl_ref[...]
        xh = xh_ref[...]
        dr = _ln_bwd_math(dy, xh, rs_ref[...], g_ref[...])
        dr_ref[...] = dr
        drb_ref[...] = dr.astype(BF16)
        dg_ref[...] += _rowsum8(dy * xh)
        db_ref[...] += _rowsum8(dy)

        @pl.when(pl.program_id(0) == t // tm - 1)
        def _():
            _fold8(dg_ref)
            _fold8(db_ref)

    rows = pl.BlockSpec((tm, d), lambda i: (i, 0))
    acc = pl.BlockSpec((SUBLANES, d), lambda i: (0, 0))
    return pl.pallas_call(
        body, name="ln_bwd", grid=(t // tm,),
        in_specs=[rows, rows, rows, pl.BlockSpec((tm, 1), lambda i: (i, 0)), pl.BlockSpec((1, d), lambda i: (0, 0))],
        out_specs=[rows, acc, acc, rows],
        out_shape=[jax.ShapeDtypeStruct((t, d), F32), jax.ShapeDtypeStruct((SUBLANES, d), F32),
                   jax.ShapeDtypeStruct((SUBLANES, d), F32), jax.ShapeDtypeStruct((t, d), BF16)],
        compiler_params=_params("arbitrary"),
    )(dr_next, dlin, xh, rstd, gamma)


def _shift_rows(x, halo, shift, row):
    out = pltpu.roll(x, shift, 0)
    n = halo.shape[0]
    for r in range(shift):
        out = jnp.where(row == r, halo[n - shift + r:n - shift + r + 1, :], out)
    return out


def _unshift_rows(x, halo, shift, row, tm):
    out = pltpu.roll(x, tm - shift, 0)
    for r in range(shift):
        out = jnp.where(row == tm - shift + r, halo[r:r + 1, :], out)
    return out


def _conv_pre(ug_ref, halo_ref, wc_ref, bc_ref, first, tm):
    ug = ug_ref[...].astype(F32)
    halo = jnp.where(first, 0.0, halo_ref[...].astype(F32))
    row = lax.broadcasted_iota(jnp.int32, ug.shape, 0)
    wc = wc_ref[...]
    um1 = _shift_rows(ug, halo, 1, row)
    um2 = _shift_rows(ug, halo, 2, row)
    c = bc_ref[...] + wc[2:3, :] * ug + wc[1:2, :] * um1 + wc[0:1, :] * um2
    return c, ug, um1, um2


INV_SQRT2 = 1.0 / math.sqrt(2.0)
INV_SQRT2PI = 1.0 / math.sqrt(2.0 * math.pi)


def _conv_glu_fwd(u, wc, bc, seq, tm):
    t, f2 = u.shape
    f = f2 // 2
    per_seq = seq // tm
    hb = tm // HALO

    def body(ug_ref, halo_ref, uv_ref, wc_ref, bc_ref, a_ref):
        first = (pl.program_id(0) % per_seq) == 0
        c, _, _, _ = _conv_pre(ug_ref, halo_ref, wc_ref, bc_ref, first, tm)
        gelu = 0.5 * c * (1.0 + lax.erf(c * INV_SQRT2))
        a_ref[...] = (gelu * uv_ref[...].astype(F32)).astype(BF16)

    return pl.pallas_call(
        body, name="conv_glu_fwd", grid=(t // tm,),
        in_specs=[pl.BlockSpec((tm, f), lambda i: (i, 0)),
                  pl.BlockSpec((HALO, f), lambda i: (jnp.maximum(i * hb - 1, 0), 0)),
                  pl.BlockSpec((tm, f), lambda i: (i, 1)),
                  pl.BlockSpec((3, f), lambda i: (0, 0)), pl.BlockSpec((1, f), lambda i: (0, 0))],
        out_specs=pl.BlockSpec((tm, f), lambda i: (i, 0)),
        out_shape=jax.ShapeDtypeStruct((t, f), BF16),
        compiler_params=_params("parallel"),
    )(u, u, u, wc, bc)


def _conv_glu_bwd1(u, da, wc, bc, seq, tm):
    t, f2 = u.shape
    f = f2 // 2
    per_seq = seq // tm
    hb = tm // HALO

    def body(ug_ref, halo_ref, uv_ref, da_ref, wc_ref, bc_ref, dc_ref, duv_ref):
        first = (pl.program_id(0) % per_seq) == 0
        c, _, _, _ = _conv_pre(ug_ref, halo_ref, wc_ref, bc_ref, first, tm)
        cdf = 0.5 * (1.0 + lax.erf(c * INV_SQRT2))
        pdf = jnp.exp(-0.5 * c * c) * INV_SQRT2PI
        da = da_ref[...].astype(F32)
        duv_ref[...] = (da * (c * cdf)).astype(BF16)
        dc_ref[...] = da * uv_ref[...].astype(F32) * (cdf + c * pdf)

    rows = pl.BlockSpec((tm, f), lambda i: (i, 0))
    return pl.pallas_call(
        body, name="conv_glu_bwd1", grid=(t // tm,),
        in_specs=[rows, pl.BlockSpec((HALO, f), lambda i: (jnp.maximum(i * hb - 1, 0), 0)),
                  pl.BlockSpec((tm, f), lambda i: (i, 1)), rows,
                  pl.BlockSpec((3, f), lambda i: (0, 0)), pl.BlockSpec((1, f), lambda i: (0, 0))],
        out_specs=[rows, rows],
        out_shape=[jax.ShapeDtypeStruct((t, f), F32), jax.ShapeDtypeStruct((t, f), BF16)],
        compiler_params=_params("parallel"),
    )(u, u, u, da, wc, bc)


def _conv_glu_bwd2(u, dc, wc, seq, tm):
    t, f2 = u.shape
    f = f2 // 2
    per_seq = seq // tm
    hb = tm // HALO
    nblk = t // SUBLANES

    def body(ug_ref, halo_ref, dc_ref, nxt_ref, wc_ref, dug_ref, w0_ref, w1_ref, w2_ref, b_ref):
        i = pl.program_id(0)

        @pl.when(i == 0)
        def _():
            for r in (w0_ref, w1_ref, w2_ref, b_ref):
                r[...] = jnp.zeros_like(r)

        first = (i % per_seq) == 0
        last = (i % per_seq) == per_seq - 1
        ug = ug_ref[...].astype(F32)
        halo = jnp.where(first, 0.0, halo_ref[...].astype(F32))
        nxt = jnp.where(last, 0.0, nxt_ref[...])
        row = lax.broadcasted_iota(jnp.int32, ug.shape, 0)
        dc = dc_ref[...]
        wc = wc_ref[...]
        dp1 = _unshift_rows(dc, nxt, 1, row, tm)
        dp2 = _unshift_rows(dc, nxt, 2, row, tm)
        dug_ref[...] = (wc[2:3, :] * dc + wc[1:2, :] * dp1 + wc[0:1, :] * dp2).astype(BF16)
        w2_ref[...] += _rowsum8(dc * ug)
        w1_ref[...] += _rowsum8(dc * _shift_rows(ug, halo, 1, row))
        w0_ref[...] += _rowsum8(dc * _shift_rows(ug, halo, 2, row))
        b_ref[...] += _rowsum8(dc)

        @pl.when(i == t // tm - 1)
        def _():
            for r in (w0_ref, w1_ref, w2_ref, b_ref):
                _fold8(r)

    rows = pl.BlockSpec((tm, f), lambda i: (i, 0))
    acc = pl.BlockSpec((SUBLANES, f), lambda i: (0, 0))
    return pl.pallas_call(
        body, name="conv_glu_bwd2", grid=(t // tm,),
        in_specs=[rows, pl.BlockSpec((HALO, f), lambda i: (jnp.maximum(i * hb - 1, 0), 0)),
                  rows, pl.BlockSpec((SUBLANES, f), lambda i: (jnp.minimum((i + 1) * (tm // SUBLANES), nblk - 1), 0)),
                  pl.BlockSpec((3, f), lambda i: (0, 0))],
        out_specs=[rows, acc, acc, acc, acc],
        out_shape=[jax.ShapeDtypeStruct((t, f), BF16)] + [jax.ShapeDtypeStruct((SUBLANES, f), F32)] * 4,
        compiler_params=_params("arbitrary"),
    )(u, u, dc, dc, wc)


def _colsum(x, tm, name):
    t, n = x.shape

    def body(x_ref, o_ref):
        @pl.when(pl.program_id(0) == 0)
        def _():
            o_ref[...] = jnp.zeros_like(o_ref)

        o_ref[...] += _rowsum8(x_ref[...].astype(F32))

        @pl.when(pl.program_id(0) == t // tm - 1)
        def _():
            _fold8(o_ref)

    return pl.pallas_call(
        body, name=name, grid=(t // tm,),
        in_specs=[pl.BlockSpec((tm, n), lambda i: (i, 0))],
        out_specs=pl.BlockSpec((SUBLANES, n), lambda i: (0, 0)),
        out_shape=jax.ShapeDtypeStruct((SUBLANES, n), F32),
        compiler_params=_params("arbitrary"),
    )(x)


def _adamw(w, gparts, m, v, name):
    p, r, c = gparts.shape
    tr = r
    for cand in (512, 256, 128, 64, 32, 16, 8):
        if cand * p <= 1024 and r % cand == 0 and r > cand:
            tr = cand
            break
    c1 = 1.0 - ADAM_B1 ** ADAM_STEP
    c2 = 1.0 - ADAM_B2 ** ADAM_STEP

    def body(w_ref, g_ref, m_ref, v_ref, go_ref, d_ref, mo_ref, vo_ref):
        g = g_ref[0].astype(F32)
        for i in range(1, p):
            g = g + g_ref[i].astype(F32)
        mn = ADAM_B1 * m_ref[...] + (1.0 - ADAM_B1) * g
        vn = ADAM_B2 * v_ref[...] + (1.0 - ADAM_B2) * (g * g)
        go_ref[...] = g
        mo_ref[...] = mn
        vo_ref[...] = vn
        d_ref[...] = -ADAM_LR * ((mn / c1) / (jnp.sqrt(vn / c2) + ADAM_EPS) + ADAM_WD * w_ref[...])

    blk = pl.BlockSpec((tr, c), lambda i: (i, 0))
    return pl.pallas_call(
        body, name=name, grid=(r // tr,),
        in_specs=[blk, pl.BlockSpec((p, tr, c), lambda i: (0, i, 0)), blk, blk],
        out_specs=[blk] * 4,
        out_shape=[jax.ShapeDtypeStruct((r, c), F32)] * 4,
        compiler_params=_params("parallel"),
    )(w, gparts, m, v)


MESH = pl.DeviceIdType.MESH
ANY = pl.BlockSpec(memory_space=pl.ANY)


def _all_gather(xs, name):
    n = len(xs)

    def body(*refs):
        x_refs, out_refs = refs[:n], refs[n:2 * n]
        send_sems, recv_sems, local_sems = refs[2 * n:]
        x, y, c = lax.axis_index("x"), lax.axis_index("y"), lax.axis_index("c")
        me, sibling = (x, y, c), (x, y, 1 - c)
        chips = [(1 - x, y), (x, 1 - y), (1 - x, 1 - y)]

        def slot(a, px, py, pc):
            return out_refs[a].at[4 * px + 2 * py + pc]

        def copy(a, k, block, to, src=None):
            return pltpu.make_async_remote_copy(
                src_ref=slot(a, *block) if src is None else src, dst_ref=slot(a, *block),
                send_sem=send_sems.at[k * n + a], recv_sem=recv_sems.at[k * n + a], device_id=to, device_id_type=MESH)

        arrays = range(n)
        mine = [pltpu.make_async_copy(x_refs[a], slot(a, *me), local_sems.at[a]) for a in arrays]
        first = [copy(a, 0, me, sibling, src=x_refs[a]) for a in arrays]
        first += [copy(a, 1 + j, me, (*chip, c), src=x_refs[a]) for j, chip in enumerate(chips) for a in arrays]
        for cp in mine + first:
            cp.start()
        passed = []
        for j, chip in enumerate(chips):
            for a in arrays:
                copy(a, 1 + j, (*chip, c), me).wait_recv()
                passed.append(copy(a, 4 + j, (*chip, c), sibling))
                passed[-1].start()
        for a in arrays:
            copy(a, 0, sibling, me).wait_recv()
        for j, chip in enumerate(chips):
            for a in arrays:
                copy(a, 4 + j, (*chip, 1 - c), me).wait_recv()
        for cp in first + passed:
            cp.wait_send()
        for cp in mine:
            cp.wait()

    return pl.pallas_call(
        body, name=name,
        out_shape=[jax.ShapeDtypeStruct((N_DEV,) + x.shape, x.dtype) for x in xs],
        in_specs=[ANY] * n, out_specs=[ANY] * n,
        scratch_shapes=[pltpu.SemaphoreType.DMA((7 * n,)), pltpu.SemaphoreType.DMA((7 * n,)),
                        pltpu.SemaphoreType.DMA((n,))],
    )(*xs)


def _peer_copies(kind, src_refs, dst_refs, send_sems, recv_sems, local_sems):
    n = len(src_refs)
    x, y, c = lax.axis_index("x"), lax.axis_index("y"), lax.axis_index("c")
    mine = 4 * x + 2 * y + c

    def src(a, idx):
        return src_refs[a] if kind == "spread" else src_refs[a].at[idx]

    copies = [pltpu.make_async_copy(src(a, mine), dst_refs[a].at[mine], local_sems.at[a]) for a in range(n)]
    for k in range(1, N_DEV):
        px = 1 - x if k & 4 else x
        py = 1 - y if k & 2 else y
        pc = 1 - c if k & 1 else c
        for a in range(n):
            copies.append(pltpu.make_async_remote_copy(
                src_ref=src(a, 4 * px + 2 * py + pc), dst_ref=dst_refs[a].at[mine],
                send_sem=send_sems.at[(k - 1) * n + a], recv_sem=recv_sems.at[(k - 1) * n + a],
                device_id=(px, py, pc), device_id_type=MESH))
    return copies


def _peer_shapes(kind, arrays):
    n = len(arrays)
    outs = [jax.ShapeDtypeStruct(((N_DEV,) + a.shape) if kind == "spread" else a.shape, a.dtype) for a in arrays]
    sems = [pltpu.SemaphoreType.DMA((7 * n,)), pltpu.SemaphoreType.DMA((7 * n,)), pltpu.SemaphoreType.DMA((n,))]
    return outs, sems


def _exchange(gs, name):
    n = len(gs)

    def body(*refs):
        copies = _peer_copies("exchange", refs[:n], refs[n:2 * n], *refs[2 * n:])
        for cp in copies:
            cp.start()
        for cp in copies:
            cp.wait()

    outs, sems = _peer_shapes("exchange", gs)
    return pl.pallas_call(body, name=name, out_shape=outs, in_specs=[ANY] * n, out_specs=[ANY] * n,
                          scratch_shapes=sems)(*gs)


def _tile(n, pref, unit=LANES):
    if n <= pref:
        return n
    best = None
    for cand in range(unit, pref + 1, unit):
        if n % cand == 0:
            best = cand
    assert best is not None, (n, pref, unit)
    return best


LATE_KEYS = dict(w_proj_sb="wp_sb", w_proj_fox="wp_fx", w_out="w_out", w_up="w_up", w_conv="w_conv", w_down="w_down")


def _layer_step(x, target, w, attn_blk, late=None):
    b, s, d = x.shape
    t = b * s
    w = dict(w)
    e = w["w_qkv"].shape[1] // 6
    h = e // HEAD_DIM
    f = w["b_conv"].shape[1]
    x2 = x.reshape(t, d)
    tg = target.reshape(t, d)
    pp = 2 if (e // LANES) % 2 == 0 else 1
    tm = _tile(t, 512, HALO)
    tmc = _tile(s, 256, HALO)
    tkt = _tile(t, 2048, HALO)
    td = _tile(d, 1024)
    tf = _tile(f, 1408)
    t2f = _tile(2 * f, 1408)
    tqkv = _tile(6 * e, 1024)
    tg2 = _tile(2 * d, 1024)
    xb = x2.astype(BF16)

    qkv = _matmul(xb, w["w_qkv"], NN, tm=tm, tn=tqkv, tk=d, out_dtype=BF16, name="in_qkv", bias=w["b_qkv"], b_outer=True)
    gate = _matmul(xb, w["w_g"], NN, tm=tm, tn=tg2, tk=d, out_dtype=F32, name="in_gate", bias=w["b_g"], b_outer=True)
    fl = _matmul(xb, w["w_f"], NN, tm=tm, tn=LANES, tk=d, out_dtype=F32, name="in_forget", bias=w["b_f"])
    nr = s // LANES
    f2 = fl.reshape(b, s, LANES)[:, :, :h].transpose(0, 2, 1).reshape(b * h * nr, LANES)
    c2 = _scan_rows(f2, nr, "fwd")
    c = c2.reshape(b * h, s, 1)
    ct = c2.reshape(b * h, 1, s)
    o_sb, tot, first = _sb_fwd(qkv, b, s, e, attn_blk, pp)
    if late is None:
        o_fx, lse = _fox_fwd(qkv, c, ct, b, s, e, attn_blk, pp)
    else:
        o_fx, lse, *gathered = _fox_fwd(qkv, c, ct, b, s, e, attn_blk, pp, comm=("spread", late[1]))
        for name, g in zip(late[0], gathered):
            w[LATE_KEYS[name]] = _join(g, name)
    merged, y_sb, y_fx = _proj_gate_fwd(o_sb, o_fx, w["wp_sb"], w["wp_fx"], gate, tm)
    x1, xh1, rs1, x1b = _mm_res_ln(merged, w["w_out"], x2, w["ln1_g"], w["ln1_b"], tm, "out_ln1")
    u = _matmul(x1b, w["w_up"], NN, tm=tm, tn=t2f, tk=d, out_dtype=BF16, name="ffn_up", b_outer=True)
    act = _conv_glu_fwd(u, w["w_conv"], w["b_conv"], s, tmc)
    xo, xh2, rs2, _ = _mm_res_ln(act, w["w_down"], x1, w["ln2_g"], w["ln2_b"], tm, "down_ln2")

    gr = {}
    dr2, dg2, db2, ls, dr2b = _loss_ln_bwd(xo, xh2, rs2, w["ln2_g"], tg, tm)
    gr["ln2_g"], gr["ln2_b"] = dg2[0:1], db2[0:1]
    da = _matmul(dr2b, w["w_down"], NT, tm=tm, tn=tf, tk=d, out_dtype=BF16, name="d_act", b_outer=True)
    gr["w_down"] = _matmul(act, dr2b, TN, tm=tf, tn=td, tk=tkt, out_dtype=F32, name="dw_down")
    dc, du_v = _conv_glu_bwd1(u, da, w["w_conv"], w["b_conv"], s, tmc)
    du_g, gw0, gw1, gw2, gbc = _conv_glu_bwd2(u, dc, w["w_conv"], s, tmc)
    gr["w_conv"] = jnp.concatenate([gw0[0:1], gw1[0:1], gw2[0:1]], axis=0)
    gr["b_conv"] = gbc[0:1]
    du = jnp.concatenate([du_g, du_v], axis=1)
    dlin1 = _matmul(du, w["w_up"], NT, tm=tm, tn=td, tk=2 * f, out_dtype=F32, name="d_x1")
    gr["w_up"] = _matmul(x1b, du, TN, tm=td, tn=t2f, tk=tkt, out_dtype=F32, name="dw_up")
    dr1, dg1, db1, dr1b = _ln_bwd(dr2, dlin1, xh1, rs1, w["ln1_g"], tm)
    gr["ln1_g"], gr["ln1_b"] = dg1[0:1], db1[0:1]
    dmg = _matmul(dr1b, w["w_out"], NT, tm=tm, tn=td, tk=d, out_dtype=F32, name="d_merged")
    gr["w_out"] = _matmul(merged, dr1b, TN, tm=td, tn=td, tk=tkt, out_dtype=F32, name="dw_out")
    dy_sb, dy_fx, dgate = _gate_bwd(dmg, y_sb, y_fx, gate, tm)
    do_sb = _matmul(dy_sb, w["wp_sb"], NT, tm=tm, tn=e, tk=d, out_dtype=BF16, name="d_o_sb")
    do_fx = _matmul(dy_fx, w["wp_fx"], NT, tm=tm, tn=e, tk=d, out_dtype=BF16, name="d_o_fx")
    gr["wp_sb"] = _matmul(o_sb, dy_sb, TN, tm=e, tn=td, tk=tkt, out_dtype=F32, name="dwp_sb")
    gr["wp_fx"] = _matmul(o_fx, dy_fx, TN, tm=e, tn=td, tk=tkt, out_dtype=F32, name="dwp_fx")
    dq_sb, dk_sb, dv_sb = _sb_bwd(qkv, do_sb, tot, first, b, s, e, attn_blk, pp)
    landed = None
    if late is None:
        dq_fx, dk_fx, dv_fx, dct = _fox_bwd(qkv, c, ct, do_fx, o_fx, lse, b, s, e, attn_blk, pp)
    else:
        blocks = [_cut(gr[LATE_KEYS[n]], n).astype(BF16 if n in MATMUL_OPERANDS else F32) for n in late[0]]
        dq_fx, dk_fx, dv_fx, dct, *got = _fox_bwd(qkv, c, ct, do_fx, o_fx, lse, b, s, e, attn_blk, pp,
                                                   comm=("exchange", blocks))
        landed = dict(zip(late[0], got))
    dqkv = jnp.concatenate([dq_sb, dk_sb, dv_sb, dq_fx, dk_fx, dv_fx], axis=1)
    df2 = _scan_rows(f2, nr, "bwd", dct.reshape(b * h * nr, LANES))
    df = jnp.pad(df2.reshape(b, h, s).transpose(0, 2, 1).reshape(t, h), ((0, 0), (0, LANES - h))).astype(BF16)
    dx = _matmul(dqkv, w["w_qkv"], NT, tm=tm, tn=td, tk=6 * e, out_dtype=F32, name="dx_qkv", res=dr1, res_scale=ALPHA)
    dx = _matmul(dgate, w["w_g"], NT, tm=tm, tn=td, tk=2 * d, out_dtype=F32, name="dx_gate", res=dx)
    dx = _matmul(df, w["w_f"], NT, tm=tm, tn=td, tk=LANES, out_dtype=F32, name="dx_forget", res=dx)
    gr["w_qkv"] = _matmul(xb, dqkv, TN, tm=td, tn=tqkv, tk=tkt, out_dtype=F32, name="dw_qkv")
    gr["w_g"] = _matmul(xb, dgate, TN, tm=td, tn=tg2, tk=tkt, out_dtype=F32, name="dw_gate")
    gr["w_f"] = _matmul(xb, df, TN, tm=td, tn=LANES, tk=tkt, out_dtype=F32, name="dw_forget")
    gr["b_qkv"] = _colsum(dqkv, tm, "db_qkv")[0:1]
    gr["b_g"] = _colsum(dgate, tm, "db_gate")[0:1]
    gr["b_f"] = _colsum(df, tm, "db_forget")[0:1]
    return ls[0:1, 0:1], dx.reshape(b, s, d), gr, landed


SHARDED = ("w_in", "w_proj_sb", "w_proj_fox", "w_out", "w_up", "w_conv", "w_down")
ROW_SHARDED = ("w_out", "w_down")
REPLICATED = ("b_in", "ln1_g", "ln1_b", "b_conv", "ln2_g", "ln2_b")
WEIGHTS = ("w_in", "b_in", "w_proj_sb", "w_proj_fox", "w_out", "ln1_g", "ln1_b", "w_up", "w_conv", "b_conv",
           "w_down", "ln2_g", "ln2_b")
MATMUL_OPERANDS = ("w_in", "w_proj_sb", "w_proj_fox", "w_out", "w_up", "w_down")


def _cut(full, name):
    r, c = full.shape
    if name in ROW_SHARDED:
        return full.reshape(N_DEV, r // N_DEV, c)
    cs = c // N_DEV
    return jnp.stack([full[:, j * cs:(j + 1) * cs] for j in range(N_DEV)], axis=0)


def _join(blocks, name):
    p, r, c = blocks.shape
    if name in ROW_SHARDED:
        return blocks.reshape(p * r, c)
    return jnp.concatenate([blocks[j] for j in range(p)], axis=1)


def kernel(x, w_in, b_in, w_proj_sb, w_proj_fox, w_out, ln1_g, ln1_b, w_up, w_conv, b_conv, w_down, ln2_g, ln2_b, loss_target, m_w_in, m_b_in, m_w_proj_sb, m_w_proj_fox, m_w_out, m_ln1_g, m_ln1_b, m_w_up, m_w_conv, m_b_conv, m_w_down, m_ln2_g, m_ln2_b, v_w_in, v_b_in, v_w_proj_sb, v_w_proj_fox, v_w_out, v_ln1_g, v_ln1_b, v_w_up, v_w_conv, v_b_conv, v_w_down, v_ln2_g, v_ln2_b):
    wts = dict(w_in=w_in, b_in=b_in, w_proj_sb=w_proj_sb, w_proj_fox=w_proj_fox, w_out=w_out, ln1_g=ln1_g, ln1_b=ln1_b,
               w_up=w_up, w_conv=w_conv, b_conv=b_conv, w_down=w_down, ln2_g=ln2_g, ln2_b=ln2_b)
    mom = dict(w_in=m_w_in, b_in=m_b_in, w_proj_sb=m_w_proj_sb, w_proj_fox=m_w_proj_fox, w_out=m_w_out, ln1_g=m_ln1_g,
               ln1_b=m_ln1_b, w_up=m_w_up, w_conv=m_w_conv, b_conv=m_b_conv, w_down=m_w_down, ln2_g=m_ln2_g, ln2_b=m_ln2_b)
    var = dict(w_in=v_w_in, b_in=v_b_in, w_proj_sb=v_w_proj_sb, w_proj_fox=v_w_proj_fox, w_out=v_w_out, ln1_g=v_ln1_g,
               ln1_b=v_ln1_b, w_up=v_w_up, w_conv=v_w_conv, b_conv=v_b_conv, w_down=v_w_down, ln2_g=v_ln2_g, ln2_b=v_ln2_b)
    shard = {n: wts[n].reshape(wts[n].shape[-2:]) for n in WEIGHTS}

    w_in_full = _join(_all_gather([shard["w_in"].astype(BF16)], "gather_w_in")[0], "w_in")
    late_names = [n for n in SHARDED if n != "w_in"]
    late = (late_names, [shard[n].astype(BF16) if n in MATMUL_OPERANDS else shard[n] for n in late_names])
    e = shard["w_proj_sb"].shape[0]
    h = e // HEAD_DIM
    nq = 6 * e

    def cut_in(a, pad):
        fcols = a[:, nq:nq + h]
        if pad:
            fcols = jnp.pad(fcols, ((0, 0), (0, LANES - h)))
        return a[:, :nq], a[:, nq + h:], fcols

    w_qkv, w_g, w_f = cut_in(w_in_full, True)
    b_qkv, b_g, b_f = cut_in(shard["b_in"], True)
    w = dict(w_qkv=w_qkv, w_g=w_g, w_f=w_f, b_qkv=b_qkv, b_g=b_g, b_f=b_f, b_conv=shard["b_conv"],
             ln1_g=shard["ln1_g"], ln1_b=shard["ln1_b"], ln2_g=shard["ln2_g"], ln2_b=shard["ln2_b"])

    loss_local, grad_x, gr, gsum = _layer_step(x, loss_target, w, min(256, x.shape[1]), late)
    loss = lax.psum(loss_local[0, 0], ("x", "y", "c"))

    local = dict(
        w_in=jnp.concatenate([gr["w_qkv"], gr["w_f"][:, :h], gr["w_g"]], axis=1),
        b_in=jnp.concatenate([gr["b_qkv"], gr["b_f"][:, :h], gr["b_g"]], axis=1),
        w_proj_sb=gr["wp_sb"], w_proj_fox=gr["wp_fx"], w_out=gr["w_out"], w_up=gr["w_up"], w_conv=gr["w_conv"],
        w_down=gr["w_down"], ln1_g=gr["ln1_g"], ln1_b=gr["ln1_b"], b_conv=gr["b_conv"], ln2_g=gr["ln2_g"],
        ln2_b=gr["ln2_b"])

    gsum["w_in"] = _exchange([_cut(local["w_in"], "w_in").astype(BF16)], "exchange_w_in")[0]

    parts = _all_gather([jnp.concatenate([local[n] for n in REPLICATED], axis=1)], "gather_small_grads")[0]
    off = 0
    for n in REPLICATED:
        gsum[n] = parts[:, :, off:off + shard[n].size]
        off += shard[n].size

    grads, deltas, new_m, new_v = [], [], [], []
    for n in WEIGHTS:
        shp = wts[n].shape
        g, dl, mn, vn = _adamw(shard[n], gsum[n], mom[n].reshape(shard[n].shape), var[n].reshape(shard[n].shape),
                               "adamw_" + n)
        grads.append(g.reshape(shp))
        deltas.append(dl.reshape(shp))
        new_m.append(mn.reshape(shp))
        new_v.append(vn.reshape(shp))
    return (loss, grad_x, *grads, *deltas, *new_m, *new_v)
```

```python
import functools
import math

import jax
import jax.numpy as jnp
from jax import lax
from jax.experimental import pallas as pl
from jax.experimental.pallas import tpu as pltpu

F32 = jnp.float32
BF16 = jnp.bfloat16

HEAD_DIM = 64
LN_EPS = 1e-5
DEPTH = 1
ALPHA = (2.0 * DEPTH) ** 0.25
ADAM_LR, ADAM_B1, ADAM_B2, ADAM_EPS, ADAM_WD, ADAM_STEP = 0.001, 0.9, 0.999, 1e-08, 0.01, 10
N_DEV = 8
LANES = 128
SUBLANES = 8
HALO = 16
VMEM_LIMIT = 56 * 1024 * 1024

NN = ((1,), (0,))
NT = ((1,), (1,))
TN = ((0,), (0,))


def _dot(a, b, dims):
    return lax.dot_general(a, b, (dims, ((), ())), preferred_element_type=F32)


def _params(*sem):
    return pltpu.CompilerParams(dimension_semantics=sem, vmem_limit_bytes=VMEM_LIMIT)


def _iotas(blk):
    row = lax.broadcasted_iota(jnp.int32, (blk, blk), 0)
    col = lax.broadcasted_iota(jnp.int32, (blk, blk), 1)
    return row, col


def _sb_terms(z):
    e = jnp.exp(-jnp.abs(z))
    lb = jnp.minimum(z, 0.0) - jnp.log(1.0 + e)
    return lb, lb - z, e


def _pair_specs(s, blk, e, pp, branch):
    w = pp * LANES
    nq = s // blk
    ng = e // w
    base = 3 * branch * ng
    q_in = pl.BlockSpec((blk, w), lambda b, g, i: (b * nq + i, base + g))
    k_in = pl.BlockSpec((s, w), lambda b, g, i: (b, base + ng + g))
    v_in = pl.BlockSpec((s, w), lambda b, g, i: (b, base + 2 * ng + g))
    q_out = pl.BlockSpec((blk, w), lambda b, g, i: (b * nq + i, g))
    kv_out = pl.BlockSpec((s, w), lambda b, g, i: (b, g))
    rows = pl.BlockSpec((2 * pp, blk, 1), lambda b, g, i: (b * ng + g, i, 0))
    krow = pl.BlockSpec((2 * pp, 1, s), lambda b, g, i: (b * ng + g, 0, 0))
    return q_in, k_in, v_in, q_out, kv_out, rows, krow


def _to_column(r):
    row, col = _iotas(r.shape[1])
    return jnp.sum(jnp.where(row == col, r, 0.0), axis=1, keepdims=True)


def _half_masks(x):
    low = lax.broadcasted_iota(jnp.int32, x.shape, 1) < HEAD_DIM
    zero = jnp.zeros_like(x)
    return jnp.concatenate([jnp.where(low, x, zero), jnp.where(low, zero, x)], axis=0)


def _tri_sums(xs, tri):
    hi = [x.astype(BF16) for x in xs]
    lo = [(x - h.astype(F32)).astype(BF16) for x, h in zip(xs, hi)]
    n = len(xs)
    blk = xs[0].shape[0]
    r = _dot(jnp.concatenate(hi + lo, axis=0), tri, NN)
    return [r[i * blk:(i + 1) * blk] + r[(n + i) * blk:(n + i + 1) * blk] for i in range(n)]


def _sb_fwd(qkv, b, s, e, blk, pp):
    scale = HEAD_DIM ** -0.5
    nh = 2 * pp
    t = b * s

    def body(q_ref, k_ref, v_ref, o_ref, tot_ref, first_ref):
        qi = pl.program_id(2)
        qm = [_half_masks((q_ref[:, p * LANES:(p + 1) * LANES] * scale).astype(BF16)) for p in range(pp)]
        row, col = _iotas(blk)
        strict = col < row
        after = (row > col).astype(BF16)

        def block(j, carry, diag):
            off = pl.multiple_of(j * blk, blk)
            o_acc, run = carry
            zz = [_dot(qm[p], k_ref[pl.ds(off, blk), p * LANES:(p + 1) * LANES], NT) for p in range(pp)]
            z = [zz[h // 2][(h % 2) * blk:(h % 2 + 1) * blk] for h in range(nh)]
            terms = [_sb_terms(z[h]) for h in range(nh)]
            lom = [jnp.where(strict, terms[h][1], 0.0) if diag else terms[h][1] for h in range(nh)]
            sfx = _tri_sums(lom, after)
            a = [jnp.exp(terms[h][0] + sfx[h] + run[h]) for h in range(nh)]
            if diag:
                a = [jnp.where(strict, a[h], 0.0) for h in range(nh)]
            ab = [a[h].astype(BF16) for h in range(nh)]
            o_new = tuple(
                o_acc[p] + _dot(jnp.concatenate([ab[2 * p], ab[2 * p + 1]], axis=1),
                                _half_masks(v_ref[pl.ds(off, blk), p * LANES:(p + 1) * LANES]), NN)
                for p in range(pp))
            return o_new, tuple(run[h] + sfx[h][:, 0:1] + lom[h][:, 0:1] for h in range(nh))

        def alive(run):
            m = run[0]
            for h in range(1, nh):
                m = jnp.maximum(m, run[h])
            return jnp.max(m) > DEAD

        o_acc, run = block(qi, ((jnp.zeros((blk, LANES), F32),) * pp, (jnp.zeros((blk, 1), F32),) * nh), True)

        def step(c):
            j, _, o_acc, run = c
            o_acc, run = block(j, (o_acc, run), False)
            return j - 1, alive(run), o_acc, run

        j, _, o_acc, run = lax.while_loop(lambda c: jnp.logical_and(c[0] >= 0, c[1]), step,
                                          (qi - 1, alive(run), o_acc, run))
        for p in range(pp):
            o_ref[:, p * LANES:(p + 1) * LANES] = o_acc[p].astype(o_ref.dtype)
        for h in range(nh):
            tot_ref[h] = run[h]
            first_ref[h] = jnp.zeros((blk, 1), F32) + (j + 1).astype(F32)

    q_in, k_in, v_in, q_out, _, rows, _ = _pair_specs(s, blk, e, pp, 0)
    return pl.pallas_call(
        body, name="sb_fwd", grid=(b, e // (pp * LANES), s // blk),
        in_specs=[q_in, k_in, v_in], out_specs=[q_out, rows, rows],
        out_shape=[jax.ShapeDtypeStruct((t, e), BF16)] + [jax.ShapeDtypeStruct((b * e // HEAD_DIM, s, 1), F32)] * 2,
        compiler_params=_params("parallel", "parallel", "arbitrary"),
    )(qkv, qkv, qkv)


def _sb_bwd(qkv, do, tot, first, b, s, e, blk, pp):
    scale = HEAD_DIM ** -0.5
    nh = 2 * pp
    t = b * s
    nq = s // blk

    def body(q_ref, k_ref, v_ref, do_ref, tot_ref, first_ref, dq_ref, dk_ref, dv_ref, dk_acc, dv_acc):
        qi = pl.program_id(2)

        @pl.when(qi == 0)
        def _():
            dk_acc[...] = jnp.zeros_like(dk_acc)
            dv_acc[...] = jnp.zeros_like(dv_acc)

        qm = [_half_masks((q_ref[:, p * LANES:(p + 1) * LANES] * scale).astype(BF16)) for p in range(pp)]
        dom = [_half_masks(do_ref[:, p * LANES:(p + 1) * LANES].astype(BF16)) for p in range(pp)]
        tot_t = [tot_ref[h] for h in range(nh)]
        row, col = _iotas(blk)
        strict = col < row
        upto = (row <= col).astype(BF16)
        before = (row < col).astype(BF16)

        def block(j, carry, diag):
            off = pl.multiple_of(j * blk, blk)
            dq_acc, cl, cg = carry
            kp = [k_ref[pl.ds(off, blk), p * LANES:(p + 1) * LANES] for p in range(pp)]
            vp = [v_ref[pl.ds(off, blk), p * LANES:(p + 1) * LANES] for p in range(pp)]
            zz = [_dot(qm[p], kp[p], NT) for p in range(pp)]
            dd = [_dot(dom[p], vp[p], NT) for p in range(pp)]
            z = [zz[h // 2][(h % 2) * blk:(h % 2 + 1) * blk] for h in range(nh)]
            da = [dd[h // 2][(h % 2) * blk:(h % 2 + 1) * blk] for h in range(nh)]
            terms = [_sb_terms(z[h]) for h in range(nh)]
            lom = [jnp.where(strict, terms[h][1], 0.0) if diag else terms[h][1] for h in range(nh)]
            pre = _tri_sums(lom, upto)
            a = [jnp.exp(terms[h][0] + (tot_t[h] - cl[h] - pre[h])) for h in range(nh)]
            if diag:
                a = [jnp.where(strict, a[h], 0.0) for h in range(nh)]
            g = [a[h] * da[h] for h in range(nh)]
            pw = _tri_sums(g, before)
            dzb = []
            for h in range(nh):
                ex = terms[h][2]
                r = 1.0 / (1.0 + ex)
                er = ex * r
                pos = z[h] >= 0.0
                dz = g[h] * jnp.where(pos, er, r) - (cg[h] + pw[h]) * jnp.where(pos, r, er)
                if diag:
                    dz = jnp.where(strict, dz, 0.0)
                dzb.append(dz.astype(BF16))
            ab = [a[h].astype(BF16) for h in range(nh)]
            for p in range(pp):
                cols = slice(p * LANES, (p + 1) * LANES)
                dk_acc[pl.ds(off, blk), cols] += _dot(jnp.concatenate([dzb[2 * p], dzb[2 * p + 1]], axis=0), qm[p], TN)
                dv_acc[pl.ds(off, blk), cols] += _dot(jnp.concatenate([ab[2 * p], ab[2 * p + 1]], axis=0), dom[p], TN)
            dq_new = tuple(dq_acc[p] + _dot(jnp.concatenate([dzb[2 * p], dzb[2 * p + 1]], axis=1), _half_masks(kp[p]), NN)
                           for p in range(pp))
            return (dq_new, tuple(cl[h] + pre[h][:, blk - 1:blk] for h in range(nh)),
                    tuple(cg[h] + pw[h][:, blk - 1:blk] + g[h][:, blk - 1:blk] for h in range(nh)))

        zero1 = (jnp.zeros((blk, 1), F32),) * nh
        j0 = jnp.clip(jnp.max(first_ref[0]).astype(jnp.int32), 0, qi)
        carry = lax.fori_loop(j0, qi, lambda j, c: block(j, c, False), ((jnp.zeros((blk, LANES), F32),) * pp, zero1, zero1))
        dq_acc, _, _ = block(qi, carry, True)
        for p in range(pp):
            dq_ref[:, p * LANES:(p + 1) * LANES] = (dq_acc[p] * scale).astype(BF16)

        @pl.when(qi == nq - 1)
        def _():
            dk_ref[...] = dk_acc[...].astype(BF16)
            dv_ref[...] = dv_acc[...].astype(BF16)

    q_in, k_in, v_in, q_out, kv_out, rows, _ = _pair_specs(s, blk, e, pp, 0)
    w = pp * LANES
    return pl.pallas_call(
        body, name="sb_bwd", grid=(b, e // w, nq),
        in_specs=[q_in, k_in, v_in, q_out, rows, rows], out_specs=[q_out, kv_out, kv_out],
        out_shape=[jax.ShapeDtypeStruct((t, e), BF16)] * 3,
        scratch_shapes=[pltpu.VMEM((s, w), F32), pltpu.VMEM((s, w), F32)],
        compiler_params=_params("parallel", "parallel", "arbitrary"),
    )(qkv, qkv, qkv, do, tot, first)


NEG = -1e30
DEAD = -110.0


def _ride_along(comm, src_refs, dst_refs, sems, first, last):
    if comm is None:
        return lambda: None

    @pl.when(first)
    def _():
        for cp in _peer_copies(comm[0], src_refs, dst_refs, *sems):
            cp.start()

    def finish():
        @pl.when(last)
        def _():
            for cp in _peer_copies(comm[0], src_refs, dst_refs, *sems):
                cp.wait()

    return finish


def _grid_ends(grid):
    ids = [pl.program_id(a) for a in range(len(grid))]
    first = functools.reduce(jnp.logical_and, [i == 0 for i in ids])
    last = functools.reduce(jnp.logical_and, [i == g - 1 for i, g in zip(ids, grid)])
    return first, last


def _fox_fwd(qkv, ct, b, s, e, blk, pp, comm=None):
    scale = HEAD_DIM ** -0.5
    nh = 2 * pp
    t = b * s
    nc = len(comm[1]) if comm else 0
    grid = (b, e // (pp * LANES), s // blk)

    def body(*refs):
        q_ref, k_ref, v_ref, ct_ref = refs[:4]
        o_ref, lse_ref = refs[4 + nc:6 + nc]
        finish = _ride_along(comm, refs[4:4 + nc], refs[6 + nc:6 + 2 * nc], refs[6 + 2 * nc:], *_grid_ends(grid))
        qi = pl.program_id(2)
        qm = [_half_masks((q_ref[:, p * LANES:(p + 1) * LANES] * scale).astype(BF16)) for p in range(pp)]
        cq = [_to_column(ct_ref[h, :, pl.ds(pl.multiple_of(qi * blk, blk), blk)]) for h in range(nh)]
        row, col = _iotas(blk)
        causal = col <= row
        low = lax.broadcasted_iota(jnp.int32, (blk, LANES), 1) < HEAD_DIM

        def block(j, carry, diag):
            off = pl.multiple_of(j * blk, blk)
            m, l, acc = carry
            zz = [_dot(qm[p], k_ref[pl.ds(off, blk), p * LANES:(p + 1) * LANES], NT) for p in range(pp)]
            z = [zz[h // 2][(h % 2) * blk:(h % 2 + 1) * blk] + (cq[h] - ct_ref[h, :, pl.ds(off, blk)]) for h in range(nh)]
            if diag:
                z = [jnp.where(causal, z[h], NEG) for h in range(nh)]
            m_new = tuple(jnp.maximum(m[h], jnp.max(z[h], axis=1, keepdims=True)) for h in range(nh))
            w = [jnp.exp(m[h] - m_new[h]) for h in range(nh)]
            pr = [jnp.exp(z[h] - m_new[h]) for h in range(nh)]
            pb = [pr[h].astype(BF16) for h in range(nh)]
            pv = [_dot(jnp.concatenate([pb[2 * p], pb[2 * p + 1]], axis=1),
                       _half_masks(v_ref[pl.ds(off, blk), p * LANES:(p + 1) * LANES]), NN) for p in range(pp)]
            acc_new = tuple(jnp.where(low, w[2 * p], w[2 * p + 1]) * acc[p] + pv[p] for p in range(pp))
            l_new = tuple(w[h] * l[h] + jnp.sum(pr[h], axis=1, keepdims=True) for h in range(nh))
            return m_new, l_new, acc_new

        zero = ((jnp.full((blk, 1), NEG, F32),) * nh, (jnp.zeros((blk, 1), F32),) * nh, (jnp.zeros((blk, LANES), F32),) * pp)
        carry = block(qi, zero, True)
        m, l, acc = lax.fori_loop(0, qi, lambda j, c_: block(j, c_, False), carry)
        for p in range(pp):
            o_ref[:, p * LANES:(p + 1) * LANES] = acc[p] / jnp.where(low, l[2 * p], l[2 * p + 1])
        for h in range(nh):
            lse_ref[h] = m[h] + jnp.log(l[h])
        finish()

    q_in, k_in, v_in, q_out, _, rows, krow = _pair_specs(s, blk, e, pp, 1)
    outs, sems = _peer_shapes(comm[0], comm[1]) if comm else ([], [])
    return pl.pallas_call(
        body, name="fox_fwd", grid=grid,
        in_specs=[q_in, k_in, v_in, krow] + [ANY] * nc, out_specs=[q_out, rows] + [ANY] * nc,
        out_shape=[jax.ShapeDtypeStruct((t, e), F32), jax.ShapeDtypeStruct((b * e // HEAD_DIM, s, 1), F32)] + outs,
        scratch_shapes=sems,
        compiler_params=_params("arbitrary", "arbitrary", "arbitrary"),
    )(qkv, qkv, qkv, ct, *(comm[1] if comm else ()))


def _fox_bwd(qkv, ct, do, o, lse, b, s, e, blk, pp, comm=None):
    scale = HEAD_DIM ** -0.5
    nh = 2 * pp
    t = b * s
    nq = s // blk
    nc = len(comm[1]) if comm else 0
    w = pp * LANES
    grid = (b, e // w, nq)

    def body(*refs):
        q_ref, k_ref, v_ref, ct_ref, do_ref, o_ref, lse_ref = refs[:7]
        dq_ref, dk_ref, dv_ref, dct_ref = refs[7 + nc:11 + nc]
        dk_acc, dv_acc = refs[11 + 2 * nc:13 + 2 * nc]
        finish = _ride_along(comm, refs[7:7 + nc], refs[11 + nc:11 + 2 * nc], refs[13 + 2 * nc:], *_grid_ends(grid))
        qi = pl.program_id(2)

        @pl.when(qi == 0)
        def _():
            dk_acc[...] = jnp.zeros_like(dk_acc)
            dv_acc[...] = jnp.zeros_like(dv_acc)
            dct_ref[...] = jnp.zeros_like(dct_ref)

        qm = [_half_masks((q_ref[:, p * LANES:(p + 1) * LANES] * scale).astype(BF16)) for p in range(pp)]
        dob = [do_ref[:, p * LANES:(p + 1) * LANES].astype(BF16) for p in range(pp)]
        dom = [_half_masks(dob[p]) for p in range(pp)]
        low = lax.broadcasted_iota(jnp.int32, (blk, LANES), 1) < HEAD_DIM
        delta = []
        for p in range(pp):
            prod = dob[p].astype(F32) * o_ref[:, p * LANES:(p + 1) * LANES]
            delta.append(jnp.sum(jnp.where(low, prod, 0.0), axis=1, keepdims=True))
            delta.append(jnp.sum(jnp.where(low, 0.0, prod), axis=1, keepdims=True))
        cq = [_to_column(ct_ref[h, :, pl.ds(pl.multiple_of(qi * blk, blk), blk)]) for h in range(nh)]
        lse_t = [lse_ref[h] for h in range(nh)]
        row, col = _iotas(blk)
        causal = col <= row

        def block(j, dq_acc, diag):
            off = pl.multiple_of(j * blk, blk)
            kp = [k_ref[pl.ds(off, blk), p * LANES:(p + 1) * LANES] for p in range(pp)]
            zz = [_dot(qm[p], kp[p], NT) for p in range(pp)]
            dd = [_dot(dom[p], v_ref[pl.ds(off, blk), p * LANES:(p + 1) * LANES], NT) for p in range(pp)]
            z = [zz[h // 2][(h % 2) * blk:(h % 2 + 1) * blk] + (cq[h] - ct_ref[h, :, pl.ds(off, blk)]) for h in range(nh)]
            pr = [jnp.exp(z[h] - lse_t[h]) for h in range(nh)]
            if diag:
                pr = [jnp.where(causal, pr[h], 0.0) for h in range(nh)]
            ds = [pr[h] * (dd[h // 2][(h % 2) * blk:(h % 2 + 1) * blk] - delta[h]) for h in range(nh)]
            dsb = [ds[h].astype(BF16) for h in range(nh)]
            pb = [pr[h].astype(BF16) for h in range(nh)]
            for p in range(pp):
                cols = slice(p * LANES, (p + 1) * LANES)
                dk_acc[pl.ds(off, blk), cols] += _dot(jnp.concatenate([dsb[2 * p], dsb[2 * p + 1]], axis=0), qm[p], TN)
                dv_acc[pl.ds(off, blk), cols] += _dot(jnp.concatenate([pb[2 * p], pb[2 * p + 1]], axis=0), dom[p], TN)
            for h in range(nh):
                dct_ref[h, :, pl.ds(off, blk)] -= jnp.sum(ds[h], axis=0, keepdims=True)
            return tuple(dq_acc[p] + _dot(jnp.concatenate([dsb[2 * p], dsb[2 * p + 1]], axis=1), _half_masks(kp[p]), NN)
                         for p in range(pp))

        dq_acc = lax.fori_loop(0, qi, lambda j, a: block(j, a, False), (jnp.zeros((blk, LANES), F32),) * pp)
        dq_acc = block(qi, dq_acc, True)
        for p in range(pp):
            dq_ref[:, p * LANES:(p + 1) * LANES] = (dq_acc[p] * scale).astype(BF16)

        @pl.when(qi == nq - 1)
        def _():
            dk_ref[...] = dk_acc[...].astype(BF16)
            dv_ref[...] = dv_acc[...].astype(BF16)

        finish()

    q_in, k_in, v_in, q_out, kv_out, rows, krow = _pair_specs(s, blk, e, pp, 1)
    outs, sems = _peer_shapes(comm[0], comm[1]) if comm else ([], [])
    return pl.pallas_call(
        body, name="fox_bwd", grid=grid,
        in_specs=[q_in, k_in, v_in, krow, q_out, q_out, rows] + [ANY] * nc,
        out_specs=[q_out, kv_out, kv_out, krow] + [ANY] * nc,
        out_shape=[jax.ShapeDtypeStruct((t, e), BF16)] * 3 + [jax.ShapeDtypeStruct((b * e // HEAD_DIM, 1, s), F32)] + outs,
        scratch_shapes=[pltpu.VMEM((s, w), F32), pltpu.VMEM((s, w), F32)] + sems,
        compiler_params=_params("arbitrary", "arbitrary", "arbitrary"),
    )(qkv, qkv, qkv, ct, do, o, lse, *(comm[1] if comm else ()))


def _scan_rows(f2, group, mode, d2=None):
    n = f2.shape[0]

    def body(*refs):
        f_ref, o_ref = refs[0], refs[-1]
        f = f_ref[...]
        row, col = _iotas(LANES)
        grow = lax.broadcasted_iota(jnp.int32, (n, n), 0)
        gcol = lax.broadcasted_iota(jnp.int32, (n, n), 1)
        same = (grow // group) == (gcol // group)
        e = jnp.exp(-jnp.abs(f))
        if mode == "fwd":
            x = jnp.minimum(f, 0.0) - jnp.log1p(e)
            within = (row <= col).astype(F32)
            earlier = (same & (gcol < grow)).astype(F32)
        else:
            x = refs[1][...]
            within = (row >= col).astype(F32)
            earlier = (same & (gcol > grow)).astype(F32)
        y = jnp.dot(x, within, preferred_element_type=F32, precision=lax.Precision.HIGHEST)
        tot = jnp.sum(x, axis=1, keepdims=True)
        y = y + jnp.dot(earlier, tot, preferred_element_type=F32, precision=lax.Precision.HIGHEST)
        if mode == "bwd":
            r = 1.0 / (1.0 + e)
            y = y * jnp.where(f >= 0.0, e * r, r)
        o_ref[...] = y

    args = (f2,) if mode == "fwd" else (f2, d2)
    return pl.pallas_call(body, name="logf_" + mode, out_shape=jax.ShapeDtypeStruct(f2.shape, F32),
                          compiler_params=_params())(*args)


def _matmul(a, b, dims, *, tm, tn, tk, out_dtype, name, bias=None, res=None, res_scale=1.0, b_outer=False):
    def ij(g0, g1):
        return (g1, g0) if b_outer else (g0, g1)

    if dims == NN:
        (m, kk), n = a.shape, b.shape[1]
        a_spec = pl.BlockSpec((tm, tk), lambda g0, g1, k: (ij(g0, g1)[0], k))
        b_spec = pl.BlockSpec((tk, tn), lambda g0, g1, k: (k, ij(g0, g1)[1]))
    elif dims == NT:
        (m, kk), n = a.shape, b.shape[0]
        a_spec = pl.BlockSpec((tm, tk), lambda g0, g1, k: (ij(g0, g1)[0], k))
        b_spec = pl.BlockSpec((tn, tk), lambda g0, g1, k: (ij(g0, g1)[1], k))
    else:
        (kk, m), n = a.shape, b.shape[1]
        a_spec = pl.BlockSpec((tk, tm), lambda g0, g1, k: (k, ij(g0, g1)[0]))
        b_spec = pl.BlockSpec((tk, tn), lambda g0, g1, k: (k, ij(g0, g1)[1]))
    assert m % tm == 0 and n % tn == 0 and kk % tk == 0, (name, m, n, kk, tm, tn, tk)
    nk = kk // tk
    extras, extra_specs = [], []
    if bias is not None:
        extras.append(bias)
        extra_specs.append(pl.BlockSpec((1, tn), lambda g0, g1, k: (0, ij(g0, g1)[1])))
    if res is not None:
        extras.append(res)
        extra_specs.append(pl.BlockSpec((tm, tn), lambda g0, g1, k: ij(g0, g1)))

    def finish(out, rest, o_ref):
        idx = 0
        if bias is not None:
            out = out + rest[idx][...]
            idx += 1
        if res is not None:
            out = out + res_scale * rest[idx][...]
        o_ref[...] = out.astype(o_ref.dtype)

    def body_single(a_ref, b_ref, *rest):
        finish(_dot(a_ref[...].astype(BF16), b_ref[...].astype(BF16), dims), rest, rest[-1])

    def body_acc(a_ref, b_ref, *rest):
        o_ref, acc_ref = rest[-2], rest[-1]
        k = pl.program_id(2)
        part = _dot(a_ref[...].astype(BF16), b_ref[...].astype(BF16), dims)

        @pl.when(k == 0)
        def _():
            acc_ref[...] = part

        @pl.when(k > 0)
        def _():
            acc_ref[...] += part

        @pl.when(k == nk - 1)
        def _():
            finish(acc_ref[...], rest, o_ref)

    grid = (n // tn, m // tm, nk) if b_outer else (m // tm, n // tn, nk)
    return pl.pallas_call(
        body_single if nk == 1 else body_acc, name=name, grid=grid,
        in_specs=[a_spec, b_spec] + extra_specs,
        out_specs=pl.BlockSpec((tm, tn), lambda g0, g1, k: ij(g0, g1)),
        out_shape=jax.ShapeDtypeStruct((m, n), out_dtype),
        scratch_shapes=[] if nk == 1 else [pltpu.VMEM((tm, tn), F32)],
        compiler_params=_params("parallel", "parallel", "arbitrary"),
    )(a, b, *extras)


def _input_grad(dh, w_all, dr, tm, comm=None):
    t, kk = dh.shape
    d = w_all.shape[0]
    nc = len(comm[1]) if comm else 0
    grid = (t // tm,)

    def body(*refs):
        dh_ref, w_ref, dr_ref = refs[:3]
        o_ref = refs[3 + nc]
        finish = _ride_along(comm, refs[3:3 + nc], refs[4 + nc:4 + 2 * nc], refs[4 + 2 * nc:], *_grid_ends(grid))
        o_ref[...] = ALPHA * dr_ref[...] + _dot(dh_ref[...], w_ref[...], NT)
        finish()

    rows = pl.BlockSpec((tm, d), lambda i: (i, 0))
    outs, sems = _peer_shapes(comm[0], comm[1]) if comm else ([], [])
    return pl.pallas_call(
        body, name="input_grad", grid=grid,
        in_specs=[pl.BlockSpec((tm, kk), lambda i: (i, 0)), pl.BlockSpec((d, kk), lambda i: (0, 0)), rows] + [ANY] * nc,
        out_specs=[rows] + [ANY] * nc,
        out_shape=[jax.ShapeDtypeStruct((t, d), F32)] + outs,
        scratch_shapes=sems,
        compiler_params=_params("arbitrary"),
    )(dh, w_all, dr, *(comm[1] if comm else ()))


def _sigmoid(x):
    e = jnp.exp(-jnp.abs(x))
    r = 1.0 / (1.0 + e)
    return jnp.where(x >= 0.0, r, e * r)


def _proj_gate_fwd(o_sb, o_fx, wp_sb, wp_fx, g, tm):
    t, e = o_sb.shape
    d = wp_sb.shape[1]

    def body(osb_ref, ofx_ref, wsb_ref, wfx_ref, gsb_ref, gfx_ref, mg_ref, ysb_ref, yfx_ref):
        ysb = _dot(osb_ref[...].astype(BF16), wsb_ref[...], NN)
        yfx = _dot(ofx_ref[...].astype(BF16), wfx_ref[...], NN)
        ysb_ref[...] = ysb
        yfx_ref[...] = yfx
        mg_ref[...] = (_sigmoid(gsb_ref[...]) * ysb + _sigmoid(gfx_ref[...]) * yfx).astype(BF16)

    rows_e = pl.BlockSpec((tm, e), lambda i: (i, 0))
    rows_d = pl.BlockSpec((tm, d), lambda i: (i, 0))
    w_spec = pl.BlockSpec((e, d), lambda i: (0, 0))
    return pl.pallas_call(
        body, name="proj_gate_fwd", grid=(t // tm,),
        in_specs=[rows_e, rows_e, w_spec, w_spec, rows_d, pl.BlockSpec((tm, d), lambda i: (i, 1))],
        out_specs=[rows_d, rows_d, rows_d],
        out_shape=[jax.ShapeDtypeStruct((t, d), BF16), jax.ShapeDtypeStruct((t, d), F32), jax.ShapeDtypeStruct((t, d), F32)],
        compiler_params=_params("parallel"),
    )(o_sb, o_fx, wp_sb, wp_fx, g, g)


def _gate_bwd(dmg, y_sb, y_fx, g, tm):
    t, d = dmg.shape

    def body(dm_ref, ysb_ref, yfx_ref, gsb_ref, gfx_ref, dysb_ref, dyfx_ref, dg_ref):
        dm = dm_ref[...]
        ssb = _sigmoid(gsb_ref[...])
        sfx = _sigmoid(gfx_ref[...])
        dysb_ref[...] = (dm * ssb).astype(BF16)
        dyfx_ref[...] = (dm * sfx).astype(BF16)
        dg_ref[:, 0:d] = (dm * ysb_ref[...] * ssb * (1.0 - ssb)).astype(BF16)
        dg_ref[:, d:2 * d] = (dm * yfx_ref[...] * sfx * (1.0 - sfx)).astype(BF16)

    rows = pl.BlockSpec((tm, d), lambda i: (i, 0))
    rows1 = pl.BlockSpec((tm, d), lambda i: (i, 1))
    return pl.pallas_call(
        body, name="gate_bwd", grid=(t // tm,),
        in_specs=[rows, rows, rows, rows, rows1],
        out_specs=[rows, rows, pl.BlockSpec((tm, 2 * d), lambda i: (i, 0))],
        out_shape=[jax.ShapeDtypeStruct((t, d), BF16)] * 2 + [jax.ShapeDtypeStruct((t, 2 * d), BF16)],
        compiler_params=_params("parallel"),
    )(dmg, y_sb, y_fx, g, g)


def _mm_res_ln(a, w, xres, gamma, beta, tm, name):
    t, kk = a.shape
    d = w.shape[1]

    def body(a_ref, w_ref, x_ref, g_ref, b_ref, xn_ref, xh_ref, rs_ref, xb_ref):
        r = ALPHA * x_ref[...] + _dot(a_ref[...].astype(BF16), w_ref[...], NN)
        mean = jnp.mean(r, axis=1, keepdims=True)
        cen = r - mean
        rstd = lax.rsqrt(jnp.mean(cen * cen, axis=1, keepdims=True) + LN_EPS)
        xh = cen * rstd
        xn = xh * g_ref[...] + b_ref[...]
        xh_ref[...] = xh
        xn_ref[...] = xn
        xb_ref[...] = xn.astype(BF16)
        rs_ref[...] = rstd

    rows_d = pl.BlockSpec((tm, d), lambda i: (i, 0))
    vec = pl.BlockSpec((1, d), lambda i: (0, 0))
    return pl.pallas_call(
        body, name=name, grid=(t // tm,),
        in_specs=[pl.BlockSpec((tm, kk), lambda i: (i, 0)), pl.BlockSpec((kk, d), lambda i: (0, 0)), rows_d, vec, vec],
        out_specs=[rows_d, rows_d, pl.BlockSpec((tm, 1), lambda i: (i, 0)), rows_d],
        out_shape=[jax.ShapeDtypeStruct((t, d), F32), jax.ShapeDtypeStruct((t, d), F32), jax.ShapeDtypeStruct((t, 1), F32),
                   jax.ShapeDtypeStruct((t, d), BF16)],
        compiler_params=_params("parallel"),
    )(a, w, xres, gamma, beta)


def _ln_bwd_math(dy, xh, rstd, gamma):
    dxh = dy * gamma
    m1 = jnp.mean(dxh, axis=1, keepdims=True)
    m2 = jnp.mean(dxh * xh, axis=1, keepdims=True)
    return rstd * (dxh - m1 - xh * m2)


def _rowsum8(x):
    tm, n = x.shape
    return jnp.sum(x.reshape(tm // SUBLANES, SUBLANES, n), axis=0)


def _fold8(ref):
    ref[0:1, :] = jnp.sum(ref[...], axis=0, keepdims=True)


def _loss_ln_bwd(x2, xh, rstd, gamma, target, tm):
    t, d = x2.shape

    def body(x_ref, xh_ref, rs_ref, g_ref, tg_ref, dr_ref, dg_ref, db_ref, ls_ref, drb_ref):
        @pl.when(pl.program_id(0) == 0)
        def _():
            dg_ref[...] = jnp.zeros_like(dg_ref)
            db_ref[...] = jnp.zeros_like(db_ref)
            ls_ref[...] = jnp.zeros_like(ls_ref)

        err = x_ref[...] - tg_ref[...]
        xh = xh_ref[...]
        dy = err * (1.0 / d)
        dr = _ln_bwd_math(dy, xh, rs_ref[...], g_ref[...])
        dr_ref[...] = dr
        drb_ref[...] = dr.astype(BF16)
        dg_ref[...] += _rowsum8(dy * xh)
        db_ref[...] += _rowsum8(dy)
        sq = _rowsum8(err * err)
        part = sq[:, 0:LANES]
        for j in range(1, d // LANES):
            part = part + sq[:, j * LANES:(j + 1) * LANES]
        ls_ref[...] += part * (0.5 / d)

        @pl.when(pl.program_id(0) == t // tm - 1)
        def _():
            _fold8(dg_ref)
            _fold8(db_ref)
            ls_ref[0:1, 0:1] = jnp.sum(jnp.sum(ls_ref[...], axis=0, keepdims=True), axis=1, keepdims=True)

    rows = pl.BlockSpec((tm, d), lambda i: (i, 0))
    acc = pl.BlockSpec((SUBLANES, d), lambda i: (0, 0))
    return pl.pallas_call(
        body, name="loss_ln_bwd", grid=(t // tm,),
        in_specs=[rows, rows, pl.BlockSpec((tm, 1), lambda i: (i, 0)), pl.BlockSpec((1, d), lambda i: (0, 0)), rows],
        out_specs=[rows, acc, acc, pl.BlockSpec((SUBLANES, LANES), lambda i: (0, 0)), rows],
        out_shape=[jax.ShapeDtypeStruct((t, d), F32), jax.ShapeDtypeStruct((SUBLANES, d), F32),
                   jax.ShapeDtypeStruct((SUBLANES, d), F32), jax.ShapeDtypeStruct((SUBLANES, LANES), F32),
                   jax.ShapeDtypeStruct((t, d), BF16)],
        compiler_params=_params("arbitrary"),
    )(x2, xh, rstd, gamma, target)


def _ln_bwd(dr_next, dlin, xh, rstd, gamma, tm):
    t, d = xh.shape

    def body(dn_ref, dl_ref, xh_ref, rs_ref, g_ref, dr_ref, dg_ref, db_ref, drb_ref):
        @pl.when(pl.program_id(0) == 0)
        def _():
            dg_ref[...] = jnp.zeros_like(dg_ref)
            db_ref[...] = jnp.zeros_like(db_ref)

        dy = ALPHA * dn_ref[...] + dl_ref[...]
        xh = xh_ref[...]
        dr = _ln_bwd_math(dy, xh, rs_ref[...], g_ref[...])
        dr_ref[...] = dr
        drb_ref[...] = dr.astype(BF16)
        dg_ref[...] += _rowsum8(dy * xh)
        db_ref[...] += _rowsum8(dy)

        @pl.when(pl.program_id(0) == t // tm - 1)
        def _():
            _fold8(dg_ref)
            _fold8(db_ref)

    rows = pl.BlockSpec((tm, d), lambda i: (i, 0))
    acc = pl.BlockSpec((SUBLANES, d), lambda i: (0, 0))
    return pl.pallas_call(
        body, name="ln_bwd", grid=(t // tm,),
        in_specs=[rows, rows, rows, pl.BlockSpec((tm, 1), lambda i: (i, 0)), pl.BlockSpec((1, d), lambda i: (0, 0))],
        out_specs=[rows, acc, acc, rows],
        out_shape=[jax.ShapeDtypeStruct((t, d), F32), jax.ShapeDtypeStruct((SUBLANES, d), F32),
                   jax.ShapeDtypeStruct((SUBLANES, d), F32), jax.ShapeDtypeStruct((t, d), BF16)],
        compiler_params=_params("arbitrary"),
    )(dr_next, dlin, xh, rstd, gamma)


def _shift_rows(x, halo, shift, row):
    out = pltpu.roll(x, shift, 0)
    n = halo.shape[0]
    for r in range(shift):
        out = jnp.where(row == r, halo[n - shift + r:n - shift + r + 1, :], out)
    return out


def _unshift_rows(x, halo, shift, row, tm):
    out = pltpu.roll(x, tm - shift, 0)
    for r in range(shift):
        out = jnp.where(row == tm - shift + r, halo[r:r + 1, :], out)
    return out


def _conv_pre(ug_ref, halo_ref, wc_ref, bc_ref, first, tm):
    ug = ug_ref[...].astype(F32)
    halo = jnp.where(first, 0.0, halo_ref[...].astype(F32))
    row = lax.broadcasted_iota(jnp.int32, ug.shape, 0)
    wc = wc_ref[...]
    um1 = _shift_rows(ug, halo, 1, row)
    um2 = _shift_rows(ug, halo, 2, row)
    c = bc_ref[...] + wc[2:3, :] * ug + wc[1:2, :] * um1 + wc[0:1, :] * um2
    return c, ug, um1, um2


INV_SQRT2 = 1.0 / math.sqrt(2.0)
INV_SQRT2PI = 1.0 / math.sqrt(2.0 * math.pi)


def _conv_glu_fwd(u, wc, bc, seq, tm):
    t, f2 = u.shape
    f = f2 // 2
    per_seq = seq // tm
    hb = tm // HALO

    def body(ug_ref, halo_ref, uv_ref, wc_ref, bc_ref, a_ref):
        first = (pl.program_id(0) % per_seq) == 0
        c, _, _, _ = _conv_pre(ug_ref, halo_ref, wc_ref, bc_ref, first, tm)
        gelu = 0.5 * c * (1.0 + lax.erf(c * INV_SQRT2))
        a_ref[...] = (gelu * uv_ref[...].astype(F32)).astype(BF16)

    return pl.pallas_call(
        body, name="conv_glu_fwd", grid=(t // tm,),
        in_specs=[pl.BlockSpec((tm, f), lambda i: (i, 0)),
                  pl.BlockSpec((HALO, f), lambda i: (jnp.maximum(i * hb - 1, 0), 0)),
                  pl.BlockSpec((tm, f), lambda i: (i, 1)),
                  pl.BlockSpec((3, f), lambda i: (0, 0)), pl.BlockSpec((1, f), lambda i: (0, 0))],
        out_specs=pl.BlockSpec((tm, f), lambda i: (i, 0)),
        out_shape=jax.ShapeDtypeStruct((t, f), BF16),
        compiler_params=_params("parallel"),
    )(u, u, u, wc, bc)


def _conv_glu_bwd1(u, da, wc, bc, seq, tm):
    t, f2 = u.shape
    f = f2 // 2
    per_seq = seq // tm
    hb = tm // HALO

    def body(ug_ref, halo_ref, uv_ref, da_ref, wc_ref, bc_ref, dc_ref, duv_ref):
        first = (pl.program_id(0) % per_seq) == 0
        c, _, _, _ = _conv_pre(ug_ref, halo_ref, wc_ref, bc_ref, first, tm)
        cdf = 0.5 * (1.0 + lax.erf(c * INV_SQRT2))
        pdf = jnp.exp(-0.5 * c * c) * INV_SQRT2PI
        da = da_ref[...].astype(F32)
        duv_ref[...] = (da * (c * cdf)).astype(BF16)
        dc_ref[...] = da * uv_ref[...].astype(F32) * (cdf + c * pdf)

    rows = pl.BlockSpec((tm, f), lambda i: (i, 0))
    return pl.pallas_call(
        body, name="conv_glu_bwd1", grid=(t // tm,),
        in_specs=[rows, pl.BlockSpec((HALO, f), lambda i: (jnp.maximum(i * hb - 1, 0), 0)),
                  pl.BlockSpec((tm, f), lambda i: (i, 1)), rows,
                  pl.BlockSpec((3, f), lambda i: (0, 0)), pl.BlockSpec((1, f), lambda i: (0, 0))],
        out_specs=[rows, rows],
        out_shape=[jax.ShapeDtypeStruct((t, f), F32), jax.ShapeDtypeStruct((t, f), BF16)],
        compiler_params=_params("parallel"),
    )(u, u, u, da, wc, bc)


def _conv_glu_bwd2(u, dc, wc, seq, tm):
    t, f2 = u.shape
    f = f2 // 2
    per_seq = seq // tm
    hb = tm // HALO
    nblk = t // SUBLANES

    def body(ug_ref, halo_ref, dc_ref, nxt_ref, wc_ref, dug_ref, w0_ref, w1_ref, w2_ref, b_ref):
        i = pl.program_id(0)

        @pl.when(i == 0)
        def _():
            for r in (w0_ref, w1_ref, w2_ref, b_ref):
                r[...] = jnp.zeros_like(r)

        first = (i % per_seq) == 0
        last = (i % per_seq) == per_seq - 1
        ug = ug_ref[...].astype(F32)
        halo = jnp.where(first, 0.0, halo_ref[...].astype(F32))
        nxt = jnp.where(last, 0.0, nxt_ref[...])
        row = lax.broadcasted_iota(jnp.int32, ug.shape, 0)
        dc = dc_ref[...]
        wc = wc_ref[...]
        dp1 = _unshift_rows(dc, nxt, 1, row, tm)
        dp2 = _unshift_rows(dc, nxt, 2, row, tm)
        dug_ref[...] = (wc[2:3, :] * dc + wc[1:2, :] * dp1 + wc[0:1, :] * dp2).astype(BF16)
        w2_ref[...] += _rowsum8(dc * ug)
        w1_ref[...] += _rowsum8(dc * _shift_rows(ug, halo, 1, row))
        w0_ref[...] += _rowsum8(dc * _shift_rows(ug, halo, 2, row))
        b_ref[...] += _rowsum8(dc)

        @pl.when(i == t // tm - 1)
        def _():
            for r in (w0_ref, w1_ref, w2_ref, b_ref):
                _fold8(r)

    rows = pl.BlockSpec((tm, f), lambda i: (i, 0))
    acc = pl.BlockSpec((SUBLANES, f), lambda i: (0, 0))
    return pl.pallas_call(
        body, name="conv_glu_bwd2", grid=(t // tm,),
        in_specs=[rows, pl.BlockSpec((HALO, f), lambda i: (jnp.maximum(i * hb - 1, 0), 0)),
                  rows, pl.BlockSpec((SUBLANES, f), lambda i: (jnp.minimum((i + 1) * (tm // SUBLANES), nblk - 1), 0)),
                  pl.BlockSpec((3, f), lambda i: (0, 0))],
        out_specs=[rows, acc, acc, acc, acc],
        out_shape=[jax.ShapeDtypeStruct((t, f), BF16)] + [jax.ShapeDtypeStruct((SUBLANES, f), F32)] * 4,
        compiler_params=_params("arbitrary"),
    )(u, u, dc, dc, wc)


def _colsum(x, tm, name):
    t, n = x.shape

    def body(x_ref, o_ref):
        @pl.when(pl.program_id(0) == 0)
        def _():
            o_ref[...] = jnp.zeros_like(o_ref)

        o_ref[...] += _rowsum8(x_ref[...].astype(F32))

        @pl.when(pl.program_id(0) == t // tm - 1)
        def _():
            _fold8(o_ref)

    return pl.pallas_call(
        body, name=name, grid=(t // tm,),
        in_specs=[pl.BlockSpec((tm, n), lambda i: (i, 0))],
        out_specs=pl.BlockSpec((SUBLANES, n), lambda i: (0, 0)),
        out_shape=jax.ShapeDtypeStruct((SUBLANES, n), F32),
        compiler_params=_params("arbitrary"),
    )(x)


def _adamw(w, gparts, m, v, name):
    p, r, c = gparts.shape
    tr = r
    for cand in (512, 256, 128, 64, 32, 16, 8):
        if cand * p <= 1024 and r % cand == 0 and r > cand:
            tr = cand
            break
    c1 = 1.0 - ADAM_B1 ** ADAM_STEP
    c2 = 1.0 - ADAM_B2 ** ADAM_STEP

    def body(w_ref, g_ref, m_ref, v_ref, go_ref, d_ref, mo_ref, vo_ref):
        g = g_ref[0].astype(F32)
        for i in range(1, p):
            g = g + g_ref[i].astype(F32)
        mn = ADAM_B1 * m_ref[...] + (1.0 - ADAM_B1) * g
        vn = ADAM_B2 * v_ref[...] + (1.0 - ADAM_B2) * (g * g)
        go_ref[...] = g
        mo_ref[...] = mn
        vo_ref[...] = vn
        d_ref[...] = -ADAM_LR * ((mn / c1) / (jnp.sqrt(vn / c2) + ADAM_EPS) + ADAM_WD * w_ref[...])

    blk = pl.BlockSpec((tr, c), lambda i: (i, 0))
    return pl.pallas_call(
        body, name=name, grid=(r // tr,),
        in_specs=[blk, pl.BlockSpec((p, tr, c), lambda i: (0, i, 0)), blk, blk],
        out_specs=[blk] * 4,
        out_shape=[jax.ShapeDtypeStruct((r, c), F32)] * 4,
        compiler_params=_params("parallel"),
    )(w, gparts, m, v)


MESH = pl.DeviceIdType.MESH
ANY = pl.BlockSpec(memory_space=pl.ANY)


def _all_gather(xs, name):
    n = len(xs)

    def body(*refs):
        x_refs, out_refs = refs[:n], refs[n:2 * n]
        send_sems, recv_sems, local_sems = refs[2 * n:]
        x, y, c = lax.axis_index("x"), lax.axis_index("y"), lax.axis_index("c")
        me, sibling = (x, y, c), (x, y, 1 - c)
        chips = [(1 - x, y), (x, 1 - y), (1 - x, 1 - y)]

        def slot(a, px, py, pc):
            return out_refs[a].at[4 * px + 2 * py + pc]

        def copy(a, k, block, to, src=None):
            return pltpu.make_async_remote_copy(
                src_ref=slot(a, *block) if src is None else src, dst_ref=slot(a, *block),
                send_sem=send_sems.at[k * n + a], recv_sem=recv_sems.at[k * n + a], device_id=to, device_id_type=MESH)

        arrays = range(n)
        mine = [pltpu.make_async_copy(x_refs[a], slot(a, *me), local_sems.at[a]) for a in arrays]
        first = [copy(a, 0, me, sibling, src=x_refs[a]) for a in arrays]
        first += [copy(a, 1 + j, me, (*chip, c), src=x_refs[a]) for j, chip in enumerate(chips) for a in arrays]
        for cp in mine + first:
            cp.start()
        passed = []
        for j, chip in enumerate(chips):
            for a in arrays:
                copy(a, 1 + j, (*chip, c), me).wait_recv()
                passed.append(copy(a, 4 + j, (*chip, c), sibling))
                passed[-1].start()
        for a in arrays:
            copy(a, 0, sibling, me).wait_recv()
        for j, chip in enumerate(chips):
            for a in arrays:
                copy(a, 4 + j, (*chip, 1 - c), me).wait_recv()
        for cp in first + passed:
            cp.wait_send()
        for cp in mine:
            cp.wait()

    return pl.pallas_call(
        body, name=name,
        out_shape=[jax.ShapeDtypeStruct((N_DEV,) + x.shape, x.dtype) for x in xs],
        in_specs=[ANY] * n, out_specs=[ANY] * n,
        scratch_shapes=[pltpu.SemaphoreType.DMA((7 * n,)), pltpu.SemaphoreType.DMA((7 * n,)),
                        pltpu.SemaphoreType.DMA((n,))],
    )(*xs)


def _peer_copies(kind, src_refs, dst_refs, send_sems, recv_sems, local_sems):
    n = len(src_refs)
    x, y, c = lax.axis_index("x"), lax.axis_index("y"), lax.axis_index("c")
    mine = 4 * x + 2 * y + c

    def src(a, idx):
        return src_refs[a] if kind == "spread" else src_refs[a].at[idx]

    copies = [pltpu.make_async_copy(src(a, mine), dst_refs[a].at[mine], local_sems.at[a]) for a in range(n)]
    for k in range(1, N_DEV):
        px = 1 - x if k & 4 else x
        py = 1 - y if k & 2 else y
        pc = 1 - c if k & 1 else c
        for a in range(n):
            copies.append(pltpu.make_async_remote_copy(
                src_ref=src(a, 4 * px + 2 * py + pc), dst_ref=dst_refs[a].at[mine],
                send_sem=send_sems.at[(k - 1) * n + a], recv_sem=recv_sems.at[(k - 1) * n + a],
                device_id=(px, py, pc), device_id_type=MESH))
    return copies


def _peer_shapes(kind, arrays):
    n = len(arrays)
    outs = [jax.ShapeDtypeStruct(((N_DEV,) + a.shape) if kind == "spread" else a.shape, a.dtype) for a in arrays]
    sems = [pltpu.SemaphoreType.DMA((7 * n,)), pltpu.SemaphoreType.DMA((7 * n,)), pltpu.SemaphoreType.DMA((n,))]
    return outs, sems


def _exchange(gs, name):
    n = len(gs)

    def body(*refs):
        copies = _peer_copies("exchange", refs[:n], refs[n:2 * n], *refs[2 * n:])
        for cp in copies:
            cp.start()
        for cp in copies:
            cp.wait()

    outs, sems = _peer_shapes("exchange", gs)
    return pl.pallas_call(body, name=name, out_shape=outs, in_specs=[ANY] * n, out_specs=[ANY] * n,
                          scratch_shapes=sems)(*gs)


def _tile(n, pref, unit=LANES):
    if n <= pref:
        return n
    best = None
    for cand in range(unit, pref + 1, unit):
        if n % cand == 0:
            best = cand
    assert best is not None, (n, pref, unit)
    return best


LATE_KEYS = dict(w_proj_sb="wp_sb", w_proj_fox="wp_fx", w_out="w_out", w_up="w_up", w_conv="w_conv", w_down="w_down")


def _layer_step(x, target, w, attn_blk, late=None):
    b, s, d = x.shape
    t = b * s
    w = dict(w)
    e = w["w_qkv"].shape[1] // 6
    h = e // HEAD_DIM
    f = w["b_conv"].shape[1]
    x2 = x.reshape(t, d)
    tg = target.reshape(t, d)
    pp = 2 if (e // LANES) % 2 == 0 else 1
    tm = _tile(t, 512, HALO)
    tmc = _tile(s, 256, HALO)
    tkt = _tile(t, 2048, HALO)
    td = _tile(d, 1024)
    tf = _tile(f, 1408)
    t2f = _tile(2 * f, 1408)
    tqkv = _tile(6 * e, 1024)
    tg2 = _tile(2 * d, 1024)
    xb = x2.astype(BF16)

    qkv = _matmul(xb, w["w_qkv"], NN, tm=tm, tn=tqkv, tk=d, out_dtype=BF16, name="in_qkv", bias=w["b_qkv"], b_outer=True)
    gate = _matmul(xb, jnp.concatenate([w["w_g"], w["w_f"]], axis=1), NN, tm=tm, tn=2 * d + LANES, tk=d, out_dtype=F32,
                   name="in_gate", bias=jnp.concatenate([w["b_g"], w["b_f"]], axis=1), b_outer=True)
    nr = s // LANES
    f2 = gate[:, 2 * d:2 * d + h].reshape(b, s, h).transpose(0, 2, 1).reshape(b * h * nr, LANES)
    ct = _scan_rows(f2, nr, "fwd").reshape(b * h, 1, s)
    o_sb, tot, first = _sb_fwd(qkv, b, s, e, attn_blk, pp)
    if late is None:
        o_fx, lse = _fox_fwd(qkv, ct, b, s, e, attn_blk, pp)
    else:
        o_fx, lse, *gathered = _fox_fwd(qkv, ct, b, s, e, attn_blk, pp, comm=("spread", late[1]))
        for name, g in zip(late[0], gathered):
            w[LATE_KEYS[name]] = _join(g, name)
    merged, y_sb, y_fx = _proj_gate_fwd(o_sb, o_fx, w["wp_sb"], w["wp_fx"], gate, tm)
    x1, xh1, rs1, x1b = _mm_res_ln(merged, w["w_out"], x2, w["ln1_g"], w["ln1_b"], tm, "out_ln1")
    u = _matmul(x1b, w["w_up"], NN, tm=tm, tn=t2f, tk=d, out_dtype=BF16, name="ffn_up", b_outer=True)
    act = _conv_glu_fwd(u, w["w_conv"], w["b_conv"], s, tmc)
    xo, xh2, rs2, _ = _mm_res_ln(act, w["w_down"], x1, w["ln2_g"], w["ln2_b"], tm, "down_ln2")

    gr = {}
    dr2, dg2, db2, ls, dr2b = _loss_ln_bwd(xo, xh2, rs2, w["ln2_g"], tg, tm)
    gr["ln2_g"], gr["ln2_b"] = dg2[0:1], db2[0:1]
    da = _matmul(dr2b, w["w_down"], NT, tm=tm, tn=tf, tk=d, out_dtype=BF16, name="d_act", b_outer=True)
    gr["w_down"] = _matmul(act, dr2b, TN, tm=tf, tn=td, tk=tkt, out_dtype=F32, name="dw_down")
    dc, du_v = _conv_glu_bwd1(u, da, w["w_conv"], w["b_conv"], s, tmc)
    du_g, gw0, gw1, gw2, gbc = _conv_glu_bwd2(u, dc, w["w_conv"], s, tmc)
    gr["w_conv"] = jnp.concatenate([gw0[0:1], gw1[0:1], gw2[0:1]], axis=0)
    gr["b_conv"] = gbc[0:1]
    du = jnp.concatenate([du_g, du_v], axis=1)
    dlin1 = _matmul(du, w["w_up"], NT, tm=tm, tn=td, tk=2 * f, out_dtype=F32, name="d_x1")
    gr["w_up"] = _matmul(x1b, du, TN, tm=td, tn=t2f, tk=tkt, out_dtype=F32, name="dw_up")
    dr1, dg1, db1, dr1b = _ln_bwd(dr2, dlin1, xh1, rs1, w["ln1_g"], tm)
    gr["ln1_g"], gr["ln1_b"] = dg1[0:1], db1[0:1]
    dmg = _matmul(dr1b, w["w_out"], NT, tm=tm, tn=td, tk=d, out_dtype=F32, name="d_merged")
    gr["w_out"] = _matmul(merged, dr1b, TN, tm=td, tn=td, tk=tkt, out_dtype=F32, name="dw_out")
    dy_sb, dy_fx, dgate = _gate_bwd(dmg, y_sb, y_fx, gate, tm)
    do_sb = _matmul(dy_sb, w["wp_sb"], NT, tm=tm, tn=e, tk=d, out_dtype=BF16, name="d_o_sb")
    do_fx = _matmul(dy_fx, w["wp_fx"], NT, tm=tm, tn=e, tk=d, out_dtype=BF16, name="d_o_fx")
    gr["wp_sb"] = _matmul(o_sb, dy_sb, TN, tm=e, tn=td, tk=tkt, out_dtype=F32, name="dwp_sb")
    gr["wp_fx"] = _matmul(o_fx, dy_fx, TN, tm=e, tn=td, tk=tkt, out_dtype=F32, name="dwp_fx")
    dq_sb, dk_sb, dv_sb = _sb_bwd(qkv, do_sb, tot, first, b, s, e, attn_blk, pp)
    landed = None
    if late is None:
        dq_fx, dk_fx, dv_fx, dct = _fox_bwd(qkv, ct, do_fx, o_fx, lse, b, s, e, attn_blk, pp)
    else:
        blocks = [_cut(gr[LATE_KEYS[n]], n).astype(BF16 if n in MATMUL_OPERANDS else F32) for n in late[0]]
        dq_fx, dk_fx, dv_fx, dct, *got = _fox_bwd(qkv, ct, do_fx, o_fx, lse, b, s, e, attn_blk, pp,
                                                   comm=("exchange", blocks))
        landed = dict(zip(late[0], got))
    dqkv = jnp.concatenate([dq_sb, dk_sb, dv_sb, dq_fx, dk_fx, dv_fx], axis=1)
    df2 = _scan_rows(f2, nr, "bwd", dct.reshape(b * h * nr, LANES))
    df = jnp.pad(df2.reshape(b, h, s).transpose(0, 2, 1).reshape(t, h), ((0, 0), (0, LANES - h))).astype(BF16)
    gr["w_qkv"] = _matmul(xb, dqkv, TN, tm=td, tn=tqkv, tk=tkt, out_dtype=F32, name="dw_qkv")
    gr["w_g"] = _matmul(xb, dgate, TN, tm=td, tn=tg2, tk=tkt, out_dtype=F32, name="dw_gate")
    gr["w_f"] = _matmul(xb, df, TN, tm=td, tn=LANES, tk=tkt, out_dtype=F32, name="dw_forget")
    gr["b_qkv"] = _colsum(dqkv, tm, "db_qkv")[0:1]
    gr["b_g"] = _colsum(dgate, tm, "db_gate")[0:1]
    gr["b_f"] = _colsum(df, tm, "db_forget")[0:1]
    comm = None
    if late is not None:
        comm = ("exchange", [_cut(_w_in_layout(gr["w_qkv"], gr["w_f"], gr["w_g"], h), "w_in").astype(BF16)])
    dx, *got = _input_grad(jnp.concatenate([dqkv, dgate, df], axis=1),
                           jnp.concatenate([w["w_qkv"], w["w_g"], w["w_f"]], axis=1), dr1, tm, comm)
    if late is not None:
        landed["w_in"] = got[0]
    return ls[0:1, 0:1], dx.reshape(b, s, d), gr, landed


SHARDED = ("w_in", "w_proj_sb", "w_proj_fox", "w_out", "w_up", "w_conv", "w_down")
ROW_SHARDED = ("w_out", "w_down")
REPLICATED = ("b_in", "ln1_g", "ln1_b", "b_conv", "ln2_g", "ln2_b")
WEIGHTS = ("w_in", "b_in", "w_proj_sb", "w_proj_fox", "w_out", "ln1_g", "ln1_b", "w_up", "w_conv", "b_conv",
           "w_down", "ln2_g", "ln2_b")
MATMUL_OPERANDS = ("w_in", "w_proj_sb", "w_proj_fox", "w_out", "w_up", "w_down")


def _w_in_layout(g_qkv, g_f, g_g, h):
    return jnp.concatenate([g_qkv, g_f[:, :h], g_g], axis=1)


def _cut(full, name):
    r, c = full.shape
    if name in ROW_SHARDED:
        return full.reshape(N_DEV, r // N_DEV, c)
    cs = c // N_DEV
    return jnp.stack([full[:, j * cs:(j + 1) * cs] for j in range(N_DEV)], axis=0)


def _join(blocks, name):
    p, r, c = blocks.shape
    if name in ROW_SHARDED:
        return blocks.reshape(p * r, c)
    return jnp.concatenate([blocks[j] for j in range(p)], axis=1)


def kernel(x, w_in, b_in, w_proj_sb, w_proj_fox, w_out, ln1_g, ln1_b, w_up, w_conv, b_conv, w_down, ln2_g, ln2_b, loss_target, m_w_in, m_b_in, m_w_proj_sb, m_w_proj_fox, m_w_out, m_ln1_g, m_ln1_b, m_w_up, m_w_conv, m_b_conv, m_w_down, m_ln2_g, m_ln2_b, v_w_in, v_b_in, v_w_proj_sb, v_w_proj_fox, v_w_out, v_ln1_g, v_ln1_b, v_w_up, v_w_conv, v_b_conv, v_w_down, v_ln2_g, v_ln2_b):
    wts = dict(w_in=w_in, b_in=b_in, w_proj_sb=w_proj_sb, w_proj_fox=w_proj_fox, w_out=w_out, ln1_g=ln1_g, ln1_b=ln1_b,
               w_up=w_up, w_conv=w_conv, b_conv=b_conv, w_down=w_down, ln2_g=ln2_g, ln2_b=ln2_b)
    mom = dict(w_in=m_w_in, b_in=m_b_in, w_proj_sb=m_w_proj_sb, w_proj_fox=m_w_proj_fox, w_out=m_w_out, ln1_g=m_ln1_g,
               ln1_b=m_ln1_b, w_up=m_w_up, w_conv=m_w_conv, b_conv=m_b_conv, w_down=m_w_down, ln2_g=m_ln2_g, ln2_b=m_ln2_b)
    var = dict(w_in=v_w_in, b_in=v_b_in, w_proj_sb=v_w_proj_sb, w_proj_fox=v_w_proj_fox, w_out=v_w_out, ln1_g=v_ln1_g,
               ln1_b=v_ln1_b, w_up=v_w_up, w_conv=v_w_conv, b_conv=v_b_conv, w_down=v_w_down, ln2_g=v_ln2_g, ln2_b=v_ln2_b)
    shard = {n: wts[n].reshape(wts[n].shape[-2:]) for n in WEIGHTS}

    w_in_full = _join(_all_gather([shard["w_in"].astype(BF16)], "gather_w_in")[0], "w_in")
    late_names = [n for n in SHARDED if n != "w_in"]
    late = (late_names, [shard[n].astype(BF16) if n in MATMUL_OPERANDS else shard[n] for n in late_names])
    e = shard["w_proj_sb"].shape[0]
    h = e // HEAD_DIM
    nq = 6 * e

    def cut_in(a, pad):
        fcols = a[:, nq:nq + h]
        if pad:
            fcols = jnp.pad(fcols, ((0, 0), (0, LANES - h)))
        return a[:, :nq], a[:, nq + h:], fcols

    w_qkv, w_g, w_f = cut_in(w_in_full, True)
    b_qkv, b_g, b_f = cut_in(shard["b_in"], True)
    w = dict(w_qkv=w_qkv, w_g=w_g, w_f=w_f, b_qkv=b_qkv, b_g=b_g, b_f=b_f, b_conv=shard["b_conv"],
             ln1_g=shard["ln1_g"], ln1_b=shard["ln1_b"], ln2_g=shard["ln2_g"], ln2_b=shard["ln2_b"])

    loss_local, grad_x, gr, gsum = _layer_step(x, loss_target, w, min(256, x.shape[1]), late)
    loss = lax.psum(loss_local[0, 0], ("x", "y", "c"))

    local = dict(b_in=_w_in_layout(gr["b_qkv"], gr["b_f"], gr["b_g"], h), ln1_g=gr["ln1_g"], ln1_b=gr["ln1_b"],
                 b_conv=gr["b_conv"], ln2_g=gr["ln2_g"], ln2_b=gr["ln2_b"])
    parts = _all_gather([jnp.concatenate([local[n] for n in REPLICATED], axis=1)], "gather_small_grads")[0]
    off = 0
    for n in REPLICATED:
        gsum[n] = parts[:, :, off:off + shard[n].size]
        off += shard[n].size

    grads, deltas, new_m, new_v = [], [], [], []
    for n in WEIGHTS:
        shp = wts[n].shape
        g, dl, mn, vn = _adamw(shard[n], gsum[n], mom[n].reshape(shard[n].shape), var[n].reshape(shard[n].shape),
                               "adamw_" + n)
        grads.append(g.reshape(shp))
        deltas.append(dl.reshape(shp))
        new_m.append(mn.reshape(shp))
        new_v.append(vn.reshape(shp))
    return (loss, grad_x, *grads, *deltas, *new_m, *new_v)
```

```python
import functools
import math

import jax
import jax.numpy as jnp
from jax import lax
from jax.experimental import pallas as pl
from jax.experimental.pallas import tpu as pltpu

F32 = jnp.float32
BF16 = jnp.bfloat16

HEAD_DIM = 64
LN_EPS = 1e-5
DEPTH = 1
ALPHA = (2.0 * DEPTH) ** 0.25
ADAM_LR, ADAM_B1, ADAM_B2, ADAM_EPS, ADAM_WD, ADAM_STEP = 0.001, 0.9, 0.999, 1e-08, 0.01, 10
N_DEV = 8
LANES = 128
SUBLANES = 8
HALO = 16
VMEM_LIMIT = 56 * 1024 * 1024

NN = ((1,), (0,))
NT = ((1,), (1,))
TN = ((0,), (0,))


def _dot(a, b, dims):
    return lax.dot_general(a, b, (dims, ((), ())), preferred_element_type=F32)


def _params(*sem):
    return pltpu.CompilerParams(dimension_semantics=sem, vmem_limit_bytes=VMEM_LIMIT)


def _iotas(blk):
    row = lax.broadcasted_iota(jnp.int32, (blk, blk), 0)
    col = lax.broadcasted_iota(jnp.int32, (blk, blk), 1)
    return row, col


def _sb_terms(z):
    e = jnp.exp(-jnp.abs(z))
    lb = jnp.minimum(z, 0.0) - jnp.log(1.0 + e)
    return lb, lb - z, e


def _pair_specs(s, blk, e, pp, branch):
    w = pp * LANES
    nq = s // blk
    ng = e // w
    base = 3 * branch * ng
    q_in = pl.BlockSpec((blk, w), lambda b, g, i: (b * nq + i, base + g))
    k_in = pl.BlockSpec((s, w), lambda b, g, i: (b, base + ng + g))
    v_in = pl.BlockSpec((s, w), lambda b, g, i: (b, base + 2 * ng + g))
    q_out = pl.BlockSpec((blk, w), lambda b, g, i: (b * nq + i, g))
    kv_out = pl.BlockSpec((s, w), lambda b, g, i: (b, g))
    rows = pl.BlockSpec((2 * pp, blk, 1), lambda b, g, i: (b * ng + g, i, 0))
    krow = pl.BlockSpec((2 * pp, 1, s), lambda b, g, i: (b * ng + g, 0, 0))
    return q_in, k_in, v_in, q_out, kv_out, rows, krow


def _to_column(r):
    row, col = _iotas(r.shape[1])
    return jnp.sum(jnp.where(row == col, r, 0.0), axis=1, keepdims=True)


def _half_masks(x):
    low = lax.broadcasted_iota(jnp.int32, x.shape, 1) < HEAD_DIM
    zero = jnp.zeros_like(x)
    return jnp.concatenate([jnp.where(low, x, zero), jnp.where(low, zero, x)], axis=0)


def _tri_sums(xs, tri):
    hi = [x.astype(BF16) for x in xs]
    lo = [(x - h.astype(F32)).astype(BF16) for x, h in zip(xs, hi)]
    n = len(xs)
    blk = xs[0].shape[0]
    r = _dot(jnp.concatenate(hi + lo, axis=0), tri, NN)
    return [r[i * blk:(i + 1) * blk] + r[(n + i) * blk:(n + i + 1) * blk] for i in range(n)]


def _sb_fwd(qkv, b, s, e, blk, pp):
    scale = HEAD_DIM ** -0.5
    nh = 2 * pp
    t = b * s

    def body(q_ref, k_ref, v_ref, o_ref, tot_ref, first_ref):
        qi = pl.program_id(2)
        qm = [_half_masks((q_ref[:, p * LANES:(p + 1) * LANES] * scale).astype(BF16)) for p in range(pp)]
        row, col = _iotas(blk)
        strict = col < row
        after = (row > col).astype(BF16)

        def block(j, carry, diag):
            off = pl.multiple_of(j * blk, blk)
            o_acc, run = carry
            zz = [_dot(qm[p], k_ref[pl.ds(off, blk), p * LANES:(p + 1) * LANES], NT) for p in range(pp)]
            z = [zz[h // 2][(h % 2) * blk:(h % 2 + 1) * blk] for h in range(nh)]
            terms = [_sb_terms(z[h]) for h in range(nh)]
            lom = [jnp.where(strict, terms[h][1], 0.0) if diag else terms[h][1] for h in range(nh)]
            sfx = _tri_sums(lom, after)
            a = [jnp.exp(terms[h][0] + sfx[h] + run[h]) for h in range(nh)]
            if diag:
                a = [jnp.where(strict, a[h], 0.0) for h in range(nh)]
            ab = [a[h].astype(BF16) for h in range(nh)]
            o_new = tuple(
                o_acc[p] + _dot(jnp.concatenate([ab[2 * p], ab[2 * p + 1]], axis=1),
                                _half_masks(v_ref[pl.ds(off, blk), p * LANES:(p + 1) * LANES]), NN)
                for p in range(pp))
            return o_new, tuple(run[h] + sfx[h][:, 0:1] + lom[h][:, 0:1] for h in range(nh))

        def alive(run):
            m = run[0]
            for h in range(1, nh):
                m = jnp.maximum(m, run[h])
            return jnp.max(m) > DEAD

        o_acc, run = block(qi, ((jnp.zeros((blk, LANES), F32),) * pp, (jnp.zeros((blk, 1), F32),) * nh), True)

        def step(c):
            j, _, o_acc, run = c
            o_acc, run = block(j, (o_acc, run), False)
            return j - 1, alive(run), o_acc, run

        j, _, o_acc, run = lax.while_loop(lambda c: jnp.logical_and(c[0] >= 0, c[1]), step,
                                          (qi - 1, alive(run), o_acc, run))
        for p in range(pp):
            o_ref[:, p * LANES:(p + 1) * LANES] = o_acc[p].astype(o_ref.dtype)
        for h in range(nh):
            tot_ref[h] = run[h]
            first_ref[h] = jnp.zeros((blk, 1), F32) + (j + 1).astype(F32)

    q_in, k_in, v_in, q_out, _, rows, _ = _pair_specs(s, blk, e, pp, 0)
    return pl.pallas_call(
        body, name="sb_fwd", grid=(b, e // (pp * LANES), s // blk),
        in_specs=[q_in, k_in, v_in], out_specs=[q_out, rows, rows],
        out_shape=[jax.ShapeDtypeStruct((t, e), BF16)] + [jax.ShapeDtypeStruct((b * e // HEAD_DIM, s, 1), F32)] * 2,
        compiler_params=_params("parallel", "parallel", "arbitrary"),
    )(qkv, qkv, qkv)


def _sb_bwd(qkv, do, tot, first, b, s, e, blk, pp):
    scale = HEAD_DIM ** -0.5
    nh = 2 * pp
    t = b * s
    nq = s // blk

    def body(q_ref, k_ref, v_ref, do_ref, tot_ref, first_ref, dq_ref, dk_ref, dv_ref, dk_acc, dv_acc):
        qi = pl.program_id(2)

        @pl.when(qi == 0)
        def _():
            dk_acc[...] = jnp.zeros_like(dk_acc)
            dv_acc[...] = jnp.zeros_like(dv_acc)

        qm = [_half_masks((q_ref[:, p * LANES:(p + 1) * LANES] * scale).astype(BF16)) for p in range(pp)]
        dom = [_half_masks(do_ref[:, p * LANES:(p + 1) * LANES].astype(BF16)) for p in range(pp)]
        tot_t = [tot_ref[h] for h in range(nh)]
        row, col = _iotas(blk)
        strict = col < row
        upto = (row <= col).astype(BF16)
        before = (row < col).astype(BF16)

        def block(j, carry, diag):
            off = pl.multiple_of(j * blk, blk)
            dq_acc, cl, cg = carry
            kp = [k_ref[pl.ds(off, blk), p * LANES:(p + 1) * LANES] for p in range(pp)]
            vp = [v_ref[pl.ds(off, blk), p * LANES:(p + 1) * LANES] for p in range(pp)]
            zz = [_dot(qm[p], kp[p], NT) for p in range(pp)]
            dd = [_dot(dom[p], vp[p], NT) for p in range(pp)]
            z = [zz[h // 2][(h % 2) * blk:(h % 2 + 1) * blk] for h in range(nh)]
            da = [dd[h // 2][(h % 2) * blk:(h % 2 + 1) * blk] for h in range(nh)]
            terms = [_sb_terms(z[h]) for h in range(nh)]
            lom = [jnp.where(strict, terms[h][1], 0.0) if diag else terms[h][1] for h in range(nh)]
            pre = _tri_sums(lom, upto)
            a = [jnp.exp(terms[h][0] + (tot_t[h] - cl[h] - pre[h])) for h in range(nh)]
            if diag:
                a = [jnp.where(strict, a[h], 0.0) for h in range(nh)]
            g = [a[h] * da[h] for h in range(nh)]
            pw = _tri_sums(g, before)
            dzb = []
            for h in range(nh):
                ex = terms[h][2]
                r = 1.0 / (1.0 + ex)
                er = ex * r
                pos = z[h] >= 0.0
                dz = g[h] * jnp.where(pos, er, r) - (cg[h] + pw[h]) * jnp.where(pos, r, er)
                if diag:
                    dz = jnp.where(strict, dz, 0.0)
                dzb.append(dz.astype(BF16))
            ab = [a[h].astype(BF16) for h in range(nh)]
            for p in range(pp):
                cols = slice(p * LANES, (p + 1) * LANES)
                dk_acc[pl.ds(off, blk), cols] += _dot(jnp.concatenate([dzb[2 * p], dzb[2 * p + 1]], axis=0), qm[p], TN)
                dv_acc[pl.ds(off, blk), cols] += _dot(jnp.concatenate([ab[2 * p], ab[2 * p + 1]], axis=0), dom[p], TN)
            dq_new = tuple(dq_acc[p] + _dot(jnp.concatenate([dzb[2 * p], dzb[2 * p + 1]], axis=1), _half_masks(kp[p]), NN)
                           for p in range(pp))
            return (dq_new, tuple(cl[h] + pre[h][:, blk - 1:blk] for h in range(nh)),
                    tuple(cg[h] + pw[h][:, blk - 1:blk] + g[h][:, blk - 1:blk] for h in range(nh)))

        zero1 = (jnp.zeros((blk, 1), F32),) * nh
        j0 = jnp.clip(jnp.max(first_ref[0]).astype(jnp.int32), 0, qi)
        carry = lax.fori_loop(j0, qi, lambda j, c: block(j, c, False), ((jnp.zeros((blk, LANES), F32),) * pp, zero1, zero1))
        dq_acc, _, _ = block(qi, carry, True)
        for p in range(pp):
            dq_ref[:, p * LANES:(p + 1) * LANES] = (dq_acc[p] * scale).astype(BF16)

        @pl.when(qi == nq - 1)
        def _():
            dk_ref[...] = dk_acc[...].astype(BF16)
            dv_ref[...] = dv_acc[...].astype(BF16)

    q_in, k_in, v_in, q_out, kv_out, rows, _ = _pair_specs(s, blk, e, pp, 0)
    w = pp * LANES
    return pl.pallas_call(
        body, name="sb_bwd", grid=(b, e // w, nq),
        in_specs=[q_in, k_in, v_in, q_out, rows, rows], out_specs=[q_out, kv_out, kv_out],
        out_shape=[jax.ShapeDtypeStruct((t, e), BF16)] * 3,
        scratch_shapes=[pltpu.VMEM((s, w), F32), pltpu.VMEM((s, w), F32)],
        compiler_params=_params("parallel", "parallel", "arbitrary"),
    )(qkv, qkv, qkv, do, tot, first)


NEG = -1e30
DEAD = -110.0


def _ride_along(comm, src_refs, dst_refs, sems, first, last):
    if comm is None:
        return lambda: None

    @pl.when(first)
    def _():
        for cp in _peer_copies(comm[0], src_refs, dst_refs, *sems):
            cp.start()

    def finish():
        @pl.when(last)
        def _():
            for cp in _peer_copies(comm[0], src_refs, dst_refs, *sems):
                cp.wait()

    return finish


def _grid_ends(grid):
    ids = [pl.program_id(a) for a in range(len(grid))]
    first = functools.reduce(jnp.logical_and, [i == 0 for i in ids])
    last = functools.reduce(jnp.logical_and, [i == g - 1 for i, g in zip(ids, grid)])
    return first, last


def _fox_fwd(qkv, ct, b, s, e, blk, pp, comm=None):
    scale = HEAD_DIM ** -0.5
    nh = 2 * pp
    t = b * s
    nc = len(comm[1]) if comm else 0
    grid = (b, e // (pp * LANES), s // blk)

    def body(*refs):
        q_ref, k_ref, v_ref, ct_ref = refs[:4]
        o_ref, lse_ref = refs[4 + nc:6 + nc]
        finish = _ride_along(comm, refs[4:4 + nc], refs[6 + nc:6 + 2 * nc], refs[6 + 2 * nc:], *_grid_ends(grid))
        qi = pl.program_id(2)
        qm = [_half_masks((q_ref[:, p * LANES:(p + 1) * LANES] * scale).astype(BF16)) for p in range(pp)]
        cq = [_to_column(ct_ref[h, :, pl.ds(pl.multiple_of(qi * blk, blk), blk)]) for h in range(nh)]
        row, col = _iotas(blk)
        causal = col <= row
        low = lax.broadcasted_iota(jnp.int32, (blk, LANES), 1) < HEAD_DIM

        def block(j, carry, diag):
            off = pl.multiple_of(j * blk, blk)
            m, l, acc = carry
            zz = [_dot(qm[p], k_ref[pl.ds(off, blk), p * LANES:(p + 1) * LANES], NT) for p in range(pp)]
            z = [zz[h // 2][(h % 2) * blk:(h % 2 + 1) * blk] + (cq[h] - ct_ref[h, :, pl.ds(off, blk)]) for h in range(nh)]
            if diag:
                z = [jnp.where(causal, z[h], NEG) for h in range(nh)]
            m_new = tuple(jnp.maximum(m[h], jnp.max(z[h], axis=1, keepdims=True)) for h in range(nh))
            w = [jnp.exp(m[h] - m_new[h]) for h in range(nh)]
            pr = [jnp.exp(z[h] - m_new[h]) for h in range(nh)]
            pb = [pr[h].astype(BF16) for h in range(nh)]
            pv = [_dot(jnp.concatenate([pb[2 * p], pb[2 * p + 1]], axis=1),
                       _half_masks(v_ref[pl.ds(off, blk), p * LANES:(p + 1) * LANES]), NN) for p in range(pp)]
            acc_new = tuple(jnp.where(low, w[2 * p], w[2 * p + 1]) * acc[p] + pv[p] for p in range(pp))
            l_new = tuple(w[h] * l[h] + jnp.sum(pr[h], axis=1, keepdims=True) for h in range(nh))
            return m_new, l_new, acc_new

        zero = ((jnp.full((blk, 1), NEG, F32),) * nh, (jnp.zeros((blk, 1), F32),) * nh, (jnp.zeros((blk, LANES), F32),) * pp)
        carry = block(qi, zero, True)
        m, l, acc = lax.fori_loop(0, qi, lambda j, c_: block(j, c_, False), carry)
        for p in range(pp):
            o_ref[:, p * LANES:(p + 1) * LANES] = acc[p] / jnp.where(low, l[2 * p], l[2 * p + 1])
        for h in range(nh):
            lse_ref[h] = m[h] + jnp.log(l[h])
        finish()

    q_in, k_in, v_in, q_out, _, rows, krow = _pair_specs(s, blk, e, pp, 1)
    outs, sems = _peer_shapes(comm[0], comm[1]) if comm else ([], [])
    return pl.pallas_call(
        body, name="fox_fwd", grid=grid,
        in_specs=[q_in, k_in, v_in, krow] + [ANY] * nc, out_specs=[q_out, rows] + [ANY] * nc,
        out_shape=[jax.ShapeDtypeStruct((t, e), F32), jax.ShapeDtypeStruct((b * e // HEAD_DIM, s, 1), F32)] + outs,
        scratch_shapes=sems,
        compiler_params=_params("arbitrary", "arbitrary", "arbitrary"),
    )(qkv, qkv, qkv, ct, *(comm[1] if comm else ()))


def _fox_bwd(qkv, ct, do, o, lse, b, s, e, blk, pp, comm=None):
    scale = HEAD_DIM ** -0.5
    nh = 2 * pp
    t = b * s
    nq = s // blk
    nc = len(comm[1]) if comm else 0
    w = pp * LANES
    grid = (b, e // w, nq)

    def body(*refs):
        q_ref, k_ref, v_ref, ct_ref, do_ref, o_ref, lse_ref = refs[:7]
        dq_ref, dk_ref, dv_ref, dct_ref = refs[7 + nc:11 + nc]
        dk_acc, dv_acc = refs[11 + 2 * nc:13 + 2 * nc]
        finish = _ride_along(comm, refs[7:7 + nc], refs[11 + nc:11 + 2 * nc], refs[13 + 2 * nc:], *_grid_ends(grid))
        qi = pl.program_id(2)

        @pl.when(qi == 0)
        def _():
            dk_acc[...] = jnp.zeros_like(dk_acc)
            dv_acc[...] = jnp.zeros_like(dv_acc)
            dct_ref[...] = jnp.zeros_like(dct_ref)

        qm = [_half_masks((q_ref[:, p * LANES:(p + 1) * LANES] * scale).astype(BF16)) for p in range(pp)]
        dob = [do_ref[:, p * LANES:(p + 1) * LANES].astype(BF16) for p in range(pp)]
        dom = [_half_masks(dob[p]) for p in range(pp)]
        low = lax.broadcasted_iota(jnp.int32, (blk, LANES), 1) < HEAD_DIM
        delta = []
        for p in range(pp):
            prod = dob[p].astype(F32) * o_ref[:, p * LANES:(p + 1) * LANES]
            delta.append(jnp.sum(jnp.where(low, prod, 0.0), axis=1, keepdims=True))
            delta.append(jnp.sum(jnp.where(low, 0.0, prod), axis=1, keepdims=True))
        cq = [_to_column(ct_ref[h, :, pl.ds(pl.multiple_of(qi * blk, blk), blk)]) for h in range(nh)]
        lse_t = [lse_ref[h] for h in range(nh)]
        row, col = _iotas(blk)
        causal = col <= row

        def block(j, dq_acc, diag):
            off = pl.multiple_of(j * blk, blk)
            kp = [k_ref[pl.ds(off, blk), p * LANES:(p + 1) * LANES] for p in range(pp)]
            zz = [_dot(qm[p], kp[p], NT) for p in range(pp)]
            dd = [_dot(dom[p], v_ref[pl.ds(off, blk), p * LANES:(p + 1) * LANES], NT) for p in range(pp)]
            z = [zz[h // 2][(h % 2) * blk:(h % 2 + 1) * blk] + (cq[h] - ct_ref[h, :, pl.ds(off, blk)]) for h in range(nh)]
            pr = [jnp.exp(z[h] - lse_t[h]) for h in range(nh)]
            if diag:
                pr = [jnp.where(causal, pr[h], 0.0) for h in range(nh)]
            ds = [pr[h] * (dd[h // 2][(h % 2) * blk:(h % 2 + 1) * blk] - delta[h]) for h in range(nh)]
            dsb = [ds[h].astype(BF16) for h in range(nh)]
            pb = [pr[h].astype(BF16) for h in range(nh)]
            for p in range(pp):
                cols = slice(p * LANES, (p + 1) * LANES)
                dk_acc[pl.ds(off, blk), cols] += _dot(jnp.concatenate([dsb[2 * p], dsb[2 * p + 1]], axis=0), qm[p], TN)
                dv_acc[pl.ds(off, blk), cols] += _dot(jnp.concatenate([pb[2 * p], pb[2 * p + 1]], axis=0), dom[p], TN)
            for h in range(nh):
                dct_ref[h, :, pl.ds(off, blk)] -= jnp.sum(ds[h], axis=0, keepdims=True)
            return tuple(dq_acc[p] + _dot(jnp.concatenate([dsb[2 * p], dsb[2 * p + 1]], axis=1), _half_masks(kp[p]), NN)
                         for p in range(pp))

        dq_acc = lax.fori_loop(0, qi, lambda j, a: block(j, a, False), (jnp.zeros((blk, LANES), F32),) * pp)
        dq_acc = block(qi, dq_acc, True)
        for p in range(pp):
            dq_ref[:, p * LANES:(p + 1) * LANES] = (dq_acc[p] * scale).astype(BF16)

        @pl.when(qi == nq - 1)
        def _():
            dk_ref[...] = dk_acc[...].astype(BF16)
            dv_ref[...] = dv_acc[...].astype(BF16)

        finish()

    q_in, k_in, v_in, q_out, kv_out, rows, krow = _pair_specs(s, blk, e, pp, 1)
    outs, sems = _peer_shapes(comm[0], comm[1]) if comm else ([], [])
    return pl.pallas_call(
        body, name="fox_bwd", grid=grid,
        in_specs=[q_in, k_in, v_in, krow, q_out, q_out, rows] + [ANY] * nc,
        out_specs=[q_out, kv_out, kv_out, krow] + [ANY] * nc,
        out_shape=[jax.ShapeDtypeStruct((t, e), BF16)] * 3 + [jax.ShapeDtypeStruct((b * e // HEAD_DIM, 1, s), F32)] + outs,
        scratch_shapes=[pltpu.VMEM((s, w), F32), pltpu.VMEM((s, w), F32)] + sems,
        compiler_params=_params("arbitrary", "arbitrary", "arbitrary"),
    )(qkv, qkv, qkv, ct, do, o, lse, *(comm[1] if comm else ()))


def _scan_rows(f2, group, mode, d2=None):
    n = f2.shape[0]

    def body(*refs):
        f_ref, o_ref = refs[0], refs[-1]
        f = f_ref[...]
        row, col = _iotas(LANES)
        grow = lax.broadcasted_iota(jnp.int32, (n, n), 0)
        gcol = lax.broadcasted_iota(jnp.int32, (n, n), 1)
        same = (grow // group) == (gcol // group)
        e = jnp.exp(-jnp.abs(f))
        if mode == "fwd":
            x = jnp.minimum(f, 0.0) - jnp.log1p(e)
            within = (row <= col).astype(F32)
            earlier = (same & (gcol < grow)).astype(F32)
        else:
            x = refs[1][...]
            within = (row >= col).astype(F32)
            earlier = (same & (gcol > grow)).astype(F32)
        y = jnp.dot(x, within, preferred_element_type=F32, precision=lax.Precision.HIGHEST)
        tot = jnp.sum(x, axis=1, keepdims=True)
        y = y + jnp.dot(earlier, tot, preferred_element_type=F32, precision=lax.Precision.HIGHEST)
        if mode == "bwd":
            r = 1.0 / (1.0 + e)
            y = y * jnp.where(f >= 0.0, e * r, r)
        o_ref[...] = y

    args = (f2,) if mode == "fwd" else (f2, d2)
    return pl.pallas_call(body, name="logf_" + mode, out_shape=jax.ShapeDtypeStruct(f2.shape, F32),
                          compiler_params=_params())(*args)


def _matmul(a, b, dims, *, tm, tn, tk, out_dtype, name, bias=None, res=None, res_scale=1.0, b_outer=False):
    def ij(g0, g1):
        return (g1, g0) if b_outer else (g0, g1)

    if dims == NN:
        (m, kk), n = a.shape, b.shape[1]
        a_spec = pl.BlockSpec((tm, tk), lambda g0, g1, k: (ij(g0, g1)[0], k))
        b_spec = pl.BlockSpec((tk, tn), lambda g0, g1, k: (k, ij(g0, g1)[1]))
    elif dims == NT:
        (m, kk), n = a.shape, b.shape[0]
        a_spec = pl.BlockSpec((tm, tk), lambda g0, g1, k: (ij(g0, g1)[0], k))
        b_spec = pl.BlockSpec((tn, tk), lambda g0, g1, k: (ij(g0, g1)[1], k))
    else:
        (kk, m), n = a.shape, b.shape[1]
        a_spec = pl.BlockSpec((tk, tm), lambda g0, g1, k: (k, ij(g0, g1)[0]))
        b_spec = pl.BlockSpec((tk, tn), lambda g0, g1, k: (k, ij(g0, g1)[1]))
    assert m % tm == 0 and n % tn == 0 and kk % tk == 0, (name, m, n, kk, tm, tn, tk)
    nk = kk // tk
    extras, extra_specs = [], []
    if bias is not None:
        extras.append(bias)
        extra_specs.append(pl.BlockSpec((1, tn), lambda g0, g1, k: (0, ij(g0, g1)[1])))
    if res is not None:
        extras.append(res)
        extra_specs.append(pl.BlockSpec((tm, tn), lambda g0, g1, k: ij(g0, g1)))

    def finish(out, rest, o_ref):
        idx = 0
        if bias is not None:
            out = out + rest[idx][...]
            idx += 1
        if res is not None:
            out = out + res_scale * rest[idx][...]
        o_ref[...] = out.astype(o_ref.dtype)

    def body_single(a_ref, b_ref, *rest):
        finish(_dot(a_ref[...].astype(BF16), b_ref[...].astype(BF16), dims), rest, rest[-1])

    def body_acc(a_ref, b_ref, *rest):
        o_ref, acc_ref = rest[-2], rest[-1]
        k = pl.program_id(2)
        part = _dot(a_ref[...].astype(BF16), b_ref[...].astype(BF16), dims)

        @pl.when(k == 0)
        def _():
            acc_ref[...] = part

        @pl.when(k > 0)
        def _():
            acc_ref[...] += part

        @pl.when(k == nk - 1)
        def _():
            finish(acc_ref[...], rest, o_ref)

    grid = (n // tn, m // tm, nk) if b_outer else (m // tm, n // tn, nk)
    return pl.pallas_call(
        body_single if nk == 1 else body_acc, name=name, grid=grid,
        in_specs=[a_spec, b_spec] + extra_specs,
        out_specs=pl.BlockSpec((tm, tn), lambda g0, g1, k: ij(g0, g1)),
        out_shape=jax.ShapeDtypeStruct((m, n), out_dtype),
        scratch_shapes=[] if nk == 1 else [pltpu.VMEM((tm, tn), F32)],
        compiler_params=_params("parallel", "parallel", "arbitrary"),
    )(a, b, *extras)


def _input_grad(dh, w_all, dr, tm, comm=None):
    t, kk = dh.shape
    d = w_all.shape[0]
    nc = len(comm[1]) if comm else 0
    grid = (t // tm,)

    def body(*refs):
        dh_ref, w_ref, dr_ref = refs[:3]
        o_ref = refs[3 + nc]
        finish = _ride_along(comm, refs[3:3 + nc], refs[4 + nc:4 + 2 * nc], refs[4 + 2 * nc:], *_grid_ends(grid))
        o_ref[...] = ALPHA * dr_ref[...] + _dot(dh_ref[...], w_ref[...], NT)
        finish()

    rows = pl.BlockSpec((tm, d), lambda i: (i, 0))
    outs, sems = _peer_shapes(comm[0], comm[1]) if comm else ([], [])
    return pl.pallas_call(
        body, name="input_grad", grid=grid,
        in_specs=[pl.BlockSpec((tm, kk), lambda i: (i, 0)), pl.BlockSpec((d, kk), lambda i: (0, 0)), rows] + [ANY] * nc,
        out_specs=[rows] + [ANY] * nc,
        out_shape=[jax.ShapeDtypeStruct((t, d), F32)] + outs,
        scratch_shapes=sems,
        compiler_params=_params("arbitrary"),
    )(dh, w_all, dr, *(comm[1] if comm else ()))


def _sigmoid(x):
    e = jnp.exp(-jnp.abs(x))
    r = 1.0 / (1.0 + e)
    return jnp.where(x >= 0.0, r, e * r)


def _proj_gate_fwd(o_sb, o_fx, wp_sb, wp_fx, g, tm):
    t, e = o_sb.shape
    d = wp_sb.shape[1]

    def body(osb_ref, ofx_ref, wsb_ref, wfx_ref, gsb_ref, gfx_ref, mg_ref, ysb_ref, yfx_ref):
        ysb = _dot(osb_ref[...].astype(BF16), wsb_ref[...], NN)
        yfx = _dot(ofx_ref[...].astype(BF16), wfx_ref[...], NN)
        ysb_ref[...] = ysb
        yfx_ref[...] = yfx
        mg_ref[...] = (_sigmoid(gsb_ref[...]) * ysb + _sigmoid(gfx_ref[...]) * yfx).astype(BF16)

    rows_e = pl.BlockSpec((tm, e), lambda i: (i, 0))
    rows_d = pl.BlockSpec((tm, d), lambda i: (i, 0))
    w_spec = pl.BlockSpec((e, d), lambda i: (0, 0))
    return pl.pallas_call(
        body, name="proj_gate_fwd", grid=(t // tm,),
        in_specs=[rows_e, rows_e, w_spec, w_spec, rows_d, pl.BlockSpec((tm, d), lambda i: (i, 1))],
        out_specs=[rows_d, rows_d, rows_d],
        out_shape=[jax.ShapeDtypeStruct((t, d), BF16), jax.ShapeDtypeStruct((t, d), F32), jax.ShapeDtypeStruct((t, d), F32)],
        compiler_params=_params("parallel"),
    )(o_sb, o_fx, wp_sb, wp_fx, g, g)


def _gate_bwd(dmg, y_sb, y_fx, g, tm):
    t, d = dmg.shape

    def body(dm_ref, ysb_ref, yfx_ref, gsb_ref, gfx_ref, dysb_ref, dyfx_ref, dg_ref):
        dm = dm_ref[...]
        ssb = _sigmoid(gsb_ref[...])
        sfx = _sigmoid(gfx_ref[...])
        dysb_ref[...] = (dm * ssb).astype(BF16)
        dyfx_ref[...] = (dm * sfx).astype(BF16)
        dg_ref[:, 0:d] = (dm * ysb_ref[...] * ssb * (1.0 - ssb)).astype(BF16)
        dg_ref[:, d:2 * d] = (dm * yfx_ref[...] * sfx * (1.0 - sfx)).astype(BF16)

    rows = pl.BlockSpec((tm, d), lambda i: (i, 0))
    rows1 = pl.BlockSpec((tm, d), lambda i: (i, 1))
    return pl.pallas_call(
        body, name="gate_bwd", grid=(t // tm,),
        in_specs=[rows, rows, rows, rows, rows1],
        out_specs=[rows, rows, pl.BlockSpec((tm, 2 * d), lambda i: (i, 0))],
        out_shape=[jax.ShapeDtypeStruct((t, d), BF16)] * 2 + [jax.ShapeDtypeStruct((t, 2 * d), BF16)],
        compiler_params=_params("parallel"),
    )(dmg, y_sb, y_fx, g, g)


def _mm_res_ln(a, w, xres, gamma, beta, tm, name):
    t, kk = a.shape
    d = w.shape[1]

    def body(a_ref, w_ref, x_ref, g_ref, b_ref, xn_ref, xh_ref, rs_ref, xb_ref):
        r = ALPHA * x_ref[...] + _dot(a_ref[...].astype(BF16), w_ref[...], NN)
        mean = jnp.mean(r, axis=1, keepdims=True)
        cen = r - mean
        rstd = lax.rsqrt(jnp.mean(cen * cen, axis=1, keepdims=True) + LN_EPS)
        xh = cen * rstd
        xn = xh * g_ref[...] + b_ref[...]
        xh_ref[...] = xh
        xn_ref[...] = xn
        xb_ref[...] = xn.astype(BF16)
        rs_ref[...] = rstd

    rows_d = pl.BlockSpec((tm, d), lambda i: (i, 0))
    vec = pl.BlockSpec((1, d), lambda i: (0, 0))
    return pl.pallas_call(
        body, name=name, grid=(t // tm,),
        in_specs=[pl.BlockSpec((tm, kk), lambda i: (i, 0)), pl.BlockSpec((kk, d), lambda i: (0, 0)), rows_d, vec, vec],
        out_specs=[rows_d, rows_d, pl.BlockSpec((tm, 1), lambda i: (i, 0)), rows_d],
        out_shape=[jax.ShapeDtypeStruct((t, d), F32), jax.ShapeDtypeStruct((t, d), F32), jax.ShapeDtypeStruct((t, 1), F32),
                   jax.ShapeDtypeStruct((t, d), BF16)],
        compiler_params=_params("parallel"),
    )(a, w, xres, gamma, beta)


def _ln_bwd_math(dy, xh, rstd, gamma):
    dxh = dy * gamma
    m1 = jnp.mean(dxh, axis=1, keepdims=True)
    m2 = jnp.mean(dxh * xh, axis=1, keepdims=True)
    return rstd * (dxh - m1 - xh * m2)


def _rowsum8(x):
    tm, n = x.shape
    return jnp.sum(x.reshape(tm // SUBLANES, SUBLANES, n), axis=0)


def _fold8(ref):
    ref[0:1, :] = jnp.sum(ref[...], axis=0, keepdims=True)


def _loss_ln_bwd(x2, xh, rstd, gamma, target, tm):
    t, d = x2.shape

    def body(x_ref, xh_ref, rs_ref, g_ref, tg_ref, dr_ref, dg_ref, db_ref, ls_ref, drb_ref):
        @pl.when(pl.program_id(0) == 0)
        def _():
            dg_ref[...] = jnp.zeros_like(dg_ref)
            db_ref[...] = jnp.zeros_like(db_ref)
            ls_ref[...] = jnp.zeros_like(ls_ref)

        err = x_ref[...] - tg_ref[...]
        xh = xh_ref[...]
        dy = err * (1.0 / d)
        dr = _ln_bwd_math(dy, xh, rs_ref[...], g_ref[...])
        dr_ref[...] = dr
        drb_ref[...] = dr.astype(BF16)
        dg_ref[...] += _rowsum8(dy * xh)
        db_ref[...] += _rowsum8(dy)
        sq = _rowsum8(err * err)
        part = sq[:, 0:LANES]
        for j in range(1, d // LANES):
            part = part + sq[:, j * LANES:(j + 1) * LANES]
        ls_ref[...] += part * (0.5 / d)

        @pl.when(pl.program_id(0) == t // tm - 1)
        def _():
            _fold8(dg_ref)
            _fold8(db_ref)
            ls_ref[0:1, 0:1] = jnp.sum(jnp.sum(ls_ref[...], axis=0, keepdims=True), axis=1, keepdims=True)

    rows = pl.BlockSpec((tm, d), lambda i: (i, 0))
    acc = pl.BlockSpec((SUBLANES, d), lambda i: (0, 0))
    return pl.pallas_call(
        body, name="loss_ln_bwd", grid=(t // tm,),
        in_specs=[rows, rows, pl.BlockSpec((tm, 1), lambda i: (i, 0)), pl.BlockSpec((1, d), lambda i: (0, 0)), rows],
        out_specs=[rows, acc, acc, pl.BlockSpec((SUBLANES, LANES), lambda i: (0, 0)), rows],
        out_shape=[jax.ShapeDtypeStruct((t, d), F32), jax.ShapeDtypeStruct((SUBLANES, d), F32),
                   jax.ShapeDtypeStruct((SUBLANES, d), F32), jax.ShapeDtypeStruct((SUBLANES, LANES), F32),
                   jax.ShapeDtypeStruct((t, d), BF16)],
        compiler_params=_params("arbitrary"),
    )(x2, xh, rstd, gamma, target)


def _ln_bwd(dr_next, dlin, xh, rstd, gamma, tm):
    t, d = xh.shape

    def body(dn_ref, dl_ref, xh_ref, rs_ref, g_ref, dr_ref, dg_ref, db_ref, drb_ref):
        @pl.when(pl.program_id(0) == 0)
        def _():
            dg_ref[...] = jnp.zeros_like(dg_ref)
            db_ref[...] = jnp.zeros_like(db_ref)

        dy = ALPHA * dn_ref[...] + dl_ref[...]
        xh = xh_ref[...]
        dr = _ln_bwd_math(dy, xh, rs_ref[...], g_ref[...])
        dr_ref[...] = dr
        drb_ref[...] = dr.astype(BF16)
        dg_ref[...] += _rowsum8(dy * xh)
        db_ref[...] += _rowsum8(dy)

        @pl.when(pl.program_id(0) == t // tm - 1)
        def _():
            _fold8(dg_ref)
            _fold8(db_ref)

    rows = pl.BlockSpec((tm, d), lambda i: (i, 0))
    acc = pl.BlockSpec((SUBLANES, d), lambda i: (0, 0))
    return pl.pallas_call(
        body, name="ln_bwd", grid=(t // tm,),
        in_specs=[rows, rows, rows, pl.BlockSpec((tm, 1), lambda i: (i, 0)), pl.BlockSpec((1, d), lambda i: (0, 0))],
        out_specs=[rows, acc, acc, rows],
        out_shape=[jax.ShapeDtypeStruct((t, d), F32), jax.ShapeDtypeStruct((SUBLANES, d), F32),
                   jax.ShapeDtypeStruct((SUBLANES, d), F32), jax.ShapeDtypeStruct((t, d), BF16)],
        compiler_params=_params("arbitrary"),
    )(dr_next, dlin, xh, rstd, gamma)


def _shift_rows(x, halo, shift, row):
    out = pltpu.roll(x, shift, 0)
    n = halo.shape[0]
    for r in range(shift):
        out = jnp.where(row == r, halo[n - shift + r:n - shift + r + 1, :], out)
    return out


def _unshift_rows(x, halo, shift, row, tm):
    out = pltpu.roll(x, tm - shift, 0)
    for r in range(shift):
        out = jnp.where(row == tm - shift + r, halo[r:r + 1, :], out)
    return out


def _conv_pre(ug_ref, halo_ref, wc_ref, bc_ref, first, tm):
    ug = ug_ref[...].astype(F32)
    halo = jnp.where(first, 0.0, halo_ref[...].astype(F32))
    row = lax.broadcasted_iota(jnp.int32, ug.shape, 0)
    wc = wc_ref[...]
    um1 = _shift_rows(ug, halo, 1, row)
    um2 = _shift_rows(ug, halo, 2, row)
    c = bc_ref[...] + wc[2:3, :] * ug + wc[1:2, :] * um1 + wc[0:1, :] * um2
    return c, ug, um1, um2


INV_SQRT2 = 1.0 / math.sqrt(2.0)
INV_SQRT2PI = 1.0 / math.sqrt(2.0 * math.pi)


def _conv_glu_fwd(u, wc, bc, seq, tm):
    t, f2 = u.shape
    f = f2 // 2
    per_seq = seq // tm
    hb = tm // HALO

    def body(ug_ref, halo_ref, uv_ref, wc_ref, bc_ref, a_ref):
        first = (pl.program_id(0) % per_seq) == 0
        c, _, _, _ = _conv_pre(ug_ref, halo_ref, wc_ref, bc_ref, first, tm)
        gelu = 0.5 * c * (1.0 + lax.erf(c * INV_SQRT2))
        a_ref[...] = (gelu * uv_ref[...].astype(F32)).astype(BF16)

    return pl.pallas_call(
        body, name="conv_glu_fwd", grid=(t // tm,),
        in_specs=[pl.BlockSpec((tm, f), lambda i: (i, 0)),
                  pl.BlockSpec((HALO, f), lambda i: (jnp.maximum(i * hb - 1, 0), 0)),
                  pl.BlockSpec((tm, f), lambda i: (i, 1)),
                  pl.BlockSpec((3, f), lambda i: (0, 0)), pl.BlockSpec((1, f), lambda i: (0, 0))],
        out_specs=pl.BlockSpec((tm, f), lambda i: (i, 0)),
        out_shape=jax.ShapeDtypeStruct((t, f), BF16),
        compiler_params=_params("parallel"),
    )(u, u, u, wc, bc)


def _conv_glu_bwd1(u, da, wc, bc, seq, tm):
    t, f2 = u.shape
    f = f2 // 2
    per_seq = seq // tm
    hb = tm // HALO

    def body(ug_ref, halo_ref, uv_ref, da_ref, wc_ref, bc_ref, dc_ref, duv_ref):
        first = (pl.program_id(0) % per_seq) == 0
        c, _, _, _ = _conv_pre(ug_ref, halo_ref, wc_ref, bc_ref, first, tm)
        cdf = 0.5 * (1.0 + lax.erf(c * INV_SQRT2))
        pdf = jnp.exp(-0.5 * c * c) * INV_SQRT2PI
        da = da_ref[...].astype(F32)
        duv_ref[...] = (da * (c * cdf)).astype(BF16)
        dc_ref[...] = (da * uv_ref[...].astype(F32) * (cdf + c * pdf)).astype(BF16)

    rows = pl.BlockSpec((tm, f), lambda i: (i, 0))
    return pl.pallas_call(
        body, name="conv_glu_bwd1", grid=(t // tm,),
        in_specs=[rows, pl.BlockSpec((HALO, f), lambda i: (jnp.maximum(i * hb - 1, 0), 0)),
                  pl.BlockSpec((tm, f), lambda i: (i, 1)), rows,
                  pl.BlockSpec((3, f), lambda i: (0, 0)), pl.BlockSpec((1, f), lambda i: (0, 0))],
        out_specs=[rows, rows],
        out_shape=[jax.ShapeDtypeStruct((t, f), BF16), jax.ShapeDtypeStruct((t, f), BF16)],
        compiler_params=_params("parallel"),
    )(u, u, u, da, wc, bc)


def _conv_glu_bwd2(u, dc, wc, seq, tm):
    t, f2 = u.shape
    f = f2 // 2
    per_seq = seq // tm
    hb = tm // HALO
    nblk = t // HALO

    def body(ug_ref, halo_ref, dc_ref, nxt_ref, wc_ref, dug_ref, w0_ref, w1_ref, w2_ref, b_ref):
        i = pl.program_id(0)

        @pl.when(i == 0)
        def _():
            for r in (w0_ref, w1_ref, w2_ref, b_ref):
                r[...] = jnp.zeros_like(r)

        first = (i % per_seq) == 0
        last = (i % per_seq) == per_seq - 1
        ug = ug_ref[...].astype(F32)
        halo = jnp.where(first, 0.0, halo_ref[...].astype(F32))
        nxt = jnp.where(last, 0.0, nxt_ref[...].astype(F32))
        row = lax.broadcasted_iota(jnp.int32, ug.shape, 0)
        dc = dc_ref[...].astype(F32)
        wc = wc_ref[...]
        dp1 = _unshift_rows(dc, nxt, 1, row, tm)
        dp2 = _unshift_rows(dc, nxt, 2, row, tm)
        dug_ref[...] = (wc[2:3, :] * dc + wc[1:2, :] * dp1 + wc[0:1, :] * dp2).astype(BF16)
        w2_ref[...] += _rowsum8(dc * ug)
        w1_ref[...] += _rowsum8(dc * _shift_rows(ug, halo, 1, row))
        w0_ref[...] += _rowsum8(dc * _shift_rows(ug, halo, 2, row))
        b_ref[...] += _rowsum8(dc)

        @pl.when(i == t // tm - 1)
        def _():
            for r in (w0_ref, w1_ref, w2_ref, b_ref):
                _fold8(r)

    rows = pl.BlockSpec((tm, f), lambda i: (i, 0))
    acc = pl.BlockSpec((SUBLANES, f), lambda i: (0, 0))
    return pl.pallas_call(
        body, name="conv_glu_bwd2", grid=(t // tm,),
        in_specs=[rows, pl.BlockSpec((HALO, f), lambda i: (jnp.maximum(i * hb - 1, 0), 0)),
                  rows, pl.BlockSpec((HALO, f), lambda i: (jnp.minimum((i + 1) * hb, nblk - 1), 0)),
                  pl.BlockSpec((3, f), lambda i: (0, 0))],
        out_specs=[rows, acc, acc, acc, acc],
        out_shape=[jax.ShapeDtypeStruct((t, f), BF16)] + [jax.ShapeDtypeStruct((SUBLANES, f), F32)] * 4,
        compiler_params=_params("arbitrary"),
    )(u, u, dc, dc, wc)


def _colsum(x, tm, name):
    t, n = x.shape

    def body(x_ref, o_ref):
        @pl.when(pl.program_id(0) == 0)
        def _():
            o_ref[...] = jnp.zeros_like(o_ref)

        o_ref[...] += _rowsum8(x_ref[...].astype(F32))

        @pl.when(pl.program_id(0) == t // tm - 1)
        def _():
            _fold8(o_ref)

    return pl.pallas_call(
        body, name=name, grid=(t // tm,),
        in_specs=[pl.BlockSpec((tm, n), lambda i: (i, 0))],
        out_specs=pl.BlockSpec((SUBLANES, n), lambda i: (0, 0)),
        out_shape=jax.ShapeDtypeStruct((SUBLANES, n), F32),
        compiler_params=_params("arbitrary"),
    )(x)


def _adamw(w, gparts, m, v, name):
    p, r, c = gparts.shape
    tr = r
    for cand in (512, 256, 128, 64, 32, 16, 8):
        if cand * p <= 1024 and r % cand == 0 and r > cand:
            tr = cand
            break
    c1 = 1.0 - ADAM_B1 ** ADAM_STEP
    c2 = 1.0 - ADAM_B2 ** ADAM_STEP

    def body(w_ref, g_ref, m_ref, v_ref, go_ref, d_ref, mo_ref, vo_ref):
        g = g_ref[0].astype(F32)
        for i in range(1, p):
            g = g + g_ref[i].astype(F32)
        mn = ADAM_B1 * m_ref[...] + (1.0 - ADAM_B1) * g
        vn = ADAM_B2 * v_ref[...] + (1.0 - ADAM_B2) * (g * g)
        go_ref[...] = g
        mo_ref[...] = mn
        vo_ref[...] = vn
        d_ref[...] = -ADAM_LR * ((mn / c1) / (jnp.sqrt(vn / c2) + ADAM_EPS) + ADAM_WD * w_ref[...])

    blk = pl.BlockSpec((tr, c), lambda i: (i, 0))
    return pl.pallas_call(
        body, name=name, grid=(r // tr,),
        in_specs=[blk, pl.BlockSpec((p, tr, c), lambda i: (0, i, 0)), blk, blk],
        out_specs=[blk] * 4,
        out_shape=[jax.ShapeDtypeStruct((r, c), F32)] * 4,
        compiler_params=_params("parallel"),
    )(w, gparts, m, v)


MESH = pl.DeviceIdType.MESH
ANY = pl.BlockSpec(memory_space=pl.ANY)


def _all_gather(xs, name):
    n = len(xs)

    def body(*refs):
        x_refs, out_refs = refs[:n], refs[n:2 * n]
        send_sems, recv_sems, local_sems = refs[2 * n:]
        x, y, c = lax.axis_index("x"), lax.axis_index("y"), lax.axis_index("c")
        me, sibling = (x, y, c), (x, y, 1 - c)
        chips = [(1 - x, y), (x, 1 - y), (1 - x, 1 - y)]

        def slot(a, px, py, pc):
            return out_refs[a].at[4 * px + 2 * py + pc]

        def copy(a, k, block, to, src=None):
            return pltpu.make_async_remote_copy(
                src_ref=slot(a, *block) if src is None else src, dst_ref=slot(a, *block),
                send_sem=send_sems.at[k * n + a], recv_sem=recv_sems.at[k * n + a], device_id=to, device_id_type=MESH)

        arrays = range(n)
        mine = [pltpu.make_async_copy(x_refs[a], slot(a, *me), local_sems.at[a]) for a in arrays]
        first = [copy(a, 0, me, sibling, src=x_refs[a]) for a in arrays]
        first += [copy(a, 1 + j, me, (*chip, c), src=x_refs[a]) for j, chip in enumerate(chips) for a in arrays]
        for cp in mine + first:
            cp.start()
        passed = []
        for j, chip in enumerate(chips):
            for a in arrays:
                copy(a, 1 + j, (*chip, c), me).wait_recv()
                passed.append(copy(a, 4 + j, (*chip, c), sibling))
                passed[-1].start()
        for a in arrays:
            copy(a, 0, sibling, me).wait_recv()
        for j, chip in enumerate(chips):
            for a in arrays:
                copy(a, 4 + j, (*chip, 1 - c), me).wait_recv()
        for cp in first + passed:
            cp.wait_send()
        for cp in mine:
            cp.wait()

    return pl.pallas_call(
        body, name=name,
        out_shape=[jax.ShapeDtypeStruct((N_DEV,) + x.shape, x.dtype) for x in xs],
        in_specs=[ANY] * n, out_specs=[ANY] * n,
        scratch_shapes=[pltpu.SemaphoreType.DMA((7 * n,)), pltpu.SemaphoreType.DMA((7 * n,)),
                        pltpu.SemaphoreType.DMA((n,))],
    )(*xs)


def _peer_copies(kind, src_refs, dst_refs, send_sems, recv_sems, local_sems):
    n = len(src_refs)
    x, y, c = lax.axis_index("x"), lax.axis_index("y"), lax.axis_index("c")
    mine = 4 * x + 2 * y + c

    def src(a, idx):
        return src_refs[a] if kind == "spread" else src_refs[a].at[idx]

    copies = [pltpu.make_async_copy(src(a, mine), dst_refs[a].at[mine], local_sems.at[a]) for a in range(n)]
    for k in range(1, N_DEV):
        px = 1 - x if k & 4 else x
        py = 1 - y if k & 2 else y
        pc = 1 - c if k & 1 else c
        for a in range(n):
            copies.append(pltpu.make_async_remote_copy(
                src_ref=src(a, 4 * px + 2 * py + pc), dst_ref=dst_refs[a].at[mine],
                send_sem=send_sems.at[(k - 1) * n + a], recv_sem=recv_sems.at[(k - 1) * n + a],
                device_id=(px, py, pc), device_id_type=MESH))
    return copies


def _peer_shapes(kind, arrays):
    n = len(arrays)
    outs = [jax.ShapeDtypeStruct(((N_DEV,) + a.shape) if kind == "spread" else a.shape, a.dtype) for a in arrays]
    sems = [pltpu.SemaphoreType.DMA((7 * n,)), pltpu.SemaphoreType.DMA((7 * n,)), pltpu.SemaphoreType.DMA((n,))]
    return outs, sems


def _exchange(gs, name):
    n = len(gs)

    def body(*refs):
        copies = _peer_copies("exchange", refs[:n], refs[n:2 * n], *refs[2 * n:])
        for cp in copies:
            cp.start()
        for cp in copies:
            cp.wait()

    outs, sems = _peer_shapes("exchange", gs)
    return pl.pallas_call(body, name=name, out_shape=outs, in_specs=[ANY] * n, out_specs=[ANY] * n,
                          scratch_shapes=sems)(*gs)


def _tile(n, pref, unit=LANES):
    if n <= pref:
        return n
    best = None
    for cand in range(unit, pref + 1, unit):
        if n % cand == 0:
            best = cand
    assert best is not None, (n, pref, unit)
    return best


LATE_KEYS = dict(w_proj_sb="wp_sb", w_proj_fox="wp_fx", w_out="w_out", w_up="w_up", w_conv="w_conv", w_down="w_down")


def _layer_step(x, target, w, attn_blk, late=None):
    b, s, d = x.shape
    t = b * s
    w = dict(w)
    e = w["w_qkv"].shape[1] // 6
    h = e // HEAD_DIM
    f = w["b_conv"].shape[1]
    x2 = x.reshape(t, d)
    tg = target.reshape(t, d)
    pp = 2 if (e // LANES) % 2 == 0 else 1
    fox_blk = min(2 * attn_blk, s)
    tm = _tile(t, 512, HALO)
    tmc = _tile(s, 256, HALO)
    tkt = _tile(t, 2048, HALO)
    td = _tile(d, 1024)
    tf = _tile(f, 1408)
    t2f = _tile(2 * f, 1408)
    tqkv = _tile(6 * e, 1024)
    tg2 = _tile(2 * d, 1024)
    xb = x2.astype(BF16)

    qkv = _matmul(xb, w["w_qkv"], NN, tm=tm, tn=tqkv, tk=d, out_dtype=BF16, name="in_qkv", bias=w["b_qkv"], b_outer=True)
    gate = _matmul(xb, jnp.concatenate([w["w_g"], w["w_f"]], axis=1), NN, tm=tm, tn=2 * d + LANES, tk=d, out_dtype=F32,
                   name="in_gate", bias=jnp.concatenate([w["b_g"], w["b_f"]], axis=1), b_outer=True)
    nr = s // LANES
    f2 = gate[:, 2 * d:2 * d + h].reshape(b, s, h).transpose(0, 2, 1).reshape(b * h * nr, LANES)
    ct = _scan_rows(f2, nr, "fwd").reshape(b * h, 1, s)
    o_sb, tot, first = _sb_fwd(qkv, b, s, e, attn_blk, pp)
    if late is None:
        o_fx, lse = _fox_fwd(qkv, ct, b, s, e, fox_blk, 1)
    else:
        o_fx, lse, *gathered = _fox_fwd(qkv, ct, b, s, e, fox_blk, 1, comm=("spread", late[1]))
        for name, g in zip(late[0], gathered):
            w[LATE_KEYS[name]] = _join(g, name)
    merged, y_sb, y_fx = _proj_gate_fwd(o_sb, o_fx, w["wp_sb"], w["wp_fx"], gate, tm)
    x1, xh1, rs1, x1b = _mm_res_ln(merged, w["w_out"], x2, w["ln1_g"], w["ln1_b"], tm, "out_ln1")
    u = _matmul(x1b, w["w_up"], NN, tm=tm, tn=t2f, tk=d, out_dtype=BF16, name="ffn_up", b_outer=True)
    act = _conv_glu_fwd(u, w["w_conv"], w["b_conv"], s, tmc)
    xo, xh2, rs2, _ = _mm_res_ln(act, w["w_down"], x1, w["ln2_g"], w["ln2_b"], tm, "down_ln2")

    gr = {}
    dr2, dg2, db2, ls, dr2b = _loss_ln_bwd(xo, xh2, rs2, w["ln2_g"], tg, tm)
    gr["ln2_g"], gr["ln2_b"] = dg2[0:1], db2[0:1]
    da = _matmul(dr2b, w["w_down"], NT, tm=tm, tn=tf, tk=d, out_dtype=BF16, name="d_act", b_outer=True)
    gr["w_down"] = _matmul(act, dr2b, TN, tm=tf, tn=td, tk=tkt, out_dtype=F32, name="dw_down")
    dc, du_v = _conv_glu_bwd1(u, da, w["w_conv"], w["b_conv"], s, tmc)
    du_g, gw0, gw1, gw2, gbc = _conv_glu_bwd2(u, dc, w["w_conv"], s, tmc)
    gr["w_conv"] = jnp.concatenate([gw0[0:1], gw1[0:1], gw2[0:1]], axis=0)
    gr["b_conv"] = gbc[0:1]
    du = jnp.concatenate([du_g, du_v], axis=1)
    dlin1 = _matmul(du, w["w_up"], NT, tm=tm, tn=td, tk=2 * f, out_dtype=F32, name="d_x1")
    gr["w_up"] = _matmul(x1b, du, TN, tm=td, tn=t2f, tk=tkt, out_dtype=F32, name="dw_up")
    dr1, dg1, db1, dr1b = _ln_bwd(dr2, dlin1, xh1, rs1, w["ln1_g"], tm)
    gr["ln1_g"], gr["ln1_b"] = dg1[0:1], db1[0:1]
    dmg = _matmul(dr1b, w["w_out"], NT, tm=tm, tn=td, tk=d, out_dtype=F32, name="d_merged")
    gr["w_out"] = _matmul(merged, dr1b, TN, tm=td, tn=td, tk=tkt, out_dtype=F32, name="dw_out")
    dy_sb, dy_fx, dgate = _gate_bwd(dmg, y_sb, y_fx, gate, tm)
    do_sb = _matmul(dy_sb, w["wp_sb"], NT, tm=tm, tn=e, tk=d, out_dtype=BF16, name="d_o_sb")
    do_fx = _matmul(dy_fx, w["wp_fx"], NT, tm=tm, tn=e, tk=d, out_dtype=BF16, name="d_o_fx")
    gr["wp_sb"] = _matmul(o_sb, dy_sb, TN, tm=e, tn=td, tk=tkt, out_dtype=F32, name="dwp_sb")
    gr["wp_fx"] = _matmul(o_fx, dy_fx, TN, tm=e, tn=td, tk=tkt, out_dtype=F32, name="dwp_fx")
    dq_sb, dk_sb, dv_sb = _sb_bwd(qkv, do_sb, tot, first, b, s, e, attn_blk, pp)
    landed = None
    if late is None:
        dq_fx, dk_fx, dv_fx, dct = _fox_bwd(qkv, ct, do_fx, o_fx, lse, b, s, e, fox_blk, 1)
    else:
        blocks = [_cut(gr[LATE_KEYS[n]], n).astype(BF16 if n in MATMUL_OPERANDS else F32) for n in late[0]]
        dq_fx, dk_fx, dv_fx, dct, *got = _fox_bwd(qkv, ct, do_fx, o_fx, lse, b, s, e, fox_blk, 1,
                                                   comm=("exchange", blocks))
        landed = dict(zip(late[0], got))
    dqkv = jnp.concatenate([dq_sb, dk_sb, dv_sb, dq_fx, dk_fx, dv_fx], axis=1)
    df2 = _scan_rows(f2, nr, "bwd", dct.reshape(b * h * nr, LANES))
    df = jnp.pad(df2.reshape(b, h, s).transpose(0, 2, 1).reshape(t, h), ((0, 0), (0, LANES - h))).astype(BF16)
    gr["w_qkv"] = _matmul(xb, dqkv, TN, tm=td, tn=tqkv, tk=tkt, out_dtype=F32, name="dw_qkv")
    gr["w_g"] = _matmul(xb, dgate, TN, tm=td, tn=tg2, tk=tkt, out_dtype=F32, name="dw_gate")
    gr["w_f"] = _matmul(xb, df, TN, tm=td, tn=LANES, tk=tkt, out_dtype=F32, name="dw_forget")
    gr["b_qkv"] = _colsum(dqkv, tm, "db_qkv")[0:1]
    gr["b_g"] = _colsum(dgate, tm, "db_gate")[0:1]
    gr["b_f"] = _colsum(df, tm, "db_forget")[0:1]
    comm = None
    if late is not None:
        comm = ("exchange", [_cut(_w_in_layout(gr["w_qkv"], gr["w_f"], gr["w_g"], h), "w_in").astype(BF16)])
    dx, *got = _input_grad(jnp.concatenate([dqkv, dgate, df], axis=1),
                           jnp.concatenate([w["w_qkv"], w["w_g"], w["w_f"]], axis=1), dr1, tm, comm)
    if late is not None:
        landed["w_in"] = got[0]
    return ls[0:1, 0:1], dx.reshape(b, s, d), gr, landed


SHARDED = ("w_in", "w_proj_sb", "w_proj_fox", "w_out", "w_up", "w_conv", "w_down")
ROW_SHARDED = ("w_out", "w_down")
REPLICATED = ("b_in", "ln1_g", "ln1_b", "b_conv", "ln2_g", "ln2_b")
WEIGHTS = ("w_in", "b_in", "w_proj_sb", "w_proj_fox", "w_out", "ln1_g", "ln1_b", "w_up", "w_conv", "b_conv",
           "w_down", "ln2_g", "ln2_b")
MATMUL_OPERANDS = ("w_in", "w_proj_sb", "w_proj_fox", "w_out", "w_up", "w_down")


def _w_in_layout(g_qkv, g_f, g_g, h):
    return jnp.concatenate([g_qkv, g_f[:, :h], g_g], axis=1)


def _cut(full, name):
    r, c = full.shape
    if name in ROW_SHARDED:
        return full.reshape(N_DEV, r // N_DEV, c)
    cs = c // N_DEV
    return jnp.stack([full[:, j * cs:(j + 1) * cs] for j in range(N_DEV)], axis=0)


def _join(blocks, name):
    p, r, c = blocks.shape
    if name in ROW_SHARDED:
        return blocks.reshape(p * r, c)
    return jnp.concatenate([blocks[j] for j in range(p)], axis=1)


def kernel(x, w_in, b_in, w_proj_sb, w_proj_fox, w_out, ln1_g, ln1_b, w_up, w_conv, b_conv, w_down, ln2_g, ln2_b, loss_target, m_w_in, m_b_in, m_w_proj_sb, m_w_proj_fox, m_w_out, m_ln1_g, m_ln1_b, m_w_up, m_w_conv, m_b_conv, m_w_down, m_ln2_g, m_ln2_b, v_w_in, v_b_in, v_w_proj_sb, v_w_proj_fox, v_w_out, v_ln1_g, v_ln1_b, v_w_up, v_w_conv, v_b_conv, v_w_down, v_ln2_g, v_ln2_b):
    wts = dict(w_in=w_in, b_in=b_in, w_proj_sb=w_proj_sb, w_proj_fox=w_proj_fox, w_out=w_out, ln1_g=ln1_g, ln1_b=ln1_b,
               w_up=w_up, w_conv=w_conv, b_conv=b_conv, w_down=w_down, ln2_g=ln2_g, ln2_b=ln2_b)
    mom = dict(w_in=m_w_in, b_in=m_b_in, w_proj_sb=m_w_proj_sb, w_proj_fox=m_w_proj_fox, w_out=m_w_out, ln1_g=m_ln1_g,
               ln1_b=m_ln1_b, w_up=m_w_up, w_conv=m_w_conv, b_conv=m_b_conv, w_down=m_w_down, ln2_g=m_ln2_g, ln2_b=m_ln2_b)
    var = dict(w_in=v_w_in, b_in=v_b_in, w_proj_sb=v_w_proj_sb, w_proj_fox=v_w_proj_fox, w_out=v_w_out, ln1_g=v_ln1_g,
               ln1_b=v_ln1_b, w_up=v_w_up, w_conv=v_w_conv, b_conv=v_b_conv, w_down=v_w_down, ln2_g=v_ln2_g, ln2_b=v_ln2_b)
    shard = {n: wts[n].reshape(wts[n].shape[-2:]) for n in WEIGHTS}

    w_in_full = _join(_all_gather([shard["w_in"].astype(BF16)], "gather_w_in")[0], "w_in")
    late_names = [n for n in SHARDED if n != "w_in"]
    late = (late_names, [shard[n].astype(BF16) if n in MATMUL_OPERANDS else shard[n] for n in late_names])
    e = shard["w_proj_sb"].shape[0]
    h = e // HEAD_DIM
    nq = 6 * e

    def cut_in(a, pad):
        fcols = a[:, nq:nq + h]
        if pad:
            fcols = jnp.pad(fcols, ((0, 0), (0, LANES - h)))
        return a[:, :nq], a[:, nq + h:], fcols

    w_qkv, w_g, w_f = cut_in(w_in_full, True)
    b_qkv, b_g, b_f = cut_in(shard["b_in"], True)
    w = dict(w_qkv=w_qkv, w_g=w_g, w_f=w_f, b_qkv=b_qkv, b_g=b_g, b_f=b_f, b_conv=shard["b_conv"],
             ln1_g=shard["ln1_g"], ln1_b=shard["ln1_b"], ln2_g=shard["ln2_g"], ln2_b=shard["ln2_b"])

    loss_local, grad_x, gr, gsum = _layer_step(x, loss_target, w, min(256, x.shape[1]), late)
    loss = lax.psum(loss_local[0, 0], ("x", "y", "c"))

    local = dict(b_in=_w_in_layout(gr["b_qkv"], gr["b_f"], gr["b_g"], h), ln1_g=gr["ln1_g"], ln1_b=gr["ln1_b"],
                 b_conv=gr["b_conv"], ln2_g=gr["ln2_g"], ln2_b=gr["ln2_b"])
    parts = _all_gather([jnp.concatenate([local[n] for n in REPLICATED], axis=1)], "gather_small_grads")[0]
    off = 0
    for n in REPLICATED:
        gsum[n] = parts[:, :, off:off + shard[n].size]
        off += shard[n].size

    grads, deltas, new_m, new_v = [], [], [], []
    for n in WEIGHTS:
        shp = wts[n].shape
        g, dl, mn, vn = _adamw(shard[n], gsum[n], mom[n].reshape(shard[n].shape), var[n].reshape(shard[n].shape),
                               "adamw_" + n)
        grads.append(g.reshape(shp))
        deltas.append(dl.reshape(shp))
        new_m.append(mn.reshape(shp))
        new_v.append(vn.reshape(shp))
    return (loss, grad_x, *grads, *deltas, *new_m, *new_v)
```

```python
import functools
import math

import jax
import jax.numpy as jnp
from jax import lax
from jax.experimental import pallas as pl
from jax.experimental.pallas import tpu as pltpu

F32 = jnp.float32
BF16 = jnp.bfloat16

HEAD_DIM = 64
LN_EPS = 1e-5
DEPTH = 1
ALPHA = (2.0 * DEPTH) ** 0.25
ADAM_LR, ADAM_B1, ADAM_B2, ADAM_EPS, ADAM_WD, ADAM_STEP = 0.001, 0.9, 0.999, 1e-08, 0.01, 10
N_DEV = 8
LANES = 128
SUBLANES = 8
HALO = 16
VMEM_LIMIT = 56 * 1024 * 1024

NN = ((1,), (0,))
NT = ((1,), (1,))
TN = ((0,), (0,))


def _dot(a, b, dims):
    return lax.dot_general(a, b, (dims, ((), ())), preferred_element_type=F32)


def _params(*sem):
    return pltpu.CompilerParams(dimension_semantics=sem, vmem_limit_bytes=VMEM_LIMIT)


def _iotas(blk):
    row = lax.broadcasted_iota(jnp.int32, (blk, blk), 0)
    col = lax.broadcasted_iota(jnp.int32, (blk, blk), 1)
    return row, col


def _sb_terms(z):
    e = jnp.exp(-jnp.abs(z))
    lb = jnp.minimum(z, 0.0) - jnp.log(1.0 + e)
    return lb, lb - z, e


def _pair_specs(s, blk, e, pp, branch):
    w = pp * LANES
    nq = s // blk
    ng = e // w
    base = 3 * branch * ng
    q_in = pl.BlockSpec((blk, w), lambda b, g, i: (b * nq + i, base + g))
    k_in = pl.BlockSpec((s, w), lambda b, g, i: (b, base + ng + g))
    v_in = pl.BlockSpec((s, w), lambda b, g, i: (b, base + 2 * ng + g))
    q_out = pl.BlockSpec((blk, w), lambda b, g, i: (b * nq + i, g))
    kv_out = pl.BlockSpec((s, w), lambda b, g, i: (b, g))
    rows = pl.BlockSpec((2 * pp, blk, 1), lambda b, g, i: (b * ng + g, i, 0))
    krow = pl.BlockSpec((2 * pp, 1, s), lambda b, g, i: (b * ng + g, 0, 0))
    return q_in, k_in, v_in, q_out, kv_out, rows, krow


def _to_column(r):
    row, col = _iotas(r.shape[1])
    return jnp.sum(jnp.where(row == col, r, 0.0), axis=1, keepdims=True)


def _half_masks(x):
    low = lax.broadcasted_iota(jnp.int32, x.shape, 1) < HEAD_DIM
    zero = jnp.zeros_like(x)
    return jnp.concatenate([jnp.where(low, x, zero), jnp.where(low, zero, x)], axis=0)


def _tri_sums(xs, tri):
    hi = [x.astype(BF16) for x in xs]
    lo = [(x - h.astype(F32)).astype(BF16) for x, h in zip(xs, hi)]
    n = len(xs)
    blk = xs[0].shape[0]
    r = _dot(jnp.concatenate(hi + lo, axis=0), tri, NN)
    return [r[i * blk:(i + 1) * blk] + r[(n + i) * blk:(n + i + 1) * blk] for i in range(n)]


def _sb_fwd(qkv, b, s, e, blk, pp):
    scale = HEAD_DIM ** -0.5
    nh = 2 * pp
    t = b * s

    def body(q_ref, k_ref, v_ref, o_ref, tot_ref, first_ref):
        qi = pl.program_id(2)
        qm = [_half_masks((q_ref[:, p * LANES:(p + 1) * LANES] * scale).astype(BF16)) for p in range(pp)]
        row, col = _iotas(blk)
        strict = col < row
        after = (row > col).astype(BF16)

        def block(j, carry, diag):
            off = pl.multiple_of(j * blk, blk)
            o_acc, run = carry
            zz = [_dot(qm[p], k_ref[pl.ds(off, blk), p * LANES:(p + 1) * LANES], NT) for p in range(pp)]
            z = [zz[h // 2][(h % 2) * blk:(h % 2 + 1) * blk] for h in range(nh)]
            terms = [_sb_terms(z[h]) for h in range(nh)]
            lom = [jnp.where(strict, terms[h][1], 0.0) if diag else terms[h][1] for h in range(nh)]
            sfx = _tri_sums(lom, after)
            a = [jnp.exp(terms[h][0] + sfx[h] + run[h]) for h in range(nh)]
            if diag:
                a = [jnp.where(strict, a[h], 0.0) for h in range(nh)]
            ab = [a[h].astype(BF16) for h in range(nh)]
            o_new = tuple(
                o_acc[p] + _dot(jnp.concatenate([ab[2 * p], ab[2 * p + 1]], axis=1),
                                _half_masks(v_ref[pl.ds(off, blk), p * LANES:(p + 1) * LANES]), NN)
                for p in range(pp))
            return o_new, tuple(run[h] + sfx[h][:, 0:1] + lom[h][:, 0:1] for h in range(nh))

        def alive(run):
            m = run[0]
            for h in range(1, nh):
                m = jnp.maximum(m, run[h])
            return jnp.max(m) > DEAD

        o_acc, run = block(qi, ((jnp.zeros((blk, LANES), F32),) * pp, (jnp.zeros((blk, 1), F32),) * nh), True)

        def step(c):
            j, _, o_acc, run = c
            o_acc, run = block(j, (o_acc, run), False)
            return j - 1, alive(run), o_acc, run

        j, _, o_acc, run = lax.while_loop(lambda c: jnp.logical_and(c[0] >= 0, c[1]), step,
                                          (qi - 1, alive(run), o_acc, run))
        for p in range(pp):
            o_ref[:, p * LANES:(p + 1) * LANES] = o_acc[p].astype(o_ref.dtype)
        for h in range(nh):
            tot_ref[h] = run[h]
            first_ref[h] = jnp.zeros((blk, 1), F32) + (j + 1).astype(F32)

    q_in, k_in, v_in, q_out, _, rows, _ = _pair_specs(s, blk, e, pp, 0)
    return pl.pallas_call(
        body, name="sb_fwd", grid=(b, e // (pp * LANES), s // blk),
        in_specs=[q_in, k_in, v_in], out_specs=[q_out, rows, rows],
        out_shape=[jax.ShapeDtypeStruct((t, e), BF16)] + [jax.ShapeDtypeStruct((b * e // HEAD_DIM, s, 1), F32)] * 2,
        compiler_params=_params("parallel", "parallel", "arbitrary"),
    )(qkv, qkv, qkv)


def _sb_bwd(qkv, do, tot, first, b, s, e, blk, pp):
    scale = HEAD_DIM ** -0.5
    nh = 2 * pp
    t = b * s
    nq = s // blk

    def body(q_ref, k_ref, v_ref, do_ref, tot_ref, first_ref, dq_ref, dk_ref, dv_ref, dk_acc, dv_acc):
        qi = pl.program_id(2)

        @pl.when(qi == 0)
        def _():
            dk_acc[...] = jnp.zeros_like(dk_acc)
            dv_acc[...] = jnp.zeros_like(dv_acc)

        qm = [_half_masks((q_ref[:, p * LANES:(p + 1) * LANES] * scale).astype(BF16)) for p in range(pp)]
        dom = [_half_masks(do_ref[:, p * LANES:(p + 1) * LANES].astype(BF16)) for p in range(pp)]
        tot_t = [tot_ref[h] for h in range(nh)]
        row, col = _iotas(blk)
        strict = col < row
        upto = (row <= col).astype(BF16)
        before = (row < col).astype(BF16)

        def block(j, carry, diag):
            off = pl.multiple_of(j * blk, blk)
            dq_acc, cl, cg = carry
            kp = [k_ref[pl.ds(off, blk), p * LANES:(p + 1) * LANES] for p in range(pp)]
            vp = [v_ref[pl.ds(off, blk), p * LANES:(p + 1) * LANES] for p in range(pp)]
            zz = [_dot(qm[p], kp[p], NT) for p in range(pp)]
            dd = [_dot(dom[p], vp[p], NT) for p in range(pp)]
            z = [zz[h // 2][(h % 2) * blk:(h % 2 + 1) * blk] for h in range(nh)]
            da = [dd[h // 2][(h % 2) * blk:(h % 2 + 1) * blk] for h in range(nh)]
            terms = [_sb_terms(z[h]) for h in range(nh)]
            lom = [jnp.where(strict, terms[h][1], 0.0) if diag else terms[h][1] for h in range(nh)]
            pre = _tri_sums(lom, upto)
            a = [jnp.exp(terms[h][0] + (tot_t[h] - cl[h] - pre[h])) for h in range(nh)]
            if diag:
                a = [jnp.where(strict, a[h], 0.0) for h in range(nh)]
            g = [a[h] * da[h] for h in range(nh)]
            pw = _tri_sums(g, before)
            dzb = []
            for h in range(nh):
                ex = terms[h][2]
                r = 1.0 / (1.0 + ex)
                er = ex * r
                pos = z[h] >= 0.0
                dz = g[h] * jnp.where(pos, er, r) - (cg[h] + pw[h]) * jnp.where(pos, r, er)
                if diag:
                    dz = jnp.where(strict, dz, 0.0)
                dzb.append(dz.astype(BF16))
            ab = [a[h].astype(BF16) for h in range(nh)]
            for p in range(pp):
                cols = slice(p * LANES, (p + 1) * LANES)
                dk_acc[pl.ds(off, blk), cols] += _dot(jnp.concatenate([dzb[2 * p], dzb[2 * p + 1]], axis=0), qm[p], TN)
                dv_acc[pl.ds(off, blk), cols] += _dot(jnp.concatenate([ab[2 * p], ab[2 * p + 1]], axis=0), dom[p], TN)
            dq_new = tuple(dq_acc[p] + _dot(jnp.concatenate([dzb[2 * p], dzb[2 * p + 1]], axis=1), _half_masks(kp[p]), NN)
                           for p in range(pp))
            return (dq_new, tuple(cl[h] + pre[h][:, blk - 1:blk] for h in range(nh)),
                    tuple(cg[h] + pw[h][:, blk - 1:blk] + g[h][:, blk - 1:blk] for h in range(nh)))

        zero1 = (jnp.zeros((blk, 1), F32),) * nh
        j0 = jnp.clip(jnp.max(first_ref[0]).astype(jnp.int32), 0, qi)
        carry = lax.fori_loop(j0, qi, lambda j, c: block(j, c, False), ((jnp.zeros((blk, LANES), F32),) * pp, zero1, zero1))
        dq_acc, _, _ = block(qi, carry, True)
        for p in range(pp):
            dq_ref[:, p * LANES:(p + 1) * LANES] = (dq_acc[p] * scale).astype(BF16)

        @pl.when(qi == nq - 1)
        def _():
            dk_ref[...] = dk_acc[...].astype(BF16)
            dv_ref[...] = dv_acc[...].astype(BF16)

    q_in, k_in, v_in, q_out, kv_out, rows, _ = _pair_specs(s, blk, e, pp, 0)
    w = pp * LANES
    return pl.pallas_call(
        body, name="sb_bwd", grid=(b, e // w, nq),
        in_specs=[q_in, k_in, v_in, q_out, rows, rows], out_specs=[q_out, kv_out, kv_out],
        out_shape=[jax.ShapeDtypeStruct((t, e), BF16)] * 3,
        scratch_shapes=[pltpu.VMEM((s, w), F32), pltpu.VMEM((s, w), F32)],
        compiler_params=_params("parallel", "parallel", "arbitrary"),
    )(qkv, qkv, qkv, do, tot, first)


NEG = -1e30
DEAD = -110.0


def _ride_along(comm, src_refs, dst_refs, sems, first, last):
    if comm is None:
        return lambda: None

    @pl.when(first)
    def _():
        for cp in _peer_copies(comm[0], src_refs, dst_refs, *sems):
            cp.start()

    def finish():
        @pl.when(last)
        def _():
            for cp in _peer_copies(comm[0], src_refs, dst_refs, *sems):
                cp.wait()

    return finish


def _grid_ends(grid):
    ids = [pl.program_id(a) for a in range(len(grid))]
    first = functools.reduce(jnp.logical_and, [i == 0 for i in ids])
    last = functools.reduce(jnp.logical_and, [i == g - 1 for i, g in zip(ids, grid)])
    return first, last


def _fox_fwd(qkv, ct, b, s, e, blk, pp, comm=None):
    scale = HEAD_DIM ** -0.5
    nh = 2 * pp
    t = b * s
    nc = len(comm[1]) if comm else 0
    grid = (b, e // (pp * LANES), s // blk)

    def body(*refs):
        q_ref, k_ref, v_ref, ct_ref = refs[:4]
        o_ref, lse_ref = refs[4 + nc:6 + nc]
        finish = _ride_along(comm, refs[4:4 + nc], refs[6 + nc:6 + 2 * nc], refs[6 + 2 * nc:], *_grid_ends(grid))
        qi = pl.program_id(2)
        qm = [_half_masks((q_ref[:, p * LANES:(p + 1) * LANES] * scale).astype(BF16)) for p in range(pp)]
        cq = [_to_column(ct_ref[h, :, pl.ds(pl.multiple_of(qi * blk, blk), blk)]) for h in range(nh)]
        row, col = _iotas(blk)
        causal = col <= row
        low = lax.broadcasted_iota(jnp.int32, (blk, LANES), 1) < HEAD_DIM

        def block(j, carry, diag):
            off = pl.multiple_of(j * blk, blk)
            m, l, acc = carry
            zz = [_dot(qm[p], k_ref[pl.ds(off, blk), p * LANES:(p + 1) * LANES], NT) for p in range(pp)]
            z = [zz[h // 2][(h % 2) * blk:(h % 2 + 1) * blk] + (cq[h] - ct_ref[h, :, pl.ds(off, blk)]) for h in range(nh)]
            if diag:
                z = [jnp.where(causal, z[h], NEG) for h in range(nh)]
            m_new = tuple(jnp.maximum(m[h], jnp.max(z[h], axis=1, keepdims=True)) for h in range(nh))
            w = [jnp.exp(m[h] - m_new[h]) for h in range(nh)]
            pr = [jnp.exp(z[h] - m_new[h]) for h in range(nh)]
            pb = [pr[h].astype(BF16) for h in range(nh)]
            pv = [_dot(jnp.concatenate([pb[2 * p], pb[2 * p + 1]], axis=1),
                       _half_masks(v_ref[pl.ds(off, blk), p * LANES:(p + 1) * LANES]), NN) for p in range(pp)]
            acc_new = tuple(jnp.where(low, w[2 * p], w[2 * p + 1]) * acc[p] + pv[p] for p in range(pp))
            l_new = tuple(w[h] * l[h] + jnp.sum(pr[h], axis=1, keepdims=True) for h in range(nh))
            return m_new, l_new, acc_new

        zero = ((jnp.full((blk, 1), NEG, F32),) * nh, (jnp.zeros((blk, 1), F32),) * nh, (jnp.zeros((blk, LANES), F32),) * pp)
        carry = block(qi, zero, True)
        m, l, acc = lax.fori_loop(0, qi, lambda j, c_: block(j, c_, False), carry)
        for p in range(pp):
            o_ref[:, p * LANES:(p + 1) * LANES] = acc[p] / jnp.where(low, l[2 * p], l[2 * p + 1])
        for h in range(nh):
            lse_ref[h] = m[h] + jnp.log(l[h])
        finish()

    q_in, k_in, v_in, q_out, _, rows, krow = _pair_specs(s, blk, e, pp, 1)
    outs, sems = _peer_shapes(comm[0], comm[1]) if comm else ([], [])
    return pl.pallas_call(
        body, name="fox_fwd", grid=grid,
        in_specs=[q_in, k_in, v_in, krow] + [ANY] * nc, out_specs=[q_out, rows] + [ANY] * nc,
        out_shape=[jax.ShapeDtypeStruct((t, e), F32), jax.ShapeDtypeStruct((b * e // HEAD_DIM, s, 1), F32)] + outs,
        scratch_shapes=sems,
        compiler_params=_params("arbitrary", "arbitrary", "arbitrary"),
    )(qkv, qkv, qkv, ct, *(comm[1] if comm else ()))


def _fox_bwd(qkv, ct, do, o, lse, b, s, e, blk, pp, comm=None):
    scale = HEAD_DIM ** -0.5
    nh = 2 * pp
    t = b * s
    nq = s // blk
    nc = len(comm[1]) if comm else 0
    w = pp * LANES
    grid = (b, e // w, nq)

    def body(*refs):
        q_ref, k_ref, v_ref, ct_ref, do_ref, o_ref, lse_ref = refs[:7]
        dq_ref, dk_ref, dv_ref, dct_ref = refs[7 + nc:11 + nc]
        dk_acc, dv_acc = refs[11 + 2 * nc:13 + 2 * nc]
        finish = _ride_along(comm, refs[7:7 + nc], refs[11 + nc:11 + 2 * nc], refs[13 + 2 * nc:], *_grid_ends(grid))
        qi = pl.program_id(2)

        @pl.when(qi == 0)
        def _():
            dk_acc[...] = jnp.zeros_like(dk_acc)
            dv_acc[...] = jnp.zeros_like(dv_acc)
            dct_ref[...] = jnp.zeros_like(dct_ref)

        qm = [_half_masks((q_ref[:, p * LANES:(p + 1) * LANES] * scale).astype(BF16)) for p in range(pp)]
        dob = [do_ref[:, p * LANES:(p + 1) * LANES].astype(BF16) for p in range(pp)]
        dom = [_half_masks(dob[p]) for p in range(pp)]
        low = lax.broadcasted_iota(jnp.int32, (blk, LANES), 1) < HEAD_DIM
        delta = []
        for p in range(pp):
            prod = dob[p].astype(F32) * o_ref[:, p * LANES:(p + 1) * LANES]
            delta.append(jnp.sum(jnp.where(low, prod, 0.0), axis=1, keepdims=True))
            delta.append(jnp.sum(jnp.where(low, 0.0, prod), axis=1, keepdims=True))
        cq = [_to_column(ct_ref[h, :, pl.ds(pl.multiple_of(qi * blk, blk), blk)]) for h in range(nh)]
        lse_t = [lse_ref[h] for h in range(nh)]
        row, col = _iotas(blk)
        causal = col <= row

        def block(j, dq_acc, diag):
            off = pl.multiple_of(j * blk, blk)
            kp = [k_ref[pl.ds(off, blk), p * LANES:(p + 1) * LANES] for p in range(pp)]
            zz = [_dot(qm[p], kp[p], NT) for p in range(pp)]
            dd = [_dot(dom[p], v_ref[pl.ds(off, blk), p * LANES:(p + 1) * LANES], NT) for p in range(pp)]
            z = [zz[h // 2][(h % 2) * blk:(h % 2 + 1) * blk] + (cq[h] - ct_ref[h, :, pl.ds(off, blk)]) for h in range(nh)]
            pr = [jnp.exp(z[h] - lse_t[h]) for h in range(nh)]
            if diag:
                pr = [jnp.where(causal, pr[h], 0.0) for h in range(nh)]
            ds = [pr[h] * (dd[h // 2][(h % 2) * blk:(h % 2 + 1) * blk] - delta[h]) for h in range(nh)]
            dsb = [ds[h].astype(BF16) for h in range(nh)]
            pb = [pr[h].astype(BF16) for h in range(nh)]
            for p in range(pp):
                cols = slice(p * LANES, (p + 1) * LANES)
                dk_acc[pl.ds(off, blk), cols] += _dot(jnp.concatenate([dsb[2 * p], dsb[2 * p + 1]], axis=0), qm[p], TN)
                dv_acc[pl.ds(off, blk), cols] += _dot(jnp.concatenate([pb[2 * p], pb[2 * p + 1]], axis=0), dom[p], TN)
            for h in range(nh):
                dct_ref[h, :, pl.ds(off, blk)] -= jnp.sum(ds[h], axis=0, keepdims=True)
            return tuple(dq_acc[p] + _dot(jnp.concatenate([dsb[2 * p], dsb[2 * p + 1]], axis=1), _half_masks(kp[p]), NN)
                         for p in range(pp))

        dq_acc = lax.fori_loop(0, qi, lambda j, a: block(j, a, False), (jnp.zeros((blk, LANES), F32),) * pp)
        dq_acc = block(qi, dq_acc, True)
        for p in range(pp):
            dq_ref[:, p * LANES:(p + 1) * LANES] = (dq_acc[p] * scale).astype(BF16)

        @pl.when(qi == nq - 1)
        def _():
            dk_ref[...] = dk_acc[...].astype(BF16)
            dv_ref[...] = dv_acc[...].astype(BF16)

        finish()

    q_in, k_in, v_in, q_out, kv_out, rows, krow = _pair_specs(s, blk, e, pp, 1)
    outs, sems = _peer_shapes(comm[0], comm[1]) if comm else ([], [])
    return pl.pallas_call(
        body, name="fox_bwd", grid=grid,
        in_specs=[q_in, k_in, v_in, krow, q_out, q_out, rows] + [ANY] * nc,
        out_specs=[q_out, kv_out, kv_out, krow] + [ANY] * nc,
        out_shape=[jax.ShapeDtypeStruct((t, e), BF16)] * 3 + [jax.ShapeDtypeStruct((b * e // HEAD_DIM, 1, s), F32)] + outs,
        scratch_shapes=[pltpu.VMEM((s, w), F32), pltpu.VMEM((s, w), F32)] + sems,
        compiler_params=_params("arbitrary", "arbitrary", "arbitrary"),
    )(qkv, qkv, qkv, ct, do, o, lse, *(comm[1] if comm else ()))


def _scan_rows(f2, group, mode, d2=None):
    n = f2.shape[0]

    def body(*refs):
        f_ref, o_ref = refs[0], refs[-1]
        f = f_ref[...]
        row, col = _iotas(LANES)
        grow = lax.broadcasted_iota(jnp.int32, (n, n), 0)
        gcol = lax.broadcasted_iota(jnp.int32, (n, n), 1)
        same = (grow // group) == (gcol // group)
        e = jnp.exp(-jnp.abs(f))
        if mode == "fwd":
            x = jnp.minimum(f, 0.0) - jnp.log1p(e)
            within = (row <= col).astype(F32)
            earlier = (same & (gcol < grow)).astype(F32)
        else:
            x = refs[1][...]
            within = (row >= col).astype(F32)
            earlier = (same & (gcol > grow)).astype(F32)
        y = jnp.dot(x, within, preferred_element_type=F32, precision=lax.Precision.HIGHEST)
        tot = jnp.sum(x, axis=1, keepdims=True)
        y = y + jnp.dot(earlier, tot, preferred_element_type=F32, precision=lax.Precision.HIGHEST)
        if mode == "bwd":
            r = 1.0 / (1.0 + e)
            y = y * jnp.where(f >= 0.0, e * r, r)
        o_ref[...] = y

    args = (f2,) if mode == "fwd" else (f2, d2)
    return pl.pallas_call(body, name="logf_" + mode, out_shape=jax.ShapeDtypeStruct(f2.shape, F32),
                          compiler_params=_params())(*args)


def _matmul(a, b, dims, *, tm, tn, tk, out_dtype, name, bias=None, res=None, res_scale=1.0, b_outer=False):
    def ij(g0, g1):
        return (g1, g0) if b_outer else (g0, g1)

    if dims == NN:
        (m, kk), n = a.shape, b.shape[1]
        a_spec = pl.BlockSpec((tm, tk), lambda g0, g1, k: (ij(g0, g1)[0], k))
        b_spec = pl.BlockSpec((tk, tn), lambda g0, g1, k: (k, ij(g0, g1)[1]))
    elif dims == NT:
        (m, kk), n = a.shape, b.shape[0]
        a_spec = pl.BlockSpec((tm, tk), lambda g0, g1, k: (ij(g0, g1)[0], k))
        b_spec = pl.BlockSpec((tn, tk), lambda g0, g1, k: (ij(g0, g1)[1], k))
    else:
        (kk, m), n = a.shape, b.shape[1]
        a_spec = pl.BlockSpec((tk, tm), lambda g0, g1, k: (k, ij(g0, g1)[0]))
        b_spec = pl.BlockSpec((tk, tn), lambda g0, g1, k: (k, ij(g0, g1)[1]))
    assert m % tm == 0 and n % tn == 0 and kk % tk == 0, (name, m, n, kk, tm, tn, tk)
    nk = kk // tk
    extras, extra_specs = [], []
    if bias is not None:
        extras.append(bias)
        extra_specs.append(pl.BlockSpec((1, tn), lambda g0, g1, k: (0, ij(g0, g1)[1])))
    if res is not None:
        extras.append(res)
        extra_specs.append(pl.BlockSpec((tm, tn), lambda g0, g1, k: ij(g0, g1)))

    def finish(out, rest, o_ref):
        idx = 0
        if bias is not None:
            out = out + rest[idx][...]
            idx += 1
        if res is not None:
            out = out + res_scale * rest[idx][...]
        o_ref[...] = out.astype(o_ref.dtype)

    def body_single(a_ref, b_ref, *rest):
        finish(_dot(a_ref[...].astype(BF16), b_ref[...].astype(BF16), dims), rest, rest[-1])

    def body_acc(a_ref, b_ref, *rest):
        o_ref, acc_ref = rest[-2], rest[-1]
        k = pl.program_id(2)
        part = _dot(a_ref[...].astype(BF16), b_ref[...].astype(BF16), dims)

        @pl.when(k == 0)
        def _():
            acc_ref[...] = part

        @pl.when(k > 0)
        def _():
            acc_ref[...] += part

        @pl.when(k == nk - 1)
        def _():
            finish(acc_ref[...], rest, o_ref)

    grid = (n // tn, m // tm, nk) if b_outer else (m // tm, n // tn, nk)
    return pl.pallas_call(
        body_single if nk == 1 else body_acc, name=name, grid=grid,
        in_specs=[a_spec, b_spec] + extra_specs,
        out_specs=pl.BlockSpec((tm, tn), lambda g0, g1, k: ij(g0, g1)),
        out_shape=jax.ShapeDtypeStruct((m, n), out_dtype),
        scratch_shapes=[] if nk == 1 else [pltpu.VMEM((tm, tn), F32)],
        compiler_params=_params("parallel", "parallel", "arbitrary"),
    )(a, b, *extras)


def _input_grad(dhs, ws, dr, tm, comm=None):
    t, d = dr.shape
    npc = len(dhs)
    nc = len(comm[1]) if comm else 0
    grid = (t // tm,)

    def body(*refs):
        dr_ref = refs[2 * npc]
        first_in = 2 * npc + 1
        o_ref = refs[first_in + nc]
        finish = _ride_along(comm, refs[first_in:first_in + nc], refs[first_in + nc + 1:first_in + 2 * nc + 1],
                             refs[first_in + 2 * nc + 1:], *_grid_ends(grid))
        out = ALPHA * dr_ref[...]
        for p in range(npc):
            out = out + _dot(refs[p][...], refs[npc + p][...], NT)
        o_ref[...] = out
        finish()

    rows = pl.BlockSpec((tm, d), lambda i: (i, 0))
    outs, sems = _peer_shapes(comm[0], comm[1]) if comm else ([], [])
    return pl.pallas_call(
        body, name="input_grad", grid=grid,
        in_specs=[pl.BlockSpec((tm, a.shape[1]), lambda i: (i, 0)) for a in dhs]
        + [pl.BlockSpec(w.shape, lambda i: (0, 0)) for w in ws] + [rows] + [ANY] * nc,
        out_specs=[rows] + [ANY] * nc,
        out_shape=[jax.ShapeDtypeStruct((t, d), F32)] + outs,
        scratch_shapes=sems,
        compiler_params=_params("arbitrary"),
    )(*dhs, *ws, dr, *(comm[1] if comm else ()))


def _sigmoid(x):
    e = jnp.exp(-jnp.abs(x))
    r = 1.0 / (1.0 + e)
    return jnp.where(x >= 0.0, r, e * r)


def _proj_gate_fwd(o_sb, o_fx, wp_sb, wp_fx, g, tm):
    t, e = o_sb.shape
    d = wp_sb.shape[1]

    def body(osb_ref, ofx_ref, wsb_ref, wfx_ref, gsb_ref, gfx_ref, mg_ref, ysb_ref, yfx_ref):
        ysb = _dot(osb_ref[...].astype(BF16), wsb_ref[...], NN)
        yfx = _dot(ofx_ref[...].astype(BF16), wfx_ref[...], NN)
        ysb_ref[...] = ysb
        yfx_ref[...] = yfx
        mg_ref[...] = (_sigmoid(gsb_ref[...]) * ysb + _sigmoid(gfx_ref[...]) * yfx).astype(BF16)

    rows_e = pl.BlockSpec((tm, e), lambda i: (i, 0))
    rows_d = pl.BlockSpec((tm, d), lambda i: (i, 0))
    w_spec = pl.BlockSpec((e, d), lambda i: (0, 0))
    return pl.pallas_call(
        body, name="proj_gate_fwd", grid=(t // tm,),
        in_specs=[rows_e, rows_e, w_spec, w_spec, rows_d, pl.BlockSpec((tm, d), lambda i: (i, 1))],
        out_specs=[rows_d, rows_d, rows_d],
        out_shape=[jax.ShapeDtypeStruct((t, d), BF16), jax.ShapeDtypeStruct((t, d), F32), jax.ShapeDtypeStruct((t, d), F32)],
        compiler_params=_params("parallel"),
    )(o_sb, o_fx, wp_sb, wp_fx, g, g)


def _gate_bwd(dmg, y_sb, y_fx, g, tm):
    t, d = dmg.shape

    def body(dm_ref, ysb_ref, yfx_ref, gsb_ref, gfx_ref, dysb_ref, dyfx_ref, dg_ref):
        dm = dm_ref[...]
        ssb = _sigmoid(gsb_ref[...])
        sfx = _sigmoid(gfx_ref[...])
        dysb_ref[...] = (dm * ssb).astype(BF16)
        dyfx_ref[...] = (dm * sfx).astype(BF16)
        dg_ref[:, 0:d] = (dm * ysb_ref[...] * ssb * (1.0 - ssb)).astype(BF16)
        dg_ref[:, d:2 * d] = (dm * yfx_ref[...] * sfx * (1.0 - sfx)).astype(BF16)

    rows = pl.BlockSpec((tm, d), lambda i: (i, 0))
    rows1 = pl.BlockSpec((tm, d), lambda i: (i, 1))
    return pl.pallas_call(
        body, name="gate_bwd", grid=(t // tm,),
        in_specs=[rows, rows, rows, rows, rows1],
        out_specs=[rows, rows, pl.BlockSpec((tm, 2 * d), lambda i: (i, 0))],
        out_shape=[jax.ShapeDtypeStruct((t, d), BF16)] * 2 + [jax.ShapeDtypeStruct((t, 2 * d), BF16)],
        compiler_params=_params("parallel"),
    )(dmg, y_sb, y_fx, g, g)


def _mm_res_ln(a, w, xres, gamma, beta, tm, name):
    t, kk = a.shape
    d = w.shape[1]

    def body(a_ref, w_ref, x_ref, g_ref, b_ref, xn_ref, xh_ref, rs_ref, xb_ref):
        r = ALPHA * x_ref[...] + _dot(a_ref[...].astype(BF16), w_ref[...], NN)
        mean = jnp.mean(r, axis=1, keepdims=True)
        cen = r - mean
        rstd = lax.rsqrt(jnp.mean(cen * cen, axis=1, keepdims=True) + LN_EPS)
        xh = cen * rstd
        xn = xh * g_ref[...] + b_ref[...]
        xh_ref[...] = xh
        xn_ref[...] = xn
        xb_ref[...] = xn.astype(BF16)
        rs_ref[...] = rstd

    rows_d = pl.BlockSpec((tm, d), lambda i: (i, 0))
    vec = pl.BlockSpec((1, d), lambda i: (0, 0))
    return pl.pallas_call(
        body, name=name, grid=(t // tm,),
        in_specs=[pl.BlockSpec((tm, kk), lambda i: (i, 0)), pl.BlockSpec((kk, d), lambda i: (0, 0)), rows_d, vec, vec],
        out_specs=[rows_d, rows_d, pl.BlockSpec((tm, 1), lambda i: (i, 0)), rows_d],
        out_shape=[jax.ShapeDtypeStruct((t, d), F32), jax.ShapeDtypeStruct((t, d), F32), jax.ShapeDtypeStruct((t, 1), F32),
                   jax.ShapeDtypeStruct((t, d), BF16)],
        compiler_params=_params("parallel"),
    )(a, w, xres, gamma, beta)


def _ln_bwd_math(dy, xh, rstd, gamma):
    dxh = dy * gamma
    m1 = jnp.mean(dxh, axis=1, keepdims=True)
    m2 = jnp.mean(dxh * xh, axis=1, keepdims=True)
    return rstd * (dxh - m1 - xh * m2)


def _rowsum8(x):
    tm, n = x.shape
    return jnp.sum(x.reshape(tm // SUBLANES, SUBLANES, n), axis=0)


def _fold8(ref):
    ref[0:1, :] = jnp.sum(ref[...], axis=0, keepdims=True)


def _loss_ln_bwd(x2, xh, rstd, gamma, target, tm):
    t, d = x2.shape

    def body(x_ref, xh_ref, rs_ref, g_ref, tg_ref, dr_ref, dg_ref, db_ref, ls_ref, drb_ref):
        @pl.when(pl.program_id(0) == 0)
        def _():
            dg_ref[...] = jnp.zeros_like(dg_ref)
            db_ref[...] = jnp.zeros_like(db_ref)
            ls_ref[...] = jnp.zeros_like(ls_ref)

        err = x_ref[...] - tg_ref[...]
        xh = xh_ref[...]
        dy = err * (1.0 / d)
        dr = _ln_bwd_math(dy, xh, rs_ref[...], g_ref[...])
        dr_ref[...] = dr
        drb_ref[...] = dr.astype(BF16)
        dg_ref[...] += _rowsum8(dy * xh)
        db_ref[...] += _rowsum8(dy)
        sq = _rowsum8(err * err)
        part = sq[:, 0:LANES]
        for j in range(1, d // LANES):
            part = part + sq[:, j * LANES:(j + 1) * LANES]
        ls_ref[...] += part * (0.5 / d)

        @pl.when(pl.program_id(0) == t // tm - 1)
        def _():
            _fold8(dg_ref)
            _fold8(db_ref)
            ls_ref[0:1, 0:1] = jnp.sum(jnp.sum(ls_ref[...], axis=0, keepdims=True), axis=1, keepdims=True)

    rows = pl.BlockSpec((tm, d), lambda i: (i, 0))
    acc = pl.BlockSpec((SUBLANES, d), lambda i: (0, 0))
    return pl.pallas_call(
        body, name="loss_ln_bwd", grid=(t // tm,),
        in_specs=[rows, rows, pl.BlockSpec((tm, 1), lambda i: (i, 0)), pl.BlockSpec((1, d), lambda i: (0, 0)), rows],
        out_specs=[rows, acc, acc, pl.BlockSpec((SUBLANES, LANES), lambda i: (0, 0)), rows],
        out_shape=[jax.ShapeDtypeStruct((t, d), F32), jax.ShapeDtypeStruct((SUBLANES, d), F32),
                   jax.ShapeDtypeStruct((SUBLANES, d), F32), jax.ShapeDtypeStruct((SUBLANES, LANES), F32),
                   jax.ShapeDtypeStruct((t, d), BF16)],
        compiler_params=_params("arbitrary"),
    )(x2, xh, rstd, gamma, target)


def _ln_bwd(dr_next, dlin, xh, rstd, gamma, tm):
    t, d = xh.shape

    def body(dn_ref, dl_ref, xh_ref, rs_ref, g_ref, dr_ref, dg_ref, db_ref, drb_ref):
        @pl.when(pl.program_id(0) == 0)
        def _():
            dg_ref[...] = jnp.zeros_like(dg_ref)
            db_ref[...] = jnp.zeros_like(db_ref)

        dy = ALPHA * dn_ref[...] + dl_ref[...]
        xh = xh_ref[...]
        dr = _ln_bwd_math(dy, xh, rs_ref[...], g_ref[...])
        dr_ref[...] = dr
        drb_ref[...] = dr.astype(BF16)
        dg_ref[...] += _rowsum8(dy * xh)
        db_ref[...] += _rowsum8(dy)

        @pl.when(pl.program_id(0) == t // tm - 1)
        def _():
            _fold8(dg_ref)
            _fold8(db_ref)

    rows = pl.BlockSpec((tm, d), lambda i: (i, 0))
    acc = pl.BlockSpec((SUBLANES, d), lambda i: (0, 0))
    return pl.pallas_call(
        body, name="ln_bwd", grid=(t // tm,),
        in_specs=[rows, rows, rows, pl.BlockSpec((tm, 1), lambda i: (i, 0)), pl.BlockSpec((1, d), lambda i: (0, 0))],
        out_specs=[rows, acc, acc, rows],
        out_shape=[jax.ShapeDtypeStruct((t, d), F32), jax.ShapeDtypeStruct((SUBLANES, d), F32),
                   jax.ShapeDtypeStruct((SUBLANES, d), F32), jax.ShapeDtypeStruct((t, d), BF16)],
        compiler_params=_params("arbitrary"),
    )(dr_next, dlin, xh, rstd, gamma)


def _shift_rows(x, halo, shift, row):
    out = pltpu.roll(x, shift, 0)
    n = halo.shape[0]
    for r in range(shift):
        out = jnp.where(row == r, halo[n - shift + r:n - shift + r + 1, :], out)
    return out


def _unshift_rows(x, halo, shift, row, tm):
    out = pltpu.roll(x, tm - shift, 0)
    for r in range(shift):
        out = jnp.where(row == tm - shift + r, halo[r:r + 1, :], out)
    return out


def _conv_pre(ug_ref, halo_ref, wc_ref, bc_ref, first, tm):
    ug = ug_ref[...].astype(F32)
    halo = jnp.where(first, 0.0, halo_ref[...].astype(F32))
    row = lax.broadcasted_iota(jnp.int32, ug.shape, 0)
    wc = wc_ref[...]
    um1 = _shift_rows(ug, halo, 1, row)
    um2 = _shift_rows(ug, halo, 2, row)
    c = bc_ref[...] + wc[2:3, :] * ug + wc[1:2, :] * um1 + wc[0:1, :] * um2
    return c, ug, um1, um2


INV_SQRT2 = 1.0 / math.sqrt(2.0)
INV_SQRT2PI = 1.0 / math.sqrt(2.0 * math.pi)


def _conv_glu_fwd(u, wc, bc, seq, tm):
    t, f2 = u.shape
    f = f2 // 2
    per_seq = seq // tm
    hb = tm // HALO

    def body(ug_ref, halo_ref, uv_ref, wc_ref, bc_ref, a_ref):
        first = (pl.program_id(0) % per_seq) == 0
        c, _, _, _ = _conv_pre(ug_ref, halo_ref, wc_ref, bc_ref, first, tm)
        gelu = 0.5 * c * (1.0 + lax.erf(c * INV_SQRT2))
        a_ref[...] = (gelu * uv_ref[...].astype(F32)).astype(BF16)

    return pl.pallas_call(
        body, name="conv_glu_fwd", grid=(t // tm,),
        in_specs=[pl.BlockSpec((tm, f), lambda i: (i, 0)),
                  pl.BlockSpec((HALO, f), lambda i: (jnp.maximum(i * hb - 1, 0), 0)),
                  pl.BlockSpec((tm, f), lambda i: (i, 1)),
                  pl.BlockSpec((3, f), lambda i: (0, 0)), pl.BlockSpec((1, f), lambda i: (0, 0))],
        out_specs=pl.BlockSpec((tm, f), lambda i: (i, 0)),
        out_shape=jax.ShapeDtypeStruct((t, f), BF16),
        compiler_params=_params("parallel"),
    )(u, u, u, wc, bc)


def _conv_glu_bwd1(u, da, wc, bc, seq, tm):
    t, f2 = u.shape
    f = f2 // 2
    per_seq = seq // tm
    hb = tm // HALO

    def body(ug_ref, halo_ref, uv_ref, da_ref, wc_ref, bc_ref, dc_ref, duv_ref):
        first = (pl.program_id(0) % per_seq) == 0
        c, _, _, _ = _conv_pre(ug_ref, halo_ref, wc_ref, bc_ref, first, tm)
        cdf = 0.5 * (1.0 + lax.erf(c * INV_SQRT2))
        pdf = jnp.exp(-0.5 * c * c) * INV_SQRT2PI
        da = da_ref[...].astype(F32)
        duv_ref[...] = (da * (c * cdf)).astype(BF16)
        dc_ref[...] = (da * uv_ref[...].astype(F32) * (cdf + c * pdf)).astype(BF16)

    rows = pl.BlockSpec((tm, f), lambda i: (i, 0))
    return pl.pallas_call(
        body, name="conv_glu_bwd1", grid=(t // tm,),
        in_specs=[rows, pl.BlockSpec((HALO, f), lambda i: (jnp.maximum(i * hb - 1, 0), 0)),
                  pl.BlockSpec((tm, f), lambda i: (i, 1)), rows,
                  pl.BlockSpec((3, f), lambda i: (0, 0)), pl.BlockSpec((1, f), lambda i: (0, 0))],
        out_specs=[rows, rows],
        out_shape=[jax.ShapeDtypeStruct((t, f), BF16), jax.ShapeDtypeStruct((t, f), BF16)],
        compiler_params=_params("parallel"),
    )(u, u, u, da, wc, bc)


def _conv_glu_bwd2(u, dc, wc, seq, tm):
    t, f2 = u.shape
    f = f2 // 2
    per_seq = seq // tm
    hb = tm // HALO
    nblk = t // HALO

    def body(ug_ref, halo_ref, dc_ref, nxt_ref, wc_ref, dug_ref, w0_ref, w1_ref, w2_ref, b_ref):
        i = pl.program_id(0)

        @pl.when(i == 0)
        def _():
            for r in (w0_ref, w1_ref, w2_ref, b_ref):
                r[...] = jnp.zeros_like(r)

        first = (i % per_seq) == 0
        last = (i % per_seq) == per_seq - 1
        ug = ug_ref[...].astype(F32)
        halo = jnp.where(first, 0.0, halo_ref[...].astype(F32))
        nxt = jnp.where(last, 0.0, nxt_ref[...].astype(F32))
        row = lax.broadcasted_iota(jnp.int32, ug.shape, 0)
        dc = dc_ref[...].astype(F32)
        wc = wc_ref[...]
        dp1 = _unshift_rows(dc, nxt, 1, row, tm)
        dp2 = _unshift_rows(dc, nxt, 2, row, tm)
        dug_ref[...] = (wc[2:3, :] * dc + wc[1:2, :] * dp1 + wc[0:1, :] * dp2).astype(BF16)
        w2_ref[...] += _rowsum8(dc * ug)
        w1_ref[...] += _rowsum8(dc * _shift_rows(ug, halo, 1, row))
        w0_ref[...] += _rowsum8(dc * _shift_rows(ug, halo, 2, row))
        b_ref[...] += _rowsum8(dc)

        @pl.when(i == t // tm - 1)
        def _():
            for r in (w0_ref, w1_ref, w2_ref, b_ref):
                _fold8(r)

    rows = pl.BlockSpec((tm, f), lambda i: (i, 0))
    acc = pl.BlockSpec((SUBLANES, f), lambda i: (0, 0))
    return pl.pallas_call(
        body, name="conv_glu_bwd2", grid=(t // tm,),
        in_specs=[rows, pl.BlockSpec((HALO, f), lambda i: (jnp.maximum(i * hb - 1, 0), 0)),
                  rows, pl.BlockSpec((HALO, f), lambda i: (jnp.minimum((i + 1) * hb, nblk - 1), 0)),
                  pl.BlockSpec((3, f), lambda i: (0, 0))],
        out_specs=[rows, acc, acc, acc, acc],
        out_shape=[jax.ShapeDtypeStruct((t, f), BF16)] + [jax.ShapeDtypeStruct((SUBLANES, f), F32)] * 4,
        compiler_params=_params("arbitrary"),
    )(u, u, dc, dc, wc)


def _colsum(x, tm, name):
    t, n = x.shape

    def body(x_ref, o_ref):
        @pl.when(pl.program_id(0) == 0)
        def _():
            o_ref[...] = jnp.zeros_like(o_ref)

        o_ref[...] += _rowsum8(x_ref[...].astype(F32))

        @pl.when(pl.program_id(0) == t // tm - 1)
        def _():
            _fold8(o_ref)

    return pl.pallas_call(
        body, name=name, grid=(t // tm,),
        in_specs=[pl.BlockSpec((tm, n), lambda i: (i, 0))],
        out_specs=pl.BlockSpec((SUBLANES, n), lambda i: (0, 0)),
        out_shape=jax.ShapeDtypeStruct((SUBLANES, n), F32),
        compiler_params=_params("arbitrary"),
    )(x)


def _adamw(w, gparts, m, v, name):
    p, r, c = gparts.shape
    tr = r
    for cand in (512, 256, 128, 64, 32, 16, 8):
        if cand * p <= 1024 and r % cand == 0 and r > cand:
            tr = cand
            break
    c1 = 1.0 - ADAM_B1 ** ADAM_STEP
    c2 = 1.0 - ADAM_B2 ** ADAM_STEP

    def body(w_ref, g_ref, m_ref, v_ref, go_ref, d_ref, mo_ref, vo_ref):
        g = g_ref[0].astype(F32)
        for i in range(1, p):
            g = g + g_ref[i].astype(F32)
        mn = ADAM_B1 * m_ref[...] + (1.0 - ADAM_B1) * g
        vn = ADAM_B2 * v_ref[...] + (1.0 - ADAM_B2) * (g * g)
        go_ref[...] = g
        mo_ref[...] = mn
        vo_ref[...] = vn
        d_ref[...] = -ADAM_LR * ((mn / c1) / (jnp.sqrt(vn / c2) + ADAM_EPS) + ADAM_WD * w_ref[...])

    blk = pl.BlockSpec((tr, c), lambda i: (i, 0))
    return pl.pallas_call(
        body, name=name, grid=(r // tr,),
        in_specs=[blk, pl.BlockSpec((p, tr, c), lambda i: (0, i, 0)), blk, blk],
        out_specs=[blk] * 4,
        out_shape=[jax.ShapeDtypeStruct((r, c), F32)] * 4,
        compiler_params=_params("parallel"),
    )(w, gparts, m, v)


MESH = pl.DeviceIdType.MESH
ANY = pl.BlockSpec(memory_space=pl.ANY)


def _all_gather(xs, name):
    n = len(xs)

    def body(*refs):
        x_refs, out_refs = refs[:n], refs[n:2 * n]
        send_sems, recv_sems, local_sems = refs[2 * n:]
        x, y, c = lax.axis_index("x"), lax.axis_index("y"), lax.axis_index("c")
        me, sibling = (x, y, c), (x, y, 1 - c)
        chips = [(1 - x, y), (x, 1 - y), (1 - x, 1 - y)]

        def slot(a, px, py, pc):
            return out_refs[a].at[4 * px + 2 * py + pc]

        def copy(a, k, block, to, src=None):
            return pltpu.make_async_remote_copy(
                src_ref=slot(a, *block) if src is None else src, dst_ref=slot(a, *block),
                send_sem=send_sems.at[k * n + a], recv_sem=recv_sems.at[k * n + a], device_id=to, device_id_type=MESH)

        arrays = range(n)
        mine = [pltpu.make_async_copy(x_refs[a], slot(a, *me), local_sems.at[a]) for a in arrays]
        first = [copy(a, 0, me, sibling, src=x_refs[a]) for a in arrays]
        first += [copy(a, 1 + j, me, (*chip, c), src=x_refs[a]) for j, chip in enumerate(chips) for a in arrays]
        for cp in mine + first:
            cp.start()
        passed = []
        for j, chip in enumerate(chips):
            for a in arrays:
                copy(a, 1 + j, (*chip, c), me).wait_recv()
                passed.append(copy(a, 4 + j, (*chip, c), sibling))
                passed[-1].start()
        for a in arrays:
            copy(a, 0, sibling, me).wait_recv()
        for j, chip in enumerate(chips):
            for a in arrays:
                copy(a, 4 + j, (*chip, 1 - c), me).wait_recv()
        for cp in first + passed:
            cp.wait_send()
        for cp in mine:
            cp.wait()

    return pl.pallas_call(
        body, name=name,
        out_shape=[jax.ShapeDtypeStruct((N_DEV,) + x.shape, x.dtype) for x in xs],
        in_specs=[ANY] * n, out_specs=[ANY] * n,
        scratch_shapes=[pltpu.SemaphoreType.DMA((7 * n,)), pltpu.SemaphoreType.DMA((7 * n,)),
                        pltpu.SemaphoreType.DMA((n,))],
    )(*xs)


def _peer_copies(kind, src_refs, dst_refs, send_sems, recv_sems, local_sems):
    n = len(src_refs)
    x, y, c = lax.axis_index("x"), lax.axis_index("y"), lax.axis_index("c")
    mine = 4 * x + 2 * y + c

    def src(a, idx):
        return src_refs[a] if kind == "spread" else src_refs[a].at[idx]

    copies = [pltpu.make_async_copy(src(a, mine), dst_refs[a].at[mine], local_sems.at[a]) for a in range(n)]
    for k in range(1, N_DEV):
        px = 1 - x if k & 4 else x
        py = 1 - y if k & 2 else y
        pc = 1 - c if k & 1 else c
        for a in range(n):
            copies.append(pltpu.make_async_remote_copy(
                src_ref=src(a, 4 * px + 2 * py + pc), dst_ref=dst_refs[a].at[mine],
                send_sem=send_sems.at[(k - 1) * n + a], recv_sem=recv_sems.at[(k - 1) * n + a],
                device_id=(px, py, pc), device_id_type=MESH))
    return copies


def _peer_shapes(kind, arrays):
    n = len(arrays)
    outs = [jax.ShapeDtypeStruct(((N_DEV,) + a.shape) if kind == "spread" else a.shape, a.dtype) for a in arrays]
    sems = [pltpu.SemaphoreType.DMA((7 * n,)), pltpu.SemaphoreType.DMA((7 * n,)), pltpu.SemaphoreType.DMA((n,))]
    return outs, sems


def _exchange(gs, name):
    n = len(gs)

    def body(*refs):
        copies = _peer_copies("exchange", refs[:n], refs[n:2 * n], *refs[2 * n:])
        for cp in copies:
            cp.start()
        for cp in copies:
            cp.wait()

    outs, sems = _peer_shapes("exchange", gs)
    return pl.pallas_call(body, name=name, out_shape=outs, in_specs=[ANY] * n, out_specs=[ANY] * n,
                          scratch_shapes=sems)(*gs)


def _tile(n, pref, unit=LANES):
    if n <= pref:
        return n
    best = None
    for cand in range(unit, pref + 1, unit):
        if n % cand == 0:
            best = cand
    assert best is not None, (n, pref, unit)
    return best


LATE_KEYS = dict(w_proj_sb="wp_sb", w_proj_fox="wp_fx", w_out="w_out", w_up="w_up", w_conv="w_conv", w_down="w_down")


def _layer_step(x, target, w, attn_blk, late=None):
    b, s, d = x.shape
    t = b * s
    w = dict(w)
    e = w["w_qkv"].shape[1] // 6
    h = e // HEAD_DIM
    f = w["b_conv"].shape[1]
    x2 = x.reshape(t, d)
    tg = target.reshape(t, d)
    pp = 2 if (e // LANES) % 2 == 0 else 1
    fox_blk = min(2 * attn_blk, s)
    tm = _tile(t, 512, HALO)
    tmc = _tile(s, 256, HALO)
    tkt = _tile(t, 2048, HALO)
    td = _tile(d, 1024)
    tf = _tile(f, 1408)
    t2f = _tile(2 * f, 1408)
    tqkv = _tile(6 * e, 1024)
    tg2 = _tile(2 * d, 1024)
    xb = x2.astype(BF16)

    qkv = _matmul(xb, w["w_qkv"], NN, tm=tm, tn=tqkv, tk=d, out_dtype=BF16, name="in_qkv", bias=w["b_qkv"], b_outer=True)
    gate = _matmul(xb, jnp.concatenate([w["w_g"], w["w_f"]], axis=1), NN, tm=tm, tn=2 * d + LANES, tk=d, out_dtype=F32,
                   name="in_gate", bias=jnp.concatenate([w["b_g"], w["b_f"]], axis=1), b_outer=True)
    nr = s // LANES
    f2 = gate[:, 2 * d:2 * d + h].reshape(b, s, h).transpose(0, 2, 1).reshape(b * h * nr, LANES)
    ct = _scan_rows(f2, nr, "fwd").reshape(b * h, 1, s)
    o_sb, tot, first = _sb_fwd(qkv, b, s, e, attn_blk, pp)
    if late is None:
        o_fx, lse = _fox_fwd(qkv, ct, b, s, e, fox_blk, 1)
    else:
        o_fx, lse, *gathered = _fox_fwd(qkv, ct, b, s, e, fox_blk, 1, comm=("spread", late[1]))
        for name, g in zip(late[0], gathered):
            w[LATE_KEYS[name]] = _join(g, name)
    merged, y_sb, y_fx = _proj_gate_fwd(o_sb, o_fx, w["wp_sb"], w["wp_fx"], gate, tm)
    x1, xh1, rs1, x1b = _mm_res_ln(merged, w["w_out"], x2, w["ln1_g"], w["ln1_b"], tm, "out_ln1")
    u = _matmul(x1b, w["w_up"], NN, tm=tm, tn=t2f, tk=d, out_dtype=BF16, name="ffn_up", b_outer=True)
    act = _conv_glu_fwd(u, w["w_conv"], w["b_conv"], s, tmc)
    xo, xh2, rs2, _ = _mm_res_ln(act, w["w_down"], x1, w["ln2_g"], w["ln2_b"], tm, "down_ln2")

    gr = {}
    dr2, dg2, db2, ls, dr2b = _loss_ln_bwd(xo, xh2, rs2, w["ln2_g"], tg, tm)
    gr["ln2_g"], gr["ln2_b"] = dg2[0:1], db2[0:1]
    da = _matmul(dr2b, w["w_down"], NT, tm=tm, tn=tf, tk=d, out_dtype=BF16, name="d_act", b_outer=True)
    gr["w_down"] = _matmul(act, dr2b, TN, tm=tf, tn=td, tk=tkt, out_dtype=BF16, name="dw_down")
    dc, du_v = _conv_glu_bwd1(u, da, w["w_conv"], w["b_conv"], s, tmc)
    du_g, gw0, gw1, gw2, gbc = _conv_glu_bwd2(u, dc, w["w_conv"], s, tmc)
    gr["w_conv"] = jnp.concatenate([gw0[0:1], gw1[0:1], gw2[0:1]], axis=0)
    gr["b_conv"] = gbc[0:1]
    du = jnp.concatenate([du_g, du_v], axis=1)
    dlin1 = _matmul(du, w["w_up"], NT, tm=tm, tn=td, tk=2 * f, out_dtype=F32, name="d_x1")
    gr["w_up"] = _matmul(x1b, du, TN, tm=td, tn=t2f, tk=tkt, out_dtype=BF16, name="dw_up")
    dr1, dg1, db1, dr1b = _ln_bwd(dr2, dlin1, xh1, rs1, w["ln1_g"], tm)
    gr["ln1_g"], gr["ln1_b"] = dg1[0:1], db1[0:1]
    dmg = _matmul(dr1b, w["w_out"], NT, tm=tm, tn=td, tk=d, out_dtype=F32, name="d_merged")
    gr["w_out"] = _matmul(merged, dr1b, TN, tm=td, tn=td, tk=tkt, out_dtype=BF16, name="dw_out")
    dy_sb, dy_fx, dgate = _gate_bwd(dmg, y_sb, y_fx, gate, tm)
    do_sb = _matmul(dy_sb, w["wp_sb"], NT, tm=tm, tn=e, tk=d, out_dtype=BF16, name="d_o_sb")
    do_fx = _matmul(dy_fx, w["wp_fx"], NT, tm=tm, tn=e, tk=d, out_dtype=BF16, name="d_o_fx")
    gr["wp_sb"] = _matmul(o_sb, dy_sb, TN, tm=e, tn=td, tk=tkt, out_dtype=BF16, name="dwp_sb")
    gr["wp_fx"] = _matmul(o_fx, dy_fx, TN, tm=e, tn=td, tk=tkt, out_dtype=BF16, name="dwp_fx")
    dq_sb, dk_sb, dv_sb = _sb_bwd(qkv, do_sb, tot, first, b, s, e, attn_blk, pp)
    landed = None
    if late is None:
        dq_fx, dk_fx, dv_fx, dct = _fox_bwd(qkv, ct, do_fx, o_fx, lse, b, s, e, fox_blk, 1)
    else:
        blocks = [_cut(gr[LATE_KEYS[n]], n).astype(BF16 if n in MATMUL_OPERANDS else F32) for n in late[0]]
        dq_fx, dk_fx, dv_fx, dct, *got = _fox_bwd(qkv, ct, do_fx, o_fx, lse, b, s, e, fox_blk, 1,
                                                   comm=("exchange", blocks))
        landed = dict(zip(late[0], got))
    dqkv = jnp.concatenate([dq_sb, dk_sb, dv_sb, dq_fx, dk_fx, dv_fx], axis=1)
    df2 = _scan_rows(f2, nr, "bwd", dct.reshape(b * h * nr, LANES))
    df = jnp.pad(df2.reshape(b, h, s).transpose(0, 2, 1).reshape(t, h), ((0, 0), (0, LANES - h))).astype(BF16)
    gr["w_qkv"] = _matmul(xb, dqkv, TN, tm=td, tn=tqkv, tk=tkt, out_dtype=BF16, name="dw_qkv")
    gr["w_g"] = _matmul(xb, dgate, TN, tm=td, tn=tg2, tk=tkt, out_dtype=BF16, name="dw_gate")
    gr["w_f"] = _matmul(xb, df, TN, tm=td, tn=LANES, tk=tkt, out_dtype=BF16, name="dw_forget")
    gr["b_qkv"] = _colsum(dqkv, tm, "db_qkv")[0:1]
    gr["b_g"] = _colsum(dgate, tm, "db_gate")[0:1]
    gr["b_f"] = _colsum(df, tm, "db_forget")[0:1]
    comm = None
    if late is not None:
        comm = ("exchange", [_cut(_w_in_layout(gr["w_qkv"], gr["w_f"], gr["w_g"], h), "w_in").astype(BF16)])
    dx, *got = _input_grad([dqkv, dgate, df], [w["w_qkv"], w["w_g"], w["w_f"]], dr1, tm, comm)
    if late is not None:
        landed["w_in"] = got[0]
    return ls[0:1, 0:1], dx.reshape(b, s, d), gr, landed


SHARDED = ("w_in", "w_proj_sb", "w_proj_fox", "w_out", "w_up", "w_conv", "w_down")
ROW_SHARDED = ("w_out", "w_down")
REPLICATED = ("b_in", "ln1_g", "ln1_b", "b_conv", "ln2_g", "ln2_b")
WEIGHTS = ("w_in", "b_in", "w_proj_sb", "w_proj_fox", "w_out", "ln1_g", "ln1_b", "w_up", "w_conv", "b_conv",
           "w_down", "ln2_g", "ln2_b")
MATMUL_OPERANDS = ("w_in", "w_proj_sb", "w_proj_fox", "w_out", "w_up", "w_down")


def _w_in_layout(g_qkv, g_f, g_g, h):
    return jnp.concatenate([g_qkv, g_f[:, :h], g_g], axis=1)


def _cut(full, name):
    r, c = full.shape
    if name in ROW_SHARDED:
        return full.reshape(N_DEV, r // N_DEV, c)
    cs = c // N_DEV
    return jnp.stack([full[:, j * cs:(j + 1) * cs] for j in range(N_DEV)], axis=0)


def _join(blocks, name):
    p, r, c = blocks.shape
    if name in ROW_SHARDED:
        return blocks.reshape(p * r, c)
    return jnp.concatenate([blocks[j] for j in range(p)], axis=1)


def kernel(x, w_in, b_in, w_proj_sb, w_proj_fox, w_out, ln1_g, ln1_b, w_up, w_conv, b_conv, w_down, ln2_g, ln2_b, loss_target, m_w_in, m_b_in, m_w_proj_sb, m_w_proj_fox, m_w_out, m_ln1_g, m_ln1_b, m_w_up, m_w_conv, m_b_conv, m_w_down, m_ln2_g, m_ln2_b, v_w_in, v_b_in, v_w_proj_sb, v_w_proj_fox, v_w_out, v_ln1_g, v_ln1_b, v_w_up, v_w_conv, v_b_conv, v_w_down, v_ln2_g, v_ln2_b):
    wts = dict(w_in=w_in, b_in=b_in, w_proj_sb=w_proj_sb, w_proj_fox=w_proj_fox, w_out=w_out, ln1_g=ln1_g, ln1_b=ln1_b,
               w_up=w_up, w_conv=w_conv, b_conv=b_conv, w_down=w_down, ln2_g=ln2_g, ln2_b=ln2_b)
    mom = dict(w_in=m_w_in, b_in=m_b_in, w_proj_sb=m_w_proj_sb, w_proj_fox=m_w_proj_fox, w_out=m_w_out, ln1_g=m_ln1_g,
               ln1_b=m_ln1_b, w_up=m_w_up, w_conv=m_w_conv, b_conv=m_b_conv, w_down=m_w_down, ln2_g=m_ln2_g, ln2_b=m_ln2_b)
    var = dict(w_in=v_w_in, b_in=v_b_in, w_proj_sb=v_w_proj_sb, w_proj_fox=v_w_proj_fox, w_out=v_w_out, ln1_g=v_ln1_g,
               ln1_b=v_ln1_b, w_up=v_w_up, w_conv=v_w_conv, b_conv=v_b_conv, w_down=v_w_down, ln2_g=v_ln2_g, ln2_b=v_ln2_b)
    shard = {n: wts[n].reshape(wts[n].shape[-2:]) for n in WEIGHTS}

    w_in_full = _join(_all_gather([shard["w_in"].astype(BF16)], "gather_w_in")[0], "w_in")
    late_names = [n for n in SHARDED if n != "w_in"]
    late = (late_names, [shard[n].astype(BF16) if n in MATMUL_OPERANDS else shard[n] for n in late_names])
    e = shard["w_proj_sb"].shape[0]
    h = e // HEAD_DIM
    nq = 6 * e

    def cut_in(a, pad):
        fcols = a[:, nq:nq + h]
        if pad:
            fcols = jnp.pad(fcols, ((0, 0), (0, LANES - h)))
        return a[:, :nq], a[:, nq + h:], fcols

    w_qkv, w_g, w_f = cut_in(w_in_full, True)
    b_qkv, b_g, b_f = cut_in(shard["b_in"], True)
    w = dict(w_qkv=w_qkv, w_g=w_g, w_f=w_f, b_qkv=b_qkv, b_g=b_g, b_f=b_f, b_conv=shard["b_conv"],
             ln1_g=shard["ln1_g"], ln1_b=shard["ln1_b"], ln2_g=shard["ln2_g"], ln2_b=shard["ln2_b"])

    loss_local, grad_x, gr, gsum = _layer_step(x, loss_target, w, min(256, x.shape[1]), late)
    loss = lax.psum(loss_local[0, 0], ("x", "y", "c"))

    local = dict(b_in=_w_in_layout(gr["b_qkv"], gr["b_f"], gr["b_g"], h), ln1_g=gr["ln1_g"], ln1_b=gr["ln1_b"],
                 b_conv=gr["b_conv"], ln2_g=gr["ln2_g"], ln2_b=gr["ln2_b"])
    parts = _all_gather([jnp.concatenate([local[n] for n in REPLICATED], axis=1)], "gather_small_grads")[0]
    off = 0
    for n in REPLICATED:
        gsum[n] = parts[:, :, off:off + shard[n].size]
        off += shard[n].size

    grads, deltas, new_m, new_v = [], [], [], []
    for n in WEIGHTS:
        shp = wts[n].shape
        g, dl, mn, vn = _adamw(shard[n], gsum[n], mom[n].reshape(shard[n].shape), var[n].reshape(shard[n].shape),
                               "adamw_" + n)
        grads.append(g.reshape(shp))
        deltas.append(dl.reshape(shp))
        new_m.append(mn.reshape(shp))
        new_v.append(vn.reshape(shp))
    return (loss, grad_x, *grads, *deltas, *new_m, *new_v)
```

```python
import functools
import math

import jax
import jax.numpy as jnp
from jax import lax
from jax.experimental import pallas as pl
from jax.experimental.pallas import tpu as pltpu

F32 = jnp.float32
BF16 = jnp.bfloat16

HEAD_DIM = 64
LN_EPS = 1e-5
DEPTH = 1
ALPHA = (2.0 * DEPTH) ** 0.25
ADAM_LR, ADAM_B1, ADAM_B2, ADAM_EPS, ADAM_WD, ADAM_STEP = 0.001, 0.9, 0.999, 1e-08, 0.01, 10
N_DEV = 8
LANES = 128
SUBLANES = 8
HALO = 16
VMEM_LIMIT = 56 * 1024 * 1024

NN = ((1,), (0,))
NT = ((1,), (1,))
TN = ((0,), (0,))


def _dot(a, b, dims):
    return lax.dot_general(a, b, (dims, ((), ())), preferred_element_type=F32)


def _params(*sem):
    return pltpu.CompilerParams(dimension_semantics=sem, vmem_limit_bytes=VMEM_LIMIT)


def _iotas(blk):
    row = lax.broadcasted_iota(jnp.int32, (blk, blk), 0)
    col = lax.broadcasted_iota(jnp.int32, (blk, blk), 1)
    return row, col


def _sb_terms(z):
    e = jnp.exp(-jnp.abs(z))
    lb = jnp.minimum(z, 0.0) - jnp.log(1.0 + e)
    return lb, lb - z, e


def _pair_specs(s, blk, e, pp, branch):
    w = pp * LANES
    nq = s // blk
    ng = e // w
    base = 3 * branch * ng
    q_in = pl.BlockSpec((blk, w), lambda b, g, i: (b * nq + i, base + g))
    k_in = pl.BlockSpec((s, w), lambda b, g, i: (b, base + ng + g))
    v_in = pl.BlockSpec((s, w), lambda b, g, i: (b, base + 2 * ng + g))
    q_out = pl.BlockSpec((blk, w), lambda b, g, i: (b * nq + i, g))
    kv_out = pl.BlockSpec((s, w), lambda b, g, i: (b, g))
    rows = pl.BlockSpec((2 * pp, blk, 1), lambda b, g, i: (b * ng + g, i, 0))
    krow = pl.BlockSpec((2 * pp, 1, s), lambda b, g, i: (b * ng + g, 0, 0))
    return q_in, k_in, v_in, q_out, kv_out, rows, krow


def _to_column(r):
    row, col = _iotas(r.shape[1])
    return jnp.sum(jnp.where(row == col, r, 0.0), axis=1, keepdims=True)


def _half_masks(x):
    low = lax.broadcasted_iota(jnp.int32, x.shape, 1) < HEAD_DIM
    zero = jnp.zeros_like(x)
    return jnp.concatenate([jnp.where(low, x, zero), jnp.where(low, zero, x)], axis=0)


def _tri_sums(xs, tri):
    hi = [x.astype(BF16) for x in xs]
    lo = [(x - h.astype(F32)).astype(BF16) for x, h in zip(xs, hi)]
    n = len(xs)
    blk = xs[0].shape[0]
    r = _dot(jnp.concatenate(hi + lo, axis=0), tri, NN)
    return [r[i * blk:(i + 1) * blk] + r[(n + i) * blk:(n + i + 1) * blk] for i in range(n)]


def _sb_fwd(qkv, b, s, e, blk, pp):
    scale = HEAD_DIM ** -0.5
    nh = 2 * pp
    t = b * s

    def body(q_ref, k_ref, v_ref, o_ref, tot_ref, first_ref):
        qi = pl.program_id(2)
        qm = [_half_masks((q_ref[:, p * LANES:(p + 1) * LANES] * scale).astype(BF16)) for p in range(pp)]
        row, col = _iotas(blk)
        strict = col < row
        after = (row > col).astype(BF16)

        def block(j, carry, diag):
            off = pl.multiple_of(j * blk, blk)
            o_acc, run = carry
            zz = [_dot(qm[p], k_ref[pl.ds(off, blk), p * LANES:(p + 1) * LANES], NT) for p in range(pp)]
            z = [zz[h // 2][(h % 2) * blk:(h % 2 + 1) * blk] for h in range(nh)]
            terms = [_sb_terms(z[h]) for h in range(nh)]
            lom = [jnp.where(strict, terms[h][1], 0.0) if diag else terms[h][1] for h in range(nh)]
            sfx = _tri_sums(lom, after)
            a = [jnp.exp(terms[h][0] + sfx[h] + run[h]) for h in range(nh)]
            if diag:
                a = [jnp.where(strict, a[h], 0.0) for h in range(nh)]
            ab = [a[h].astype(BF16) for h in range(nh)]
            o_new = tuple(
                o_acc[p] + _dot(jnp.concatenate([ab[2 * p], ab[2 * p + 1]], axis=1),
                                _half_masks(v_ref[pl.ds(off, blk), p * LANES:(p + 1) * LANES]), NN)
                for p in range(pp))
            return o_new, tuple(run[h] + sfx[h][:, 0:1] + lom[h][:, 0:1] for h in range(nh))

        def alive(run):
            m = run[0]
            for h in range(1, nh):
                m = jnp.maximum(m, run[h])
            return jnp.max(m) > DEAD

        o_acc, run = block(qi, ((jnp.zeros((blk, LANES), F32),) * pp, (jnp.zeros((blk, 1), F32),) * nh), True)

        def step(c):
            j, _, o_acc, run = c
            o_acc, run = block(j, (o_acc, run), False)
            return j - 1, alive(run), o_acc, run

        j, _, o_acc, run = lax.while_loop(lambda c: jnp.logical_and(c[0] >= 0, c[1]), step,
                                          (qi - 1, alive(run), o_acc, run))
        for p in range(pp):
            o_ref[:, p * LANES:(p + 1) * LANES] = o_acc[p].astype(o_ref.dtype)
        for h in range(nh):
            tot_ref[h] = run[h]
            first_ref[h] = jnp.zeros((blk, 1), F32) + (j + 1).astype(F32)

    q_in, k_in, v_in, q_out, _, rows, _ = _pair_specs(s, blk, e, pp, 0)
    return pl.pallas_call(
        body, name="sb_fwd", grid=(b, e // (pp * LANES), s // blk),
        in_specs=[q_in, k_in, v_in], out_specs=[q_out, rows, rows],
        out_shape=[jax.ShapeDtypeStruct((t, e), BF16)] + [jax.ShapeDtypeStruct((b * e // HEAD_DIM, s, 1), F32)] * 2,
        compiler_params=_params("parallel", "parallel", "arbitrary"),
    )(qkv, qkv, qkv)


def _sb_bwd(qkv, do, tot, first, b, s, e, blk, pp):
    scale = HEAD_DIM ** -0.5
    nh = 2 * pp
    t = b * s
    nq = s // blk

    def body(q_ref, k_ref, v_ref, do_ref, tot_ref, first_ref, dq_ref, dk_ref, dv_ref, dk_acc, dv_acc):
        qi = pl.program_id(2)

        @pl.when(qi == 0)
        def _():
            dk_acc[...] = jnp.zeros_like(dk_acc)
            dv_acc[...] = jnp.zeros_like(dv_acc)

        qm = [_half_masks((q_ref[:, p * LANES:(p + 1) * LANES] * scale).astype(BF16)) for p in range(pp)]
        dom = [_half_masks(do_ref[:, p * LANES:(p + 1) * LANES].astype(BF16)) for p in range(pp)]
        tot_t = [tot_ref[h] for h in range(nh)]
        row, col = _iotas(blk)
        strict = col < row
        upto = (row <= col).astype(BF16)
        before = (row < col).astype(BF16)

        def block(j, carry, diag):
            off = pl.multiple_of(j * blk, blk)
            dq_acc, cl, cg = carry
            kp = [k_ref[pl.ds(off, blk), p * LANES:(p + 1) * LANES] for p in range(pp)]
            vp = [v_ref[pl.ds(off, blk), p * LANES:(p + 1) * LANES] for p in range(pp)]
            zz = [_dot(qm[p], kp[p], NT) for p in range(pp)]
            dd = [_dot(dom[p], vp[p], NT) for p in range(pp)]
            z = [zz[h // 2][(h % 2) * blk:(h % 2 + 1) * blk] for h in range(nh)]
            da = [dd[h // 2][(h % 2) * blk:(h % 2 + 1) * blk] for h in range(nh)]
            terms = [_sb_terms(z[h]) for h in range(nh)]
            lom = [jnp.where(strict, terms[h][1], 0.0) if diag else terms[h][1] for h in range(nh)]
            pre = _tri_sums(lom, upto)
            a = [jnp.exp(terms[h][0] + (tot_t[h] - cl[h] - pre[h])) for h in range(nh)]
            if diag:
                a = [jnp.where(strict, a[h], 0.0) for h in range(nh)]
            g = [a[h] * da[h] for h in range(nh)]
            pw = _tri_sums(g, before)
            dzb = []
            for h in range(nh):
                ex = terms[h][2]
                r = 1.0 / (1.0 + ex)
                er = ex * r
                pos = z[h] >= 0.0
                dz = g[h] * jnp.where(pos, er, r) - (cg[h] + pw[h]) * jnp.where(pos, r, er)
                if diag:
                    dz = jnp.where(strict, dz, 0.0)
                dzb.append(dz.astype(BF16))
            ab = [a[h].astype(BF16) for h in range(nh)]
            for p in range(pp):
                cols = slice(p * LANES, (p + 1) * LANES)
                dk_acc[pl.ds(off, blk), cols] += _dot(jnp.concatenate([dzb[2 * p], dzb[2 * p + 1]], axis=0), qm[p], TN)
                dv_acc[pl.ds(off, blk), cols] += _dot(jnp.concatenate([ab[2 * p], ab[2 * p + 1]], axis=0), dom[p], TN)
            dq_new = tuple(dq_acc[p] + _dot(jnp.concatenate([dzb[2 * p], dzb[2 * p + 1]], axis=1), _half_masks(kp[p]), NN)
                           for p in range(pp))
            return (dq_new, tuple(cl[h] + pre[h][:, blk - 1:blk] for h in range(nh)),
                    tuple(cg[h] + pw[h][:, blk - 1:blk] + g[h][:, blk - 1:blk] for h in range(nh)))

        zero1 = (jnp.zeros((blk, 1), F32),) * nh
        j0 = jnp.clip(jnp.max(first_ref[0]).astype(jnp.int32), 0, qi)
        carry = lax.fori_loop(j0, qi, lambda j, c: block(j, c, False), ((jnp.zeros((blk, LANES), F32),) * pp, zero1, zero1))
        dq_acc, _, _ = block(qi, carry, True)
        for p in range(pp):
            dq_ref[:, p * LANES:(p + 1) * LANES] = (dq_acc[p] * scale).astype(BF16)

        @pl.when(qi == nq - 1)
        def _():
            dk_ref[...] = dk_acc[...].astype(BF16)
            dv_ref[...] = dv_acc[...].astype(BF16)

    q_in, k_in, v_in, q_out, kv_out, rows, _ = _pair_specs(s, blk, e, pp, 0)
    w = pp * LANES
    return pl.pallas_call(
        body, name="sb_bwd", grid=(b, e // w, nq),
        in_specs=[q_in, k_in, v_in, q_out, rows, rows], out_specs=[q_out, kv_out, kv_out],
        out_shape=[jax.ShapeDtypeStruct((t, e), BF16)] * 3,
        scratch_shapes=[pltpu.VMEM((s, w), F32), pltpu.VMEM((s, w), F32)],
        compiler_params=_params("parallel", "parallel", "arbitrary"),
    )(qkv, qkv, qkv, do, tot, first)


NEG = -1e30
DEAD = -110.0


def _ride_along(comm, src_refs, dst_refs, sems, first, last):
    if comm is None:
        return lambda: None

    @pl.when(first)
    def _():
        for cp in _peer_copies(comm[0], src_refs, dst_refs, *sems):
            cp.start()

    def finish():
        @pl.when(last)
        def _():
            for cp in _peer_copies(comm[0], src_refs, dst_refs, *sems):
                cp.wait()

    return finish


def _grid_ends(grid):
    ids = [pl.program_id(a) for a in range(len(grid))]
    first = functools.reduce(jnp.logical_and, [i == 0 for i in ids])
    last = functools.reduce(jnp.logical_and, [i == g - 1 for i, g in zip(ids, grid)])
    return first, last


def _fox_fwd(qkv, ct, b, s, e, blk, pp, comm=None):
    scale = HEAD_DIM ** -0.5
    nh = 2 * pp
    t = b * s
    nc = len(comm[1]) if comm else 0
    grid = (b, e // (pp * LANES), s // blk)

    def body(*refs):
        q_ref, k_ref, v_ref, ct_ref = refs[:4]
        o_ref, lse_ref = refs[4 + nc:6 + nc]
        finish = _ride_along(comm, refs[4:4 + nc], refs[6 + nc:6 + 2 * nc], refs[6 + 2 * nc:], *_grid_ends(grid))
        qi = pl.program_id(2)
        qm = [_half_masks((q_ref[:, p * LANES:(p + 1) * LANES] * scale).astype(BF16)) for p in range(pp)]
        cq = [_to_column(ct_ref[h, :, pl.ds(pl.multiple_of(qi * blk, blk), blk)]) for h in range(nh)]
        row, col = _iotas(blk)
        causal = col <= row
        low = lax.broadcasted_iota(jnp.int32, (blk, LANES), 1) < HEAD_DIM

        def block(j, carry, diag):
            off = pl.multiple_of(j * blk, blk)
            m, l, acc = carry
            zz = [_dot(qm[p], k_ref[pl.ds(off, blk), p * LANES:(p + 1) * LANES], NT) for p in range(pp)]
            z = [zz[h // 2][(h % 2) * blk:(h % 2 + 1) * blk] + (cq[h] - ct_ref[h, :, pl.ds(off, blk)]) for h in range(nh)]
            if diag:
                z = [jnp.where(causal, z[h], NEG) for h in range(nh)]
            m_new = tuple(jnp.maximum(m[h], jnp.max(z[h], axis=1, keepdims=True)) for h in range(nh))
            w = [jnp.exp(m[h] - m_new[h]) for h in range(nh)]
            pr = [jnp.exp(z[h] - m_new[h]) for h in range(nh)]
            pb = [pr[h].astype(BF16) for h in range(nh)]
            pv = [_dot(jnp.concatenate([pb[2 * p], pb[2 * p + 1]], axis=1),
                       _half_masks(v_ref[pl.ds(off, blk), p * LANES:(p + 1) * LANES]), NN) for p in range(pp)]
            acc_new = tuple(jnp.where(low, w[2 * p], w[2 * p + 1]) * acc[p] + pv[p] for p in range(pp))
            l_new = tuple(w[h] * l[h] + jnp.sum(pr[h], axis=1, keepdims=True) for h in range(nh))
            return m_new, l_new, acc_new

        zero = ((jnp.full((blk, 1), NEG, F32),) * nh, (jnp.zeros((blk, 1), F32),) * nh, (jnp.zeros((blk, LANES), F32),) * pp)
        carry = block(qi, zero, True)
        m, l, acc = lax.fori_loop(0, qi, lambda j, c_: block(j, c_, False), carry)
        for p in range(pp):
            o_ref[:, p * LANES:(p + 1) * LANES] = acc[p] / jnp.where(low, l[2 * p], l[2 * p + 1])
        for h in range(nh):
            lse_ref[h] = m[h] + jnp.log(l[h])
        finish()

    q_in, k_in, v_in, q_out, _, rows, krow = _pair_specs(s, blk, e, pp, 1)
    outs, sems = _peer_shapes(comm[0], comm[1]) if comm else ([], [])
    return pl.pallas_call(
        body, name="fox_fwd", grid=grid,
        in_specs=[q_in, k_in, v_in, krow] + [ANY] * nc, out_specs=[q_out, rows] + [ANY] * nc,
        out_shape=[jax.ShapeDtypeStruct((t, e), F32), jax.ShapeDtypeStruct((b * e // HEAD_DIM, s, 1), F32)] + outs,
        scratch_shapes=sems,
        compiler_params=_params("arbitrary", "arbitrary", "arbitrary"),
    )(qkv, qkv, qkv, ct, *(comm[1] if comm else ()))


def _fox_bwd(qkv, ct, do, o, lse, b, s, e, blk, pp, comm=None):
    scale = HEAD_DIM ** -0.5
    nh = 2 * pp
    t = b * s
    nq = s // blk
    nc = len(comm[1]) if comm else 0
    w = pp * LANES
    grid = (b, e // w, nq)

    def body(*refs):
        q_ref, k_ref, v_ref, ct_ref, do_ref, o_ref, lse_ref = refs[:7]
        dq_ref, dk_ref, dv_ref, dct_ref = refs[7 + nc:11 + nc]
        dk_acc, dv_acc = refs[11 + 2 * nc:13 + 2 * nc]
        finish = _ride_along(comm, refs[7:7 + nc], refs[11 + nc:11 + 2 * nc], refs[13 + 2 * nc:], *_grid_ends(grid))
        qi = pl.program_id(2)

        @pl.when(qi == 0)
        def _():
            dk_acc[...] = jnp.zeros_like(dk_acc)
            dv_acc[...] = jnp.zeros_like(dv_acc)
            dct_ref[...] = jnp.zeros_like(dct_ref)

        qm = [_half_masks((q_ref[:, p * LANES:(p + 1) * LANES] * scale).astype(BF16)) for p in range(pp)]
        dob = [do_ref[:, p * LANES:(p + 1) * LANES].astype(BF16) for p in range(pp)]
        dom = [_half_masks(dob[p]) for p in range(pp)]
        low = lax.broadcasted_iota(jnp.int32, (blk, LANES), 1) < HEAD_DIM
        delta = []
        for p in range(pp):
            prod = dob[p].astype(F32) * o_ref[:, p * LANES:(p + 1) * LANES]
            delta.append(jnp.sum(jnp.where(low, prod, 0.0), axis=1, keepdims=True))
            delta.append(jnp.sum(jnp.where(low, 0.0, prod), axis=1, keepdims=True))
        cq = [_to_column(ct_ref[h, :, pl.ds(pl.multiple_of(qi * blk, blk), blk)]) for h in range(nh)]
        lse_t = [lse_ref[h] for h in range(nh)]
        row, col = _iotas(blk)
        causal = col <= row

        def block(j, dq_acc, diag):
            off = pl.multiple_of(j * blk, blk)
            kp = [k_ref[pl.ds(off, blk), p * LANES:(p + 1) * LANES] for p in range(pp)]
            zz = [_dot(qm[p], kp[p], NT) for p in range(pp)]
            dd = [_dot(dom[p], v_ref[pl.ds(off, blk), p * LANES:(p + 1) * LANES], NT) for p in range(pp)]
            z = [zz[h // 2][(h % 2) * blk:(h % 2 + 1) * blk] + (cq[h] - ct_ref[h, :, pl.ds(off, blk)]) for h in range(nh)]
            pr = [jnp.exp(z[h] - lse_t[h]) for h in range(nh)]
            if diag:
                pr = [jnp.where(causal, pr[h], 0.0) for h in range(nh)]
            ds = [pr[h] * (dd[h // 2][(h % 2) * blk:(h % 2 + 1) * blk] - delta[h]) for h in range(nh)]
            dsb = [ds[h].astype(BF16) for h in range(nh)]
            pb = [pr[h].astype(BF16) for h in range(nh)]
            for p in range(pp):
                cols = slice(p * LANES, (p + 1) * LANES)
                dk_acc[pl.ds(off, blk), cols] += _dot(jnp.concatenate([dsb[2 * p], dsb[2 * p + 1]], axis=0), qm[p], TN)
                dv_acc[pl.ds(off, blk), cols] += _dot(jnp.concatenate([pb[2 * p], pb[2 * p + 1]], axis=0), dom[p], TN)
            for h in range(nh):
                dct_ref[h, :, pl.ds(off, blk)] -= jnp.sum(ds[h], axis=0, keepdims=True)
            return tuple(dq_acc[p] + _dot(jnp.concatenate([dsb[2 * p], dsb[2 * p + 1]], axis=1), _half_masks(kp[p]), NN)
                         for p in range(pp))

        dq_acc = lax.fori_loop(0, qi, lambda j, a: block(j, a, False), (jnp.zeros((blk, LANES), F32),) * pp)
        dq_acc = block(qi, dq_acc, True)
        for p in range(pp):
            dq_ref[:, p * LANES:(p + 1) * LANES] = (dq_acc[p] * scale).astype(BF16)

        @pl.when(qi == nq - 1)
        def _():
            dk_ref[...] = dk_acc[...].astype(BF16)
            dv_ref[...] = dv_acc[...].astype(BF16)

        finish()

    q_in, k_in, v_in, q_out, kv_out, rows, krow = _pair_specs(s, blk, e, pp, 1)
    outs, sems = _peer_shapes(comm[0], comm[1]) if comm else ([], [])
    return pl.pallas_call(
        body, name="fox_bwd", grid=grid,
        in_specs=[q_in, k_in, v_in, krow, q_out, q_out, rows] + [ANY] * nc,
        out_specs=[q_out, kv_out, kv_out, krow] + [ANY] * nc,
        out_shape=[jax.ShapeDtypeStruct((t, e), BF16)] * 3 + [jax.ShapeDtypeStruct((b * e // HEAD_DIM, 1, s), F32)] + outs,
        scratch_shapes=[pltpu.VMEM((s, w), F32), pltpu.VMEM((s, w), F32)] + sems,
        compiler_params=_params("arbitrary", "arbitrary", "arbitrary"),
    )(qkv, qkv, qkv, ct, do, o, lse, *(comm[1] if comm else ()))


def _scan_rows(f2, group, mode, d2=None):
    n = f2.shape[0]

    def body(*refs):
        f_ref, o_ref = refs[0], refs[-1]
        f = f_ref[...]
        row, col = _iotas(LANES)
        grow = lax.broadcasted_iota(jnp.int32, (n, n), 0)
        gcol = lax.broadcasted_iota(jnp.int32, (n, n), 1)
        same = (grow // group) == (gcol // group)
        e = jnp.exp(-jnp.abs(f))
        if mode == "fwd":
            x = jnp.minimum(f, 0.0) - jnp.log1p(e)
            within = (row <= col).astype(F32)
            earlier = (same & (gcol < grow)).astype(F32)
        else:
            x = refs[1][...]
            within = (row >= col).astype(F32)
            earlier = (same & (gcol > grow)).astype(F32)
        y = jnp.dot(x, within, preferred_element_type=F32, precision=lax.Precision.HIGHEST)
        tot = jnp.sum(x, axis=1, keepdims=True)
        y = y + jnp.dot(earlier, tot, preferred_element_type=F32, precision=lax.Precision.HIGHEST)
        if mode == "bwd":
            r = 1.0 / (1.0 + e)
            y = y * jnp.where(f >= 0.0, e * r, r)
        o_ref[...] = y

    args = (f2,) if mode == "fwd" else (f2, d2)
    return pl.pallas_call(body, name="logf_" + mode, out_shape=jax.ShapeDtypeStruct(f2.shape, F32),
                          compiler_params=_params())(*args)


def _matmul(a, b, dims, *, tm, tn, tk, out_dtype, name, bias=None, res=None, res_scale=1.0, b_outer=False):
    def ij(g0, g1):
        return (g1, g0) if b_outer else (g0, g1)

    if dims == NN:
        (m, kk), n = a.shape, b.shape[1]
        a_spec = pl.BlockSpec((tm, tk), lambda g0, g1, k: (ij(g0, g1)[0], k))
        b_spec = pl.BlockSpec((tk, tn), lambda g0, g1, k: (k, ij(g0, g1)[1]))
    elif dims == NT:
        (m, kk), n = a.shape, b.shape[0]
        a_spec = pl.BlockSpec((tm, tk), lambda g0, g1, k: (ij(g0, g1)[0], k))
        b_spec = pl.BlockSpec((tn, tk), lambda g0, g1, k: (ij(g0, g1)[1], k))
    else:
        (kk, m), n = a.shape, b.shape[1]
        a_spec = pl.BlockSpec((tk, tm), lambda g0, g1, k: (k, ij(g0, g1)[0]))
        b_spec = pl.BlockSpec((tk, tn), lambda g0, g1, k: (k, ij(g0, g1)[1]))
    assert m % tm == 0 and n % tn == 0 and kk % tk == 0, (name, m, n, kk, tm, tn, tk)
    nk = kk // tk
    extras, extra_specs = [], []
    if bias is not None:
        extras.append(bias)
        extra_specs.append(pl.BlockSpec((1, tn), lambda g0, g1, k: (0, ij(g0, g1)[1])))
    if res is not None:
        extras.append(res)
        extra_specs.append(pl.BlockSpec((tm, tn), lambda g0, g1, k: ij(g0, g1)))

    def finish(out, rest, o_ref):
        idx = 0
        if bias is not None:
            out = out + rest[idx][...]
            idx += 1
        if res is not None:
            out = out + res_scale * rest[idx][...]
        o_ref[...] = out.astype(o_ref.dtype)

    def body_single(a_ref, b_ref, *rest):
        finish(_dot(a_ref[...].astype(BF16), b_ref[...].astype(BF16), dims), rest, rest[-1])

    def body_acc(a_ref, b_ref, *rest):
        o_ref, acc_ref = rest[-2], rest[-1]
        k = pl.program_id(2)
        part = _dot(a_ref[...].astype(BF16), b_ref[...].astype(BF16), dims)

        @pl.when(k == 0)
        def _():
            acc_ref[...] = part

        @pl.when(k > 0)
        def _():
            acc_ref[...] += part

        @pl.when(k == nk - 1)
        def _():
            finish(acc_ref[...], rest, o_ref)

    grid = (n // tn, m // tm, nk) if b_outer else (m // tm, n // tn, nk)
    return pl.pallas_call(
        body_single if nk == 1 else body_acc, name=name, grid=grid,
        in_specs=[a_spec, b_spec] + extra_specs,
        out_specs=pl.BlockSpec((tm, tn), lambda g0, g1, k: ij(g0, g1)),
        out_shape=jax.ShapeDtypeStruct((m, n), out_dtype),
        scratch_shapes=[] if nk == 1 else [pltpu.VMEM((tm, tn), F32)],
        compiler_params=_params("parallel", "parallel", "arbitrary"),
    )(a, b, *extras)


def _input_grad(dhs, ws, dr, tm, comm=None):
    t, d = dr.shape
    npc = len(dhs)
    nc = len(comm[1]) if comm else 0
    grid = (t // tm,)

    def body(*refs):
        dr_ref = refs[2 * npc]
        first_in = 2 * npc + 1
        o_ref = refs[first_in + nc]
        finish = _ride_along(comm, refs[first_in:first_in + nc], refs[first_in + nc + 1:first_in + 2 * nc + 1],
                             refs[first_in + 2 * nc + 1:], *_grid_ends(grid))
        out = ALPHA * dr_ref[...]
        for p in range(npc):
            out = out + _dot(refs[p][...], refs[npc + p][...], NT)
        o_ref[...] = out
        finish()

    rows = pl.BlockSpec((tm, d), lambda i: (i, 0))
    outs, sems = _peer_shapes(comm[0], comm[1]) if comm else ([], [])
    return pl.pallas_call(
        body, name="input_grad", grid=grid,
        in_specs=[pl.BlockSpec((tm, a.shape[1]), lambda i: (i, 0)) for a in dhs]
        + [pl.BlockSpec(w.shape, lambda i: (0, 0)) for w in ws] + [rows] + [ANY] * nc,
        out_specs=[rows] + [ANY] * nc,
        out_shape=[jax.ShapeDtypeStruct((t, d), F32)] + outs,
        scratch_shapes=sems,
        compiler_params=_params("arbitrary"),
    )(*dhs, *ws, dr, *(comm[1] if comm else ()))


def _sigmoid(x):
    e = jnp.exp(-jnp.abs(x))
    r = 1.0 / (1.0 + e)
    return jnp.where(x >= 0.0, r, e * r)


def _proj_gate_fwd(o_sb, o_fx, wp_sb, wp_fx, g, tm):
    t, e = o_sb.shape
    d = wp_sb.shape[1]

    def body(osb_ref, ofx_ref, wsb_ref, wfx_ref, gsb_ref, gfx_ref, mg_ref, ysb_ref, yfx_ref):
        ysb = _dot(osb_ref[...].astype(BF16), wsb_ref[...], NN)
        yfx = _dot(ofx_ref[...].astype(BF16), wfx_ref[...], NN)
        ysb_ref[...] = ysb
        yfx_ref[...] = yfx
        mg_ref[...] = (_sigmoid(gsb_ref[...]) * ysb + _sigmoid(gfx_ref[...]) * yfx).astype(BF16)

    rows_e = pl.BlockSpec((tm, e), lambda i: (i, 0))
    rows_d = pl.BlockSpec((tm, d), lambda i: (i, 0))
    w_spec = pl.BlockSpec((e, d), lambda i: (0, 0))
    return pl.pallas_call(
        body, name="proj_gate_fwd", grid=(t // tm,),
        in_specs=[rows_e, rows_e, w_spec, w_spec, rows_d, pl.BlockSpec((tm, d), lambda i: (i, 1))],
        out_specs=[rows_d, rows_d, rows_d],
        out_shape=[jax.ShapeDtypeStruct((t, d), BF16), jax.ShapeDtypeStruct((t, d), F32), jax.ShapeDtypeStruct((t, d), F32)],
        compiler_params=_params("parallel"),
    )(o_sb, o_fx, wp_sb, wp_fx, g, g)


def _gate_bwd(dmg, y_sb, y_fx, g, tm):
    t, d = dmg.shape

    def body(dm_ref, ysb_ref, yfx_ref, gsb_ref, gfx_ref, dysb_ref, dyfx_ref, dg_ref):
        dm = dm_ref[...]
        ssb = _sigmoid(gsb_ref[...])
        sfx = _sigmoid(gfx_ref[...])
        dysb_ref[...] = (dm * ssb).astype(BF16)
        dyfx_ref[...] = (dm * sfx).astype(BF16)
        dg_ref[:, 0:d] = (dm * ysb_ref[...] * ssb * (1.0 - ssb)).astype(BF16)
        dg_ref[:, d:2 * d] = (dm * yfx_ref[...] * sfx * (1.0 - sfx)).astype(BF16)

    rows = pl.BlockSpec((tm, d), lambda i: (i, 0))
    rows1 = pl.BlockSpec((tm, d), lambda i: (i, 1))
    return pl.pallas_call(
        body, name="gate_bwd", grid=(t // tm,),
        in_specs=[rows, rows, rows, rows, rows1],
        out_specs=[rows, rows, pl.BlockSpec((tm, 2 * d), lambda i: (i, 0))],
        out_shape=[jax.ShapeDtypeStruct((t, d), BF16)] * 2 + [jax.ShapeDtypeStruct((t, 2 * d), BF16)],
        compiler_params=_params("parallel"),
    )(dmg, y_sb, y_fx, g, g)


def _mm_res_ln(a, w, xres, gamma, beta, tm, name):
    t, kk = a.shape
    d = w.shape[1]

    def body(a_ref, w_ref, x_ref, g_ref, b_ref, xn_ref, xh_ref, rs_ref, xb_ref):
        r = ALPHA * x_ref[...] + _dot(a_ref[...].astype(BF16), w_ref[...], NN)
        mean = jnp.mean(r, axis=1, keepdims=True)
        cen = r - mean
        rstd = lax.rsqrt(jnp.mean(cen * cen, axis=1, keepdims=True) + LN_EPS)
        xh = cen * rstd
        xn = xh * g_ref[...] + b_ref[...]
        xh_ref[...] = xh
        xn_ref[...] = xn
        xb_ref[...] = xn.astype(BF16)
        rs_ref[...] = rstd

    rows_d = pl.BlockSpec((tm, d), lambda i: (i, 0))
    vec = pl.BlockSpec((1, d), lambda i: (0, 0))
    return pl.pallas_call(
        body, name=name, grid=(t // tm,),
        in_specs=[pl.BlockSpec((tm, kk), lambda i: (i, 0)), pl.BlockSpec((kk, d), lambda i: (0, 0)), rows_d, vec, vec],
        out_specs=[rows_d, rows_d, pl.BlockSpec((tm, 1), lambda i: (i, 0)), rows_d],
        out_shape=[jax.ShapeDtypeStruct((t, d), F32), jax.ShapeDtypeStruct((t, d), F32), jax.ShapeDtypeStruct((t, 1), F32),
                   jax.ShapeDtypeStruct((t, d), BF16)],
        compiler_params=_params("parallel"),
    )(a, w, xres, gamma, beta)


def _ln_bwd_math(dy, xh, rstd, gamma):
    dxh = dy * gamma
    m1 = jnp.mean(dxh, axis=1, keepdims=True)
    m2 = jnp.mean(dxh * xh, axis=1, keepdims=True)
    return rstd * (dxh - m1 - xh * m2)


def _rowsum8(x):
    tm, n = x.shape
    return jnp.sum(x.reshape(tm // SUBLANES, SUBLANES, n), axis=0)


def _fold8(ref):
    ref[0:1, :] = jnp.sum(ref[...], axis=0, keepdims=True)


def _loss_ln_bwd(x2, xh, rstd, gamma, target, tm):
    t, d = x2.shape

    def body(x_ref, xh_ref, rs_ref, g_ref, tg_ref, dr_ref, dg_ref, db_ref, ls_ref, drb_ref):
        @pl.when(pl.program_id(0) == 0)
        def _():
            dg_ref[...] = jnp.zeros_like(dg_ref)
            db_ref[...] = jnp.zeros_like(db_ref)
            ls_ref[...] = jnp.zeros_like(ls_ref)

        err = x_ref[...] - tg_ref[...]
        xh = xh_ref[...]
        dy = err * (1.0 / d)
        dr = _ln_bwd_math(dy, xh, rs_ref[...], g_ref[...])
        dr_ref[...] = dr
        drb_ref[...] = dr.astype(BF16)
        dg_ref[...] += _rowsum8(dy * xh)
        db_ref[...] += _rowsum8(dy)
        sq = _rowsum8(err * err)
        part = sq[:, 0:LANES]
        for j in range(1, d // LANES):
            part = part + sq[:, j * LANES:(j + 1) * LANES]
        ls_ref[...] += part * (0.5 / d)

        @pl.when(pl.program_id(0) == t // tm - 1)
        def _():
            _fold8(dg_ref)
            _fold8(db_ref)
            ls_ref[0:1, 0:1] = jnp.sum(jnp.sum(ls_ref[...], axis=0, keepdims=True), axis=1, keepdims=True)

    rows = pl.BlockSpec((tm, d), lambda i: (i, 0))
    acc = pl.BlockSpec((SUBLANES, d), lambda i: (0, 0))
    return pl.pallas_call(
        body, name="loss_ln_bwd", grid=(t // tm,),
        in_specs=[rows, rows, pl.BlockSpec((tm, 1), lambda i: (i, 0)), pl.BlockSpec((1, d), lambda i: (0, 0)), rows],
        out_specs=[rows, acc, acc, pl.BlockSpec((SUBLANES, LANES), lambda i: (0, 0)), rows],
        out_shape=[jax.ShapeDtypeStruct((t, d), F32), jax.ShapeDtypeStruct((SUBLANES, d), F32),
                   jax.ShapeDtypeStruct((SUBLANES, d), F32), jax.ShapeDtypeStruct((SUBLANES, LANES), F32),
                   jax.ShapeDtypeStruct((t, d), BF16)],
        compiler_params=_params("arbitrary"),
    )(x2, xh, rstd, gamma, target)


def _ln_bwd(dr_next, dlin, xh, rstd, gamma, tm):
    t, d = xh.shape

    def body(dn_ref, dl_ref, xh_ref, rs_ref, g_ref, dr_ref, dg_ref, db_ref, drb_ref):
        @pl.when(pl.program_id(0) == 0)
        def _():
            dg_ref[...] = jnp.zeros_like(dg_ref)
            db_ref[...] = jnp.zeros_like(db_ref)

        dy = ALPHA * dn_ref[...] + dl_ref[...]
        xh = xh_ref[...]
        dr = _ln_bwd_math(dy, xh, rs_ref[...], g_ref[...])
        dr_ref[...] = dr
        drb_ref[...] = dr.astype(BF16)
        dg_ref[...] += _rowsum8(dy * xh)
        db_ref[...] += _rowsum8(dy)

        @pl.when(pl.program_id(0) == t // tm - 1)
        def _():
            _fold8(dg_ref)
            _fold8(db_ref)

    rows = pl.BlockSpec((tm, d), lambda i: (i, 0))
    acc = pl.BlockSpec((SUBLANES, d), lambda i: (0, 0))
    return pl.pallas_call(
        body, name="ln_bwd", grid=(t // tm,),
        in_specs=[rows, rows, rows, pl.BlockSpec((tm, 1), lambda i: (i, 0)), pl.BlockSpec((1, d), lambda i: (0, 0))],
        out_specs=[rows, acc, acc, rows],
        out_shape=[jax.ShapeDtypeStruct((t, d), F32), jax.ShapeDtypeStruct((SUBLANES, d), F32),
                   jax.ShapeDtypeStruct((SUBLANES, d), F32), jax.ShapeDtypeStruct((t, d), BF16)],
        compiler_params=_params("arbitrary"),
    )(dr_next, dlin, xh, rstd, gamma)


def _shift_rows(x, halo, shift, row):
    out = pltpu.roll(x, shift, 0)
    n = halo.shape[0]
    for r in range(shift):
        out = jnp.where(row == r, halo[n - shift + r:n - shift + r + 1, :], out)
    return out


def _unshift_rows(x, halo, shift, row, tm):
    out = pltpu.roll(x, tm - shift, 0)
    for r in range(shift):
        out = jnp.where(row == tm - shift + r, halo[r:r + 1, :], out)
    return out


def _conv_pre(ug_ref, halo_ref, wc_ref, bc_ref, first, tm):
    ug = ug_ref[...].astype(F32)
    halo = jnp.where(first, 0.0, halo_ref[...].astype(F32))
    row = lax.broadcasted_iota(jnp.int32, ug.shape, 0)
    wc = wc_ref[...]
    um1 = _shift_rows(ug, halo, 1, row)
    um2 = _shift_rows(ug, halo, 2, row)
    c = bc_ref[...] + wc[2:3, :] * ug + wc[1:2, :] * um1 + wc[0:1, :] * um2
    return c, ug, um1, um2


INV_SQRT2 = 1.0 / math.sqrt(2.0)
INV_SQRT2PI = 1.0 / math.sqrt(2.0 * math.pi)


def _conv_glu_fwd(u, wc, bc, seq, tm):
    t, f2 = u.shape
    f = f2 // 2
    per_seq = seq // tm
    hb = tm // HALO

    def body(ug_ref, halo_ref, uv_ref, wc_ref, bc_ref, a_ref):
        first = (pl.program_id(0) % per_seq) == 0
        c, _, _, _ = _conv_pre(ug_ref, halo_ref, wc_ref, bc_ref, first, tm)
        gelu = 0.5 * c * (1.0 + lax.erf(c * INV_SQRT2))
        a_ref[...] = (gelu * uv_ref[...].astype(F32)).astype(BF16)

    return pl.pallas_call(
        body, name="conv_glu_fwd", grid=(t // tm,),
        in_specs=[pl.BlockSpec((tm, f), lambda i: (i, 0)),
                  pl.BlockSpec((HALO, f), lambda i: (jnp.maximum(i * hb - 1, 0), 0)),
                  pl.BlockSpec((tm, f), lambda i: (i, 1)),
                  pl.BlockSpec((3, f), lambda i: (0, 0)), pl.BlockSpec((1, f), lambda i: (0, 0))],
        out_specs=pl.BlockSpec((tm, f), lambda i: (i, 0)),
        out_shape=jax.ShapeDtypeStruct((t, f), BF16),
        compiler_params=_params("parallel"),
    )(u, u, u, wc, bc)


def _conv_glu_bwd1(u, da, wc, bc, seq, tm):
    t, f2 = u.shape
    f = f2 // 2
    per_seq = seq // tm
    hb = tm // HALO

    def body(ug_ref, halo_ref, uv_ref, da_ref, wc_ref, bc_ref, dc_ref, duv_ref):
        first = (pl.program_id(0) % per_seq) == 0
        c, _, _, _ = _conv_pre(ug_ref, halo_ref, wc_ref, bc_ref, first, tm)
        cdf = 0.5 * (1.0 + lax.erf(c * INV_SQRT2))
        pdf = jnp.exp(-0.5 * c * c) * INV_SQRT2PI
        da = da_ref[...].astype(F32)
        duv_ref[...] = (da * (c * cdf)).astype(BF16)
        dc_ref[...] = (da * uv_ref[...].astype(F32) * (cdf + c * pdf)).astype(BF16)

    rows = pl.BlockSpec((tm, f), lambda i: (i, 0))
    return pl.pallas_call(
        body, name="conv_glu_bwd1", grid=(t // tm,),
        in_specs=[rows, pl.BlockSpec((HALO, f), lambda i: (jnp.maximum(i * hb - 1, 0), 0)),
                  pl.BlockSpec((tm, f), lambda i: (i, 1)), rows,
                  pl.BlockSpec((3, f), lambda i: (0, 0)), pl.BlockSpec((1, f), lambda i: (0, 0))],
        out_specs=[rows, pl.BlockSpec((tm, f), lambda i: (i, 1))],
        out_shape=[jax.ShapeDtypeStruct((t, f), BF16), jax.ShapeDtypeStruct((t, f2), BF16)],
        compiler_params=_params("parallel"),
    )(u, u, u, da, wc, bc)


def _conv_glu_bwd2(u, dc, wc, du, seq, tm):
    t, f2 = u.shape
    f = f2 // 2
    per_seq = seq // tm
    hb = tm // HALO
    nblk = t // HALO

    def body(ug_ref, halo_ref, dc_ref, nxt_ref, wc_ref, du_in_ref, dug_ref, w0_ref, w1_ref, w2_ref, b_ref):
        i = pl.program_id(0)

        @pl.when(i == 0)
        def _():
            for r in (w0_ref, w1_ref, w2_ref, b_ref):
                r[...] = jnp.zeros_like(r)

        first = (i % per_seq) == 0
        last = (i % per_seq) == per_seq - 1
        ug = ug_ref[...].astype(F32)
        halo = jnp.where(first, 0.0, halo_ref[...].astype(F32))
        nxt = jnp.where(last, 0.0, nxt_ref[...].astype(F32))
        row = lax.broadcasted_iota(jnp.int32, ug.shape, 0)
        dc = dc_ref[...].astype(F32)
        wc = wc_ref[...]
        dp1 = _unshift_rows(dc, nxt, 1, row, tm)
        dp2 = _unshift_rows(dc, nxt, 2, row, tm)
        dug_ref[...] = (wc[2:3, :] * dc + wc[1:2, :] * dp1 + wc[0:1, :] * dp2).astype(BF16)
        w2_ref[...] += _rowsum8(dc * ug)
        w1_ref[...] += _rowsum8(dc * _shift_rows(ug, halo, 1, row))
        w0_ref[...] += _rowsum8(dc * _shift_rows(ug, halo, 2, row))
        b_ref[...] += _rowsum8(dc)

        @pl.when(i == t // tm - 1)
        def _():
            for r in (w0_ref, w1_ref, w2_ref, b_ref):
                _fold8(r)

    rows = pl.BlockSpec((tm, f), lambda i: (i, 0))
    acc = pl.BlockSpec((SUBLANES, f), lambda i: (0, 0))
    return pl.pallas_call(
        body, name="conv_glu_bwd2", grid=(t // tm,),
        in_specs=[rows, pl.BlockSpec((HALO, f), lambda i: (jnp.maximum(i * hb - 1, 0), 0)),
                  rows, pl.BlockSpec((HALO, f), lambda i: (jnp.minimum((i + 1) * hb, nblk - 1), 0)),
                  pl.BlockSpec((3, f), lambda i: (0, 0)), pl.BlockSpec(memory_space=pl.ANY)],
        out_specs=[rows, acc, acc, acc, acc],
        out_shape=[jax.ShapeDtypeStruct((t, f2), BF16)] + [jax.ShapeDtypeStruct((SUBLANES, f), F32)] * 4,
        input_output_aliases={5: 0},
        compiler_params=_params("arbitrary"),
    )(u, u, dc, dc, wc, du)


def _colsum(x, tm, name):
    t, n = x.shape

    def body(x_ref, o_ref):
        @pl.when(pl.program_id(0) == 0)
        def _():
            o_ref[...] = jnp.zeros_like(o_ref)

        o_ref[...] += _rowsum8(x_ref[...].astype(F32))

        @pl.when(pl.program_id(0) == t // tm - 1)
        def _():
            _fold8(o_ref)

    return pl.pallas_call(
        body, name=name, grid=(t // tm,),
        in_specs=[pl.BlockSpec((tm, n), lambda i: (i, 0))],
        out_specs=pl.BlockSpec((SUBLANES, n), lambda i: (0, 0)),
        out_shape=jax.ShapeDtypeStruct((SUBLANES, n), F32),
        compiler_params=_params("arbitrary"),
    )(x)


def _adamw(w, gparts, m, v, name):
    p, r, c = gparts.shape
    tr = r
    for cand in (512, 256, 128, 64, 32, 16, 8):
        if cand * p <= 1024 and r % cand == 0 and r > cand:
            tr = cand
            break
    c1 = 1.0 - ADAM_B1 ** ADAM_STEP
    c2 = 1.0 - ADAM_B2 ** ADAM_STEP

    def body(w_ref, g_ref, m_ref, v_ref, go_ref, d_ref, mo_ref, vo_ref):
        g = g_ref[0].astype(F32)
        for i in range(1, p):
            g = g + g_ref[i].astype(F32)
        mn = ADAM_B1 * m_ref[...] + (1.0 - ADAM_B1) * g
        vn = ADAM_B2 * v_ref[...] + (1.0 - ADAM_B2) * (g * g)
        go_ref[...] = g
        mo_ref[...] = mn
        vo_ref[...] = vn
        d_ref[...] = -ADAM_LR * ((mn / c1) / (jnp.sqrt(vn / c2) + ADAM_EPS) + ADAM_WD * w_ref[...])

    blk = pl.BlockSpec((tr, c), lambda i: (i, 0))
    return pl.pallas_call(
        body, name=name, grid=(r // tr,),
        in_specs=[blk, pl.BlockSpec((p, tr, c), lambda i: (0, i, 0)), blk, blk],
        out_specs=[blk] * 4,
        out_shape=[jax.ShapeDtypeStruct((r, c), F32)] * 4,
        compiler_params=_params("parallel"),
    )(w, gparts, m, v)


MESH = pl.DeviceIdType.MESH
ANY = pl.BlockSpec(memory_space=pl.ANY)


def _all_gather(xs, name):
    n = len(xs)

    def body(*refs):
        x_refs, out_refs = refs[:n], refs[n:2 * n]
        send_sems, recv_sems, local_sems = refs[2 * n:]
        x, y, c = lax.axis_index("x"), lax.axis_index("y"), lax.axis_index("c")
        me, sibling = (x, y, c), (x, y, 1 - c)
        chips = [(1 - x, y), (x, 1 - y), (1 - x, 1 - y)]

        def slot(a, px, py, pc):
            return out_refs[a].at[4 * px + 2 * py + pc]

        def copy(a, k, block, to, src=None):
            return pltpu.make_async_remote_copy(
                src_ref=slot(a, *block) if src is None else src, dst_ref=slot(a, *block),
                send_sem=send_sems.at[k * n + a], recv_sem=recv_sems.at[k * n + a], device_id=to, device_id_type=MESH)

        arrays = range(n)
        mine = [pltpu.make_async_copy(x_refs[a], slot(a, *me), local_sems.at[a]) for a in arrays]
        first = [copy(a, 0, me, sibling, src=x_refs[a]) for a in arrays]
        first += [copy(a, 1 + j, me, (*chip, c), src=x_refs[a]) for j, chip in enumerate(chips) for a in arrays]
        for cp in mine + first:
            cp.start()
        passed = []
        for j, chip in enumerate(chips):
            for a in arrays:
                copy(a, 1 + j, (*chip, c), me).wait_recv()
                passed.append(copy(a, 4 + j, (*chip, c), sibling))
                passed[-1].start()
        for a in arrays:
            copy(a, 0, sibling, me).wait_recv()
        for j, chip in enumerate(chips):
            for a in arrays:
                copy(a, 4 + j, (*chip, 1 - c), me).wait_recv()
        for cp in first + passed:
            cp.wait_send()
        for cp in mine:
            cp.wait()

    return pl.pallas_call(
        body, name=name,
        out_shape=[jax.ShapeDtypeStruct((N_DEV,) + x.shape, x.dtype) for x in xs],
        in_specs=[ANY] * n, out_specs=[ANY] * n,
        scratch_shapes=[pltpu.SemaphoreType.DMA((7 * n,)), pltpu.SemaphoreType.DMA((7 * n,)),
                        pltpu.SemaphoreType.DMA((n,))],
    )(*xs)


def _peer_copies(kind, src_refs, dst_refs, send_sems, recv_sems, local_sems):
    n = len(src_refs)
    x, y, c = lax.axis_index("x"), lax.axis_index("y"), lax.axis_index("c")
    mine = 4 * x + 2 * y + c

    def src(a, idx):
        return src_refs[a] if kind == "spread" else src_refs[a].at[idx]

    copies = [pltpu.make_async_copy(src(a, mine), dst_refs[a].at[mine], local_sems.at[a]) for a in range(n)]
    for k in range(1, N_DEV):
        px = 1 - x if k & 4 else x
        py = 1 - y if k & 2 else y
        pc = 1 - c if k & 1 else c
        for a in range(n):
            copies.append(pltpu.make_async_remote_copy(
                src_ref=src(a, 4 * px + 2 * py + pc), dst_ref=dst_refs[a].at[mine],
                send_sem=send_sems.at[(k - 1) * n + a], recv_sem=recv_sems.at[(k - 1) * n + a],
                device_id=(px, py, pc), device_id_type=MESH))
    return copies


def _peer_shapes(kind, arrays):
    n = len(arrays)
    outs = [jax.ShapeDtypeStruct(((N_DEV,) + a.shape) if kind == "spread" else a.shape, a.dtype) for a in arrays]
    sems = [pltpu.SemaphoreType.DMA((7 * n,)), pltpu.SemaphoreType.DMA((7 * n,)), pltpu.SemaphoreType.DMA((n,))]
    return outs, sems


def _exchange(gs, name):
    n = len(gs)

    def body(*refs):
        copies = _peer_copies("exchange", refs[:n], refs[n:2 * n], *refs[2 * n:])
        for cp in copies:
            cp.start()
        for cp in copies:
            cp.wait()

    outs, sems = _peer_shapes("exchange", gs)
    return pl.pallas_call(body, name=name, out_shape=outs, in_specs=[ANY] * n, out_specs=[ANY] * n,
                          scratch_shapes=sems)(*gs)


def _tile(n, pref, unit=LANES):
    if n <= pref:
        return n
    best = None
    for cand in range(unit, pref + 1, unit):
        if n % cand == 0:
            best = cand
    assert best is not None, (n, pref, unit)
    return best


LATE_KEYS = dict(w_proj_sb="wp_sb", w_proj_fox="wp_fx", w_out="w_out", w_up="w_up", w_conv="w_conv", w_down="w_down")


def _layer_step(x, target, w, attn_blk, late=None):
    b, s, d = x.shape
    t = b * s
    w = dict(w)
    e = w["w_qkv"].shape[1] // 6
    h = e // HEAD_DIM
    f = w["b_conv"].shape[1]
    x2 = x.reshape(t, d)
    tg = target.reshape(t, d)
    pp = 2 if (e // LANES) % 2 == 0 else 1
    fox_blk = min(2 * attn_blk, s)
    tm = _tile(t, 512, HALO)
    tmo = _tile(t, 1024, HALO)
    tmc = _tile(s, 256, HALO)
    tkt = _tile(t, 2048, HALO)
    td = _tile(d, 1024)
    tf = _tile(f, 1408)
    t2f = _tile(2 * f, 1408)
    tqkv = _tile(6 * e, 1024)
    tg2 = _tile(2 * d, 1024)
    xb = x2.astype(BF16)

    qkv = _matmul(xb, w["w_qkv"], NN, tm=tmo, tn=tqkv, tk=d, out_dtype=BF16, name="in_qkv", bias=w["b_qkv"], b_outer=True)
    gate = _matmul(xb, jnp.concatenate([w["w_g"], w["w_f"]], axis=1), NN, tm=tmo, tn=2 * d + LANES, tk=d, out_dtype=F32,
                   name="in_gate", bias=jnp.concatenate([w["b_g"], w["b_f"]], axis=1), b_outer=True)
    nr = s // LANES
    f2 = gate[:, 2 * d:2 * d + h].reshape(b, s, h).transpose(0, 2, 1).reshape(b * h * nr, LANES)
    ct = _scan_rows(f2, nr, "fwd").reshape(b * h, 1, s)
    o_sb, tot, first = _sb_fwd(qkv, b, s, e, attn_blk, pp)
    if late is None:
        o_fx, lse = _fox_fwd(qkv, ct, b, s, e, fox_blk, 1)
    else:
        o_fx, lse, *gathered = _fox_fwd(qkv, ct, b, s, e, fox_blk, 1, comm=("spread", late[1]))
        for name, g in zip(late[0], gathered):
            w[LATE_KEYS[name]] = _join(g, name)
    merged, y_sb, y_fx = _proj_gate_fwd(o_sb, o_fx, w["wp_sb"], w["wp_fx"], gate, tm)
    x1, xh1, rs1, x1b = _mm_res_ln(merged, w["w_out"], x2, w["ln1_g"], w["ln1_b"], tm, "out_ln1")
    u = _matmul(x1b, w["w_up"], NN, tm=tmo, tn=t2f, tk=d, out_dtype=BF16, name="ffn_up", b_outer=True)
    act = _conv_glu_fwd(u, w["w_conv"], w["b_conv"], s, tmc)
    xo, xh2, rs2, _ = _mm_res_ln(act, w["w_down"], x1, w["ln2_g"], w["ln2_b"], tm, "down_ln2")

    gr = {}
    dr2, dg2, db2, ls, dr2b = _loss_ln_bwd(xo, xh2, rs2, w["ln2_g"], tg, tm)
    gr["ln2_g"], gr["ln2_b"] = dg2[0:1], db2[0:1]
    da = _matmul(dr2b, w["w_down"], NT, tm=tmo, tn=tf, tk=d, out_dtype=BF16, name="d_act", b_outer=True)
    gr["w_down"] = _matmul(act, dr2b, TN, tm=tf, tn=td, tk=tkt, out_dtype=BF16, name="dw_down")
    dc, du = _conv_glu_bwd1(u, da, w["w_conv"], w["b_conv"], s, tmc)
    du, gw0, gw1, gw2, gbc = _conv_glu_bwd2(u, dc, w["w_conv"], du, s, tmc)
    gr["w_conv"] = jnp.concatenate([gw0[0:1], gw1[0:1], gw2[0:1]], axis=0)
    gr["b_conv"] = gbc[0:1]
    dlin1 = _matmul(du, w["w_up"], NT, tm=tm, tn=td, tk=2 * f, out_dtype=F32, name="d_x1")
    gr["w_up"] = _matmul(x1b, du, TN, tm=td, tn=t2f, tk=tkt, out_dtype=BF16, name="dw_up")
    dr1, dg1, db1, dr1b = _ln_bwd(dr2, dlin1, xh1, rs1, w["ln1_g"], tm)
    gr["ln1_g"], gr["ln1_b"] = dg1[0:1], db1[0:1]
    dmg = _matmul(dr1b, w["w_out"], NT, tm=tm, tn=td, tk=d, out_dtype=F32, name="d_merged")
    gr["w_out"] = _matmul(merged, dr1b, TN, tm=td, tn=td, tk=tkt, out_dtype=BF16, name="dw_out")
    dy_sb, dy_fx, dgate = _gate_bwd(dmg, y_sb, y_fx, gate, tm)
    do_sb = _matmul(dy_sb, w["wp_sb"], NT, tm=tm, tn=e, tk=d, out_dtype=BF16, name="d_o_sb")
    do_fx = _matmul(dy_fx, w["wp_fx"], NT, tm=tm, tn=e, tk=d, out_dtype=BF16, name="d_o_fx")
    gr["wp_sb"] = _matmul(o_sb, dy_sb, TN, tm=e, tn=td, tk=tkt, out_dtype=BF16, name="dwp_sb")
    gr["wp_fx"] = _matmul(o_fx, dy_fx, TN, tm=e, tn=td, tk=tkt, out_dtype=BF16, name="dwp_fx")
    dq_sb, dk_sb, dv_sb = _sb_bwd(qkv, do_sb, tot, first, b, s, e, attn_blk, pp)
    landed = None
    if late is None:
        dq_fx, dk_fx, dv_fx, dct = _fox_bwd(qkv, ct, do_fx, o_fx, lse, b, s, e, fox_blk, 1)
    else:
        blocks = [_cut(gr[LATE_KEYS[n]], n).astype(BF16 if n in MATMUL_OPERANDS else F32) for n in late[0]]
        dq_fx, dk_fx, dv_fx, dct, *got = _fox_bwd(qkv, ct, do_fx, o_fx, lse, b, s, e, fox_blk, 1,
                                                   comm=("exchange", blocks))
        landed = dict(zip(late[0], got))
    dqkv = jnp.concatenate([dq_sb, dk_sb, dv_sb, dq_fx, dk_fx, dv_fx], axis=1)
    df2 = _scan_rows(f2, nr, "bwd", dct.reshape(b * h * nr, LANES))
    df = jnp.pad(df2.reshape(b, h, s).transpose(0, 2, 1).reshape(t, h), ((0, 0), (0, LANES - h))).astype(BF16)
    gr["w_qkv"] = _matmul(xb, dqkv, TN, tm=td, tn=tqkv, tk=tkt, out_dtype=BF16, name="dw_qkv")
    gr["w_g"] = _matmul(xb, dgate, TN, tm=td, tn=tg2, tk=tkt, out_dtype=BF16, name="dw_gate")
    gr["w_f"] = _matmul(xb, df, TN, tm=td, tn=LANES, tk=tkt, out_dtype=BF16, name="dw_forget")
    gr["b_qkv"] = _colsum(dqkv, tm, "db_qkv")[0:1]
    gr["b_g"] = _colsum(dgate, tm, "db_gate")[0:1]
    gr["b_f"] = _colsum(df, tm, "db_forget")[0:1]
    comm = None
    if late is not None:
        small = jnp.concatenate([_w_in_layout(gr["b_qkv"], gr["b_f"], gr["b_g"], h), gr["ln1_g"], gr["ln1_b"],
                                 gr["b_conv"], gr["ln2_g"], gr["ln2_b"]], axis=1)
        comm = ("exchange", [_cut(_w_in_layout(gr["w_qkv"], gr["w_f"], gr["w_g"], h), "w_in").astype(BF16),
                             jnp.broadcast_to(small[None], (N_DEV,) + small.shape)])
    dx, *got = _input_grad([dqkv, dgate, df], [w["w_qkv"], w["w_g"], w["w_f"]], dr1, tm, comm)
    if late is not None:
        landed["w_in"], landed["replicated"] = got
    return ls[0:1, 0:1], dx.reshape(b, s, d), gr, landed


SHARDED = ("w_in", "w_proj_sb", "w_proj_fox", "w_out", "w_up", "w_conv", "w_down")
ROW_SHARDED = ("w_out", "w_down")
REPLICATED = ("b_in", "ln1_g", "ln1_b", "b_conv", "ln2_g", "ln2_b")
WEIGHTS = ("w_in", "b_in", "w_proj_sb", "w_proj_fox", "w_out", "ln1_g", "ln1_b", "w_up", "w_conv", "b_conv",
           "w_down", "ln2_g", "ln2_b")
MATMUL_OPERANDS = ("w_in", "w_proj_sb", "w_proj_fox", "w_out", "w_up", "w_down")


def _w_in_layout(g_qkv, g_f, g_g, h):
    return jnp.concatenate([g_qkv, g_f[:, :h], g_g], axis=1)


def _cut(full, name):
    r, c = full.shape
    if name in ROW_SHARDED:
        return full.reshape(N_DEV, r // N_DEV, c)
    cs = c // N_DEV
    return jnp.stack([full[:, j * cs:(j + 1) * cs] for j in range(N_DEV)], axis=0)


def _join(blocks, name):
    p, r, c = blocks.shape
    if name in ROW_SHARDED:
        return blocks.reshape(p * r, c)
    return jnp.concatenate([blocks[j] for j in range(p)], axis=1)


def kernel(x, w_in, b_in, w_proj_sb, w_proj_fox, w_out, ln1_g, ln1_b, w_up, w_conv, b_conv, w_down, ln2_g, ln2_b, loss_target, m_w_in, m_b_in, m_w_proj_sb, m_w_proj_fox, m_w_out, m_ln1_g, m_ln1_b, m_w_up, m_w_conv, m_b_conv, m_w_down, m_ln2_g, m_ln2_b, v_w_in, v_b_in, v_w_proj_sb, v_w_proj_fox, v_w_out, v_ln1_g, v_ln1_b, v_w_up, v_w_conv, v_b_conv, v_w_down, v_ln2_g, v_ln2_b):
    wts = dict(w_in=w_in, b_in=b_in, w_proj_sb=w_proj_sb, w_proj_fox=w_proj_fox, w_out=w_out, ln1_g=ln1_g, ln1_b=ln1_b,
               w_up=w_up, w_conv=w_conv, b_conv=b_conv, w_down=w_down, ln2_g=ln2_g, ln2_b=ln2_b)
    mom = dict(w_in=m_w_in, b_in=m_b_in, w_proj_sb=m_w_proj_sb, w_proj_fox=m_w_proj_fox, w_out=m_w_out, ln1_g=m_ln1_g,
               ln1_b=m_ln1_b, w_up=m_w_up, w_conv=m_w_conv, b_conv=m_b_conv, w_down=m_w_down, ln2_g=m_ln2_g, ln2_b=m_ln2_b)
    var = dict(w_in=v_w_in, b_in=v_b_in, w_proj_sb=v_w_proj_sb, w_proj_fox=v_w_proj_fox, w_out=v_w_out, ln1_g=v_ln1_g,
               ln1_b=v_ln1_b, w_up=v_w_up, w_conv=v_w_conv, b_conv=v_b_conv, w_down=v_w_down, ln2_g=v_ln2_g, ln2_b=v_ln2_b)
    shard = {n: wts[n].reshape(wts[n].shape[-2:]) for n in WEIGHTS}

    w_in_full = _join(_all_gather([shard["w_in"].astype(BF16)], "gather_w_in")[0], "w_in")
    late_names = [n for n in SHARDED if n != "w_in"]
    late = (late_names, [shard[n].astype(BF16) if n in MATMUL_OPERANDS else shard[n] for n in late_names])
    e = shard["w_proj_sb"].shape[0]
    h = e // HEAD_DIM
    nq = 6 * e

    def cut_in(a, pad):
        fcols = a[:, nq:nq + h]
        if pad:
            fcols = jnp.pad(fcols, ((0, 0), (0, LANES - h)))
        return a[:, :nq], a[:, nq + h:], fcols

    w_qkv, w_g, w_f = cut_in(w_in_full, True)
    b_qkv, b_g, b_f = cut_in(shard["b_in"], True)
    w = dict(w_qkv=w_qkv, w_g=w_g, w_f=w_f, b_qkv=b_qkv, b_g=b_g, b_f=b_f, b_conv=shard["b_conv"],
             ln1_g=shard["ln1_g"], ln1_b=shard["ln1_b"], ln2_g=shard["ln2_g"], ln2_b=shard["ln2_b"])

    loss_local, grad_x, gr, gsum = _layer_step(x, loss_target, w, min(256, x.shape[1]), late)
    loss = lax.psum(loss_local[0, 0], ("x", "y", "c"))

    parts = gsum.pop("replicated")
    off = 0
    for n in REPLICATED:
        gsum[n] = parts[:, :, off:off + shard[n].size]
        off += shard[n].size

    grads, deltas, new_m, new_v = [], [], [], []
    for n in WEIGHTS:
        shp = wts[n].shape
        g, dl, mn, vn = _adamw(shard[n], gsum[n], mom[n].reshape(shard[n].shape), var[n].reshape(shard[n].shape),
                               "adamw_" + n)
        grads.append(g.reshape(shp))
        deltas.append(dl.reshape(shp))
        new_m.append(mn.reshape(shp))
        new_v.append(vn.reshape(shp))
    return (loss, grad_x, *grads, *deltas, *new_m, *new_v)
```

```python
import functools
import math

import jax
import jax.numpy as jnp
from jax import lax
from jax.experimental import pallas as pl
from jax.experimental.pallas import tpu as pltpu

F32 = jnp.float32
BF16 = jnp.bfloat16

HEAD_DIM = 64
LN_EPS = 1e-5
DEPTH = 1
ALPHA = (2.0 * DEPTH) ** 0.25
ADAM_LR, ADAM_B1, ADAM_B2, ADAM_EPS, ADAM_WD, ADAM_STEP = 0.001, 0.9, 0.999, 1e-08, 0.01, 10
N_DEV = 8
LANES = 128
SUBLANES = 8
HALO = 16
VMEM_LIMIT = 56 * 1024 * 1024

NN = ((1,), (0,))
NT = ((1,), (1,))
TN = ((0,), (0,))


def _dot(a, b, dims):
    return lax.dot_general(a, b, (dims, ((), ())), preferred_element_type=F32)


def _params(*sem):
    return pltpu.CompilerParams(dimension_semantics=sem, vmem_limit_bytes=VMEM_LIMIT)


def _iotas(blk):
    row = lax.broadcasted_iota(jnp.int32, (blk, blk), 0)
    col = lax.broadcasted_iota(jnp.int32, (blk, blk), 1)
    return row, col


def _sb_terms(z):
    e = jnp.exp(-jnp.abs(z))
    lb = jnp.minimum(z, 0.0) - jnp.log(1.0 + e)
    return lb, lb - z, e


def _pair_specs(s, blk, e, pp, branch):
    w = pp * LANES
    nq = s // blk
    ng = e // w
    base = 3 * branch * ng
    q_in = pl.BlockSpec((blk, w), lambda b, g, i: (b * nq + i, base + g))
    k_in = pl.BlockSpec((s, w), lambda b, g, i: (b, base + ng + g))
    v_in = pl.BlockSpec((s, w), lambda b, g, i: (b, base + 2 * ng + g))
    q_out = pl.BlockSpec((blk, w), lambda b, g, i: (b * nq + i, g))
    kv_out = pl.BlockSpec((s, w), lambda b, g, i: (b, g))
    rows = pl.BlockSpec((2 * pp, blk, 1), lambda b, g, i: (b * ng + g, i, 0))
    krow = pl.BlockSpec((2 * pp, 1, s), lambda b, g, i: (b * ng + g, 0, 0))
    return q_in, k_in, v_in, q_out, kv_out, rows, krow


def _to_column(r):
    row, col = _iotas(r.shape[1])
    return jnp.sum(jnp.where(row == col, r, 0.0), axis=1, keepdims=True)


def _half_masks(x):
    low = lax.broadcasted_iota(jnp.int32, x.shape, 1) < HEAD_DIM
    zero = jnp.zeros_like(x)
    return jnp.concatenate([jnp.where(low, x, zero), jnp.where(low, zero, x)], axis=0)


def _tri_sums(xs, tri):
    hi = [x.astype(BF16) for x in xs]
    lo = [(x - h.astype(F32)).astype(BF16) for x, h in zip(xs, hi)]
    n = len(xs)
    blk = xs[0].shape[0]
    r = _dot(jnp.concatenate(hi + lo, axis=0), tri, NN)
    return [r[i * blk:(i + 1) * blk] + r[(n + i) * blk:(n + i + 1) * blk] for i in range(n)]


def _sb_fwd(qkv, b, s, e, blk, pp):
    scale = HEAD_DIM ** -0.5
    nh = 2 * pp
    t = b * s

    def body(q_ref, k_ref, v_ref, o_ref, tot_ref, first_ref):
        qi = pl.program_id(2)
        qm = [_half_masks((q_ref[:, p * LANES:(p + 1) * LANES] * scale).astype(BF16)) for p in range(pp)]
        row, col = _iotas(blk)
        strict = col < row
        after = (row > col).astype(BF16)

        def block(j, carry, diag):
            off = pl.multiple_of(j * blk, blk)
            o_acc, run = carry
            zz = [_dot(qm[p], k_ref[pl.ds(off, blk), p * LANES:(p + 1) * LANES], NT) for p in range(pp)]
            z = [zz[h // 2][(h % 2) * blk:(h % 2 + 1) * blk] for h in range(nh)]
            terms = [_sb_terms(z[h]) for h in range(nh)]
            lom = [jnp.where(strict, terms[h][1], 0.0) if diag else terms[h][1] for h in range(nh)]
            sfx = _tri_sums(lom, after)
            a = [jnp.exp(terms[h][0] + sfx[h] + run[h]) for h in range(nh)]
            if diag:
                a = [jnp.where(strict, a[h], 0.0) for h in range(nh)]
            ab = [a[h].astype(BF16) for h in range(nh)]
            o_new = tuple(
                o_acc[p] + _dot(jnp.concatenate([ab[2 * p], ab[2 * p + 1]], axis=1),
                                _half_masks(v_ref[pl.ds(off, blk), p * LANES:(p + 1) * LANES]), NN)
                for p in range(pp))
            return o_new, tuple(run[h] + sfx[h][:, 0:1] + lom[h][:, 0:1] for h in range(nh))

        def alive(run):
            m = run[0]
            for h in range(1, nh):
                m = jnp.maximum(m, run[h])
            return jnp.max(m) > DEAD

        o_acc, run = block(qi, ((jnp.zeros((blk, LANES), F32),) * pp, (jnp.zeros((blk, 1), F32),) * nh), True)

        def step(c):
            j, _, o_acc, run = c
            o_acc, run = block(j, (o_acc, run), False)
            return j - 1, alive(run), o_acc, run

        j, _, o_acc, run = lax.while_loop(lambda c: jnp.logical_and(c[0] >= 0, c[1]), step,
                                          (qi - 1, alive(run), o_acc, run))
        for p in range(pp):
            o_ref[:, p * LANES:(p + 1) * LANES] = o_acc[p].astype(o_ref.dtype)
        for h in range(nh):
            tot_ref[h] = run[h]
            first_ref[h] = jnp.zeros((blk, 1), F32) + (j + 1).astype(F32)

    q_in, k_in, v_in, q_out, _, rows, _ = _pair_specs(s, blk, e, pp, 0)
    return pl.pallas_call(
        body, name="sb_fwd", grid=(b, e // (pp * LANES), s // blk),
        in_specs=[q_in, k_in, v_in], out_specs=[q_out, rows, rows],
        out_shape=[jax.ShapeDtypeStruct((t, e), BF16)] + [jax.ShapeDtypeStruct((b * e // HEAD_DIM, s, 1), F32)] * 2,
        compiler_params=_params("parallel", "parallel", "arbitrary"),
    )(qkv, qkv, qkv)


def _sb_bwd(qkv, do, tot, first, b, s, e, blk, pp):
    scale = HEAD_DIM ** -0.5
    nh = 2 * pp
    t = b * s
    nq = s // blk

    def body(q_ref, k_ref, v_ref, do_ref, tot_ref, first_ref, dq_ref, dk_ref, dv_ref, dk_acc, dv_acc):
        qi = pl.program_id(2)

        @pl.when(qi == 0)
        def _():
            dk_acc[...] = jnp.zeros_like(dk_acc)
            dv_acc[...] = jnp.zeros_like(dv_acc)

        qm = [_half_masks((q_ref[:, p * LANES:(p + 1) * LANES] * scale).astype(BF16)) for p in range(pp)]
        dom = [_half_masks(do_ref[:, p * LANES:(p + 1) * LANES].astype(BF16)) for p in range(pp)]
        tot_t = [tot_ref[h] for h in range(nh)]
        row, col = _iotas(blk)
        strict = col < row
        upto = (row <= col).astype(BF16)
        before = (row < col).astype(BF16)

        def block(j, carry, diag):
            off = pl.multiple_of(j * blk, blk)
            dq_acc, cl, cg = carry
            kp = [k_ref[pl.ds(off, blk), p * LANES:(p + 1) * LANES] for p in range(pp)]
            vp = [v_ref[pl.ds(off, blk), p * LANES:(p + 1) * LANES] for p in range(pp)]
            zz = [_dot(qm[p], kp[p], NT) for p in range(pp)]
            dd = [_dot(dom[p], vp[p], NT) for p in range(pp)]
            z = [zz[h // 2][(h % 2) * blk:(h % 2 + 1) * blk] for h in range(nh)]
            da = [dd[h // 2][(h % 2) * blk:(h % 2 + 1) * blk] for h in range(nh)]
            terms = [_sb_terms(z[h]) for h in range(nh)]
            lom = [jnp.where(strict, terms[h][1], 0.0) if diag else terms[h][1] for h in range(nh)]
            pre = _tri_sums(lom, upto)
            a = [jnp.exp(terms[h][0] + (tot_t[h] - cl[h] - pre[h])) for h in range(nh)]
            if diag:
                a = [jnp.where(strict, a[h], 0.0) for h in range(nh)]
            g = [a[h] * da[h] for h in range(nh)]
            pw = _tri_sums(g, before)
            dzb = []
            for h in range(nh):
                ex = terms[h][2]
                r = 1.0 / (1.0 + ex)
                er = ex * r
                pos = z[h] >= 0.0
                dz = g[h] * jnp.where(pos, er, r) - (cg[h] + pw[h]) * jnp.where(pos, r, er)
                if diag:
                    dz = jnp.where(strict, dz, 0.0)
                dzb.append(dz.astype(BF16))
            ab = [a[h].astype(BF16) for h in range(nh)]
            for p in range(pp):
                cols = slice(p * LANES, (p + 1) * LANES)
                dk_acc[pl.ds(off, blk), cols] += _dot(jnp.concatenate([dzb[2 * p], dzb[2 * p + 1]], axis=0), qm[p], TN)
                dv_acc[pl.ds(off, blk), cols] += _dot(jnp.concatenate([ab[2 * p], ab[2 * p + 1]], axis=0), dom[p], TN)
            dq_new = tuple(dq_acc[p] + _dot(jnp.concatenate([dzb[2 * p], dzb[2 * p + 1]], axis=1), _half_masks(kp[p]), NN)
                           for p in range(pp))
            return (dq_new, tuple(cl[h] + pre[h][:, blk - 1:blk] for h in range(nh)),
                    tuple(cg[h] + pw[h][:, blk - 1:blk] + g[h][:, blk - 1:blk] for h in range(nh)))

        zero1 = (jnp.zeros((blk, 1), F32),) * nh
        j0 = jnp.clip(jnp.max(first_ref[0]).astype(jnp.int32), 0, qi)
        carry = lax.fori_loop(j0, qi, lambda j, c: block(j, c, False), ((jnp.zeros((blk, LANES), F32),) * pp, zero1, zero1))
        dq_acc, _, _ = block(qi, carry, True)
        for p in range(pp):
            dq_ref[:, p * LANES:(p + 1) * LANES] = (dq_acc[p] * scale).astype(BF16)

        @pl.when(qi == nq - 1)
        def _():
            dk_ref[...] = dk_acc[...].astype(BF16)
            dv_ref[...] = dv_acc[...].astype(BF16)

    q_in, k_in, v_in, q_out, kv_out, rows, _ = _pair_specs(s, blk, e, pp, 0)
    w = pp * LANES
    return pl.pallas_call(
        body, name="sb_bwd", grid=(b, e // w, nq),
        in_specs=[q_in, k_in, v_in, q_out, rows, rows], out_specs=[q_out, kv_out, kv_out],
        out_shape=[jax.ShapeDtypeStruct((t, e), BF16)] * 3,
        scratch_shapes=[pltpu.VMEM((s, w), F32), pltpu.VMEM((s, w), F32)],
        compiler_params=_params("parallel", "parallel", "arbitrary"),
    )(qkv, qkv, qkv, do, tot, first)


NEG = -1e30
DEAD = -110.0


def _ride_along(comm, src_refs, dst_refs, sems, first, last):
    if comm is None:
        return lambda: None

    @pl.when(first)
    def _():
        for cp in _peer_copies(comm[0], src_refs, dst_refs, *sems):
            cp.start()

    def finish():
        @pl.when(last)
        def _():
            for cp in _peer_copies(comm[0], src_refs, dst_refs, *sems):
                cp.wait()

    return finish


def _grid_ends(grid):
    ids = [pl.program_id(a) for a in range(len(grid))]
    first = functools.reduce(jnp.logical_and, [i == 0 for i in ids])
    last = functools.reduce(jnp.logical_and, [i == g - 1 for i, g in zip(ids, grid)])
    return first, last


def _fox_fwd(qkv, ct, b, s, e, blk, pp, comm=None):
    scale = HEAD_DIM ** -0.5
    nh = 2 * pp
    t = b * s
    nc = len(comm[1]) if comm else 0
    grid = (b, e // (pp * LANES), s // blk)

    def body(*refs):
        q_ref, k_ref, v_ref, ct_ref = refs[:4]
        o_ref, lse_ref = refs[4 + nc:6 + nc]
        finish = _ride_along(comm, refs[4:4 + nc], refs[6 + nc:6 + 2 * nc], refs[6 + 2 * nc:], *_grid_ends(grid))
        qi = pl.program_id(2)
        qm = [_half_masks((q_ref[:, p * LANES:(p + 1) * LANES] * scale).astype(BF16)) for p in range(pp)]
        cq = [_to_column(ct_ref[h, :, pl.ds(pl.multiple_of(qi * blk, blk), blk)]) for h in range(nh)]
        row, col = _iotas(blk)
        causal = col <= row
        low = lax.broadcasted_iota(jnp.int32, (blk, LANES), 1) < HEAD_DIM

        def block(j, carry, diag):
            off = pl.multiple_of(j * blk, blk)
            m, l, acc = carry
            zz = [_dot(qm[p], k_ref[pl.ds(off, blk), p * LANES:(p + 1) * LANES], NT) for p in range(pp)]
            z = [zz[h // 2][(h % 2) * blk:(h % 2 + 1) * blk] + (cq[h] - ct_ref[h, :, pl.ds(off, blk)]) for h in range(nh)]
            if diag:
                z = [jnp.where(causal, z[h], NEG) for h in range(nh)]
            m_new = tuple(jnp.maximum(m[h], jnp.max(z[h], axis=1, keepdims=True)) for h in range(nh))
            w = [jnp.exp(m[h] - m_new[h]) for h in range(nh)]
            pr = [jnp.exp(z[h] - m_new[h]) for h in range(nh)]
            pb = [pr[h].astype(BF16) for h in range(nh)]
            pv = [_dot(jnp.concatenate([pb[2 * p], pb[2 * p + 1]], axis=1),
                       _half_masks(v_ref[pl.ds(off, blk), p * LANES:(p + 1) * LANES]), NN) for p in range(pp)]
            acc_new = tuple(jnp.where(low, w[2 * p], w[2 * p + 1]) * acc[p] + pv[p] for p in range(pp))
            l_new = tuple(w[h] * l[h] + jnp.sum(pr[h], axis=1, keepdims=True) for h in range(nh))
            return m_new, l_new, acc_new

        zero = ((jnp.full((blk, 1), NEG, F32),) * nh, (jnp.zeros((blk, 1), F32),) * nh, (jnp.zeros((blk, LANES), F32),) * pp)
        carry = block(qi, zero, True)
        m, l, acc = lax.fori_loop(0, qi, lambda j, c_: block(j, c_, False), carry)
        for p in range(pp):
            o_ref[:, p * LANES:(p + 1) * LANES] = acc[p] / jnp.where(low, l[2 * p], l[2 * p + 1])
        for h in range(nh):
            lse_ref[h] = m[h] + jnp.log(l[h])
        finish()

    q_in, k_in, v_in, q_out, _, rows, krow = _pair_specs(s, blk, e, pp, 1)
    outs, sems = _peer_shapes(comm[0], comm[1]) if comm else ([], [])
    return pl.pallas_call(
        body, name="fox_fwd", grid=grid,
        in_specs=[q_in, k_in, v_in, krow] + [ANY] * nc, out_specs=[q_out, rows] + [ANY] * nc,
        out_shape=[jax.ShapeDtypeStruct((t, e), F32), jax.ShapeDtypeStruct((b * e // HEAD_DIM, s, 1), F32)] + outs,
        scratch_shapes=sems,
        compiler_params=_params("arbitrary", "arbitrary", "arbitrary"),
    )(qkv, qkv, qkv, ct, *(comm[1] if comm else ()))


def _fox_bwd(qkv, ct, do, o, lse, b, s, e, blk, pp, comm=None):
    scale = HEAD_DIM ** -0.5
    nh = 2 * pp
    t = b * s
    nq = s // blk
    nc = len(comm[1]) if comm else 0
    w = pp * LANES
    grid = (b, e // w, nq)

    def body(*refs):
        q_ref, k_ref, v_ref, ct_ref, do_ref, o_ref, lse_ref = refs[:7]
        dq_ref, dk_ref, dv_ref, dct_ref = refs[7 + nc:11 + nc]
        dk_acc, dv_acc = refs[11 + 2 * nc:13 + 2 * nc]
        finish = _ride_along(comm, refs[7:7 + nc], refs[11 + nc:11 + 2 * nc], refs[13 + 2 * nc:], *_grid_ends(grid))
        qi = pl.program_id(2)

        @pl.when(qi == 0)
        def _():
            dk_acc[...] = jnp.zeros_like(dk_acc)
            dv_acc[...] = jnp.zeros_like(dv_acc)
            dct_ref[...] = jnp.zeros_like(dct_ref)

        qm = [_half_masks((q_ref[:, p * LANES:(p + 1) * LANES] * scale).astype(BF16)) for p in range(pp)]
        dob = [do_ref[:, p * LANES:(p + 1) * LANES].astype(BF16) for p in range(pp)]
        dom = [_half_masks(dob[p]) for p in range(pp)]
        low = lax.broadcasted_iota(jnp.int32, (blk, LANES), 1) < HEAD_DIM
        delta = []
        for p in range(pp):
            prod = dob[p].astype(F32) * o_ref[:, p * LANES:(p + 1) * LANES]
            delta.append(jnp.sum(jnp.where(low, prod, 0.0), axis=1, keepdims=True))
            delta.append(jnp.sum(jnp.where(low, 0.0, prod), axis=1, keepdims=True))
        cq = [_to_column(ct_ref[h, :, pl.ds(pl.multiple_of(qi * blk, blk), blk)]) for h in range(nh)]
        lse_t = [lse_ref[h] for h in range(nh)]
        row, col = _iotas(blk)
        causal = col <= row

        def block(j, dq_acc, diag):
            off = pl.multiple_of(j * blk, blk)
            kp = [k_ref[pl.ds(off, blk), p * LANES:(p + 1) * LANES] for p in range(pp)]
            zz = [_dot(qm[p], kp[p], NT) for p in range(pp)]
            dd = [_dot(dom[p], v_ref[pl.ds(off, blk), p * LANES:(p + 1) * LANES], NT) for p in range(pp)]
            z = [zz[h // 2][(h % 2) * blk:(h % 2 + 1) * blk] + (cq[h] - ct_ref[h, :, pl.ds(off, blk)]) for h in range(nh)]
            pr = [jnp.exp(z[h] - lse_t[h]) for h in range(nh)]
            if diag:
                pr = [jnp.where(causal, pr[h], 0.0) for h in range(nh)]
            ds = [pr[h] * (dd[h // 2][(h % 2) * blk:(h % 2 + 1) * blk] - delta[h]) for h in range(nh)]
            dsb = [ds[h].astype(BF16) for h in range(nh)]
            pb = [pr[h].astype(BF16) for h in range(nh)]
            for p in range(pp):
                cols = slice(p * LANES, (p + 1) * LANES)
                dk_acc[pl.ds(off, blk), cols] += _dot(jnp.concatenate([dsb[2 * p], dsb[2 * p + 1]], axis=0), qm[p], TN)
                dv_acc[pl.ds(off, blk), cols] += _dot(jnp.concatenate([pb[2 * p], pb[2 * p + 1]], axis=0), dom[p], TN)
            for h in range(nh):
                dct_ref[h, :, pl.ds(off, blk)] -= jnp.sum(ds[h], axis=0, keepdims=True)
            return tuple(dq_acc[p] + _dot(jnp.concatenate([dsb[2 * p], dsb[2 * p + 1]], axis=1), _half_masks(kp[p]), NN)
                         for p in range(pp))

        dq_acc = lax.fori_loop(0, qi, lambda j, a: block(j, a, False), (jnp.zeros((blk, LANES), F32),) * pp)
        dq_acc = block(qi, dq_acc, True)
        for p in range(pp):
            dq_ref[:, p * LANES:(p + 1) * LANES] = (dq_acc[p] * scale).astype(BF16)

        @pl.when(qi == nq - 1)
        def _():
            dk_ref[...] = dk_acc[...].astype(BF16)
            dv_ref[...] = dv_acc[...].astype(BF16)

        finish()

    q_in, k_in, v_in, q_out, kv_out, rows, krow = _pair_specs(s, blk, e, pp, 1)
    outs, sems = _peer_shapes(comm[0], comm[1]) if comm else ([], [])
    return pl.pallas_call(
        body, name="fox_bwd", grid=grid,
        in_specs=[q_in, k_in, v_in, krow, q_out, q_out, rows] + [ANY] * nc,
        out_specs=[q_out, kv_out, kv_out, krow] + [ANY] * nc,
        out_shape=[jax.ShapeDtypeStruct((t, e), BF16)] * 3 + [jax.ShapeDtypeStruct((b * e // HEAD_DIM, 1, s), F32)] + outs,
        scratch_shapes=[pltpu.VMEM((s, w), F32), pltpu.VMEM((s, w), F32)] + sems,
        compiler_params=_params("arbitrary", "arbitrary", "arbitrary"),
    )(qkv, qkv, qkv, ct, do, o, lse, *(comm[1] if comm else ()))


def _scan_rows(f2, group, mode, d2=None):
    n = f2.shape[0]

    def body(*refs):
        f_ref, o_ref = refs[0], refs[-1]
        f = f_ref[...]
        row, col = _iotas(LANES)
        grow = lax.broadcasted_iota(jnp.int32, (n, n), 0)
        gcol = lax.broadcasted_iota(jnp.int32, (n, n), 1)
        same = (grow // group) == (gcol // group)
        e = jnp.exp(-jnp.abs(f))
        if mode == "fwd":
            x = jnp.minimum(f, 0.0) - jnp.log1p(e)
            within = (row <= col).astype(F32)
            earlier = (same & (gcol < grow)).astype(F32)
        else:
            x = refs[1][...]
            within = (row >= col).astype(F32)
            earlier = (same & (gcol > grow)).astype(F32)
        y = jnp.dot(x, within, preferred_element_type=F32, precision=lax.Precision.HIGHEST)
        tot = jnp.sum(x, axis=1, keepdims=True)
        y = y + jnp.dot(earlier, tot, preferred_element_type=F32, precision=lax.Precision.HIGHEST)
        if mode == "bwd":
            r = 1.0 / (1.0 + e)
            y = y * jnp.where(f >= 0.0, e * r, r)
        o_ref[...] = y

    args = (f2,) if mode == "fwd" else (f2, d2)
    return pl.pallas_call(body, name="logf_" + mode, out_shape=jax.ShapeDtypeStruct(f2.shape, F32),
                          compiler_params=_params())(*args)


def _matmul(a, b, dims, *, tm, tn, tk, out_dtype, name, bias=None, res=None, res_scale=1.0, b_outer=False):
    def ij(g0, g1):
        return (g1, g0) if b_outer else (g0, g1)

    if dims == NN:
        (m, kk), n = a.shape, b.shape[1]
        a_spec = pl.BlockSpec((tm, tk), lambda g0, g1, k: (ij(g0, g1)[0], k))
        b_spec = pl.BlockSpec((tk, tn), lambda g0, g1, k: (k, ij(g0, g1)[1]))
    elif dims == NT:
        (m, kk), n = a.shape, b.shape[0]
        a_spec = pl.BlockSpec((tm, tk), lambda g0, g1, k: (ij(g0, g1)[0], k))
        b_spec = pl.BlockSpec((tn, tk), lambda g0, g1, k: (ij(g0, g1)[1], k))
    else:
        (kk, m), n = a.shape, b.shape[1]
        a_spec = pl.BlockSpec((tk, tm), lambda g0, g1, k: (k, ij(g0, g1)[0]))
        b_spec = pl.BlockSpec((tk, tn), lambda g0, g1, k: (k, ij(g0, g1)[1]))
    assert m % tm == 0 and n % tn == 0 and kk % tk == 0, (name, m, n, kk, tm, tn, tk)
    nk = kk // tk
    extras, extra_specs = [], []
    if bias is not None:
        extras.append(bias)
        extra_specs.append(pl.BlockSpec((1, tn), lambda g0, g1, k: (0, ij(g0, g1)[1])))
    if res is not None:
        extras.append(res)
        extra_specs.append(pl.BlockSpec((tm, tn), lambda g0, g1, k: ij(g0, g1)))

    def finish(out, rest, o_ref):
        idx = 0
        if bias is not None:
            out = out + rest[idx][...]
            idx += 1
        if res is not None:
            out = out + res_scale * rest[idx][...]
        o_ref[...] = out.astype(o_ref.dtype)

    def body_single(a_ref, b_ref, *rest):
        finish(_dot(a_ref[...].astype(BF16), b_ref[...].astype(BF16), dims), rest, rest[-1])

    def body_acc(a_ref, b_ref, *rest):
        o_ref, acc_ref = rest[-2], rest[-1]
        k = pl.program_id(2)
        part = _dot(a_ref[...].astype(BF16), b_ref[...].astype(BF16), dims)

        @pl.when(k == 0)
        def _():
            acc_ref[...] = part

        @pl.when(k > 0)
        def _():
            acc_ref[...] += part

        @pl.when(k == nk - 1)
        def _():
            finish(acc_ref[...], rest, o_ref)

    grid = (n // tn, m // tm, nk) if b_outer else (m // tm, n // tn, nk)
    return pl.pallas_call(
        body_single if nk == 1 else body_acc, name=name, grid=grid,
        in_specs=[a_spec, b_spec] + extra_specs,
        out_specs=pl.BlockSpec((tm, tn), lambda g0, g1, k: ij(g0, g1)),
        out_shape=jax.ShapeDtypeStruct((m, n), out_dtype),
        scratch_shapes=[] if nk == 1 else [pltpu.VMEM((tm, tn), F32)],
        compiler_params=_params("parallel", "parallel", "arbitrary"),
    )(a, b, *extras)


def _input_grad(dhs, ws, dr, tm, comm=None):
    t, d = dr.shape
    npc = len(dhs)
    nc = len(comm[1]) if comm else 0
    grid = (t // tm,)

    def body(*refs):
        dr_ref = refs[2 * npc]
        first_in = 2 * npc + 1
        o_ref = refs[first_in + nc]
        finish = _ride_along(comm, refs[first_in:first_in + nc], refs[first_in + nc + 1:first_in + 2 * nc + 1],
                             refs[first_in + 2 * nc + 1:], *_grid_ends(grid))
        out = ALPHA * dr_ref[...]
        for p in range(npc):
            out = out + _dot(refs[p][...], refs[npc + p][...], NT)
        o_ref[...] = out
        finish()

    rows = pl.BlockSpec((tm, d), lambda i: (i, 0))
    outs, sems = _peer_shapes(comm[0], comm[1]) if comm else ([], [])
    return pl.pallas_call(
        body, name="input_grad", grid=grid,
        in_specs=[pl.BlockSpec((tm, a.shape[1]), lambda i: (i, 0)) for a in dhs]
        + [pl.BlockSpec(w.shape, lambda i: (0, 0)) for w in ws] + [rows] + [ANY] * nc,
        out_specs=[rows] + [ANY] * nc,
        out_shape=[jax.ShapeDtypeStruct((t, d), F32)] + outs,
        scratch_shapes=sems,
        compiler_params=_params("arbitrary"),
    )(*dhs, *ws, dr, *(comm[1] if comm else ()))


def _sigmoid(x):
    e = jnp.exp(-jnp.abs(x))
    r = 1.0 / (1.0 + e)
    return jnp.where(x >= 0.0, r, e * r)


def _proj_gate_fwd(o_sb, o_fx, wp_sb, wp_fx, g, tm):
    t, e = o_sb.shape
    d = wp_sb.shape[1]

    def body(osb_ref, ofx_ref, wsb_ref, wfx_ref, gsb_ref, gfx_ref, mg_ref, ysb_ref, yfx_ref):
        ysb = _dot(osb_ref[...].astype(BF16), wsb_ref[...], NN)
        yfx = _dot(ofx_ref[...].astype(BF16), wfx_ref[...], NN)
        ysb_ref[...] = ysb
        yfx_ref[...] = yfx
        mg_ref[...] = (_sigmoid(gsb_ref[...]) * ysb + _sigmoid(gfx_ref[...]) * yfx).astype(BF16)

    rows_e = pl.BlockSpec((tm, e), lambda i: (i, 0))
    rows_d = pl.BlockSpec((tm, d), lambda i: (i, 0))
    w_spec = pl.BlockSpec((e, d), lambda i: (0, 0))
    return pl.pallas_call(
        body, name="proj_gate_fwd", grid=(t // tm,),
        in_specs=[rows_e, rows_e, w_spec, w_spec, rows_d, pl.BlockSpec((tm, d), lambda i: (i, 1))],
        out_specs=[rows_d, rows_d, rows_d],
        out_shape=[jax.ShapeDtypeStruct((t, d), BF16), jax.ShapeDtypeStruct((t, d), F32), jax.ShapeDtypeStruct((t, d), F32)],
        compiler_params=_params("parallel"),
    )(o_sb, o_fx, wp_sb, wp_fx, g, g)


def _gate_bwd(dr, w_out, y_sb, y_fx, g, tm):
    t, d = dr.shape

    def body(dr_ref, w_ref, ysb_ref, yfx_ref, gsb_ref, gfx_ref, dysb_ref, dyfx_ref, dg_ref):
        dm = _dot(dr_ref[...], w_ref[...], NT)
        ssb = _sigmoid(gsb_ref[...])
        sfx = _sigmoid(gfx_ref[...])
        dysb_ref[...] = (dm * ssb).astype(BF16)
        dyfx_ref[...] = (dm * sfx).astype(BF16)
        dg_ref[:, 0:d] = (dm * ysb_ref[...] * ssb * (1.0 - ssb)).astype(BF16)
        dg_ref[:, d:2 * d] = (dm * yfx_ref[...] * sfx * (1.0 - sfx)).astype(BF16)

    rows = pl.BlockSpec((tm, d), lambda i: (i, 0))
    rows1 = pl.BlockSpec((tm, d), lambda i: (i, 1))
    return pl.pallas_call(
        body, name="gate_bwd", grid=(t // tm,),
        in_specs=[rows, pl.BlockSpec((d, d), lambda i: (0, 0)), rows, rows, rows, rows1],
        out_specs=[rows, rows, pl.BlockSpec((tm, 2 * d), lambda i: (i, 0))],
        out_shape=[jax.ShapeDtypeStruct((t, d), BF16)] * 2 + [jax.ShapeDtypeStruct((t, 2 * d), BF16)],
        compiler_params=_params("parallel"),
    )(dr, w_out, y_sb, y_fx, g, g)


def _mm_res_ln(a, w, xres, gamma, beta, tm, name):
    t, kk = a.shape
    d = w.shape[1]

    def body(a_ref, w_ref, x_ref, g_ref, b_ref, xn_ref, xh_ref, rs_ref, xb_ref):
        r = ALPHA * x_ref[...] + _dot(a_ref[...].astype(BF16), w_ref[...], NN)
        mean = jnp.mean(r, axis=1, keepdims=True)
        cen = r - mean
        rstd = lax.rsqrt(jnp.mean(cen * cen, axis=1, keepdims=True) + LN_EPS)
        xh = cen * rstd
        xn = xh * g_ref[...] + b_ref[...]
        xh_ref[...] = xh
        xn_ref[...] = xn
        xb_ref[...] = xn.astype(BF16)
        rs_ref[...] = rstd

    rows_d = pl.BlockSpec((tm, d), lambda i: (i, 0))
    vec = pl.BlockSpec((1, d), lambda i: (0, 0))
    return pl.pallas_call(
        body, name=name, grid=(t // tm,),
        in_specs=[pl.BlockSpec((tm, kk), lambda i: (i, 0)), pl.BlockSpec((kk, d), lambda i: (0, 0)), rows_d, vec, vec],
        out_specs=[rows_d, rows_d, pl.BlockSpec((tm, 1), lambda i: (i, 0)), rows_d],
        out_shape=[jax.ShapeDtypeStruct((t, d), F32), jax.ShapeDtypeStruct((t, d), F32), jax.ShapeDtypeStruct((t, 1), F32),
                   jax.ShapeDtypeStruct((t, d), BF16)],
        compiler_params=_params("parallel"),
    )(a, w, xres, gamma, beta)


def _ln_bwd_math(dy, xh, rstd, gamma):
    dxh = dy * gamma
    m1 = jnp.mean(dxh, axis=1, keepdims=True)
    m2 = jnp.mean(dxh * xh, axis=1, keepdims=True)
    return rstd * (dxh - m1 - xh * m2)


def _rowsum8(x):
    tm, n = x.shape
    return jnp.sum(x.reshape(tm // SUBLANES, SUBLANES, n), axis=0)


def _fold8(ref):
    ref[0:1, :] = jnp.sum(ref[...], axis=0, keepdims=True)


def _loss_ln_bwd(x2, xh, rstd, gamma, target, tm):
    t, d = x2.shape

    def body(x_ref, xh_ref, rs_ref, g_ref, tg_ref, dr_ref, dg_ref, db_ref, ls_ref, drb_ref):
        @pl.when(pl.program_id(0) == 0)
        def _():
            dg_ref[...] = jnp.zeros_like(dg_ref)
            db_ref[...] = jnp.zeros_like(db_ref)
            ls_ref[...] = jnp.zeros_like(ls_ref)

        err = x_ref[...] - tg_ref[...]
        xh = xh_ref[...]
        dy = err * (1.0 / d)
        dr = _ln_bwd_math(dy, xh, rs_ref[...], g_ref[...])
        dr_ref[...] = dr
        drb_ref[...] = dr.astype(BF16)
        dg_ref[...] += _rowsum8(dy * xh)
        db_ref[...] += _rowsum8(dy)
        sq = _rowsum8(err * err)
        part = sq[:, 0:LANES]
        for j in range(1, d // LANES):
            part = part + sq[:, j * LANES:(j + 1) * LANES]
        ls_ref[...] += part * (0.5 / d)

        @pl.when(pl.program_id(0) == t // tm - 1)
        def _():
            _fold8(dg_ref)
            _fold8(db_ref)
            ls_ref[0:1, 0:1] = jnp.sum(jnp.sum(ls_ref[...], axis=0, keepdims=True), axis=1, keepdims=True)

    rows = pl.BlockSpec((tm, d), lambda i: (i, 0))
    acc = pl.BlockSpec((SUBLANES, d), lambda i: (0, 0))
    return pl.pallas_call(
        body, name="loss_ln_bwd", grid=(t // tm,),
        in_specs=[rows, rows, pl.BlockSpec((tm, 1), lambda i: (i, 0)), pl.BlockSpec((1, d), lambda i: (0, 0)), rows],
        out_specs=[rows, acc, acc, pl.BlockSpec((SUBLANES, LANES), lambda i: (0, 0)), rows],
        out_shape=[jax.ShapeDtypeStruct((t, d), F32), jax.ShapeDtypeStruct((SUBLANES, d), F32),
                   jax.ShapeDtypeStruct((SUBLANES, d), F32), jax.ShapeDtypeStruct((SUBLANES, LANES), F32),
                   jax.ShapeDtypeStruct((t, d), BF16)],
        compiler_params=_params("arbitrary"),
    )(x2, xh, rstd, gamma, target)


def _ln_bwd(dr_next, dlin, xh, rstd, gamma, tm):
    t, d = xh.shape

    def body(dn_ref, dl_ref, xh_ref, rs_ref, g_ref, dr_ref, dg_ref, db_ref, drb_ref):
        @pl.when(pl.program_id(0) == 0)
        def _():
            dg_ref[...] = jnp.zeros_like(dg_ref)
            db_ref[...] = jnp.zeros_like(db_ref)

        dy = ALPHA * dn_ref[...] + dl_ref[...]
        xh = xh_ref[...]
        dr = _ln_bwd_math(dy, xh, rs_ref[...], g_ref[...])
        dr_ref[...] = dr
        drb_ref[...] = dr.astype(BF16)
        dg_ref[...] += _rowsum8(dy * xh)
        db_ref[...] += _rowsum8(dy)

        @pl.when(pl.program_id(0) == t // tm - 1)
        def _():
            _fold8(dg_ref)
            _fold8(db_ref)

    rows = pl.BlockSpec((tm, d), lambda i: (i, 0))
    acc = pl.BlockSpec((SUBLANES, d), lambda i: (0, 0))
    return pl.pallas_call(
        body, name="ln_bwd", grid=(t // tm,),
        in_specs=[rows, rows, rows, pl.BlockSpec((tm, 1), lambda i: (i, 0)), pl.BlockSpec((1, d), lambda i: (0, 0))],
        out_specs=[rows, acc, acc, rows],
        out_shape=[jax.ShapeDtypeStruct((t, d), F32), jax.ShapeDtypeStruct((SUBLANES, d), F32),
                   jax.ShapeDtypeStruct((SUBLANES, d), F32), jax.ShapeDtypeStruct((t, d), BF16)],
        compiler_params=_params("arbitrary"),
    )(dr_next, dlin, xh, rstd, gamma)


def _shift_rows(x, halo, shift, row):
    out = pltpu.roll(x, shift, 0)
    n = halo.shape[0]
    for r in range(shift):
        out = jnp.where(row == r, halo[n - shift + r:n - shift + r + 1, :], out)
    return out


def _unshift_rows(x, halo, shift, row, tm):
    out = pltpu.roll(x, tm - shift, 0)
    for r in range(shift):
        out = jnp.where(row == tm - shift + r, halo[r:r + 1, :], out)
    return out


def _conv_pre(ug_ref, halo_ref, wc_ref, bc_ref, first, tm):
    ug = ug_ref[...].astype(F32)
    halo = jnp.where(first, 0.0, halo_ref[...].astype(F32))
    row = lax.broadcasted_iota(jnp.int32, ug.shape, 0)
    wc = wc_ref[...]
    um1 = _shift_rows(ug, halo, 1, row)
    um2 = _shift_rows(ug, halo, 2, row)
    c = bc_ref[...] + wc[2:3, :] * ug + wc[1:2, :] * um1 + wc[0:1, :] * um2
    return c, ug, um1, um2


INV_SQRT2 = 1.0 / math.sqrt(2.0)
INV_SQRT2PI = 1.0 / math.sqrt(2.0 * math.pi)


def _conv_glu_fwd(u, wc, bc, seq, tm):
    t, f2 = u.shape
    f = f2 // 2
    per_seq = seq // tm
    hb = tm // HALO

    def body(ug_ref, halo_ref, uv_ref, wc_ref, bc_ref, a_ref):
        first = (pl.program_id(0) % per_seq) == 0
        c, _, _, _ = _conv_pre(ug_ref, halo_ref, wc_ref, bc_ref, first, tm)
        gelu = 0.5 * c * (1.0 + lax.erf(c * INV_SQRT2))
        a_ref[...] = (gelu * uv_ref[...].astype(F32)).astype(BF16)

    return pl.pallas_call(
        body, name="conv_glu_fwd", grid=(t // tm,),
        in_specs=[pl.BlockSpec((tm, f), lambda i: (i, 0)),
                  pl.BlockSpec((HALO, f), lambda i: (jnp.maximum(i * hb - 1, 0), 0)),
                  pl.BlockSpec((tm, f), lambda i: (i, 1)),
                  pl.BlockSpec((3, f), lambda i: (0, 0)), pl.BlockSpec((1, f), lambda i: (0, 0))],
        out_specs=pl.BlockSpec((tm, f), lambda i: (i, 0)),
        out_shape=jax.ShapeDtypeStruct((t, f), BF16),
        compiler_params=_params("parallel"),
    )(u, u, u, wc, bc)


def _conv_glu_bwd1(u, da, wc, bc, seq, tm):
    t, f2 = u.shape
    f = f2 // 2
    per_seq = seq // tm
    hb = tm // HALO

    def body(ug_ref, halo_ref, uv_ref, da_ref, wc_ref, bc_ref, dc_ref, duv_ref):
        first = (pl.program_id(0) % per_seq) == 0
        c, _, _, _ = _conv_pre(ug_ref, halo_ref, wc_ref, bc_ref, first, tm)
        cdf = 0.5 * (1.0 + lax.erf(c * INV_SQRT2))
        pdf = jnp.exp(-0.5 * c * c) * INV_SQRT2PI
        da = da_ref[...].astype(F32)
        duv_ref[...] = (da * (c * cdf)).astype(BF16)
        dc_ref[...] = (da * uv_ref[...].astype(F32) * (cdf + c * pdf)).astype(BF16)

    rows = pl.BlockSpec((tm, f), lambda i: (i, 0))
    return pl.pallas_call(
        body, name="conv_glu_bwd1", grid=(t // tm,),
        in_specs=[rows, pl.BlockSpec((HALO, f), lambda i: (jnp.maximum(i * hb - 1, 0), 0)),
                  pl.BlockSpec((tm, f), lambda i: (i, 1)), rows,
                  pl.BlockSpec((3, f), lambda i: (0, 0)), pl.BlockSpec((1, f), lambda i: (0, 0))],
        out_specs=[rows, pl.BlockSpec((tm, f), lambda i: (i, 1))],
        out_shape=[jax.ShapeDtypeStruct((t, f), BF16), jax.ShapeDtypeStruct((t, f2), BF16)],
        compiler_params=_params("parallel"),
    )(u, u, u, da, wc, bc)


def _conv_glu_bwd2(u, dc, wc, du, seq, tm):
    t, f2 = u.shape
    f = f2 // 2
    per_seq = seq // tm
    hb = tm // HALO
    nblk = t // HALO

    def body(ug_ref, halo_ref, dc_ref, nxt_ref, wc_ref, du_in_ref, dug_ref, w0_ref, w1_ref, w2_ref, b_ref):
        i = pl.program_id(0)

        @pl.when(i == 0)
        def _():
            for r in (w0_ref, w1_ref, w2_ref, b_ref):
                r[...] = jnp.zeros_like(r)

        first = (i % per_seq) == 0
        last = (i % per_seq) == per_seq - 1
        ug = ug_ref[...].astype(F32)
        halo = jnp.where(first, 0.0, halo_ref[...].astype(F32))
        nxt = jnp.where(last, 0.0, nxt_ref[...].astype(F32))
        row = lax.broadcasted_iota(jnp.int32, ug.shape, 0)
        dc = dc_ref[...].astype(F32)
        wc = wc_ref[...]
        dp1 = _unshift_rows(dc, nxt, 1, row, tm)
        dp2 = _unshift_rows(dc, nxt, 2, row, tm)
        dug_ref[...] = (wc[2:3, :] * dc + wc[1:2, :] * dp1 + wc[0:1, :] * dp2).astype(BF16)
        w2_ref[...] += _rowsum8(dc * ug)
        w1_ref[...] += _rowsum8(dc * _shift_rows(ug, halo, 1, row))
        w0_ref[...] += _rowsum8(dc * _shift_rows(ug, halo, 2, row))
        b_ref[...] += _rowsum8(dc)

        @pl.when(i == t // tm - 1)
        def _():
            for r in (w0_ref, w1_ref, w2_ref, b_ref):
                _fold8(r)

    rows = pl.BlockSpec((tm, f), lambda i: (i, 0))
    acc = pl.BlockSpec((SUBLANES, f), lambda i: (0, 0))
    return pl.pallas_call(
        body, name="conv_glu_bwd2", grid=(t // tm,),
        in_specs=[rows, pl.BlockSpec((HALO, f), lambda i: (jnp.maximum(i * hb - 1, 0), 0)),
                  rows, pl.BlockSpec((HALO, f), lambda i: (jnp.minimum((i + 1) * hb, nblk - 1), 0)),
                  pl.BlockSpec((3, f), lambda i: (0, 0)), pl.BlockSpec(memory_space=pl.ANY)],
        out_specs=[rows, acc, acc, acc, acc],
        out_shape=[jax.ShapeDtypeStruct((t, f2), BF16)] + [jax.ShapeDtypeStruct((SUBLANES, f), F32)] * 4,
        input_output_aliases={5: 0},
        compiler_params=_params("arbitrary"),
    )(u, u, dc, dc, wc, du)


def _colsum(x, tm, name):
    t, n = x.shape

    def body(x_ref, o_ref):
        @pl.when(pl.program_id(0) == 0)
        def _():
            o_ref[...] = jnp.zeros_like(o_ref)

        o_ref[...] += _rowsum8(x_ref[...].astype(F32))

        @pl.when(pl.program_id(0) == t // tm - 1)
        def _():
            _fold8(o_ref)

    return pl.pallas_call(
        body, name=name, grid=(t // tm,),
        in_specs=[pl.BlockSpec((tm, n), lambda i: (i, 0))],
        out_specs=pl.BlockSpec((SUBLANES, n), lambda i: (0, 0)),
        out_shape=jax.ShapeDtypeStruct((SUBLANES, n), F32),
        compiler_params=_params("arbitrary"),
    )(x)


def _adamw(w, gparts, m, v, name):
    p, r, c = gparts.shape
    tr = r
    for cand in (512, 256, 128, 64, 32, 16, 8):
        if cand * p <= 1024 and r % cand == 0 and r > cand:
            tr = cand
            break
    c1 = 1.0 - ADAM_B1 ** ADAM_STEP
    c2 = 1.0 - ADAM_B2 ** ADAM_STEP

    def body(w_ref, g_ref, m_ref, v_ref, go_ref, d_ref, mo_ref, vo_ref):
        g = g_ref[0].astype(F32)
        for i in range(1, p):
            g = g + g_ref[i].astype(F32)
        mn = ADAM_B1 * m_ref[...] + (1.0 - ADAM_B1) * g
        vn = ADAM_B2 * v_ref[...] + (1.0 - ADAM_B2) * (g * g)
        go_ref[...] = g
        mo_ref[...] = mn
        vo_ref[...] = vn
        d_ref[...] = -ADAM_LR * ((mn / c1) / (jnp.sqrt(vn / c2) + ADAM_EPS) + ADAM_WD * w_ref[...])

    blk = pl.BlockSpec((tr, c), lambda i: (i, 0))
    return pl.pallas_call(
        body, name=name, grid=(r // tr,),
        in_specs=[blk, pl.BlockSpec((p, tr, c), lambda i: (0, i, 0)), blk, blk],
        out_specs=[blk] * 4,
        out_shape=[jax.ShapeDtypeStruct((r, c), F32)] * 4,
        compiler_params=_params("parallel"),
    )(w, gparts, m, v)


MESH = pl.DeviceIdType.MESH
ANY = pl.BlockSpec(memory_space=pl.ANY)


def _all_gather(xs, name):
    n = len(xs)

    def body(*refs):
        x_refs, out_refs = refs[:n], refs[n:2 * n]
        send_sems, recv_sems, local_sems = refs[2 * n:]
        x, y, c = lax.axis_index("x"), lax.axis_index("y"), lax.axis_index("c")
        me, sibling = (x, y, c), (x, y, 1 - c)
        chips = [(1 - x, y), (x, 1 - y), (1 - x, 1 - y)]

        def slot(a, px, py, pc):
            return out_refs[a].at[4 * px + 2 * py + pc]

        def copy(a, k, block, to, src=None):
            return pltpu.make_async_remote_copy(
                src_ref=slot(a, *block) if src is None else src, dst_ref=slot(a, *block),
                send_sem=send_sems.at[k * n + a], recv_sem=recv_sems.at[k * n + a], device_id=to, device_id_type=MESH)

        arrays = range(n)
        mine = [pltpu.make_async_copy(x_refs[a], slot(a, *me), local_sems.at[a]) for a in arrays]
        first = [copy(a, 0, me, sibling, src=x_refs[a]) for a in arrays]
        first += [copy(a, 1 + j, me, (*chip, c), src=x_refs[a]) for j, chip in enumerate(chips) for a in arrays]
        for cp in mine + first:
            cp.start()
        passed = []
        for j, chip in enumerate(chips):
            for a in arrays:
                copy(a, 1 + j, (*chip, c), me).wait_recv()
                passed.append(copy(a, 4 + j, (*chip, c), sibling))
                passed[-1].start()
        for a in arrays:
            copy(a, 0, sibling, me).wait_recv()
        for j, chip in enumerate(chips):
            for a in arrays:
                copy(a, 4 + j, (*chip, 1 - c), me).wait_recv()
        for cp in first + passed:
            cp.wait_send()
        for cp in mine:
            cp.wait()

    return pl.pallas_call(
        body, name=name,
        out_shape=[jax.ShapeDtypeStruct((N_DEV,) + x.shape, x.dtype) for x in xs],
        in_specs=[ANY] * n, out_specs=[ANY] * n,
        scratch_shapes=[pltpu.SemaphoreType.DMA((7 * n,)), pltpu.SemaphoreType.DMA((7 * n,)),
                        pltpu.SemaphoreType.DMA((n,))],
    )(*xs)


def _peer_copies(kind, src_refs, dst_refs, send_sems, recv_sems, local_sems):
    n = len(src_refs)
    x, y, c = lax.axis_index("x"), lax.axis_index("y"), lax.axis_index("c")
    mine = 4 * x + 2 * y + c

    def src(a, idx):
        return src_refs[a] if kind == "spread" else src_refs[a].at[idx]

    copies = [pltpu.make_async_copy(src(a, mine), dst_refs[a].at[mine], local_sems.at[a]) for a in range(n)]
    for k in range(1, N_DEV):
        px = 1 - x if k & 4 else x
        py = 1 - y if k & 2 else y
        pc = 1 - c if k & 1 else c
        for a in range(n):
            copies.append(pltpu.make_async_remote_copy(
                src_ref=src(a, 4 * px + 2 * py + pc), dst_ref=dst_refs[a].at[mine],
                send_sem=send_sems.at[(k - 1) * n + a], recv_sem=recv_sems.at[(k - 1) * n + a],
                device_id=(px, py, pc), device_id_type=MESH))
    return copies


def _peer_shapes(kind, arrays):
    n = len(arrays)
    outs = [jax.ShapeDtypeStruct(((N_DEV,) + a.shape) if kind == "spread" else a.shape, a.dtype) for a in arrays]
    sems = [pltpu.SemaphoreType.DMA((7 * n,)), pltpu.SemaphoreType.DMA((7 * n,)), pltpu.SemaphoreType.DMA((n,))]
    return outs, sems


def _tile(n, pref, unit=LANES):
    if n <= pref:
        return n
    best = None
    for cand in range(unit, pref + 1, unit):
        if n % cand == 0:
            best = cand
    assert best is not None, (n, pref, unit)
    return best


LATE_KEYS = dict(w_proj_sb="wp_sb", w_proj_fox="wp_fx", w_out="w_out", w_up="w_up", w_conv="w_conv", w_down="w_down")


def _layer_step(x, target, w, attn_blk, late=None):
    b, s, d = x.shape
    t = b * s
    w = dict(w)
    e = w["w_qkv"].shape[1] // 6
    h = e // HEAD_DIM
    f = w["b_conv"].shape[1]
    x2 = x.reshape(t, d)
    tg = target.reshape(t, d)
    pp = 2 if (e // LANES) % 2 == 0 else 1
    fox_blk = min(2 * attn_blk, s)
    tm = _tile(t, 512, HALO)
    tmo = _tile(t, 1024, HALO)
    tmc = _tile(s, 256, HALO)
    tkt = _tile(t, 2048, HALO)
    td = _tile(d, 1024)
    tf = _tile(f, 1408)
    t2f = _tile(2 * f, 1408)
    tqkv = _tile(6 * e, 1024)
    tg2 = _tile(2 * d, 1024)
    xb = x2.astype(BF16)

    qkv = _matmul(xb, w["w_qkv"], NN, tm=tmo, tn=tqkv, tk=d, out_dtype=BF16, name="in_qkv", bias=w["b_qkv"], b_outer=True)
    gate = _matmul(xb, jnp.concatenate([w["w_g"], w["w_f"]], axis=1), NN, tm=tmo, tn=2 * d + LANES, tk=d, out_dtype=F32,
                   name="in_gate", bias=jnp.concatenate([w["b_g"], w["b_f"]], axis=1), b_outer=True)
    nr = s // LANES
    f2 = gate[:, 2 * d:2 * d + h].reshape(b, s, h).transpose(0, 2, 1).reshape(b * h * nr, LANES)
    ct = _scan_rows(f2, nr, "fwd").reshape(b * h, 1, s)
    o_sb, tot, first = _sb_fwd(qkv, b, s, e, attn_blk, pp)
    if late is None:
        o_fx, lse = _fox_fwd(qkv, ct, b, s, e, fox_blk, 1)
    else:
        o_fx, lse, *gathered = _fox_fwd(qkv, ct, b, s, e, fox_blk, 1, comm=("spread", late[1]))
        for name, g in zip(late[0], gathered):
            w[LATE_KEYS[name]] = _join(g, name)
    merged, y_sb, y_fx = _proj_gate_fwd(o_sb, o_fx, w["wp_sb"], w["wp_fx"], gate, tm)
    x1, xh1, rs1, x1b = _mm_res_ln(merged, w["w_out"], x2, w["ln1_g"], w["ln1_b"], tm, "out_ln1")
    u = _matmul(x1b, w["w_up"], NN, tm=tmo, tn=t2f, tk=d, out_dtype=BF16, name="ffn_up", b_outer=True)
    act = _conv_glu_fwd(u, w["w_conv"], w["b_conv"], s, tmc)
    xo, xh2, rs2, _ = _mm_res_ln(act, w["w_down"], x1, w["ln2_g"], w["ln2_b"], tm, "down_ln2")

    gr = {}
    dr2, dg2, db2, ls, dr2b = _loss_ln_bwd(xo, xh2, rs2, w["ln2_g"], tg, tm)
    gr["ln2_g"], gr["ln2_b"] = dg2[0:1], db2[0:1]
    da = _matmul(dr2b, w["w_down"], NT, tm=tmo, tn=tf, tk=d, out_dtype=BF16, name="d_act", b_outer=True)
    gr["w_down"] = _matmul(act, dr2b, TN, tm=tf, tn=td, tk=tkt, out_dtype=BF16, name="dw_down")
    dc, du = _conv_glu_bwd1(u, da, w["w_conv"], w["b_conv"], s, tmc)
    du, gw0, gw1, gw2, gbc = _conv_glu_bwd2(u, dc, w["w_conv"], du, s, tmc)
    gr["w_conv"] = jnp.concatenate([gw0[0:1], gw1[0:1], gw2[0:1]], axis=0)
    gr["b_conv"] = gbc[0:1]
    dlin1 = _matmul(du, w["w_up"], NT, tm=tm, tn=td, tk=2 * f, out_dtype=F32, name="d_x1")
    gr["w_up"] = _matmul(x1b, du, TN, tm=td, tn=t2f, tk=tkt, out_dtype=BF16, name="dw_up")
    dr1, dg1, db1, dr1b = _ln_bwd(dr2, dlin1, xh1, rs1, w["ln1_g"], tm)
    gr["ln1_g"], gr["ln1_b"] = dg1[0:1], db1[0:1]
    gr["w_out"] = _matmul(merged, dr1b, TN, tm=td, tn=td, tk=tkt, out_dtype=BF16, name="dw_out")
    dy_sb, dy_fx, dgate = _gate_bwd(dr1b, w["w_out"], y_sb, y_fx, gate, tm)
    do_sb = _matmul(dy_sb, w["wp_sb"], NT, tm=tm, tn=e, tk=d, out_dtype=BF16, name="d_o_sb")
    do_fx = _matmul(dy_fx, w["wp_fx"], NT, tm=tm, tn=e, tk=d, out_dtype=BF16, name="d_o_fx")
    gr["wp_sb"] = _matmul(o_sb, dy_sb, TN, tm=e, tn=td, tk=tkt, out_dtype=BF16, name="dwp_sb")
    gr["wp_fx"] = _matmul(o_fx, dy_fx, TN, tm=e, tn=td, tk=tkt, out_dtype=BF16, name="dwp_fx")
    dq_sb, dk_sb, dv_sb = _sb_bwd(qkv, do_sb, tot, first, b, s, e, attn_blk, pp)
    landed = None
    if late is None:
        dq_fx, dk_fx, dv_fx, dct = _fox_bwd(qkv, ct, do_fx, o_fx, lse, b, s, e, fox_blk, 1)
    else:
        blocks = [_cut(gr[LATE_KEYS[n]], n).astype(BF16 if n in MATMUL_OPERANDS else F32) for n in late[0]]
        dq_fx, dk_fx, dv_fx, dct, *got = _fox_bwd(qkv, ct, do_fx, o_fx, lse, b, s, e, fox_blk, 1,
                                                   comm=("exchange", blocks))
        landed = dict(zip(late[0], got))
    dqkv = jnp.concatenate([dq_sb, dk_sb, dv_sb, dq_fx, dk_fx, dv_fx], axis=1)
    df2 = _scan_rows(f2, nr, "bwd", dct.reshape(b * h * nr, LANES))
    df = jnp.pad(df2.reshape(b, h, s).transpose(0, 2, 1).reshape(t, h), ((0, 0), (0, LANES - h))).astype(BF16)
    gr["w_qkv"] = _matmul(xb, dqkv, TN, tm=td, tn=tqkv, tk=tkt, out_dtype=BF16, name="dw_qkv")
    gr["w_g"] = _matmul(xb, dgate, TN, tm=td, tn=tg2, tk=tkt, out_dtype=BF16, name="dw_gate")
    gr["w_f"] = _matmul(xb, df, TN, tm=td, tn=LANES, tk=tkt, out_dtype=BF16, name="dw_forget")
    gr["b_qkv"] = _colsum(dqkv, tm, "db_qkv")[0:1]
    gr["b_g"] = _colsum(dgate, tm, "db_gate")[0:1]
    gr["b_f"] = _colsum(df, tm, "db_forget")[0:1]
    comm = None
    if late is not None:
        small = jnp.concatenate([_w_in_layout(gr["b_qkv"], gr["b_f"], gr["b_g"], h), gr["ln1_g"], gr["ln1_b"],
                                 gr["b_conv"], gr["ln2_g"], gr["ln2_b"]], axis=1)
        comm = ("exchange", [_cut(_w_in_layout(gr["w_qkv"], gr["w_f"], gr["w_g"], h), "w_in").astype(BF16),
                             jnp.broadcast_to(small[None], (N_DEV,) + small.shape)])
    dx, *got = _input_grad([dqkv, dgate, df], [w["w_qkv"], w["w_g"], w["w_f"]], dr1, tm, comm)
    if late is not None:
        landed["w_in"], landed["replicated"] = got
    return ls[0:1, 0:1], dx.reshape(b, s, d), gr, landed


SHARDED = ("w_in", "w_proj_sb", "w_proj_fox", "w_out", "w_up", "w_conv", "w_down")
ROW_SHARDED = ("w_out", "w_down")
REPLICATED = ("b_in", "ln1_g", "ln1_b", "b_conv", "ln2_g", "ln2_b")
WEIGHTS = ("w_in", "b_in", "w_proj_sb", "w_proj_fox", "w_out", "ln1_g", "ln1_b", "w_up", "w_conv", "b_conv",
           "w_down", "ln2_g", "ln2_b")
MATMUL_OPERANDS = ("w_in", "w_proj_sb", "w_proj_fox", "w_out", "w_up", "w_down")


def _w_in_layout(g_qkv, g_f, g_g, h):
    return jnp.concatenate([g_qkv, g_f[:, :h], g_g], axis=1)


def _cut(full, name):
    r, c = full.shape
    if name in ROW_SHARDED:
        return full.reshape(N_DEV, r // N_DEV, c)
    cs = c // N_DEV
    return jnp.stack([full[:, j * cs:(j + 1) * cs] for j in range(N_DEV)], axis=0)


def _join(blocks, name):
    p, r, c = blocks.shape
    if name in ROW_SHARDED:
        return blocks.reshape(p * r, c)
    return jnp.concatenate([blocks[j] for j in range(p)], axis=1)


def kernel(x, w_in, b_in, w_proj_sb, w_proj_fox, w_out, ln1_g, ln1_b, w_up, w_conv, b_conv, w_down, ln2_g, ln2_b, loss_target, m_w_in, m_b_in, m_w_proj_sb, m_w_proj_fox, m_w_out, m_ln1_g, m_ln1_b, m_w_up, m_w_conv, m_b_conv, m_w_down, m_ln2_g, m_ln2_b, v_w_in, v_b_in, v_w_proj_sb, v_w_proj_fox, v_w_out, v_ln1_g, v_ln1_b, v_w_up, v_w_conv, v_b_conv, v_w_down, v_ln2_g, v_ln2_b):
    wts = dict(w_in=w_in, b_in=b_in, w_proj_sb=w_proj_sb, w_proj_fox=w_proj_fox, w_out=w_out, ln1_g=ln1_g, ln1_b=ln1_b,
               w_up=w_up, w_conv=w_conv, b_conv=b_conv, w_down=w_down, ln2_g=ln2_g, ln2_b=ln2_b)
    mom = dict(w_in=m_w_in, b_in=m_b_in, w_proj_sb=m_w_proj_sb, w_proj_fox=m_w_proj_fox, w_out=m_w_out, ln1_g=m_ln1_g,
               ln1_b=m_ln1_b, w_up=m_w_up, w_conv=m_w_conv, b_conv=m_b_conv, w_down=m_w_down, ln2_g=m_ln2_g, ln2_b=m_ln2_b)
    var = dict(w_in=v_w_in, b_in=v_b_in, w_proj_sb=v_w_proj_sb, w_proj_fox=v_w_proj_fox, w_out=v_w_out, ln1_g=v_ln1_g,
               ln1_b=v_ln1_b, w_up=v_w_up, w_conv=v_w_conv, b_conv=v_b_conv, w_down=v_w_down, ln2_g=v_ln2_g, ln2_b=v_ln2_b)
    shard = {n: wts[n].reshape(wts[n].shape[-2:]) for n in WEIGHTS}

    w_in_full = _join(_all_gather([shard["w_in"].astype(BF16)], "gather_w_in")[0], "w_in")
    late_names = [n for n in SHARDED if n != "w_in"]
    late = (late_names, [shard[n].astype(BF16) if n in MATMUL_OPERANDS else shard[n] for n in late_names])
    e = shard["w_proj_sb"].shape[0]
    h = e // HEAD_DIM
    nq = 6 * e

    def cut_in(a, pad):
        fcols = a[:, nq:nq + h]
        if pad:
            fcols = jnp.pad(fcols, ((0, 0), (0, LANES - h)))
        return a[:, :nq], a[:, nq + h:], fcols

    w_qkv, w_g, w_f = cut_in(w_in_full, True)
    b_qkv, b_g, b_f = cut_in(shard["b_in"], True)
    w = dict(w_qkv=w_qkv, w_g=w_g, w_f=w_f, b_qkv=b_qkv, b_g=b_g, b_f=b_f, b_conv=shard["b_conv"],
             ln1_g=shard["ln1_g"], ln1_b=shard["ln1_b"], ln2_g=shard["ln2_g"], ln2_b=shard["ln2_b"])

    loss_local, grad_x, gr, gsum = _layer_step(x, loss_target, w, min(256, x.shape[1]), late)
    loss = lax.psum(loss_local[0, 0], ("x", "y", "c"))

    parts = gsum.pop("replicated")
    off = 0
    for n in REPLICATED:
        gsum[n] = parts[:, :, off:off + shard[n].size]
        off += shard[n].size

    grads, deltas, new_m, new_v = [], [], [], []
    for n in WEIGHTS:
        shp = wts[n].shape
        g, dl, mn, vn = _adamw(shard[n], gsum[n], mom[n].reshape(shard[n].shape), var[n].reshape(shard[n].shape),
                               "adamw_" + n)
        grads.append(g.reshape(shp))
        deltas.append(dl.reshape(shp))
        new_m.append(mn.reshape(shp))
        new_v.append(vn.reshape(shp))
    return (loss, grad_x, *grads, *deltas, *new_m, *new_v)
```

```python
import functools
import math

import jax
import jax.numpy as jnp
from jax import lax
from jax.experimental import pallas as pl
from jax.experimental.pallas import tpu as pltpu

F32 = jnp.float32
BF16 = jnp.bfloat16

HEAD_DIM = 64
LN_EPS = 1e-5
DEPTH = 1
ALPHA = (2.0 * DEPTH) ** 0.25
ADAM_LR, ADAM_B1, ADAM_B2, ADAM_EPS, ADAM_WD, ADAM_STEP = 0.001, 0.9, 0.999, 1e-08, 0.01, 10
N_DEV = 8
LANES = 128
SUBLANES = 8
HALO = 16
VMEM_LIMIT = 56 * 1024 * 1024

NN = ((1,), (0,))
NT = ((1,), (1,))
TN = ((0,), (0,))


def _dot(a, b, dims):
    return lax.dot_general(a, b, (dims, ((), ())), preferred_element_type=F32)


def _params(*sem):
    return pltpu.CompilerParams(dimension_semantics=sem, vmem_limit_bytes=VMEM_LIMIT)


def _iotas(blk):
    row = lax.broadcasted_iota(jnp.int32, (blk, blk), 0)
    col = lax.broadcasted_iota(jnp.int32, (blk, blk), 1)
    return row, col


def _sb_terms(z):
    e = jnp.exp(-jnp.abs(z))
    lb = jnp.minimum(z, 0.0) - jnp.log(1.0 + e)
    return lb, lb - z, e


def _pair_specs(s, blk, e, pp, branch):
    w = pp * LANES
    nq = s // blk
    ng = e // w
    base = 3 * branch * ng
    q_in = pl.BlockSpec((blk, w), lambda b, g, i: (b * nq + i, base + g))
    k_in = pl.BlockSpec((s, w), lambda b, g, i: (b, base + ng + g))
    v_in = pl.BlockSpec((s, w), lambda b, g, i: (b, base + 2 * ng + g))
    q_out = pl.BlockSpec((blk, w), lambda b, g, i: (b * nq + i, g))
    kv_out = pl.BlockSpec((s, w), lambda b, g, i: (b, g))
    rows = pl.BlockSpec((2 * pp, blk, 1), lambda b, g, i: (b * ng + g, i, 0))
    krow = pl.BlockSpec((2 * pp, 1, s), lambda b, g, i: (b * ng + g, 0, 0))
    return q_in, k_in, v_in, q_out, kv_out, rows, krow


def _to_column(r):
    row, col = _iotas(r.shape[1])
    return jnp.sum(jnp.where(row == col, r, 0.0), axis=1, keepdims=True)


def _half_masks(x):
    low = lax.broadcasted_iota(jnp.int32, x.shape, 1) < HEAD_DIM
    zero = jnp.zeros_like(x)
    return jnp.concatenate([jnp.where(low, x, zero), jnp.where(low, zero, x)], axis=0)


def _tri_sums(xs, tri):
    hi = [x.astype(BF16) for x in xs]
    lo = [(x - h.astype(F32)).astype(BF16) for x, h in zip(xs, hi)]
    n = len(xs)
    blk = xs[0].shape[0]
    r = _dot(jnp.concatenate(hi + lo, axis=0), tri, NN)
    return [r[i * blk:(i + 1) * blk] + r[(n + i) * blk:(n + i + 1) * blk] for i in range(n)]


def _sb_fwd(qkv, b, s, e, blk, pp):
    scale = HEAD_DIM ** -0.5
    nh = 2 * pp
    t = b * s

    def body(q_ref, k_ref, v_ref, o_ref, tot_ref, first_ref):
        qi = pl.program_id(2)
        qm = [_half_masks((q_ref[:, p * LANES:(p + 1) * LANES] * scale).astype(BF16)) for p in range(pp)]
        row, col = _iotas(blk)
        strict = col < row
        after = (row > col).astype(BF16)

        def block(j, carry, diag):
            off = pl.multiple_of(j * blk, blk)
            o_acc, run = carry
            zz = [_dot(qm[p], k_ref[pl.ds(off, blk), p * LANES:(p + 1) * LANES], NT) for p in range(pp)]
            z = [zz[h // 2][(h % 2) * blk:(h % 2 + 1) * blk] for h in range(nh)]
            terms = [_sb_terms(z[h]) for h in range(nh)]
            lom = [jnp.where(strict, terms[h][1], 0.0) if diag else terms[h][1] for h in range(nh)]
            sfx = _tri_sums(lom, after)
            a = [jnp.exp(terms[h][0] + sfx[h] + run[h]) for h in range(nh)]
            if diag:
                a = [jnp.where(strict, a[h], 0.0) for h in range(nh)]
            ab = [a[h].astype(BF16) for h in range(nh)]
            o_new = tuple(
                o_acc[p] + _dot(jnp.concatenate([ab[2 * p], ab[2 * p + 1]], axis=1),
                                _half_masks(v_ref[pl.ds(off, blk), p * LANES:(p + 1) * LANES]), NN)
                for p in range(pp))
            return o_new, tuple(run[h] + sfx[h][:, 0:1] + lom[h][:, 0:1] for h in range(nh))

        def alive(run):
            m = run[0]
            for h in range(1, nh):
                m = jnp.maximum(m, run[h])
            return jnp.max(m) > DEAD

        o_acc, run = block(qi, ((jnp.zeros((blk, LANES), F32),) * pp, (jnp.zeros((blk, 1), F32),) * nh), True)

        def step(c):
            j, _, o_acc, run = c
            o_acc, run = block(j, (o_acc, run), False)
            return j - 1, alive(run), o_acc, run

        j, _, o_acc, run = lax.while_loop(lambda c: jnp.logical_and(c[0] >= 0, c[1]), step,
                                          (qi - 1, alive(run), o_acc, run))
        for p in range(pp):
            o_ref[:, p * LANES:(p + 1) * LANES] = o_acc[p].astype(o_ref.dtype)
        for h in range(nh):
            tot_ref[h] = run[h]
            first_ref[h] = jnp.zeros((blk, 1), F32) + (j + 1).astype(F32)

    q_in, k_in, v_in, q_out, _, rows, _ = _pair_specs(s, blk, e, pp, 0)
    return pl.pallas_call(
        body, name="sb_fwd", grid=(b, e // (pp * LANES), s // blk),
        in_specs=[q_in, k_in, v_in], out_specs=[q_out, rows, rows],
        out_shape=[jax.ShapeDtypeStruct((t, e), BF16)] + [jax.ShapeDtypeStruct((b * e // HEAD_DIM, s, 1), F32)] * 2,
        compiler_params=_params("parallel", "parallel", "arbitrary"),
    )(qkv, qkv, qkv)


def _sb_bwd(qkv, do, tot, first, b, s, e, blk, pp):
    scale = HEAD_DIM ** -0.5
    nh = 2 * pp
    t = b * s
    nq = s // blk

    def body(q_ref, k_ref, v_ref, do_ref, tot_ref, first_ref, dq_ref, dk_ref, dv_ref, dk_acc, dv_acc):
        qi = pl.program_id(2)

        @pl.when(qi == 0)
        def _():
            dk_acc[...] = jnp.zeros_like(dk_acc)
            dv_acc[...] = jnp.zeros_like(dv_acc)

        qm = [_half_masks((q_ref[:, p * LANES:(p + 1) * LANES] * scale).astype(BF16)) for p in range(pp)]
        dom = [_half_masks(do_ref[:, p * LANES:(p + 1) * LANES].astype(BF16)) for p in range(pp)]
        tot_t = [tot_ref[h] for h in range(nh)]
        row, col = _iotas(blk)
        strict = col < row
        upto = (row <= col).astype(BF16)
        before = (row < col).astype(BF16)

        def block(j, carry, diag):
            off = pl.multiple_of(j * blk, blk)
            dq_acc, cl, cg = carry
            kp = [k_ref[pl.ds(off, blk), p * LANES:(p + 1) * LANES] for p in range(pp)]
            vp = [v_ref[pl.ds(off, blk), p * LANES:(p + 1) * LANES] for p in range(pp)]
            zz = [_dot(qm[p], kp[p], NT) for p in range(pp)]
            dd = [_dot(dom[p], vp[p], NT) for p in range(pp)]
            z = [zz[h // 2][(h % 2) * blk:(h % 2 + 1) * blk] for h in range(nh)]
            da = [dd[h // 2][(h % 2) * blk:(h % 2 + 1) * blk] for h in range(nh)]
            terms = [_sb_terms(z[h]) for h in range(nh)]
            lom = [jnp.where(strict, terms[h][1], 0.0) if diag else terms[h][1] for h in range(nh)]
            pre = _tri_sums(lom, upto)
            a = [jnp.exp(terms[h][0] + (tot_t[h] - cl[h] - pre[h])) for h in range(nh)]
            if diag:
                a = [jnp.where(strict, a[h], 0.0) for h in range(nh)]
            g = [a[h] * da[h] for h in range(nh)]
            pw = _tri_sums(g, before)
            dzb = []
            for h in range(nh):
                ex = terms[h][2]
                r = 1.0 / (1.0 + ex)
                er = ex * r
                pos = z[h] >= 0.0
                dz = g[h] * jnp.where(pos, er, r) - (cg[h] + pw[h]) * jnp.where(pos, r, er)
                if diag:
                    dz = jnp.where(strict, dz, 0.0)
                dzb.append(dz.astype(BF16))
            ab = [a[h].astype(BF16) for h in range(nh)]
            for p in range(pp):
                cols = slice(p * LANES, (p + 1) * LANES)
                dk_acc[pl.ds(off, blk), cols] += _dot(jnp.concatenate([dzb[2 * p], dzb[2 * p + 1]], axis=0), qm[p], TN)
                dv_acc[pl.ds(off, blk), cols] += _dot(jnp.concatenate([ab[2 * p], ab[2 * p + 1]], axis=0), dom[p], TN)
            dq_new = tuple(dq_acc[p] + _dot(jnp.concatenate([dzb[2 * p], dzb[2 * p + 1]], axis=1), _half_masks(kp[p]), NN)
                           for p in range(pp))
            return (dq_new, tuple(cl[h] + pre[h][:, blk - 1:blk] for h in range(nh)),
                    tuple(cg[h] + pw[h][:, blk - 1:blk] + g[h][:, blk - 1:blk] for h in range(nh)))

        zero1 = (jnp.zeros((blk, 1), F32),) * nh
        j0 = jnp.clip(jnp.max(first_ref[0]).astype(jnp.int32), 0, qi)
        carry = lax.fori_loop(j0, qi, lambda j, c: block(j, c, False), ((jnp.zeros((blk, LANES), F32),) * pp, zero1, zero1))
        dq_acc, _, _ = block(qi, carry, True)
        for p in range(pp):
            dq_ref[:, p * LANES:(p + 1) * LANES] = (dq_acc[p] * scale).astype(BF16)

        @pl.when(qi == nq - 1)
        def _():
            dk_ref[...] = dk_acc[...].astype(BF16)
            dv_ref[...] = dv_acc[...].astype(BF16)

    q_in, k_in, v_in, q_out, kv_out, rows, _ = _pair_specs(s, blk, e, pp, 0)
    w = pp * LANES
    return pl.pallas_call(
        body, name="sb_bwd", grid=(b, e // w, nq),
        in_specs=[q_in, k_in, v_in, q_out, rows, rows], out_specs=[q_out, kv_out, kv_out],
        out_shape=[jax.ShapeDtypeStruct((t, e), BF16)] * 3,
        scratch_shapes=[pltpu.VMEM((s, w), F32), pltpu.VMEM((s, w), F32)],
        compiler_params=_params("parallel", "parallel", "arbitrary"),
    )(qkv, qkv, qkv, do, tot, first)


NEG = -1e30
DEAD = -110.0


def _ride_along(comm, src_refs, dst_refs, sems, first, last):
    if comm is None:
        return lambda: None

    @pl.when(first)
    def _():
        for cp in _peer_copies(comm[0], src_refs, dst_refs, *sems):
            cp.start()

    def finish():
        @pl.when(last)
        def _():
            for cp in _peer_copies(comm[0], src_refs, dst_refs, *sems):
                cp.wait()

    return finish


def _grid_ends(grid):
    ids = [pl.program_id(a) for a in range(len(grid))]
    first = functools.reduce(jnp.logical_and, [i == 0 for i in ids])
    last = functools.reduce(jnp.logical_and, [i == g - 1 for i, g in zip(ids, grid)])
    return first, last


def _fox_fwd(qkv, ct, b, s, e, blk, pp, comm=None):
    scale = HEAD_DIM ** -0.5
    nh = 2 * pp
    t = b * s
    nc = len(comm[1]) if comm else 0
    grid = (b, e // (pp * LANES), s // blk)

    def body(*refs):
        q_ref, k_ref, v_ref, ct_ref = refs[:4]
        o_ref, lse_ref = refs[4 + nc:6 + nc]
        finish = _ride_along(comm, refs[4:4 + nc], refs[6 + nc:6 + 2 * nc], refs[6 + 2 * nc:], *_grid_ends(grid))
        qi = pl.program_id(2)
        qm = [_half_masks((q_ref[:, p * LANES:(p + 1) * LANES] * scale).astype(BF16)) for p in range(pp)]
        cq = [_to_column(ct_ref[h, :, pl.ds(pl.multiple_of(qi * blk, blk), blk)]) for h in range(nh)]
        row, col = _iotas(blk)
        causal = col <= row
        low = lax.broadcasted_iota(jnp.int32, (blk, LANES), 1) < HEAD_DIM

        def block(j, carry, diag):
            off = pl.multiple_of(j * blk, blk)
            m, l, acc = carry
            zz = [_dot(qm[p], k_ref[pl.ds(off, blk), p * LANES:(p + 1) * LANES], NT) for p in range(pp)]
            z = [zz[h // 2][(h % 2) * blk:(h % 2 + 1) * blk] + (cq[h] - ct_ref[h, :, pl.ds(off, blk)]) for h in range(nh)]
            if diag:
                z = [jnp.where(causal, z[h], NEG) for h in range(nh)]
            m_new = tuple(jnp.maximum(m[h], jnp.max(z[h], axis=1, keepdims=True)) for h in range(nh))
            w = [jnp.exp(m[h] - m_new[h]) for h in range(nh)]
            pr = [jnp.exp(z[h] - m_new[h]) for h in range(nh)]
            pb = [pr[h].astype(BF16) for h in range(nh)]
            pv = [_dot(jnp.concatenate([pb[2 * p], pb[2 * p + 1]], axis=1),
                       _half_masks(v_ref[pl.ds(off, blk), p * LANES:(p + 1) * LANES]), NN) for p in range(pp)]
            acc_new = tuple(jnp.where(low, w[2 * p], w[2 * p + 1]) * acc[p] + pv[p] for p in range(pp))
            l_new = tuple(w[h] * l[h] + jnp.sum(pr[h], axis=1, keepdims=True) for h in range(nh))
            return m_new, l_new, acc_new

        zero = ((jnp.full((blk, 1), NEG, F32),) * nh, (jnp.zeros((blk, 1), F32),) * nh, (jnp.zeros((blk, LANES), F32),) * pp)
        carry = block(qi, zero, True)
        m, l, acc = lax.fori_loop(0, qi, lambda j, c_: block(j, c_, False), carry)
        for p in range(pp):
            o_ref[:, p * LANES:(p + 1) * LANES] = acc[p] / jnp.where(low, l[2 * p], l[2 * p + 1])
        for h in range(nh):
            lse_ref[h] = m[h] + jnp.log(l[h])
        finish()

    q_in, k_in, v_in, q_out, _, rows, krow = _pair_specs(s, blk, e, pp, 1)
    outs, sems = _peer_shapes(comm[0], comm[1]) if comm else ([], [])
    return pl.pallas_call(
        body, name="fox_fwd", grid=grid,
        in_specs=[q_in, k_in, v_in, krow] + [ANY] * nc, out_specs=[q_out, rows] + [ANY] * nc,
        out_shape=[jax.ShapeDtypeStruct((t, e), F32), jax.ShapeDtypeStruct((b * e // HEAD_DIM, s, 1), F32)] + outs,
        scratch_shapes=sems,
        compiler_params=_params("arbitrary", "arbitrary", "arbitrary"),
    )(qkv, qkv, qkv, ct, *(comm[1] if comm else ()))


def _fox_bwd(qkv, ct, do, o, lse, b, s, e, blk, pp, comm=None):
    scale = HEAD_DIM ** -0.5
    nh = 2 * pp
    t = b * s
    nq = s // blk
    nc = len(comm[1]) if comm else 0
    w = pp * LANES
    grid = (b, e // w, nq)

    def body(*refs):
        q_ref, k_ref, v_ref, ct_ref, do_ref, o_ref, lse_ref = refs[:7]
        dq_ref, dk_ref, dv_ref, dct_ref = refs[7 + nc:11 + nc]
        dk_acc, dv_acc = refs[11 + 2 * nc:13 + 2 * nc]
        finish = _ride_along(comm, refs[7:7 + nc], refs[11 + nc:11 + 2 * nc], refs[13 + 2 * nc:], *_grid_ends(grid))
        qi = pl.program_id(2)

        @pl.when(qi == 0)
        def _():
            dk_acc[...] = jnp.zeros_like(dk_acc)
            dv_acc[...] = jnp.zeros_like(dv_acc)
            dct_ref[...] = jnp.zeros_like(dct_ref)

        qm = [_half_masks((q_ref[:, p * LANES:(p + 1) * LANES] * scale).astype(BF16)) for p in range(pp)]
        dob = [do_ref[:, p * LANES:(p + 1) * LANES].astype(BF16) for p in range(pp)]
        dom = [_half_masks(dob[p]) for p in range(pp)]
        low = lax.broadcasted_iota(jnp.int32, (blk, LANES), 1) < HEAD_DIM
        delta = []
        for p in range(pp):
            prod = dob[p].astype(F32) * o_ref[:, p * LANES:(p + 1) * LANES]
            delta.append(jnp.sum(jnp.where(low, prod, 0.0), axis=1, keepdims=True))
            delta.append(jnp.sum(jnp.where(low, 0.0, prod), axis=1, keepdims=True))
        cq = [_to_column(ct_ref[h, :, pl.ds(pl.multiple_of(qi * blk, blk), blk)]) for h in range(nh)]
        lse_t = [lse_ref[h] for h in range(nh)]
        row, col = _iotas(blk)
        causal = col <= row

        def block(j, dq_acc, diag):
            off = pl.multiple_of(j * blk, blk)
            kp = [k_ref[pl.ds(off, blk), p * LANES:(p + 1) * LANES] for p in range(pp)]
            zz = [_dot(qm[p], kp[p], NT) for p in range(pp)]
            dd = [_dot(dom[p], v_ref[pl.ds(off, blk), p * LANES:(p + 1) * LANES], NT) for p in range(pp)]
            z = [zz[h // 2][(h % 2) * blk:(h % 2 + 1) * blk] + (cq[h] - ct_ref[h, :, pl.ds(off, blk)]) for h in range(nh)]
            pr = [jnp.exp(z[h] - lse_t[h]) for h in range(nh)]
            if diag:
                pr = [jnp.where(causal, pr[h], 0.0) for h in range(nh)]
            ds = [pr[h] * (dd[h // 2][(h % 2) * blk:(h % 2 + 1) * blk] - delta[h]) for h in range(nh)]
            dsb = [ds[h].astype(BF16) for h in range(nh)]
            pb = [pr[h].astype(BF16) for h in range(nh)]
            for p in range(pp):
                cols = slice(p * LANES, (p + 1) * LANES)
                dk_acc[pl.ds(off, blk), cols] += _dot(jnp.concatenate([dsb[2 * p], dsb[2 * p + 1]], axis=0), qm[p], TN)
                dv_acc[pl.ds(off, blk), cols] += _dot(jnp.concatenate([pb[2 * p], pb[2 * p + 1]], axis=0), dom[p], TN)
            for h in range(nh):
                dct_ref[h, :, pl.ds(off, blk)] -= jnp.sum(ds[h], axis=0, keepdims=True)
            return tuple(dq_acc[p] + _dot(jnp.concatenate([dsb[2 * p], dsb[2 * p + 1]], axis=1), _half_masks(kp[p]), NN)
                         for p in range(pp))

        dq_acc = lax.fori_loop(0, qi, lambda j, a: block(j, a, False), (jnp.zeros((blk, LANES), F32),) * pp)
        dq_acc = block(qi, dq_acc, True)
        for p in range(pp):
            dq_ref[:, p * LANES:(p + 1) * LANES] = (dq_acc[p] * scale).astype(BF16)

        @pl.when(qi == nq - 1)
        def _():
            dk_ref[...] = dk_acc[...].astype(BF16)
            dv_ref[...] = dv_acc[...].astype(BF16)

        finish()

    q_in, k_in, v_in, q_out, kv_out, rows, krow = _pair_specs(s, blk, e, pp, 1)
    outs, sems = _peer_shapes(comm[0], comm[1]) if comm else ([], [])
    return pl.pallas_call(
        body, name="fox_bwd", grid=grid,
        in_specs=[q_in, k_in, v_in, krow, q_out, q_out, rows] + [ANY] * nc,
        out_specs=[q_out, kv_out, kv_out, krow] + [ANY] * nc,
        out_shape=[jax.ShapeDtypeStruct((t, e), BF16)] * 3 + [jax.ShapeDtypeStruct((b * e // HEAD_DIM, 1, s), F32)] + outs,
        scratch_shapes=[pltpu.VMEM((s, w), F32), pltpu.VMEM((s, w), F32)] + sems,
        compiler_params=_params("arbitrary", "arbitrary", "arbitrary"),
    )(qkv, qkv, qkv, ct, do, o, lse, *(comm[1] if comm else ()))


def _scan_rows(f2, group, mode, d2=None):
    n = f2.shape[0]

    def body(*refs):
        f_ref, o_ref = refs[0], refs[-1]
        f = f_ref[...]
        row, col = _iotas(LANES)
        grow = lax.broadcasted_iota(jnp.int32, (n, n), 0)
        gcol = lax.broadcasted_iota(jnp.int32, (n, n), 1)
        same = (grow // group) == (gcol // group)
        e = jnp.exp(-jnp.abs(f))
        if mode == "fwd":
            x = jnp.minimum(f, 0.0) - jnp.log1p(e)
            within = (row <= col).astype(F32)
            earlier = (same & (gcol < grow)).astype(F32)
        else:
            x = refs[1][...]
            within = (row >= col).astype(F32)
            earlier = (same & (gcol > grow)).astype(F32)
        y = jnp.dot(x, within, preferred_element_type=F32, precision=lax.Precision.HIGHEST)
        tot = jnp.sum(x, axis=1, keepdims=True)
        y = y + jnp.dot(earlier, tot, preferred_element_type=F32, precision=lax.Precision.HIGHEST)
        if mode == "bwd":
            r = 1.0 / (1.0 + e)
            y = y * jnp.where(f >= 0.0, e * r, r)
        o_ref[...] = y

    args = (f2,) if mode == "fwd" else (f2, d2)
    return pl.pallas_call(body, name="logf_" + mode, out_shape=jax.ShapeDtypeStruct(f2.shape, F32),
                          compiler_params=_params())(*args)


def _matmul(a, b, dims, *, tm, tn, tk, out_dtype, name, bias=None, res=None, res_scale=1.0, b_outer=False):
    def ij(g0, g1):
        return (g1, g0) if b_outer else (g0, g1)

    if dims == NN:
        (m, kk), n = a.shape, b.shape[1]
        a_spec = pl.BlockSpec((tm, tk), lambda g0, g1, k: (ij(g0, g1)[0], k))
        b_spec = pl.BlockSpec((tk, tn), lambda g0, g1, k: (k, ij(g0, g1)[1]))
    elif dims == NT:
        (m, kk), n = a.shape, b.shape[0]
        a_spec = pl.BlockSpec((tm, tk), lambda g0, g1, k: (ij(g0, g1)[0], k))
        b_spec = pl.BlockSpec((tn, tk), lambda g0, g1, k: (ij(g0, g1)[1], k))
    else:
        (kk, m), n = a.shape, b.shape[1]
        a_spec = pl.BlockSpec((tk, tm), lambda g0, g1, k: (k, ij(g0, g1)[0]))
        b_spec = pl.BlockSpec((tk, tn), lambda g0, g1, k: (k, ij(g0, g1)[1]))
    assert m % tm == 0 and n % tn == 0 and kk % tk == 0, (name, m, n, kk, tm, tn, tk)
    nk = kk // tk
    extras, extra_specs = [], []
    if bias is not None:
        extras.append(bias)
        extra_specs.append(pl.BlockSpec((1, tn), lambda g0, g1, k: (0, ij(g0, g1)[1])))
    if res is not None:
        extras.append(res)
        extra_specs.append(pl.BlockSpec((tm, tn), lambda g0, g1, k: ij(g0, g1)))

    def finish(out, rest, o_ref):
        idx = 0
        if bias is not None:
            out = out + rest[idx][...]
            idx += 1
        if res is not None:
            out = out + res_scale * rest[idx][...]
        o_ref[...] = out.astype(o_ref.dtype)

    def body_single(a_ref, b_ref, *rest):
        finish(_dot(a_ref[...].astype(BF16), b_ref[...].astype(BF16), dims), rest, rest[-1])

    def body_acc(a_ref, b_ref, *rest):
        o_ref, acc_ref = rest[-2], rest[-1]
        k = pl.program_id(2)
        part = _dot(a_ref[...].astype(BF16), b_ref[...].astype(BF16), dims)

        @pl.when(k == 0)
        def _():
            acc_ref[...] = part

        @pl.when(k > 0)
        def _():
            acc_ref[...] += part

        @pl.when(k == nk - 1)
        def _():
            finish(acc_ref[...], rest, o_ref)

    grid = (n // tn, m // tm, nk) if b_outer else (m // tm, n // tn, nk)
    return pl.pallas_call(
        body_single if nk == 1 else body_acc, name=name, grid=grid,
        in_specs=[a_spec, b_spec] + extra_specs,
        out_specs=pl.BlockSpec((tm, tn), lambda g0, g1, k: ij(g0, g1)),
        out_shape=jax.ShapeDtypeStruct((m, n), out_dtype),
        scratch_shapes=[] if nk == 1 else [pltpu.VMEM((tm, tn), F32)],
        compiler_params=_params("parallel", "parallel", "arbitrary"),
    )(a, b, *extras)


def _input_grad(dhs, ws, dr, tm, comm=None):
    t, d = dr.shape
    npc = len(dhs)
    nc = len(comm[1]) if comm else 0
    grid = (t // tm,)

    def body(*refs):
        dr_ref = refs[2 * npc]
        first_in = 2 * npc + 1
        o_ref = refs[first_in + nc]
        finish = _ride_along(comm, refs[first_in:first_in + nc], refs[first_in + nc + 1:first_in + 2 * nc + 1],
                             refs[first_in + 2 * nc + 1:], *_grid_ends(grid))
        out = ALPHA * dr_ref[...]
        for p in range(npc):
            out = out + _dot(refs[p][...], refs[npc + p][...], NT)
        o_ref[...] = out
        finish()

    rows = pl.BlockSpec((tm, d), lambda i: (i, 0))
    outs, sems = _peer_shapes(comm[0], comm[1]) if comm else ([], [])
    return pl.pallas_call(
        body, name="input_grad", grid=grid,
        in_specs=[pl.BlockSpec((tm, a.shape[1]), lambda i: (i, 0)) for a in dhs]
        + [pl.BlockSpec(w.shape, lambda i: (0, 0)) for w in ws] + [rows] + [ANY] * nc,
        out_specs=[rows] + [ANY] * nc,
        out_shape=[jax.ShapeDtypeStruct((t, d), F32)] + outs,
        scratch_shapes=sems,
        compiler_params=_params("arbitrary"),
    )(*dhs, *ws, dr, *(comm[1] if comm else ()))


def _sigmoid(x):
    e = jnp.exp(-jnp.abs(x))
    r = 1.0 / (1.0 + e)
    return jnp.where(x >= 0.0, r, e * r)


def _proj_gate_fwd(o_sb, o_fx, wp_sb, wp_fx, g, tm):
    t, e = o_sb.shape
    d = wp_sb.shape[1]

    def body(osb_ref, ofx_ref, wsb_ref, wfx_ref, gsb_ref, gfx_ref, mg_ref, ysb_ref, yfx_ref):
        ysb = _dot(osb_ref[...].astype(BF16), wsb_ref[...], NN)
        yfx = _dot(ofx_ref[...].astype(BF16), wfx_ref[...], NN)
        ysb_ref[...] = ysb
        yfx_ref[...] = yfx
        mg_ref[...] = (_sigmoid(gsb_ref[...]) * ysb + _sigmoid(gfx_ref[...]) * yfx).astype(BF16)

    rows_e = pl.BlockSpec((tm, e), lambda i: (i, 0))
    rows_d = pl.BlockSpec((tm, d), lambda i: (i, 0))
    w_spec = pl.BlockSpec((e, d), lambda i: (0, 0))
    return pl.pallas_call(
        body, name="proj_gate_fwd", grid=(t // tm,),
        in_specs=[rows_e, rows_e, w_spec, w_spec, rows_d, pl.BlockSpec((tm, d), lambda i: (i, 1))],
        out_specs=[rows_d, rows_d, rows_d],
        out_shape=[jax.ShapeDtypeStruct((t, d), BF16), jax.ShapeDtypeStruct((t, d), F32), jax.ShapeDtypeStruct((t, d), F32)],
        compiler_params=_params("parallel"),
    )(o_sb, o_fx, wp_sb, wp_fx, g, g)


def _gate_bwd(dr, w_out, y_sb, y_fx, g, tm):
    t, d = dr.shape

    def body(dr_ref, w_ref, ysb_ref, yfx_ref, gsb_ref, gfx_ref, dysb_ref, dyfx_ref, dg_ref):
        dm = _dot(dr_ref[...], w_ref[...], NT)
        ssb = _sigmoid(gsb_ref[...])
        sfx = _sigmoid(gfx_ref[...])
        dysb_ref[...] = (dm * ssb).astype(BF16)
        dyfx_ref[...] = (dm * sfx).astype(BF16)
        dg_ref[:, 0:d] = (dm * ysb_ref[...] * ssb * (1.0 - ssb)).astype(BF16)
        dg_ref[:, d:2 * d] = (dm * yfx_ref[...] * sfx * (1.0 - sfx)).astype(BF16)

    rows = pl.BlockSpec((tm, d), lambda i: (i, 0))
    rows1 = pl.BlockSpec((tm, d), lambda i: (i, 1))
    return pl.pallas_call(
        body, name="gate_bwd", grid=(t // tm,),
        in_specs=[rows, pl.BlockSpec((d, d), lambda i: (0, 0)), rows, rows, rows, rows1],
        out_specs=[rows, rows, pl.BlockSpec((tm, 2 * d), lambda i: (i, 0))],
        out_shape=[jax.ShapeDtypeStruct((t, d), BF16)] * 2 + [jax.ShapeDtypeStruct((t, 2 * d), BF16)],
        compiler_params=_params("parallel"),
    )(dr, w_out, y_sb, y_fx, g, g)


def _mm_res_ln(a, w, xres, gamma, beta, tm, name):
    t, kk = a.shape
    d = w.shape[1]

    def body(a_ref, w_ref, x_ref, g_ref, b_ref, xn_ref, xh_ref, rs_ref, xb_ref):
        r = ALPHA * x_ref[...] + _dot(a_ref[...].astype(BF16), w_ref[...], NN)
        mean = jnp.mean(r, axis=1, keepdims=True)
        cen = r - mean
        rstd = lax.rsqrt(jnp.mean(cen * cen, axis=1, keepdims=True) + LN_EPS)
        xh = cen * rstd
        xn = xh * g_ref[...] + b_ref[...]
        xh_ref[...] = xh
        xn_ref[...] = xn
        xb_ref[...] = xn.astype(BF16)
        rs_ref[...] = rstd

    rows_d = pl.BlockSpec((tm, d), lambda i: (i, 0))
    vec = pl.BlockSpec((1, d), lambda i: (0, 0))
    return pl.pallas_call(
        body, name=name, grid=(t // tm,),
        in_specs=[pl.BlockSpec((tm, kk), lambda i: (i, 0)), pl.BlockSpec((kk, d), lambda i: (0, 0)), rows_d, vec, vec],
        out_specs=[rows_d, rows_d, pl.BlockSpec((tm, 1), lambda i: (i, 0)), rows_d],
        out_shape=[jax.ShapeDtypeStruct((t, d), F32), jax.ShapeDtypeStruct((t, d), F32), jax.ShapeDtypeStruct((t, 1), F32),
                   jax.ShapeDtypeStruct((t, d), BF16)],
        compiler_params=_params("parallel"),
    )(a, w, xres, gamma, beta)


def _ln_bwd_math(dy, xh, rstd, gamma):
    dxh = dy * gamma
    m1 = jnp.mean(dxh, axis=1, keepdims=True)
    m2 = jnp.mean(dxh * xh, axis=1, keepdims=True)
    return rstd * (dxh - m1 - xh * m2)


def _rowsum8(x):
    tm, n = x.shape
    return jnp.sum(x.reshape(tm // SUBLANES, SUBLANES, n), axis=0)


def _fold8(ref):
    ref[0:1, :] = jnp.sum(ref[...], axis=0, keepdims=True)


def _loss_ln_bwd(x2, xh, rstd, gamma, target, tm):
    t, d = x2.shape

    def body(x_ref, xh_ref, rs_ref, g_ref, tg_ref, dr_ref, dg_ref, db_ref, ls_ref, drb_ref):
        @pl.when(pl.program_id(0) == 0)
        def _():
            dg_ref[...] = jnp.zeros_like(dg_ref)
            db_ref[...] = jnp.zeros_like(db_ref)
            ls_ref[...] = jnp.zeros_like(ls_ref)

        err = x_ref[...] - tg_ref[...]
        xh = xh_ref[...]
        dy = err * (1.0 / d)
        dr = _ln_bwd_math(dy, xh, rs_ref[...], g_ref[...])
        dr_ref[...] = dr
        drb_ref[...] = dr.astype(BF16)
        dg_ref[...] += _rowsum8(dy * xh)
        db_ref[...] += _rowsum8(dy)
        sq = _rowsum8(err * err)
        part = sq[:, 0:LANES]
        for j in range(1, d // LANES):
            part = part + sq[:, j * LANES:(j + 1) * LANES]
        ls_ref[...] += part * (0.5 / d)

        @pl.when(pl.program_id(0) == t // tm - 1)
        def _():
            _fold8(dg_ref)
            _fold8(db_ref)
            ls_ref[0:1, 0:1] = jnp.sum(jnp.sum(ls_ref[...], axis=0, keepdims=True), axis=1, keepdims=True)

    rows = pl.BlockSpec((tm, d), lambda i: (i, 0))
    acc = pl.BlockSpec((SUBLANES, d), lambda i: (0, 0))
    return pl.pallas_call(
        body, name="loss_ln_bwd", grid=(t // tm,),
        in_specs=[rows, rows, pl.BlockSpec((tm, 1), lambda i: (i, 0)), pl.BlockSpec((1, d), lambda i: (0, 0)), rows],
        out_specs=[rows, acc, acc, pl.BlockSpec((SUBLANES, LANES), lambda i: (0, 0)), rows],
        out_shape=[jax.ShapeDtypeStruct((t, d), F32), jax.ShapeDtypeStruct((SUBLANES, d), F32),
                   jax.ShapeDtypeStruct((SUBLANES, d), F32), jax.ShapeDtypeStruct((SUBLANES, LANES), F32),
                   jax.ShapeDtypeStruct((t, d), BF16)],
        compiler_params=_params("arbitrary"),
    )(x2, xh, rstd, gamma, target)


def _ln_bwd(dr_next, dlin, xh, rstd, gamma, tm):
    t, d = xh.shape

    def body(dn_ref, dl_ref, xh_ref, rs_ref, g_ref, dr_ref, dg_ref, db_ref, drb_ref):
        @pl.when(pl.program_id(0) == 0)
        def _():
            dg_ref[...] = jnp.zeros_like(dg_ref)
            db_ref[...] = jnp.zeros_like(db_ref)

        dy = ALPHA * dn_ref[...] + dl_ref[...]
        xh = xh_ref[...]
        dr = _ln_bwd_math(dy, xh, rs_ref[...], g_ref[...])
        dr_ref[...] = dr
        drb_ref[...] = dr.astype(BF16)
        dg_ref[...] += _rowsum8(dy * xh)
        db_ref[...] += _rowsum8(dy)

        @pl.when(pl.program_id(0) == t // tm - 1)
        def _():
            _fold8(dg_ref)
            _fold8(db_ref)

    rows = pl.BlockSpec((tm, d), lambda i: (i, 0))
    acc = pl.BlockSpec((SUBLANES, d), lambda i: (0, 0))
    return pl.pallas_call(
        body, name="ln_bwd", grid=(t // tm,),
        in_specs=[rows, rows, rows, pl.BlockSpec((tm, 1), lambda i: (i, 0)), pl.BlockSpec((1, d), lambda i: (0, 0))],
        out_specs=[rows, acc, acc, rows],
        out_shape=[jax.ShapeDtypeStruct((t, d), F32), jax.ShapeDtypeStruct((SUBLANES, d), F32),
                   jax.ShapeDtypeStruct((SUBLANES, d), F32), jax.ShapeDtypeStruct((t, d), BF16)],
        compiler_params=_params("arbitrary"),
    )(dr_next, dlin, xh, rstd, gamma)


def _shift_rows(x, halo, shift, row):
    out = pltpu.roll(x, shift, 0)
    n = halo.shape[0]
    for r in range(shift):
        out = jnp.where(row == r, halo[n - shift + r:n - shift + r + 1, :], out)
    return out


def _unshift_rows(x, halo, shift, row, tm):
    out = pltpu.roll(x, tm - shift, 0)
    for r in range(shift):
        out = jnp.where(row == tm - shift + r, halo[r:r + 1, :], out)
    return out


def _conv_pre(ug_ref, halo_ref, wc_ref, bc_ref, first, tm):
    ug = ug_ref[...].astype(F32)
    halo = jnp.where(first, 0.0, halo_ref[...].astype(F32))
    row = lax.broadcasted_iota(jnp.int32, ug.shape, 0)
    wc = wc_ref[...]
    um1 = _shift_rows(ug, halo, 1, row)
    um2 = _shift_rows(ug, halo, 2, row)
    c = bc_ref[...] + wc[2:3, :] * ug + wc[1:2, :] * um1 + wc[0:1, :] * um2
    return c, ug, um1, um2


INV_SQRT2 = 1.0 / math.sqrt(2.0)
INV_SQRT2PI = 1.0 / math.sqrt(2.0 * math.pi)


def _conv_glu_fwd(u, wc, bc, seq, tm):
    t, f2 = u.shape
    f = f2 // 2
    per_seq = seq // tm
    hb = tm // HALO

    def body(ug_ref, halo_ref, uv_ref, wc_ref, bc_ref, a_ref):
        first = (pl.program_id(0) % per_seq) == 0
        c, _, _, _ = _conv_pre(ug_ref, halo_ref, wc_ref, bc_ref, first, tm)
        gelu = 0.5 * c * (1.0 + lax.erf(c * INV_SQRT2))
        a_ref[...] = (gelu * uv_ref[...].astype(F32)).astype(BF16)

    return pl.pallas_call(
        body, name="conv_glu_fwd", grid=(t // tm,),
        in_specs=[pl.BlockSpec((tm, f), lambda i: (i, 0)),
                  pl.BlockSpec((HALO, f), lambda i: (jnp.maximum(i * hb - 1, 0), 0)),
                  pl.BlockSpec((tm, f), lambda i: (i, 1)),
                  pl.BlockSpec((3, f), lambda i: (0, 0)), pl.BlockSpec((1, f), lambda i: (0, 0))],
        out_specs=pl.BlockSpec((tm, f), lambda i: (i, 0)),
        out_shape=jax.ShapeDtypeStruct((t, f), BF16),
        compiler_params=_params("parallel"),
    )(u, u, u, wc, bc)


def _conv_glu_bwd1(u, da, wc, bc, seq, tm):
    t, f2 = u.shape
    f = f2 // 2
    per_seq = seq // tm
    hb = tm // HALO

    def body(ug_ref, halo_ref, uv_ref, da_ref, wc_ref, bc_ref, dc_ref, duv_ref):
        first = (pl.program_id(0) % per_seq) == 0
        c, _, _, _ = _conv_pre(ug_ref, halo_ref, wc_ref, bc_ref, first, tm)
        cdf = 0.5 * (1.0 + lax.erf(c * INV_SQRT2))
        pdf = jnp.exp(-0.5 * c * c) * INV_SQRT2PI
        da = da_ref[...].astype(F32)
        duv_ref[...] = (da * (c * cdf)).astype(BF16)
        dc_ref[...] = (da * uv_ref[...].astype(F32) * (cdf + c * pdf)).astype(BF16)

    rows = pl.BlockSpec((tm, f), lambda i: (i, 0))
    return pl.pallas_call(
        body, name="conv_glu_bwd1", grid=(t // tm,),
        in_specs=[rows, pl.BlockSpec((HALO, f), lambda i: (jnp.maximum(i * hb - 1, 0), 0)),
                  pl.BlockSpec((tm, f), lambda i: (i, 1)), rows,
                  pl.BlockSpec((3, f), lambda i: (0, 0)), pl.BlockSpec((1, f), lambda i: (0, 0))],
        out_specs=[rows, pl.BlockSpec((tm, f), lambda i: (i, 1))],
        out_shape=[jax.ShapeDtypeStruct((t, f), BF16), jax.ShapeDtypeStruct((t, f2), BF16)],
        compiler_params=_params("parallel"),
    )(u, u, u, da, wc, bc)


def _conv_glu_bwd2(u, dc, wc, du, seq, tm):
    t, f2 = u.shape
    f = f2 // 2
    per_seq = seq // tm
    hb = tm // HALO
    nblk = t // HALO

    def body(ug_ref, dc_ref, nxt_ref, wc_ref, du_in_ref, dug_ref, w0_ref, w1_ref, w2_ref, b_ref):
        i = pl.program_id(0)

        @pl.when(i == 0)
        def _():
            for r in (w0_ref, w1_ref, w2_ref, b_ref):
                r[...] = jnp.zeros_like(r)

        last = (i % per_seq) == per_seq - 1
        ug = ug_ref[...].astype(F32)
        nxt = jnp.where(last, 0.0, nxt_ref[...].astype(F32))
        row = lax.broadcasted_iota(jnp.int32, ug.shape, 0)
        dc = dc_ref[...].astype(F32)
        wc = wc_ref[...]
        dp1 = _unshift_rows(dc, nxt, 1, row, tm)
        dp2 = _unshift_rows(dc, nxt, 2, row, tm)
        dug_ref[...] = (wc[2:3, :] * dc + wc[1:2, :] * dp1 + wc[0:1, :] * dp2).astype(BF16)
        w2_ref[...] += _rowsum8(dc * ug)
        w1_ref[...] += _rowsum8(dp1 * ug)
        w0_ref[...] += _rowsum8(dp2 * ug)
        b_ref[...] += _rowsum8(dc)

        @pl.when(i == t // tm - 1)
        def _():
            for r in (w0_ref, w1_ref, w2_ref, b_ref):
                _fold8(r)

    rows = pl.BlockSpec((tm, f), lambda i: (i, 0))
    acc = pl.BlockSpec((SUBLANES, f), lambda i: (0, 0))
    return pl.pallas_call(
        body, name="conv_glu_bwd2", grid=(t // tm,),
        in_specs=[rows, rows, pl.BlockSpec((HALO, f), lambda i: (jnp.minimum((i + 1) * hb, nblk - 1), 0)),
                  pl.BlockSpec((3, f), lambda i: (0, 0)), pl.BlockSpec(memory_space=pl.ANY)],
        out_specs=[rows, acc, acc, acc, acc],
        out_shape=[jax.ShapeDtypeStruct((t, f2), BF16)] + [jax.ShapeDtypeStruct((SUBLANES, f), F32)] * 4,
        input_output_aliases={4: 0},
        compiler_params=_params("arbitrary"),
    )(u, dc, dc, wc, du)


def _colsum(x, tm, name):
    t, n = x.shape

    def body(x_ref, o_ref):
        @pl.when(pl.program_id(0) == 0)
        def _():
            o_ref[...] = jnp.zeros_like(o_ref)

        o_ref[...] += _rowsum8(x_ref[...].astype(F32))

        @pl.when(pl.program_id(0) == t // tm - 1)
        def _():
            _fold8(o_ref)

    return pl.pallas_call(
        body, name=name, grid=(t // tm,),
        in_specs=[pl.BlockSpec((tm, n), lambda i: (i, 0))],
        out_specs=pl.BlockSpec((SUBLANES, n), lambda i: (0, 0)),
        out_shape=jax.ShapeDtypeStruct((SUBLANES, n), F32),
        compiler_params=_params("arbitrary"),
    )(x)


def _adamw(w, gparts, m, v, name):
    p, r, c = gparts.shape
    tr = r
    for cand in (512, 256, 128, 64, 32, 16, 8):
        if cand * p <= 1024 and r % cand == 0 and r > cand:
            tr = cand
            break
    c1 = 1.0 - ADAM_B1 ** ADAM_STEP
    c2 = 1.0 - ADAM_B2 ** ADAM_STEP

    def body(w_ref, g_ref, m_ref, v_ref, go_ref, d_ref, mo_ref, vo_ref):
        g = g_ref[0].astype(F32)
        for i in range(1, p):
            g = g + g_ref[i].astype(F32)
        mn = ADAM_B1 * m_ref[...] + (1.0 - ADAM_B1) * g
        vn = ADAM_B2 * v_ref[...] + (1.0 - ADAM_B2) * (g * g)
        go_ref[...] = g
        mo_ref[...] = mn
        vo_ref[...] = vn
        d_ref[...] = -ADAM_LR * ((mn / c1) / (jnp.sqrt(vn / c2) + ADAM_EPS) + ADAM_WD * w_ref[...])

    blk = pl.BlockSpec((tr, c), lambda i: (i, 0))
    return pl.pallas_call(
        body, name=name, grid=(r // tr,),
        in_specs=[blk, pl.BlockSpec((p, tr, c), lambda i: (0, i, 0)), blk, blk],
        out_specs=[blk] * 4,
        out_shape=[jax.ShapeDtypeStruct((r, c), F32)] * 4,
        compiler_params=_params("parallel"),
    )(w, gparts, m, v)


MESH = pl.DeviceIdType.MESH
ANY = pl.BlockSpec(memory_space=pl.ANY)


def _all_gather(xs, name):
    n = len(xs)

    def body(*refs):
        x_refs, out_refs = refs[:n], refs[n:2 * n]
        send_sems, recv_sems, local_sems = refs[2 * n:]
        x, y, c = lax.axis_index("x"), lax.axis_index("y"), lax.axis_index("c")
        me, sibling = (x, y, c), (x, y, 1 - c)
        chips = [(1 - x, y), (x, 1 - y), (1 - x, 1 - y)]

        def slot(a, px, py, pc):
            return out_refs[a].at[4 * px + 2 * py + pc]

        def copy(a, k, block, to, src=None):
            return pltpu.make_async_remote_copy(
                src_ref=slot(a, *block) if src is None else src, dst_ref=slot(a, *block),
                send_sem=send_sems.at[k * n + a], recv_sem=recv_sems.at[k * n + a], device_id=to, device_id_type=MESH)

        arrays = range(n)
        mine = [pltpu.make_async_copy(x_refs[a], slot(a, *me), local_sems.at[a]) for a in arrays]
        first = [copy(a, 0, me, sibling, src=x_refs[a]) for a in arrays]
        first += [copy(a, 1 + j, me, (*chip, c), src=x_refs[a]) for j, chip in enumerate(chips) for a in arrays]
        for cp in mine + first:
            cp.start()
        passed = []
        for j, chip in enumerate(chips):
            for a in arrays:
                copy(a, 1 + j, (*chip, c), me).wait_recv()
                passed.append(copy(a, 4 + j, (*chip, c), sibling))
                passed[-1].start()
        for a in arrays:
            copy(a, 0, sibling, me).wait_recv()
        for j, chip in enumerate(chips):
            for a in arrays:
                copy(a, 4 + j, (*chip, 1 - c), me).wait_recv()
        for cp in first + passed:
            cp.wait_send()
        for cp in mine:
            cp.wait()

    return pl.pallas_call(
        body, name=name,
        out_shape=[jax.ShapeDtypeStruct((N_DEV,) + x.shape, x.dtype) for x in xs],
        in_specs=[ANY] * n, out_specs=[ANY] * n,
        scratch_shapes=[pltpu.SemaphoreType.DMA((7 * n,)), pltpu.SemaphoreType.DMA((7 * n,)),
                        pltpu.SemaphoreType.DMA((n,))],
    )(*xs)


def _peer_copies(kind, src_refs, dst_refs, send_sems, recv_sems, local_sems):
    n = len(src_refs)
    x, y, c = lax.axis_index("x"), lax.axis_index("y"), lax.axis_index("c")
    mine = 4 * x + 2 * y + c

    def src(a, idx):
        return src_refs[a] if kind == "spread" else src_refs[a].at[idx]

    copies = [pltpu.make_async_copy(src(a, mine), dst_refs[a].at[mine], local_sems.at[a]) for a in range(n)]
    for k in range(1, N_DEV):
        px = 1 - x if k & 4 else x
        py = 1 - y if k & 2 else y
        pc = 1 - c if k & 1 else c
        for a in range(n):
            copies.append(pltpu.make_async_remote_copy(
                src_ref=src(a, 4 * px + 2 * py + pc), dst_ref=dst_refs[a].at[mine],
                send_sem=send_sems.at[(k - 1) * n + a], recv_sem=recv_sems.at[(k - 1) * n + a],
                device_id=(px, py, pc), device_id_type=MESH))
    return copies


def _peer_shapes(kind, arrays):
    n = len(arrays)
    outs = [jax.ShapeDtypeStruct(((N_DEV,) + a.shape) if kind == "spread" else a.shape, a.dtype) for a in arrays]
    sems = [pltpu.SemaphoreType.DMA((7 * n,)), pltpu.SemaphoreType.DMA((7 * n,)), pltpu.SemaphoreType.DMA((n,))]
    return outs, sems


def _tile(n, pref, unit=LANES):
    if n <= pref:
        return n
    best = None
    for cand in range(unit, pref + 1, unit):
        if n % cand == 0:
            best = cand
    assert best is not None, (n, pref, unit)
    return best


LATE_KEYS = dict(w_proj_sb="wp_sb", w_proj_fox="wp_fx", w_out="w_out", w_up="w_up", w_conv="w_conv", w_down="w_down")


def _layer_step(x, target, w, attn_blk, late=None):
    b, s, d = x.shape
    t = b * s
    w = dict(w)
    e = w["w_qkv"].shape[1] // 6
    h = e // HEAD_DIM
    f = w["b_conv"].shape[1]
    x2 = x.reshape(t, d)
    tg = target.reshape(t, d)
    pp = 2 if (e // LANES) % 2 == 0 else 1
    fox_blk = min(2 * attn_blk, s)
    tm = _tile(t, 512, HALO)
    tmo = _tile(t, 1024, HALO)
    tmc = _tile(s, 256, HALO)
    tkt = _tile(t, 2048, HALO)
    td = _tile(d, 1024)
    tf = _tile(f, 1408)
    t2f = _tile(2 * f, 1408)
    tqkv = _tile(6 * e, 1024)
    tg2 = _tile(2 * d, 1024)
    xb = x2.astype(BF16)

    qkv = _matmul(xb, w["w_qkv"], NN, tm=tmo, tn=tqkv, tk=d, out_dtype=BF16, name="in_qkv", bias=w["b_qkv"], b_outer=True)
    gate = _matmul(xb, jnp.concatenate([w["w_g"], w["w_f"]], axis=1), NN, tm=tmo, tn=2 * d + LANES, tk=d, out_dtype=F32,
                   name="in_gate", bias=jnp.concatenate([w["b_g"], w["b_f"]], axis=1), b_outer=True)
    nr = s // LANES
    f2 = gate[:, 2 * d:2 * d + h].reshape(b, s, h).transpose(0, 2, 1).reshape(b * h * nr, LANES)
    ct = _scan_rows(f2, nr, "fwd").reshape(b * h, 1, s)
    o_sb, tot, first = _sb_fwd(qkv, b, s, e, attn_blk, pp)
    if late is None:
        o_fx, lse = _fox_fwd(qkv, ct, b, s, e, fox_blk, 1)
    else:
        o_fx, lse, *gathered = _fox_fwd(qkv, ct, b, s, e, fox_blk, 1, comm=("spread", late[1]))
        for name, g in zip(late[0], gathered):
            w[LATE_KEYS[name]] = _join(g, name)
    merged, y_sb, y_fx = _proj_gate_fwd(o_sb, o_fx, w["wp_sb"], w["wp_fx"], gate, tm)
    x1, xh1, rs1, x1b = _mm_res_ln(merged, w["w_out"], x2, w["ln1_g"], w["ln1_b"], tm, "out_ln1")
    u = _matmul(x1b, w["w_up"], NN, tm=tmo, tn=t2f, tk=d, out_dtype=BF16, name="ffn_up", b_outer=True)
    act = _conv_glu_fwd(u, w["w_conv"], w["b_conv"], s, tmc)
    xo, xh2, rs2, _ = _mm_res_ln(act, w["w_down"], x1, w["ln2_g"], w["ln2_b"], tm, "down_ln2")

    gr = {}
    dr2, dg2, db2, ls, dr2b = _loss_ln_bwd(xo, xh2, rs2, w["ln2_g"], tg, tm)
    gr["ln2_g"], gr["ln2_b"] = dg2[0:1], db2[0:1]
    da = _matmul(dr2b, w["w_down"], NT, tm=tmo, tn=tf, tk=d, out_dtype=BF16, name="d_act", b_outer=True)
    gr["w_down"] = _matmul(act, dr2b, TN, tm=tf, tn=td, tk=tkt, out_dtype=BF16, name="dw_down")
    dc, du = _conv_glu_bwd1(u, da, w["w_conv"], w["b_conv"], s, tmc)
    du, gw0, gw1, gw2, gbc = _conv_glu_bwd2(u, dc, w["w_conv"], du, s, tmc)
    gr["w_conv"] = jnp.concatenate([gw0[0:1], gw1[0:1], gw2[0:1]], axis=0)
    gr["b_conv"] = gbc[0:1]
    dlin1 = _matmul(du, w["w_up"], NT, tm=tm, tn=td, tk=2 * f, out_dtype=F32, name="d_x1")
    gr["w_up"] = _matmul(x1b, du, TN, tm=td, tn=t2f, tk=tkt, out_dtype=BF16, name="dw_up")
    dr1, dg1, db1, dr1b = _ln_bwd(dr2, dlin1, xh1, rs1, w["ln1_g"], tm)
    gr["ln1_g"], gr["ln1_b"] = dg1[0:1], db1[0:1]
    gr["w_out"] = _matmul(merged, dr1b, TN, tm=td, tn=td, tk=tkt, out_dtype=BF16, name="dw_out")
    dy_sb, dy_fx, dgate = _gate_bwd(dr1b, w["w_out"], y_sb, y_fx, gate, tm)
    do_sb = _matmul(dy_sb, w["wp_sb"], NT, tm=tm, tn=e, tk=d, out_dtype=BF16, name="d_o_sb")
    do_fx = _matmul(dy_fx, w["wp_fx"], NT, tm=tm, tn=e, tk=d, out_dtype=BF16, name="d_o_fx")
    gr["wp_sb"] = _matmul(o_sb, dy_sb, TN, tm=e, tn=td, tk=tkt, out_dtype=BF16, name="dwp_sb")
    gr["wp_fx"] = _matmul(o_fx, dy_fx, TN, tm=e, tn=td, tk=tkt, out_dtype=BF16, name="dwp_fx")
    dq_sb, dk_sb, dv_sb = _sb_bwd(qkv, do_sb, tot, first, b, s, e, attn_blk, pp)
    landed = None
    if late is None:
        dq_fx, dk_fx, dv_fx, dct = _fox_bwd(qkv, ct, do_fx, o_fx, lse, b, s, e, fox_blk, 1)
    else:
        blocks = [_cut(gr[LATE_KEYS[n]], n).astype(BF16 if n in MATMUL_OPERANDS else F32) for n in late[0]]
        dq_fx, dk_fx, dv_fx, dct, *got = _fox_bwd(qkv, ct, do_fx, o_fx, lse, b, s, e, fox_blk, 1,
                                                   comm=("exchange", blocks))
        landed = dict(zip(late[0], got))
    dqkv = jnp.concatenate([dq_sb, dk_sb, dv_sb, dq_fx, dk_fx, dv_fx], axis=1)
    df2 = _scan_rows(f2, nr, "bwd", dct.reshape(b * h * nr, LANES))
    df = jnp.pad(df2.reshape(b, h, s).transpose(0, 2, 1).reshape(t, h), ((0, 0), (0, LANES - h))).astype(BF16)
    gr["w_qkv"] = _matmul(xb, dqkv, TN, tm=td, tn=tqkv, tk=tkt, out_dtype=BF16, name="dw_qkv")
    gr["w_g"] = _matmul(xb, dgate, TN, tm=td, tn=tg2, tk=tkt, out_dtype=BF16, name="dw_gate")
    gr["w_f"] = _matmul(xb, df, TN, tm=td, tn=LANES, tk=tkt, out_dtype=BF16, name="dw_forget")
    gr["b_qkv"] = _colsum(dqkv, tm, "db_qkv")[0:1]
    gr["b_g"] = _colsum(dgate, tm, "db_gate")[0:1]
    gr["b_f"] = _colsum(df, tm, "db_forget")[0:1]
    comm = None
    if late is not None:
        small = jnp.concatenate([_w_in_layout(gr["b_qkv"], gr["b_f"], gr["b_g"], h), gr["ln1_g"], gr["ln1_b"],
                                 gr["b_conv"], gr["ln2_g"], gr["ln2_b"]], axis=1)
        comm = ("exchange", [_cut(_w_in_layout(gr["w_qkv"], gr["w_f"], gr["w_g"], h), "w_in").astype(BF16),
                             jnp.broadcast_to(small[None], (N_DEV,) + small.shape)])
    dx, *got = _input_grad([dqkv, dgate, df], [w["w_qkv"], w["w_g"], w["w_f"]], dr1, tm, comm)
    if late is not None:
        landed["w_in"], landed["replicated"] = got
    return ls[0:1, 0:1], dx.reshape(b, s, d), gr, landed


SHARDED = ("w_in", "w_proj_sb", "w_proj_fox", "w_out", "w_up", "w_conv", "w_down")
ROW_SHARDED = ("w_out", "w_down")
REPLICATED = ("b_in", "ln1_g", "ln1_b", "b_conv", "ln2_g", "ln2_b")
WEIGHTS = ("w_in", "b_in", "w_proj_sb", "w_proj_fox", "w_out", "ln1_g", "ln1_b", "w_up", "w_conv", "b_conv",
           "w_down", "ln2_g", "ln2_b")
MATMUL_OPERANDS = ("w_in", "w_proj_sb", "w_proj_fox", "w_out", "w_up", "w_down")


def _w_in_layout(g_qkv, g_f, g_g, h):
    return jnp.concatenate([g_qkv, g_f[:, :h], g_g], axis=1)


def _cut(full, name):
    r, c = full.shape
    if name in ROW_SHARDED:
        return full.reshape(N_DEV, r // N_DEV, c)
    cs = c // N_DEV
    return jnp.stack([full[:, j * cs:(j + 1) * cs] for j in range(N_DEV)], axis=0)


def _join(blocks, name):
    p, r, c = blocks.shape
    if name in ROW_SHARDED:
        return blocks.reshape(p * r, c)
    return jnp.concatenate([blocks[j] for j in range(p)], axis=1)


def kernel(x, w_in, b_in, w_proj_sb, w_proj_fox, w_out, ln1_g, ln1_b, w_up, w_conv, b_conv, w_down, ln2_g, ln2_b, loss_target, m_w_in, m_b_in, m_w_proj_sb, m_w_proj_fox, m_w_out, m_ln1_g, m_ln1_b, m_w_up, m_w_conv, m_b_conv, m_w_down, m_ln2_g, m_ln2_b, v_w_in, v_b_in, v_w_proj_sb, v_w_proj_fox, v_w_out, v_ln1_g, v_ln1_b, v_w_up, v_w_conv, v_b_conv, v_w_down, v_ln2_g, v_ln2_b):
    wts = dict(w_in=w_in, b_in=b_in, w_proj_sb=w_proj_sb, w_proj_fox=w_proj_fox, w_out=w_out, ln1_g=ln1_g, ln1_b=ln1_b,
               w_up=w_up, w_conv=w_conv, b_conv=b_conv, w_down=w_down, ln2_g=ln2_g, ln2_b=ln2_b)
    mom = dict(w_in=m_w_in, b_in=m_b_in, w_proj_sb=m_w_proj_sb, w_proj_fox=m_w_proj_fox, w_out=m_w_out, ln1_g=m_ln1_g,
               ln1_b=m_ln1_b, w_up=m_w_up, w_conv=m_w_conv, b_conv=m_b_conv, w_down=m_w_down, ln2_g=m_ln2_g, ln2_b=m_ln2_b)
    var = dict(w_in=v_w_in, b_in=v_b_in, w_proj_sb=v_w_proj_sb, w_proj_fox=v_w_proj_fox, w_out=v_w_out, ln1_g=v_ln1_g,
               ln1_b=v_ln1_b, w_up=v_w_up, w_conv=v_w_conv, b_conv=v_b_conv, w_down=v_w_down, ln2_g=v_ln2_g, ln2_b=v_ln2_b)
    shard = {n: wts[n].reshape(wts[n].shape[-2:]) for n in WEIGHTS}

    w_in_full = _join(_all_gather([shard["w_in"].astype(BF16)], "gather_w_in")[0], "w_in")
    late_names = [n for n in SHARDED if n != "w_in"]
    late = (late_names, [shard[n].astype(BF16) if n in MATMUL_OPERANDS else shard[n] for n in late_names])
    e = shard["w_proj_sb"].shape[0]
    h = e // HEAD_DIM
    nq = 6 * e

    def cut_in(a, pad):
        fcols = a[:, nq:nq + h]
        if pad:
            fcols = jnp.pad(fcols, ((0, 0), (0, LANES - h)))
        return a[:, :nq], a[:, nq + h:], fcols

    w_qkv, w_g, w_f = cut_in(w_in_full, True)
    b_qkv, b_g, b_f = cut_in(shard["b_in"], True)
    w = dict(w_qkv=w_qkv, w_g=w_g, w_f=w_f, b_qkv=b_qkv, b_g=b_g, b_f=b_f, b_conv=shard["b_conv"],
             ln1_g=shard["ln1_g"], ln1_b=shard["ln1_b"], ln2_g=shard["ln2_g"], ln2_b=shard["ln2_b"])

    loss_local, grad_x, gr, gsum = _layer_step(x, loss_target, w, min(256, x.shape[1]), late)
    loss = lax.psum(loss_local[0, 0], ("x", "y", "c"))

    parts = gsum.pop("replicated")
    off = 0
    for n in REPLICATED:
        gsum[n] = parts[:, :, off:off + shard[n].size]
        off += shard[n].size

    grads, deltas, new_m, new_v = [], [], [], []
    for n in WEIGHTS:
        shp = wts[n].shape
        g, dl, mn, vn = _adamw(shard[n], gsum[n], mom[n].reshape(shard[n].shape), var[n].reshape(shard[n].shape),
                               "adamw_" + n)
        grads.append(g.reshape(shp))
        deltas.append(dl.reshape(shp))
        new_m.append(mn.reshape(shp))
        new_v.append(vn.reshape(shp))
    return (loss, grad_x, *grads, *deltas, *new_m, *new_v)
```

```python
import functools
import math

import jax
import jax.numpy as jnp
from jax import lax
from jax.experimental import pallas as pl
from jax.experimental.pallas import tpu as pltpu

F32 = jnp.float32
BF16 = jnp.bfloat16

HEAD_DIM = 64
LN_EPS = 1e-5
DEPTH = 1
ALPHA = (2.0 * DEPTH) ** 0.25
ADAM_LR, ADAM_B1, ADAM_B2, ADAM_EPS, ADAM_WD, ADAM_STEP = 0.001, 0.9, 0.999, 1e-08, 0.01, 10
N_DEV = 8
LANES = 128
SUBLANES = 8
HALO = 16
VMEM_LIMIT = 56 * 1024 * 1024

NN = ((1,), (0,))
NT = ((1,), (1,))
TN = ((0,), (0,))


def _dot(a, b, dims):
    return lax.dot_general(a, b, (dims, ((), ())), preferred_element_type=F32)


def _params(*sem):
    return pltpu.CompilerParams(dimension_semantics=sem, vmem_limit_bytes=VMEM_LIMIT)


def _iotas(blk):
    row = lax.broadcasted_iota(jnp.int32, (blk, blk), 0)
    col = lax.broadcasted_iota(jnp.int32, (blk, blk), 1)
    return row, col


def _sb_terms(z):
    e = jnp.exp(-jnp.abs(z))
    lb = jnp.minimum(z, 0.0) - jnp.log(1.0 + e)
    return lb, lb - z, e


def _pair_specs(s, blk, e, pp, branch):
    w = pp * LANES
    nq = s // blk
    ng = e // w
    base = 3 * branch * ng
    q_in = pl.BlockSpec((blk, w), lambda b, g, i: (b * nq + i, base + g))
    k_in = pl.BlockSpec((s, w), lambda b, g, i: (b, base + ng + g))
    v_in = pl.BlockSpec((s, w), lambda b, g, i: (b, base + 2 * ng + g))
    q_out = pl.BlockSpec((blk, w), lambda b, g, i: (b * nq + i, g))
    kv_out = pl.BlockSpec((s, w), lambda b, g, i: (b, g))
    rows = pl.BlockSpec((2 * pp, blk, 1), lambda b, g, i: (b * ng + g, i, 0))
    krow = pl.BlockSpec((2 * pp, 1, s), lambda b, g, i: (b * ng + g, 0, 0))
    return q_in, k_in, v_in, q_out, kv_out, rows, krow


def _to_column(r):
    row, col = _iotas(r.shape[1])
    return jnp.sum(jnp.where(row == col, r, 0.0), axis=1, keepdims=True)


def _half_masks(x):
    low = lax.broadcasted_iota(jnp.int32, x.shape, 1) < HEAD_DIM
    zero = jnp.zeros_like(x)
    return jnp.concatenate([jnp.where(low, x, zero), jnp.where(low, zero, x)], axis=0)


def _tri_sums(xs, tri):
    hi = [x.astype(BF16) for x in xs]
    lo = [(x - h.astype(F32)).astype(BF16) for x, h in zip(xs, hi)]
    n = len(xs)
    blk = xs[0].shape[0]
    r = _dot(jnp.concatenate(hi + lo, axis=0), tri, NN)
    return [r[i * blk:(i + 1) * blk] + r[(n + i) * blk:(n + i + 1) * blk] for i in range(n)]


def _sb_fwd(qkv, b, s, e, blk, pp):
    scale = HEAD_DIM ** -0.5
    nh = 2 * pp
    t = b * s

    def body(q_ref, k_ref, v_ref, o_ref, tot_ref, first_ref):
        qi = pl.program_id(2)
        qm = [_half_masks((q_ref[:, p * LANES:(p + 1) * LANES] * scale).astype(BF16)) for p in range(pp)]
        row, col = _iotas(blk)
        strict = col < row
        after = (row > col).astype(BF16)

        def block(j, carry, diag):
            off = pl.multiple_of(j * blk, blk)
            o_acc, run = carry
            zz = [_dot(qm[p], k_ref[pl.ds(off, blk), p * LANES:(p + 1) * LANES], NT) for p in range(pp)]
            z = [zz[h // 2][(h % 2) * blk:(h % 2 + 1) * blk] for h in range(nh)]
            terms = [_sb_terms(z[h]) for h in range(nh)]
            lom = [jnp.where(strict, terms[h][1], 0.0) if diag else terms[h][1] for h in range(nh)]
            sfx = _tri_sums(lom, after)
            a = [jnp.exp(terms[h][0] + sfx[h] + run[h]) for h in range(nh)]
            if diag:
                a = [jnp.where(strict, a[h], 0.0) for h in range(nh)]
            ab = [a[h].astype(BF16) for h in range(nh)]
            o_new = tuple(
                o_acc[p] + _dot(jnp.concatenate([ab[2 * p], ab[2 * p + 1]], axis=1),
                                _half_masks(v_ref[pl.ds(off, blk), p * LANES:(p + 1) * LANES]), NN)
                for p in range(pp))
            return o_new, tuple(run[h] + sfx[h][:, 0:1] + lom[h][:, 0:1] for h in range(nh))

        def alive(run):
            m = run[0]
            for h in range(1, nh):
                m = jnp.maximum(m, run[h])
            return jnp.max(m) > DEAD

        o_acc, run = block(qi, ((jnp.zeros((blk, LANES), F32),) * pp, (jnp.zeros((blk, 1), F32),) * nh), True)

        def step(c):
            j, _, o_acc, run = c
            o_acc, run = block(j, (o_acc, run), False)
            return j - 1, alive(run), o_acc, run

        j, _, o_acc, run = lax.while_loop(lambda c: jnp.logical_and(c[0] >= 0, c[1]), step,
                                          (qi - 1, alive(run), o_acc, run))
        for p in range(pp):
            o_ref[:, p * LANES:(p + 1) * LANES] = o_acc[p].astype(o_ref.dtype)
        for h in range(nh):
            tot_ref[h] = run[h]
            first_ref[h] = jnp.zeros((blk, 1), F32) + (j + 1).astype(F32)

    q_in, k_in, v_in, q_out, _, rows, _ = _pair_specs(s, blk, e, pp, 0)
    return pl.pallas_call(
        body, name="sb_fwd", grid=(b, e // (pp * LANES), s // blk),
        in_specs=[q_in, k_in, v_in], out_specs=[q_out, rows, rows],
        out_shape=[jax.ShapeDtypeStruct((t, e), BF16)] + [jax.ShapeDtypeStruct((b * e // HEAD_DIM, s, 1), F32)] * 2,
        compiler_params=_params("parallel", "parallel", "arbitrary"),
    )(qkv, qkv, qkv)


def _sb_bwd(qkv, do, tot, first, b, s, e, blk, pp):
    scale = HEAD_DIM ** -0.5
    nh = 2 * pp
    t = b * s
    nq = s // blk

    def body(q_ref, k_ref, v_ref, do_ref, tot_ref, first_ref, dq_ref, dk_ref, dv_ref, dk_acc, dv_acc):
        qi = pl.program_id(2)

        @pl.when(qi == 0)
        def _():
            dk_acc[...] = jnp.zeros_like(dk_acc)
            dv_acc[...] = jnp.zeros_like(dv_acc)

        qm = [_half_masks((q_ref[:, p * LANES:(p + 1) * LANES] * scale).astype(BF16)) for p in range(pp)]
        dom = [_half_masks(do_ref[:, p * LANES:(p + 1) * LANES].astype(BF16)) for p in range(pp)]
        tot_t = [tot_ref[h] for h in range(nh)]
        row, col = _iotas(blk)
        strict = col < row
        upto = (row <= col).astype(BF16)
        before = (row < col).astype(BF16)

        def block(j, carry, diag):
            off = pl.multiple_of(j * blk, blk)
            dq_acc, cl, cg = carry
            kp = [k_ref[pl.ds(off, blk), p * LANES:(p + 1) * LANES] for p in range(pp)]
            vp = [v_ref[pl.ds(off, blk), p * LANES:(p + 1) * LANES] for p in range(pp)]
            zz = [_dot(qm[p], kp[p], NT) for p in range(pp)]
            dd = [_dot(dom[p], vp[p], NT) for p in range(pp)]
            z = [zz[h // 2][(h % 2) * blk:(h % 2 + 1) * blk] for h in range(nh)]
            da = [dd[h // 2][(h % 2) * blk:(h % 2 + 1) * blk] for h in range(nh)]
            terms = [_sb_terms(z[h]) for h in range(nh)]
            lom = [jnp.where(strict, terms[h][1], 0.0) if diag else terms[h][1] for h in range(nh)]
            pre = _tri_sums(lom, upto)
            a = [jnp.exp(terms[h][0] + (tot_t[h] - cl[h] - pre[h])) for h in range(nh)]
            if diag:
                a = [jnp.where(strict, a[h], 0.0) for h in range(nh)]
            g = [a[h] * da[h] for h in range(nh)]
            pw = _tri_sums(g, before)
            dzb = []
            for h in range(nh):
                ex = terms[h][2]
                r = 1.0 / (1.0 + ex)
                er = ex * r
                pos = z[h] >= 0.0
                dz = g[h] * jnp.where(pos, er, r) - (cg[h] + pw[h]) * jnp.where(pos, r, er)
                if diag:
                    dz = jnp.where(strict, dz, 0.0)
                dzb.append(dz.astype(BF16))
            ab = [a[h].astype(BF16) for h in range(nh)]
            for p in range(pp):
                cols = slice(p * LANES, (p + 1) * LANES)
                dk_acc[pl.ds(off, blk), cols] += _dot(jnp.concatenate([dzb[2 * p], dzb[2 * p + 1]], axis=0), qm[p], TN)
                dv_acc[pl.ds(off, blk), cols] += _dot(jnp.concatenate([ab[2 * p], ab[2 * p + 1]], axis=0), dom[p], TN)
            dq_new = tuple(dq_acc[p] + _dot(jnp.concatenate([dzb[2 * p], dzb[2 * p + 1]], axis=1), _half_masks(kp[p]), NN)
                           for p in range(pp))
            return (dq_new, tuple(cl[h] + pre[h][:, blk - 1:blk] for h in range(nh)),
                    tuple(cg[h] + pw[h][:, blk - 1:blk] + g[h][:, blk - 1:blk] for h in range(nh)))

        zero1 = (jnp.zeros((blk, 1), F32),) * nh
        j0 = jnp.clip(jnp.max(first_ref[0]).astype(jnp.int32), 0, qi)
        carry = lax.fori_loop(j0, qi, lambda j, c: block(j, c, False), ((jnp.zeros((blk, LANES), F32),) * pp, zero1, zero1))
        dq_acc, _, _ = block(qi, carry, True)
        for p in range(pp):
            dq_ref[:, p * LANES:(p + 1) * LANES] = (dq_acc[p] * scale).astype(BF16)

        @pl.when(qi == nq - 1)
        def _():
            dk_ref[...] = dk_acc[...].astype(BF16)
            dv_ref[...] = dv_acc[...].astype(BF16)

    q_in, k_in, v_in, q_out, kv_out, rows, _ = _pair_specs(s, blk, e, pp, 0)
    w = pp * LANES
    return pl.pallas_call(
        body, name="sb_bwd", grid=(b, e // w, nq),
        in_specs=[q_in, k_in, v_in, q_out, rows, rows], out_specs=[q_out, kv_out, kv_out],
        out_shape=[jax.ShapeDtypeStruct((t, e), BF16)] * 3,
        scratch_shapes=[pltpu.VMEM((s, w), F32), pltpu.VMEM((s, w), F32)],
        compiler_params=_params("parallel", "parallel", "arbitrary"),
    )(qkv, qkv, qkv, do, tot, first)


NEG = -1e30
DEAD = -110.0


def _ride_along(comm, src_refs, dst_refs, sems, first, last):
    if comm is None:
        return lambda: None

    @pl.when(first)
    def _():
        for cp in _peer_copies(comm[0], src_refs, dst_refs, *sems):
            cp.start()

    def finish():
        @pl.when(last)
        def _():
            for cp in _peer_copies(comm[0], src_refs, dst_refs, *sems):
                cp.wait()

    return finish


def _grid_ends(grid):
    ids = [pl.program_id(a) for a in range(len(grid))]
    first = functools.reduce(jnp.logical_and, [i == 0 for i in ids])
    last = functools.reduce(jnp.logical_and, [i == g - 1 for i, g in zip(ids, grid)])
    return first, last


def _fox_fwd(qkv, ct, b, s, e, blk, pp, comm=None):
    scale = HEAD_DIM ** -0.5
    nh = 2 * pp
    t = b * s
    nc = len(comm[1]) if comm else 0
    grid = (b, e // (pp * LANES), s // blk)

    def body(*refs):
        q_ref, k_ref, v_ref, ct_ref = refs[:4]
        o_ref, lse_ref = refs[4 + nc:6 + nc]
        finish = _ride_along(comm, refs[4:4 + nc], refs[6 + nc:6 + 2 * nc], refs[6 + 2 * nc:], *_grid_ends(grid))
        qi = pl.program_id(2)
        qm = [_half_masks((q_ref[:, p * LANES:(p + 1) * LANES] * scale).astype(BF16)) for p in range(pp)]
        cq = [_to_column(ct_ref[h, :, pl.ds(pl.multiple_of(qi * blk, blk), blk)]) for h in range(nh)]
        row, col = _iotas(blk)
        causal = col <= row
        low = lax.broadcasted_iota(jnp.int32, (blk, LANES), 1) < HEAD_DIM

        def block(j, carry, diag):
            off = pl.multiple_of(j * blk, blk)
            m, l, acc = carry
            zz = [_dot(qm[p], k_ref[pl.ds(off, blk), p * LANES:(p + 1) * LANES], NT) for p in range(pp)]
            z = [zz[h // 2][(h % 2) * blk:(h % 2 + 1) * blk] + (cq[h] - ct_ref[h, :, pl.ds(off, blk)]) for h in range(nh)]
            if diag:
                z = [jnp.where(causal, z[h], NEG) for h in range(nh)]
            m_new = tuple(jnp.maximum(m[h], jnp.max(z[h], axis=1, keepdims=True)) for h in range(nh))
            w = [jnp.exp(m[h] - m_new[h]) for h in range(nh)]
            pr = [jnp.exp(z[h] - m_new[h]) for h in range(nh)]
            pb = [pr[h].astype(BF16) for h in range(nh)]
            pv = [_dot(jnp.concatenate([pb[2 * p], pb[2 * p + 1]], axis=1),
                       _half_masks(v_ref[pl.ds(off, blk), p * LANES:(p + 1) * LANES]), NN) for p in range(pp)]
            acc_new = tuple(jnp.where(low, w[2 * p], w[2 * p + 1]) * acc[p] + pv[p] for p in range(pp))
            l_new = tuple(w[h] * l[h] + jnp.sum(pr[h], axis=1, keepdims=True) for h in range(nh))
            return m_new, l_new, acc_new

        zero = ((jnp.full((blk, 1), NEG, F32),) * nh, (jnp.zeros((blk, 1), F32),) * nh, (jnp.zeros((blk, LANES), F32),) * pp)
        carry = block(qi, zero, True)
        m, l, acc = lax.fori_loop(0, qi, lambda j, c_: block(j, c_, False), carry)
        for p in range(pp):
            o_ref[:, p * LANES:(p + 1) * LANES] = acc[p] / jnp.where(low, l[2 * p], l[2 * p + 1])
        for h in range(nh):
            lse_ref[h] = m[h] + jnp.log(l[h])
        finish()

    q_in, k_in, v_in, q_out, _, rows, krow = _pair_specs(s, blk, e, pp, 1)
    outs, sems = _peer_shapes(comm[0], comm[1]) if comm else ([], [])
    return pl.pallas_call(
        body, name="fox_fwd", grid=grid,
        in_specs=[q_in, k_in, v_in, krow] + [ANY] * nc, out_specs=[q_out, rows] + [ANY] * nc,
        out_shape=[jax.ShapeDtypeStruct((t, e), F32), jax.ShapeDtypeStruct((b * e // HEAD_DIM, s, 1), F32)] + outs,
        scratch_shapes=sems,
        compiler_params=_params("arbitrary", "arbitrary", "arbitrary"),
    )(qkv, qkv, qkv, ct, *(comm[1] if comm else ()))


def _fox_bwd(qkv, ct, do, o, lse, b, s, e, blk, pp, comm=None):
    scale = HEAD_DIM ** -0.5
    nh = 2 * pp
    t = b * s
    nq = s // blk
    nc = len(comm[1]) if comm else 0
    w = pp * LANES
    grid = (b, e // w, nq)

    def body(*refs):
        q_ref, k_ref, v_ref, ct_ref, do_ref, o_ref, lse_ref = refs[:7]
        dq_ref, dk_ref, dv_ref, dct_ref = refs[7 + nc:11 + nc]
        dk_acc, dv_acc = refs[11 + 2 * nc:13 + 2 * nc]
        finish = _ride_along(comm, refs[7:7 + nc], refs[11 + nc:11 + 2 * nc], refs[13 + 2 * nc:], *_grid_ends(grid))
        qi = pl.program_id(2)

        @pl.when(qi == 0)
        def _():
            dk_acc[...] = jnp.zeros_like(dk_acc)
            dv_acc[...] = jnp.zeros_like(dv_acc)
            dct_ref[...] = jnp.zeros_like(dct_ref)

        qm = [_half_masks((q_ref[:, p * LANES:(p + 1) * LANES] * scale).astype(BF16)) for p in range(pp)]
        dob = [do_ref[:, p * LANES:(p + 1) * LANES].astype(BF16) for p in range(pp)]
        dom = [_half_masks(dob[p]) for p in range(pp)]
        low = lax.broadcasted_iota(jnp.int32, (blk, LANES), 1) < HEAD_DIM
        delta = []
        for p in range(pp):
            prod = dob[p].astype(F32) * o_ref[:, p * LANES:(p + 1) * LANES]
            delta.append(jnp.sum(jnp.where(low, prod, 0.0), axis=1, keepdims=True))
            delta.append(jnp.sum(jnp.where(low, 0.0, prod), axis=1, keepdims=True))
        cq = [_to_column(ct_ref[h, :, pl.ds(pl.multiple_of(qi * blk, blk), blk)]) for h in range(nh)]
        lse_t = [lse_ref[h] for h in range(nh)]
        row, col = _iotas(blk)
        causal = col <= row

        def block(j, dq_acc, diag):
            off = pl.multiple_of(j * blk, blk)
            kp = [k_ref[pl.ds(off, blk), p * LANES:(p + 1) * LANES] for p in range(pp)]
            zz = [_dot(qm[p], kp[p], NT) for p in range(pp)]
            dd = [_dot(dom[p], v_ref[pl.ds(off, blk), p * LANES:(p + 1) * LANES], NT) for p in range(pp)]
            z = [zz[h // 2][(h % 2) * blk:(h % 2 + 1) * blk] + (cq[h] - ct_ref[h, :, pl.ds(off, blk)]) for h in range(nh)]
            pr = [jnp.exp(z[h] - lse_t[h]) for h in range(nh)]
            if diag:
                pr = [jnp.where(causal, pr[h], 0.0) for h in range(nh)]
            ds = [pr[h] * (dd[h // 2][(h % 2) * blk:(h % 2 + 1) * blk] - delta[h]) for h in range(nh)]
            dsb = [ds[h].astype(BF16) for h in range(nh)]
            pb = [pr[h].astype(BF16) for h in range(nh)]
            for p in range(pp):
                cols = slice(p * LANES, (p + 1) * LANES)
                dk_acc[pl.ds(off, blk), cols] += _dot(jnp.concatenate([dsb[2 * p], dsb[2 * p + 1]], axis=0), qm[p], TN)
                dv_acc[pl.ds(off, blk), cols] += _dot(jnp.concatenate([pb[2 * p], pb[2 * p + 1]], axis=0), dom[p], TN)
            for h in range(nh):
                dct_ref[h, :, pl.ds(off, blk)] -= jnp.sum(ds[h], axis=0, keepdims=True)
            return tuple(dq_acc[p] + _dot(jnp.concatenate([dsb[2 * p], dsb[2 * p + 1]], axis=1), _half_masks(kp[p]), NN)
                         for p in range(pp))

        dq_acc = lax.fori_loop(0, qi, lambda j, a: block(j, a, False), (jnp.zeros((blk, LANES), F32),) * pp)
        dq_acc = block(qi, dq_acc, True)
        for p in range(pp):
            dq_ref[:, p * LANES:(p + 1) * LANES] = (dq_acc[p] * scale).astype(BF16)

        @pl.when(qi == nq - 1)
        def _():
            dk_ref[...] = dk_acc[...].astype(BF16)
            dv_ref[...] = dv_acc[...].astype(BF16)

        finish()

    q_in, k_in, v_in, q_out, kv_out, rows, krow = _pair_specs(s, blk, e, pp, 1)
    outs, sems = _peer_shapes(comm[0], comm[1]) if comm else ([], [])
    return pl.pallas_call(
        body, name="fox_bwd", grid=grid,
        in_specs=[q_in, k_in, v_in, krow, q_out, q_out, rows] + [ANY] * nc,
        out_specs=[q_out, kv_out, kv_out, krow] + [ANY] * nc,
        out_shape=[jax.ShapeDtypeStruct((t, e), BF16)] * 3 + [jax.ShapeDtypeStruct((b * e // HEAD_DIM, 1, s), F32)] + outs,
        scratch_shapes=[pltpu.VMEM((s, w), F32), pltpu.VMEM((s, w), F32)] + sems,
        compiler_params=_params("arbitrary", "arbitrary", "arbitrary"),
    )(qkv, qkv, qkv, ct, do, o, lse, *(comm[1] if comm else ()))


def _scan_rows(f2, group, mode, d2=None):
    n = f2.shape[0]

    def body(*refs):
        f_ref, o_ref = refs[0], refs[-1]
        f = f_ref[...]
        row, col = _iotas(LANES)
        grow = lax.broadcasted_iota(jnp.int32, (n, n), 0)
        gcol = lax.broadcasted_iota(jnp.int32, (n, n), 1)
        same = (grow // group) == (gcol // group)
        e = jnp.exp(-jnp.abs(f))
        if mode == "fwd":
            x = jnp.minimum(f, 0.0) - jnp.log1p(e)
            within = (row <= col).astype(F32)
            earlier = (same & (gcol < grow)).astype(F32)
        else:
            x = refs[1][...]
            within = (row >= col).astype(F32)
            earlier = (same & (gcol > grow)).astype(F32)
        y = jnp.dot(x, within, preferred_element_type=F32, precision=lax.Precision.HIGHEST)
        tot = jnp.sum(x, axis=1, keepdims=True)
        y = y + jnp.dot(earlier, tot, preferred_element_type=F32, precision=lax.Precision.HIGHEST)
        if mode == "bwd":
            r = 1.0 / (1.0 + e)
            y = y * jnp.where(f >= 0.0, e * r, r)
        o_ref[...] = y

    args = (f2,) if mode == "fwd" else (f2, d2)
    return pl.pallas_call(body, name="logf_" + mode, out_shape=jax.ShapeDtypeStruct(f2.shape, F32),
                          compiler_params=_params())(*args)


def _matmul(a, b, dims, *, tm, tn, tk, out_dtype, name, bias=None, res=None, res_scale=1.0, b_outer=False, b_colsum=False):
    def ij(g0, g1):
        return (g1, g0) if b_outer else (g0, g1)

    if dims == NN:
        (m, kk), n = a.shape, b.shape[1]
        a_spec = pl.BlockSpec((tm, tk), lambda g0, g1, k: (ij(g0, g1)[0], k))
        b_spec = pl.BlockSpec((tk, tn), lambda g0, g1, k: (k, ij(g0, g1)[1]))
    elif dims == NT:
        (m, kk), n = a.shape, b.shape[0]
        a_spec = pl.BlockSpec((tm, tk), lambda g0, g1, k: (ij(g0, g1)[0], k))
        b_spec = pl.BlockSpec((tn, tk), lambda g0, g1, k: (ij(g0, g1)[1], k))
    else:
        (kk, m), n = a.shape, b.shape[1]
        a_spec = pl.BlockSpec((tk, tm), lambda g0, g1, k: (k, ij(g0, g1)[0]))
        b_spec = pl.BlockSpec((tk, tn), lambda g0, g1, k: (k, ij(g0, g1)[1]))
    assert m % tm == 0 and n % tn == 0 and kk % tk == 0, (name, m, n, kk, tm, tn, tk)
    nk = kk // tk
    extras, extra_specs = [], []
    if bias is not None:
        extras.append(bias)
        extra_specs.append(pl.BlockSpec((1, tn), lambda g0, g1, k: (0, ij(g0, g1)[1])))
    if res is not None:
        extras.append(res)
        extra_specs.append(pl.BlockSpec((tm, tn), lambda g0, g1, k: ij(g0, g1)))

    def finish(out, rest, o_ref):
        idx = 0
        if bias is not None:
            out = out + rest[idx][...]
            idx += 1
        if res is not None:
            out = out + res_scale * rest[idx][...]
        o_ref[...] = out.astype(o_ref.dtype)

    def body_single(a_ref, b_ref, *rest):
        if b_colsum:
            rest[-1][...] = _rowsum8(b_ref[...].astype(F32))
            _fold8(rest[-1])
        finish(_dot(a_ref[...].astype(BF16), b_ref[...].astype(BF16), dims), rest, rest[-2] if b_colsum else rest[-1])

    def body_acc(a_ref, b_ref, *rest):
        acc_ref = rest[-1]
        o_ref = rest[-3] if b_colsum else rest[-2]
        k = pl.program_id(2)
        part = _dot(a_ref[...].astype(BF16), b_ref[...].astype(BF16), dims)
        if b_colsum:
            cs_ref = rest[-2]
            pcs = _rowsum8(b_ref[...].astype(F32))

            @pl.when(k == 0)
            def _():
                cs_ref[...] = pcs

            @pl.when(k > 0)
            def _():
                cs_ref[...] += pcs

            @pl.when(k == nk - 1)
            def _():
                _fold8(cs_ref)

        @pl.when(k == 0)
        def _():
            acc_ref[...] = part

        @pl.when(k > 0)
        def _():
            acc_ref[...] += part

        @pl.when(k == nk - 1)
        def _():
            finish(acc_ref[...], rest, o_ref)

    grid = (n // tn, m // tm, nk) if b_outer else (m // tm, n // tn, nk)
    assert not b_colsum or (dims == TN and m == tm)
    outs = pl.pallas_call(
        body_single if nk == 1 else body_acc, name=name, grid=grid,
        in_specs=[a_spec, b_spec] + extra_specs,
        out_specs=[pl.BlockSpec((tm, tn), lambda g0, g1, k: ij(g0, g1))]
        + ([pl.BlockSpec((SUBLANES, tn), lambda g0, g1, k: (0, ij(g0, g1)[1]))] if b_colsum else []),
        out_shape=[jax.ShapeDtypeStruct((m, n), out_dtype)]
        + ([jax.ShapeDtypeStruct((SUBLANES, n), F32)] if b_colsum else []),
        scratch_shapes=[] if nk == 1 else [pltpu.VMEM((tm, tn), F32)],
        compiler_params=_params("parallel", "parallel", "arbitrary"),
    )(a, b, *extras)
    return outs if b_colsum else outs[0]


def _input_grad(dhs, ws, dr, tm, comm=None):
    t, d = dr.shape
    npc = len(dhs)
    nc = len(comm[1]) if comm else 0
    grid = (t // tm,)

    def body(*refs):
        dr_ref = refs[2 * npc]
        first_in = 2 * npc + 1
        o_ref = refs[first_in + nc]
        finish = _ride_along(comm, refs[first_in:first_in + nc], refs[first_in + nc + 1:first_in + 2 * nc + 1],
                             refs[first_in + 2 * nc + 1:], *_grid_ends(grid))
        out = ALPHA * dr_ref[...]
        for p in range(npc):
            out = out + _dot(refs[p][...], refs[npc + p][...], NT)
        o_ref[...] = out
        finish()

    rows = pl.BlockSpec((tm, d), lambda i: (i, 0))
    outs, sems = _peer_shapes(comm[0], comm[1]) if comm else ([], [])
    return pl.pallas_call(
        body, name="input_grad", grid=grid,
        in_specs=[pl.BlockSpec((tm, a.shape[1]), lambda i: (i, 0)) for a in dhs]
        + [pl.BlockSpec(w.shape, lambda i: (0, 0)) for w in ws] + [rows] + [ANY] * nc,
        out_specs=[rows] + [ANY] * nc,
        out_shape=[jax.ShapeDtypeStruct((t, d), F32)] + outs,
        scratch_shapes=sems,
        compiler_params=_params("arbitrary"),
    )(*dhs, *ws, dr, *(comm[1] if comm else ()))


def _sigmoid(x):
    e = jnp.exp(-jnp.abs(x))
    r = 1.0 / (1.0 + e)
    return jnp.where(x >= 0.0, r, e * r)


def _proj_gate_fwd(o_sb, o_fx, wp_sb, wp_fx, g, tm):
    t, e = o_sb.shape
    d = wp_sb.shape[1]

    def body(osb_ref, ofx_ref, wsb_ref, wfx_ref, gsb_ref, gfx_ref, mg_ref, ysb_ref, yfx_ref):
        ysb = _dot(osb_ref[...].astype(BF16), wsb_ref[...], NN)
        yfx = _dot(ofx_ref[...].astype(BF16), wfx_ref[...], NN)
        ysb_ref[...] = ysb
        yfx_ref[...] = yfx
        mg_ref[...] = (_sigmoid(gsb_ref[...]) * ysb + _sigmoid(gfx_ref[...]) * yfx).astype(BF16)

    rows_e = pl.BlockSpec((tm, e), lambda i: (i, 0))
    rows_d = pl.BlockSpec((tm, d), lambda i: (i, 0))
    w_spec = pl.BlockSpec((e, d), lambda i: (0, 0))
    return pl.pallas_call(
        body, name="proj_gate_fwd", grid=(t // tm,),
        in_specs=[rows_e, rows_e, w_spec, w_spec, rows_d, pl.BlockSpec((tm, d), lambda i: (i, 1))],
        out_specs=[rows_d, rows_d, rows_d],
        out_shape=[jax.ShapeDtypeStruct((t, d), BF16), jax.ShapeDtypeStruct((t, d), F32), jax.ShapeDtypeStruct((t, d), F32)],
        compiler_params=_params("parallel"),
    )(o_sb, o_fx, wp_sb, wp_fx, g, g)


def _gate_bwd(dr, w_out, y_sb, y_fx, g, tm):
    t, d = dr.shape

    def body(dr_ref, w_ref, ysb_ref, yfx_ref, gsb_ref, gfx_ref, dysb_ref, dyfx_ref, dg_ref):
        dm = _dot(dr_ref[...], w_ref[...], NT)
        ssb = _sigmoid(gsb_ref[...])
        sfx = _sigmoid(gfx_ref[...])
        dysb_ref[...] = (dm * ssb).astype(BF16)
        dyfx_ref[...] = (dm * sfx).astype(BF16)
        dg_ref[:, 0:d] = (dm * ysb_ref[...] * ssb * (1.0 - ssb)).astype(BF16)
        dg_ref[:, d:2 * d] = (dm * yfx_ref[...] * sfx * (1.0 - sfx)).astype(BF16)

    rows = pl.BlockSpec((tm, d), lambda i: (i, 0))
    rows1 = pl.BlockSpec((tm, d), lambda i: (i, 1))
    return pl.pallas_call(
        body, name="gate_bwd", grid=(t // tm,),
        in_specs=[rows, pl.BlockSpec((d, d), lambda i: (0, 0)), rows, rows, rows, rows1],
        out_specs=[rows, rows, pl.BlockSpec((tm, 2 * d), lambda i: (i, 0))],
        out_shape=[jax.ShapeDtypeStruct((t, d), BF16)] * 2 + [jax.ShapeDtypeStruct((t, 2 * d), BF16)],
        compiler_params=_params("parallel"),
    )(dr, w_out, y_sb, y_fx, g, g)


def _mm_res_ln(a, w, xres, gamma, beta, tm, name):
    t, kk = a.shape
    d = w.shape[1]

    def body(a_ref, w_ref, x_ref, g_ref, b_ref, xn_ref, xh_ref, rs_ref, xb_ref):
        r = ALPHA * x_ref[...] + _dot(a_ref[...].astype(BF16), w_ref[...], NN)
        mean = jnp.mean(r, axis=1, keepdims=True)
        cen = r - mean
        rstd = lax.rsqrt(jnp.mean(cen * cen, axis=1, keepdims=True) + LN_EPS)
        xh = cen * rstd
        xn = xh * g_ref[...] + b_ref[...]
        xh_ref[...] = xh
        xn_ref[...] = xn
        xb_ref[...] = xn.astype(BF16)
        rs_ref[...] = rstd

    rows_d = pl.BlockSpec((tm, d), lambda i: (i, 0))
    vec = pl.BlockSpec((1, d), lambda i: (0, 0))
    return pl.pallas_call(
        body, name=name, grid=(t // tm,),
        in_specs=[pl.BlockSpec((tm, kk), lambda i: (i, 0)), pl.BlockSpec((kk, d), lambda i: (0, 0)), rows_d, vec, vec],
        out_specs=[rows_d, rows_d, pl.BlockSpec((tm, 1), lambda i: (i, 0)), rows_d],
        out_shape=[jax.ShapeDtypeStruct((t, d), F32), jax.ShapeDtypeStruct((t, d), F32), jax.ShapeDtypeStruct((t, 1), F32),
                   jax.ShapeDtypeStruct((t, d), BF16)],
        compiler_params=_params("parallel"),
    )(a, w, xres, gamma, beta)


def _ln_bwd_math(dy, xh, rstd, gamma):
    dxh = dy * gamma
    m1 = jnp.mean(dxh, axis=1, keepdims=True)
    m2 = jnp.mean(dxh * xh, axis=1, keepdims=True)
    return rstd * (dxh - m1 - xh * m2)


def _rowsum8(x):
    tm, n = x.shape
    return jnp.sum(x.reshape(tm // SUBLANES, SUBLANES, n), axis=0)


def _fold8(ref):
    ref[0:1, :] = jnp.sum(ref[...], axis=0, keepdims=True)


def _loss_ln_bwd(x2, xh, rstd, gamma, target, tm):
    t, d = x2.shape

    def body(x_ref, xh_ref, rs_ref, g_ref, tg_ref, dr_ref, dg_ref, db_ref, ls_ref, drb_ref):
        @pl.when(pl.program_id(0) == 0)
        def _():
            dg_ref[...] = jnp.zeros_like(dg_ref)
            db_ref[...] = jnp.zeros_like(db_ref)
            ls_ref[...] = jnp.zeros_like(ls_ref)

        err = x_ref[...] - tg_ref[...]
        xh = xh_ref[...]
        dy = err * (1.0 / d)
        dr = _ln_bwd_math(dy, xh, rs_ref[...], g_ref[...])
        dr_ref[...] = dr
        drb_ref[...] = dr.astype(BF16)
        dg_ref[...] += _rowsum8(dy * xh)
        db_ref[...] += _rowsum8(dy)
        sq = _rowsum8(err * err)
        part = sq[:, 0:LANES]
        for j in range(1, d // LANES):
            part = part + sq[:, j * LANES:(j + 1) * LANES]
        ls_ref[...] += part * (0.5 / d)

        @pl.when(pl.program_id(0) == t // tm - 1)
        def _():
            _fold8(dg_ref)
            _fold8(db_ref)
            ls_ref[0:1, 0:1] = jnp.sum(jnp.sum(ls_ref[...], axis=0, keepdims=True), axis=1, keepdims=True)

    rows = pl.BlockSpec((tm, d), lambda i: (i, 0))
    acc = pl.BlockSpec((SUBLANES, d), lambda i: (0, 0))
    return pl.pallas_call(
        body, name="loss_ln_bwd", grid=(t // tm,),
        in_specs=[rows, rows, pl.BlockSpec((tm, 1), lambda i: (i, 0)), pl.BlockSpec((1, d), lambda i: (0, 0)), rows],
        out_specs=[rows, acc, acc, pl.BlockSpec((SUBLANES, LANES), lambda i: (0, 0)), rows],
        out_shape=[jax.ShapeDtypeStruct((t, d), F32), jax.ShapeDtypeStruct((SUBLANES, d), F32),
                   jax.ShapeDtypeStruct((SUBLANES, d), F32), jax.ShapeDtypeStruct((SUBLANES, LANES), F32),
                   jax.ShapeDtypeStruct((t, d), BF16)],
        compiler_params=_params("arbitrary"),
    )(x2, xh, rstd, gamma, target)


def _ln_bwd(dr_next, dlin, xh, rstd, gamma, tm):
    t, d = xh.shape

    def body(dn_ref, dl_ref, xh_ref, rs_ref, g_ref, dr_ref, dg_ref, db_ref, drb_ref):
        @pl.when(pl.program_id(0) == 0)
        def _():
            dg_ref[...] = jnp.zeros_like(dg_ref)
            db_ref[...] = jnp.zeros_like(db_ref)

        dy = ALPHA * dn_ref[...] + dl_ref[...]
        xh = xh_ref[...]
        dr = _ln_bwd_math(dy, xh, rs_ref[...], g_ref[...])
        dr_ref[...] = dr
        drb_ref[...] = dr.astype(BF16)
        dg_ref[...] += _rowsum8(dy * xh)
        db_ref[...] += _rowsum8(dy)

        @pl.when(pl.program_id(0) == t // tm - 1)
        def _():
            _fold8(dg_ref)
            _fold8(db_ref)

    rows = pl.BlockSpec((tm, d), lambda i: (i, 0))
    acc = pl.BlockSpec((SUBLANES, d), lambda i: (0, 0))
    return pl.pallas_call(
        body, name="ln_bwd", grid=(t // tm,),
        in_specs=[rows, rows, rows, pl.BlockSpec((tm, 1), lambda i: (i, 0)), pl.BlockSpec((1, d), lambda i: (0, 0))],
        out_specs=[rows, acc, acc, rows],
        out_shape=[jax.ShapeDtypeStruct((t, d), F32), jax.ShapeDtypeStruct((SUBLANES, d), F32),
                   jax.ShapeDtypeStruct((SUBLANES, d), F32), jax.ShapeDtypeStruct((t, d), BF16)],
        compiler_params=_params("arbitrary"),
    )(dr_next, dlin, xh, rstd, gamma)


def _shift_rows(x, halo, shift, row):
    out = pltpu.roll(x, shift, 0)
    n = halo.shape[0]
    for r in range(shift):
        out = jnp.where(row == r, halo[n - shift + r:n - shift + r + 1, :], out)
    return out


def _unshift_rows(x, halo, shift, row, tm):
    out = pltpu.roll(x, tm - shift, 0)
    for r in range(shift):
        out = jnp.where(row == tm - shift + r, halo[r:r + 1, :], out)
    return out


def _conv_pre(ug_ref, halo_ref, wc_ref, bc_ref, first, tm):
    ug = ug_ref[...].astype(F32)
    halo = jnp.where(first, 0.0, halo_ref[...].astype(F32))
    row = lax.broadcasted_iota(jnp.int32, ug.shape, 0)
    wc = wc_ref[...]
    um1 = _shift_rows(ug, halo, 1, row)
    um2 = _shift_rows(ug, halo, 2, row)
    c = bc_ref[...] + wc[2:3, :] * ug + wc[1:2, :] * um1 + wc[0:1, :] * um2
    return c, ug, um1, um2


INV_SQRT2 = 1.0 / math.sqrt(2.0)
INV_SQRT2PI = 1.0 / math.sqrt(2.0 * math.pi)


def _conv_glu_fwd(u, wc, bc, seq, tm):
    t, f2 = u.shape
    f = f2 // 2
    per_seq = seq // tm
    hb = tm // HALO

    def body(ug_ref, halo_ref, uv_ref, wc_ref, bc_ref, a_ref):
        first = (pl.program_id(0) % per_seq) == 0
        c, _, _, _ = _conv_pre(ug_ref, halo_ref, wc_ref, bc_ref, first, tm)
        gelu = 0.5 * c * (1.0 + lax.erf(c * INV_SQRT2))
        a_ref[...] = (gelu * uv_ref[...].astype(F32)).astype(BF16)

    return pl.pallas_call(
        body, name="conv_glu_fwd", grid=(t // tm,),
        in_specs=[pl.BlockSpec((tm, f), lambda i: (i, 0)),
                  pl.BlockSpec((HALO, f), lambda i: (jnp.maximum(i * hb - 1, 0), 0)),
                  pl.BlockSpec((tm, f), lambda i: (i, 1)),
                  pl.BlockSpec((3, f), lambda i: (0, 0)), pl.BlockSpec((1, f), lambda i: (0, 0))],
        out_specs=pl.BlockSpec((tm, f), lambda i: (i, 0)),
        out_shape=jax.ShapeDtypeStruct((t, f), BF16),
        compiler_params=_params("parallel"),
    )(u, u, u, wc, bc)


def _conv_glu_bwd1(u, da, wc, bc, seq, tm):
    t, f2 = u.shape
    f = f2 // 2
    per_seq = seq // tm
    hb = tm // HALO

    def body(ug_ref, halo_ref, uv_ref, da_ref, wc_ref, bc_ref, dc_ref, duv_ref):
        first = (pl.program_id(0) % per_seq) == 0
        c, _, _, _ = _conv_pre(ug_ref, halo_ref, wc_ref, bc_ref, first, tm)
        cdf = 0.5 * (1.0 + lax.erf(c * INV_SQRT2))
        pdf = jnp.exp(-0.5 * c * c) * INV_SQRT2PI
        da = da_ref[...].astype(F32)
        duv_ref[...] = (da * (c * cdf)).astype(BF16)
        dc_ref[...] = (da * uv_ref[...].astype(F32) * (cdf + c * pdf)).astype(BF16)

    rows = pl.BlockSpec((tm, f), lambda i: (i, 0))
    return pl.pallas_call(
        body, name="conv_glu_bwd1", grid=(t // tm,),
        in_specs=[rows, pl.BlockSpec((HALO, f), lambda i: (jnp.maximum(i * hb - 1, 0), 0)),
                  pl.BlockSpec((tm, f), lambda i: (i, 1)), rows,
                  pl.BlockSpec((3, f), lambda i: (0, 0)), pl.BlockSpec((1, f), lambda i: (0, 0))],
        out_specs=[rows, pl.BlockSpec((tm, f), lambda i: (i, 1))],
        out_shape=[jax.ShapeDtypeStruct((t, f), BF16), jax.ShapeDtypeStruct((t, f2), BF16)],
        compiler_params=_params("parallel"),
    )(u, u, u, da, wc, bc)


def _conv_glu_bwd2(u, dc, wc, du, seq, tm):
    t, f2 = u.shape
    f = f2 // 2
    per_seq = seq // tm
    hb = tm // HALO
    nblk = t // HALO

    def body(ug_ref, dc_ref, nxt_ref, wc_ref, du_in_ref, dug_ref, w0_ref, w1_ref, w2_ref, b_ref):
        i = pl.program_id(0)

        @pl.when(i == 0)
        def _():
            for r in (w0_ref, w1_ref, w2_ref, b_ref):
                r[...] = jnp.zeros_like(r)

        last = (i % per_seq) == per_seq - 1
        ug = ug_ref[...].astype(F32)
        nxt = jnp.where(last, 0.0, nxt_ref[...].astype(F32))
        row = lax.broadcasted_iota(jnp.int32, ug.shape, 0)
        dc = dc_ref[...].astype(F32)
        wc = wc_ref[...]
        dp1 = _unshift_rows(dc, nxt, 1, row, tm)
        dp2 = _unshift_rows(dc, nxt, 2, row, tm)
        dug_ref[...] = (wc[2:3, :] * dc + wc[1:2, :] * dp1 + wc[0:1, :] * dp2).astype(BF16)
        w2_ref[...] += _rowsum8(dc * ug)
        w1_ref[...] += _rowsum8(dp1 * ug)
        w0_ref[...] += _rowsum8(dp2 * ug)
        b_ref[...] += _rowsum8(dc)

        @pl.when(i == t // tm - 1)
        def _():
            for r in (w0_ref, w1_ref, w2_ref, b_ref):
                _fold8(r)

    rows = pl.BlockSpec((tm, f), lambda i: (i, 0))
    acc = pl.BlockSpec((SUBLANES, f), lambda i: (0, 0))
    return pl.pallas_call(
        body, name="conv_glu_bwd2", grid=(t // tm,),
        in_specs=[rows, rows, pl.BlockSpec((HALO, f), lambda i: (jnp.minimum((i + 1) * hb, nblk - 1), 0)),
                  pl.BlockSpec((3, f), lambda i: (0, 0)), pl.BlockSpec(memory_space=pl.ANY)],
        out_specs=[rows, acc, acc, acc, acc],
        out_shape=[jax.ShapeDtypeStruct((t, f2), BF16)] + [jax.ShapeDtypeStruct((SUBLANES, f), F32)] * 4,
        input_output_aliases={4: 0},
        compiler_params=_params("arbitrary"),
    )(u, dc, dc, wc, du)


def _colsum(x, tm, name):
    t, n = x.shape

    def body(x_ref, o_ref):
        @pl.when(pl.program_id(0) == 0)
        def _():
            o_ref[...] = jnp.zeros_like(o_ref)

        o_ref[...] += _rowsum8(x_ref[...].astype(F32))

        @pl.when(pl.program_id(0) == t // tm - 1)
        def _():
            _fold8(o_ref)

    return pl.pallas_call(
        body, name=name, grid=(t // tm,),
        in_specs=[pl.BlockSpec((tm, n), lambda i: (i, 0))],
        out_specs=pl.BlockSpec((SUBLANES, n), lambda i: (0, 0)),
        out_shape=jax.ShapeDtypeStruct((SUBLANES, n), F32),
        compiler_params=_params("arbitrary"),
    )(x)


def _adamw(w, gparts, m, v, name):
    p, r, c = gparts.shape
    tr = r
    for cand in (512, 256, 128, 64, 32, 16, 8):
        if cand * p <= 1024 and r % cand == 0 and r > cand:
            tr = cand
            break
    c1 = 1.0 - ADAM_B1 ** ADAM_STEP
    c2 = 1.0 - ADAM_B2 ** ADAM_STEP

    def body(w_ref, g_ref, m_ref, v_ref, go_ref, d_ref, mo_ref, vo_ref):
        g = g_ref[0].astype(F32)
        for i in range(1, p):
            g = g + g_ref[i].astype(F32)
        mn = ADAM_B1 * m_ref[...] + (1.0 - ADAM_B1) * g
        vn = ADAM_B2 * v_ref[...] + (1.0 - ADAM_B2) * (g * g)
        go_ref[...] = g
        mo_ref[...] = mn
        vo_ref[...] = vn
        d_ref[...] = -ADAM_LR * ((mn / c1) / (jnp.sqrt(vn / c2) + ADAM_EPS) + ADAM_WD * w_ref[...])

    blk = pl.BlockSpec((tr, c), lambda i: (i, 0))
    return pl.pallas_call(
        body, name=name, grid=(r // tr,),
        in_specs=[blk, pl.BlockSpec((p, tr, c), lambda i: (0, i, 0)), blk, blk],
        out_specs=[blk] * 4,
        out_shape=[jax.ShapeDtypeStruct((r, c), F32)] * 4,
        compiler_params=_params("parallel"),
    )(w, gparts, m, v)


MESH = pl.DeviceIdType.MESH
ANY = pl.BlockSpec(memory_space=pl.ANY)


def _all_gather(xs, name):
    n = len(xs)

    def body(*refs):
        x_refs, out_refs = refs[:n], refs[n:2 * n]
        send_sems, recv_sems, local_sems = refs[2 * n:]
        x, y, c = lax.axis_index("x"), lax.axis_index("y"), lax.axis_index("c")
        me, sibling = (x, y, c), (x, y, 1 - c)
        chips = [(1 - x, y), (x, 1 - y), (1 - x, 1 - y)]

        def slot(a, px, py, pc):
            return out_refs[a].at[4 * px + 2 * py + pc]

        def copy(a, k, block, to, src=None):
            return pltpu.make_async_remote_copy(
                src_ref=slot(a, *block) if src is None else src, dst_ref=slot(a, *block),
                send_sem=send_sems.at[k * n + a], recv_sem=recv_sems.at[k * n + a], device_id=to, device_id_type=MESH)

        arrays = range(n)
        mine = [pltpu.make_async_copy(x_refs[a], slot(a, *me), local_sems.at[a]) for a in arrays]
        first = [copy(a, 0, me, sibling, src=x_refs[a]) for a in arrays]
        first += [copy(a, 1 + j, me, (*chip, c), src=x_refs[a]) for j, chip in enumerate(chips) for a in arrays]
        for cp in mine + first:
            cp.start()
        passed = []
        for j, chip in enumerate(chips):
            for a in arrays:
                copy(a, 1 + j, (*chip, c), me).wait_recv()
                passed.append(copy(a, 4 + j, (*chip, c), sibling))
                passed[-1].start()
        for a in arrays:
            copy(a, 0, sibling, me).wait_recv()
        for j, chip in enumerate(chips):
            for a in arrays:
                copy(a, 4 + j, (*chip, 1 - c), me).wait_recv()
        for cp in first + passed:
            cp.wait_send()
        for cp in mine:
            cp.wait()

    return pl.pallas_call(
        body, name=name,
        out_shape=[jax.ShapeDtypeStruct((N_DEV,) + x.shape, x.dtype) for x in xs],
        in_specs=[ANY] * n, out_specs=[ANY] * n,
        scratch_shapes=[pltpu.SemaphoreType.DMA((7 * n,)), pltpu.SemaphoreType.DMA((7 * n,)),
                        pltpu.SemaphoreType.DMA((n,))],
    )(*xs)


def _peer_copies(kind, src_refs, dst_refs, send_sems, recv_sems, local_sems):
    n = len(src_refs)
    x, y, c = lax.axis_index("x"), lax.axis_index("y"), lax.axis_index("c")
    mine = 4 * x + 2 * y + c

    def src(a, idx):
        return src_refs[a] if kind == "spread" else src_refs[a].at[idx]

    copies = [pltpu.make_async_copy(src(a, mine), dst_refs[a].at[mine], local_sems.at[a]) for a in range(n)]
    for k in range(1, N_DEV):
        px = 1 - x if k & 4 else x
        py = 1 - y if k & 2 else y
        pc = 1 - c if k & 1 else c
        for a in range(n):
            copies.append(pltpu.make_async_remote_copy(
                src_ref=src(a, 4 * px + 2 * py + pc), dst_ref=dst_refs[a].at[mine],
                send_sem=send_sems.at[(k - 1) * n + a], recv_sem=recv_sems.at[(k - 1) * n + a],
                device_id=(px, py, pc), device_id_type=MESH))
    return copies


def _peer_shapes(kind, arrays):
    n = len(arrays)
    outs = [jax.ShapeDtypeStruct(((N_DEV,) + a.shape) if kind == "spread" else a.shape, a.dtype) for a in arrays]
    sems = [pltpu.SemaphoreType.DMA((7 * n,)), pltpu.SemaphoreType.DMA((7 * n,)), pltpu.SemaphoreType.DMA((n,))]
    return outs, sems


def _tile(n, pref, unit=LANES):
    if n <= pref:
        return n
    best = None
    for cand in range(unit, pref + 1, unit):
        if n % cand == 0:
            best = cand
    assert best is not None, (n, pref, unit)
    return best


LATE_KEYS = dict(w_proj_sb="wp_sb", w_proj_fox="wp_fx", w_out="w_out", w_up="w_up", w_conv="w_conv", w_down="w_down")


def _layer_step(x, target, w, attn_blk, late=None):
    b, s, d = x.shape
    t = b * s
    w = dict(w)
    e = w["w_qkv"].shape[1] // 6
    h = e // HEAD_DIM
    f = w["b_conv"].shape[1]
    x2 = x.reshape(t, d)
    tg = target.reshape(t, d)
    pp = 2 if (e // LANES) % 2 == 0 else 1
    fox_blk = min(2 * attn_blk, s)
    tm = _tile(t, 512, HALO)
    tmo = _tile(t, 1024, HALO)
    tmc = _tile(s, 256, HALO)
    tkt = _tile(t, 2048, HALO)
    td = _tile(d, 1024)
    tf = _tile(f, 1408)
    t2f = _tile(2 * f, 1408)
    tqkv = _tile(6 * e, 1024)
    tg2 = _tile(2 * d, 1024)
    xb = x2.astype(BF16)

    qkv = _matmul(xb, w["w_qkv"], NN, tm=tmo, tn=tqkv, tk=d, out_dtype=BF16, name="in_qkv", bias=w["b_qkv"], b_outer=True)
    gate = _matmul(xb, jnp.concatenate([w["w_g"], w["w_f"]], axis=1), NN, tm=tmo, tn=2 * d + LANES, tk=d, out_dtype=F32,
                   name="in_gate", bias=jnp.concatenate([w["b_g"], w["b_f"]], axis=1), b_outer=True)
    nr = s // LANES
    f2 = gate[:, 2 * d:2 * d + h].reshape(b, s, h).transpose(0, 2, 1).reshape(b * h * nr, LANES)
    ct = _scan_rows(f2, nr, "fwd").reshape(b * h, 1, s)
    o_sb, tot, first = _sb_fwd(qkv, b, s, e, attn_blk, pp)
    if late is None:
        o_fx, lse = _fox_fwd(qkv, ct, b, s, e, fox_blk, 1)
    else:
        o_fx, lse, *gathered = _fox_fwd(qkv, ct, b, s, e, fox_blk, 1, comm=("spread", late[1]))
        for name, g in zip(late[0], gathered):
            w[LATE_KEYS[name]] = _join(g, name)
    merged, y_sb, y_fx = _proj_gate_fwd(o_sb, o_fx, w["wp_sb"], w["wp_fx"], gate, tm)
    x1, xh1, rs1, x1b = _mm_res_ln(merged, w["w_out"], x2, w["ln1_g"], w["ln1_b"], tm, "out_ln1")
    u = _matmul(x1b, w["w_up"], NN, tm=tmo, tn=t2f, tk=d, out_dtype=BF16, name="ffn_up", b_outer=True)
    act = _conv_glu_fwd(u, w["w_conv"], w["b_conv"], s, tmc)
    xo, xh2, rs2, _ = _mm_res_ln(act, w["w_down"], x1, w["ln2_g"], w["ln2_b"], tm, "down_ln2")

    gr = {}
    dr2, dg2, db2, ls, dr2b = _loss_ln_bwd(xo, xh2, rs2, w["ln2_g"], tg, tm)
    gr["ln2_g"], gr["ln2_b"] = dg2[0:1], db2[0:1]
    da = _matmul(dr2b, w["w_down"], NT, tm=tmo, tn=tf, tk=d, out_dtype=BF16, name="d_act", b_outer=True)
    gr["w_down"] = _matmul(act, dr2b, TN, tm=tf, tn=td, tk=tkt, out_dtype=BF16, name="dw_down")
    dc, du = _conv_glu_bwd1(u, da, w["w_conv"], w["b_conv"], s, tmc)
    du, gw0, gw1, gw2, gbc = _conv_glu_bwd2(u, dc, w["w_conv"], du, s, tmc)
    gr["w_conv"] = jnp.concatenate([gw0[0:1], gw1[0:1], gw2[0:1]], axis=0)
    gr["b_conv"] = gbc[0:1]
    dlin1 = _matmul(du, w["w_up"], NT, tm=tm, tn=td, tk=2 * f, out_dtype=F32, name="d_x1")
    gr["w_up"] = _matmul(x1b, du, TN, tm=td, tn=t2f, tk=tkt, out_dtype=BF16, name="dw_up")
    dr1, dg1, db1, dr1b = _ln_bwd(dr2, dlin1, xh1, rs1, w["ln1_g"], tm)
    gr["ln1_g"], gr["ln1_b"] = dg1[0:1], db1[0:1]
    gr["w_out"] = _matmul(merged, dr1b, TN, tm=td, tn=td, tk=tkt, out_dtype=BF16, name="dw_out")
    dy_sb, dy_fx, dgate = _gate_bwd(dr1b, w["w_out"], y_sb, y_fx, gate, tm)
    do_sb = _matmul(dy_sb, w["wp_sb"], NT, tm=tm, tn=e, tk=d, out_dtype=BF16, name="d_o_sb")
    do_fx = _matmul(dy_fx, w["wp_fx"], NT, tm=tm, tn=e, tk=d, out_dtype=BF16, name="d_o_fx")
    gr["wp_sb"] = _matmul(o_sb, dy_sb, TN, tm=e, tn=td, tk=tkt, out_dtype=BF16, name="dwp_sb")
    gr["wp_fx"] = _matmul(o_fx, dy_fx, TN, tm=e, tn=td, tk=tkt, out_dtype=BF16, name="dwp_fx")
    dq_sb, dk_sb, dv_sb = _sb_bwd(qkv, do_sb, tot, first, b, s, e, attn_blk, pp)
    landed = None
    if late is None:
        dq_fx, dk_fx, dv_fx, dct = _fox_bwd(qkv, ct, do_fx, o_fx, lse, b, s, e, fox_blk, 1)
    else:
        blocks = [_cut(gr[LATE_KEYS[n]], n).astype(BF16 if n in MATMUL_OPERANDS else F32) for n in late[0]]
        dq_fx, dk_fx, dv_fx, dct, *got = _fox_bwd(qkv, ct, do_fx, o_fx, lse, b, s, e, fox_blk, 1,
                                                   comm=("exchange", blocks))
        landed = dict(zip(late[0], got))
    dqkv = jnp.concatenate([dq_sb, dk_sb, dv_sb, dq_fx, dk_fx, dv_fx], axis=1)
    df2 = _scan_rows(f2, nr, "bwd", dct.reshape(b * h * nr, LANES))
    df = jnp.pad(df2.reshape(b, h, s).transpose(0, 2, 1).reshape(t, h), ((0, 0), (0, LANES - h))).astype(BF16)
    gr["w_qkv"], cs = _matmul(xb, dqkv, TN, tm=d, tn=tqkv, tk=tkt, out_dtype=BF16, name="dw_qkv", b_colsum=True)
    gr["b_qkv"] = cs[0:1]
    gr["w_g"], cs = _matmul(xb, dgate, TN, tm=d, tn=tg2, tk=tkt, out_dtype=BF16, name="dw_gate", b_colsum=True)
    gr["b_g"] = cs[0:1]
    gr["w_f"], cs = _matmul(xb, df, TN, tm=d, tn=LANES, tk=tkt, out_dtype=BF16, name="dw_forget", b_colsum=True)
    gr["b_f"] = cs[0:1]
    comm = None
    if late is not None:
        small = jnp.concatenate([_w_in_layout(gr["b_qkv"], gr["b_f"], gr["b_g"], h), gr["ln1_g"], gr["ln1_b"],
                                 gr["b_conv"], gr["ln2_g"], gr["ln2_b"]], axis=1)
        comm = ("exchange", [_cut(_w_in_layout(gr["w_qkv"], gr["w_f"], gr["w_g"], h), "w_in").astype(BF16),
                             jnp.broadcast_to(small[None], (N_DEV,) + small.shape)])
    dx, *got = _input_grad([dqkv, dgate, df], [w["w_qkv"], w["w_g"], w["w_f"]], dr1, tm, comm)
    if late is not None:
        landed["w_in"], landed["replicated"] = got
    return ls[0:1, 0:1], dx.reshape(b, s, d), gr, landed


SHARDED = ("w_in", "w_proj_sb", "w_proj_fox", "w_out", "w_up", "w_conv", "w_down")
ROW_SHARDED = ("w_out", "w_down")
REPLICATED = ("b_in", "ln1_g", "ln1_b", "b_conv", "ln2_g", "ln2_b")
WEIGHTS = ("w_in", "b_in", "w_proj_sb", "w_proj_fox", "w_out", "ln1_g", "ln1_b", "w_up", "w_conv", "b_conv",
           "w_down", "ln2_g", "ln2_b")
MATMUL_OPERANDS = ("w_in", "w_proj_sb", "w_proj_fox", "w_out", "w_up", "w_down")


def _w_in_layout(g_qkv, g_f, g_g, h):
    return jnp.concatenate([g_qkv, g_f[:, :h], g_g], axis=1)


def _cut(full, name):
    r, c = full.shape
    if name in ROW_SHARDED:
        return full.reshape(N_DEV, r // N_DEV, c)
    cs = c // N_DEV
    return jnp.stack([full[:, j * cs:(j + 1) * cs] for j in range(N_DEV)], axis=0)


def _join(blocks, name):
    p, r, c = blocks.shape
    if name in ROW_SHARDED:
        return blocks.reshape(p * r, c)
    return jnp.concatenate([blocks[j] for j in range(p)], axis=1)


def kernel(x, w_in, b_in, w_proj_sb, w_proj_fox, w_out, ln1_g, ln1_b, w_up, w_conv, b_conv, w_down, ln2_g, ln2_b, loss_target, m_w_in, m_b_in, m_w_proj_sb, m_w_proj_fox, m_w_out, m_ln1_g, m_ln1_b, m_w_up, m_w_conv, m_b_conv, m_w_down, m_ln2_g, m_ln2_b, v_w_in, v_b_in, v_w_proj_sb, v_w_proj_fox, v_w_out, v_ln1_g, v_ln1_b, v_w_up, v_w_conv, v_b_conv, v_w_down, v_ln2_g, v_ln2_b):
    wts = dict(w_in=w_in, b_in=b_in, w_proj_sb=w_proj_sb, w_proj_fox=w_proj_fox, w_out=w_out, ln1_g=ln1_g, ln1_b=ln1_b,
               w_up=w_up, w_conv=w_conv, b_conv=b_conv, w_down=w_down, ln2_g=ln2_g, ln2_b=ln2_b)
    mom = dict(w_in=m_w_in, b_in=m_b_in, w_proj_sb=m_w_proj_sb, w_proj_fox=m_w_proj_fox, w_out=m_w_out, ln1_g=m_ln1_g,
               ln1_b=m_ln1_b, w_up=m_w_up, w_conv=m_w_conv, b_conv=m_b_conv, w_down=m_w_down, ln2_g=m_ln2_g, ln2_b=m_ln2_b)
    var = dict(w_in=v_w_in, b_in=v_b_in, w_proj_sb=v_w_proj_sb, w_proj_fox=v_w_proj_fox, w_out=v_w_out, ln1_g=v_ln1_g,
               ln1_b=v_ln1_b, w_up=v_w_up, w_conv=v_w_conv, b_conv=v_b_conv, w_down=v_w_down, ln2_g=v_ln2_g, ln2_b=v_ln2_b)
    shard = {n: wts[n].reshape(wts[n].shape[-2:]) for n in WEIGHTS}

    w_in_full = _join(_all_gather([shard["w_in"].astype(BF16)], "gather_w_in")[0], "w_in")
    late_names = [n for n in SHARDED if n != "w_in"]
    late = (late_names, [shard[n].astype(BF16) if n in MATMUL_OPERANDS else shard[n] for n in late_names])
    e = shard["w_proj_sb"].shape[0]
    h = e // HEAD_DIM
    nq = 6 * e

    def cut_in(a, pad):
        fcols = a[:, nq:nq + h]
        if pad:
            fcols = jnp.pad(fcols, ((0, 0), (0, LANES - h)))
        return a[:, :nq], a[:, nq + h:], fcols

    w_qkv, w_g, w_f = cut_in(w_in_full, True)
    b_qkv, b_g, b_f = cut_in(shard["b_in"], True)
    w = dict(w_qkv=w_qkv, w_g=w_g, w_f=w_f, b_qkv=b_qkv, b_g=b_g, b_f=b_f, b_conv=shard["b_conv"],
             ln1_g=shard["ln1_g"], ln1_b=shard["ln1_b"], ln2_g=shard["ln2_g"], ln2_b=shard["ln2_b"])

    loss_local, grad_x, gr, gsum = _layer_step(x, loss_target, w, min(256, x.shape[1]), late)
    loss = lax.psum(loss_local[0, 0], ("x", "y", "c"))

    parts = gsum.pop("replicated")
    off = 0
    for n in REPLICATED:
        gsum[n] = parts[:, :, off:off + shard[n].size]
        off += shard[n].size

    grads, deltas, new_m, new_v = [], [], [], []
    for n in WEIGHTS:
        shp = wts[n].shape
        g, dl, mn, vn = _adamw(shard[n], gsum[n], mom[n].reshape(shard[n].shape), var[n].reshape(shard[n].shape),
                               "adamw_" + n)
        grads.append(g.reshape(shp))
        deltas.append(dl.reshape(shp))
        new_m.append(mn.reshape(shp))
        new_v.append(vn.reshape(shp))
    return (loss, grad_x, *grads, *deltas, *new_m, *new_v)
```
